```python
import math
import jax
import jax.numpy as jnp
from jax import lax
import numpy as np

D_MODEL = 1024
BATCH = 4
SEQ = 4096
DEPTH = 2
DEC_BATCH = 32
DEC_SEQ = 8
PAST_LEN = 16384
PAGE_SIZE = 128

HEAD_DIM = 64
NSA_HEADS = 4
NSA_KV_HEADS = 2
NSA_GROUP = NSA_HEADS // NSA_KV_HEADS
NSA_BLOCK = 64
NSA_TOPK = 16
NSA_WINDOW = 512
Q_BLOCK = 128
NUM_BUCKETS = 32
MAX_DISTANCE = 128
CONV_CH = D_MODEL // 4
CONV_WIDTH = 31
RET_HEADS = 4
RET_W = RET_HEADS * HEAD_DIM
RET_CHUNK = 64
ROPE_BASE = 10000.0
GDN_HEADS = 4
GDN_W = GDN_HEADS * HEAD_DIM
GDN_CONV = 4
GDN_CHUNK = 64
D_FF = 2816
FFN_CONV = 3
N_BRANCH = 4
BRANCH_W = NSA_HEADS * HEAD_DIM
EPS = 1e-6
NEG_INF = -1e30
IN_SPLITS = (NSA_HEADS * HEAD_DIM, 6 * NSA_KV_HEADS * HEAD_DIM, 3 * NSA_HEADS, 2 * CONV_CH, 4 * RET_W, 4 * GDN_W + 2 * GDN_HEADS, N_BRANCH * D_MODEL)
IN_COLS = sum(IN_SPLITS)

kernel_name = "hybrid_nsa_conv_retention_gdn_step"


def split_cols(a, sizes):
    return jnp.split(a, [int(i) for i in np.cumsum(sizes)[:-1]], axis=-1)


def rms_norm(x, g):
    x32 = x.astype(jnp.float32)
    y = x32 * lax.rsqrt(jnp.mean(x32 * x32, axis=-1, keepdims=True) + EPS)
    return (y * g.astype(jnp.float32)).astype(x.dtype)


def layer_norm(x, g, b):
    x32 = x.astype(jnp.float32)
    mu = jnp.mean(x32, axis=-1, keepdims=True)
    var = jnp.mean(jnp.square(x32 - mu), axis=-1, keepdims=True)
    return ((x32 - mu) * lax.rsqrt(var + EPS) * g.astype(jnp.float32) + b.astype(jnp.float32)).astype(x.dtype)


def causal_dwconv(x, buf, w):
    xp = jnp.concatenate([buf.astype(x.dtype), x], axis=1)
    y = lax.conv_general_dilated(xp, w[:, None, :].astype(x.dtype), window_strides=(1,), padding='VALID',
                                 dimension_numbers=('NWC', 'WIO', 'NWC'), feature_group_count=x.shape[-1])
    return y, xp[:, xp.shape[1] - (w.shape[0] - 1):]


def t5_bucket(dist):
    n = jnp.maximum(dist, 0)
    exact = NUM_BUCKETS // 2
    large = exact + (jnp.log(jnp.maximum(n, 1).astype(jnp.float32) / exact) / math.log(MAX_DISTANCE / exact)
                     * (NUM_BUCKETS - exact)).astype(jnp.int32)
    return jnp.where(n < exact, n, jnp.minimum(large, NUM_BUCKETS - 1))


def masked_softmax(s, mask):
    return jax.nn.softmax(jnp.where(mask, s, NEG_INF), axis=-1) * mask


def over_query_blocks(fn, args, q_axes, T):
    if T <= Q_BLOCK or T % Q_BLOCK:
        return fn(args)
    nq = T // Q_BLOCK

    def split(a, ax):
        return jnp.moveaxis(a.reshape(a.shape[:ax] + (nq, Q_BLOCK) + a.shape[ax + 1:]), ax, 0)

    out = lax.map(fn, tuple(split(a, ax) for a, ax in zip(args, q_axes)))
    out = jnp.moveaxis(out, 0, 1)
    return out.reshape((out.shape[0], T) + out.shape[3:])


def rotary(x, pos):
    half = x.shape[-1] // 2
    inv = ROPE_BASE ** (-jnp.arange(half, dtype=jnp.float32) / half)
    ang = pos.astype(jnp.float32)[:, None] * inv[None, :]
    cos, sin = jnp.cos(ang)[None, :, None, :], jnp.sin(ang)[None, :, None, :]
    x1, x2 = x[..., :half], x[..., half:]
    return jnp.concatenate([x1 * cos - x2 * sin, x1 * sin + x2 * cos], axis=-1)


def nsa_attention(q, kv_new, gate_logits, past_rows, win_buf, pos0, cmp_pool, cmp_pe, rel_bias):
    B, T = q.shape[:2]
    dt = q.dtype
    qpos = pos0 + jnp.arange(T, dtype=jnp.int32)
    qg = (q * HEAD_DIM ** -0.5).reshape(B, T, NSA_KV_HEADS, NSA_GROUP, HEAD_DIM)
    rb = rel_bias.astype(jnp.float32).reshape(NUM_BUCKETS, NSA_KV_HEADS, NSA_GROUP)
    L = T if past_rows is None else past_rows.shape[1] + T
    nb = -(-L // NSA_BLOCK)
    new_rows = jnp.pad(kv_new[:, :, :4], ((0, 0), (0, nb * NSA_BLOCK - L), (0, 0), (0, 0), (0, 0)))
    full = new_rows if past_rows is None else jnp.concatenate([past_rows.astype(dt), new_rows], axis=1)
    blocks = full.reshape(B, nb, NSA_BLOCK, 4, NSA_KV_HEADS, HEAD_DIM)
    pe = jnp.transpose(cmp_pe, (1, 0, 2))[:, :, None, :].astype(dt)
    cmp = jnp.einsum('bnjckd,cj->bnckd', blocks[:, :, :, :2] + pe, cmp_pool.astype(dt))
    k_c, v_c = cmp[:, :, 0], cmp[:, :, 1]
    blk = jnp.arange(nb, dtype=jnp.int32)
    d_c = qpos[:, None] - (blk * NSA_BLOCK + NSA_BLOCK - 1)[None, :]
    s_c = jnp.einsum('btkgd,bnkd->bkgtn', qg, k_c).astype(jnp.float32) + jnp.transpose(rb[t5_bucket(d_c)], (2, 3, 0, 1))
    p_c = masked_softmax(s_c, d_c >= 0)
    o_c = jnp.einsum('bkgtn,bnkd->btkgd', p_c.astype(dt), v_c)
    cur = (qpos // NSA_BLOCK)[:, None]
    forced = (blk[None] == 0) | (blk[None] == cur) | (blk[None] == cur - 1)
    score = jnp.where(blk[None] <= cur, jnp.where(forced, 2.0, p_c.sum(axis=2)), -1.0)
    top_s, idx = lax.top_k(score, min(NSA_TOPK, nb))
    ok = top_s > -0.5
    ks = jnp.transpose(blocks[:, :, :, 2], (0, 3, 1, 2, 4))
    vs = jnp.transpose(blocks[:, :, :, 3], (0, 3, 1, 2, 4))
    take = jax.vmap(jax.vmap(lambda a, i: a[i]))
    kv_ix = jnp.arange(NSA_KV_HEADS)[None, :, None, None, None]
    rbk = jnp.transpose(rb, (1, 0, 2))

    def sel_block(args):
        qb, ib, okb, pb = args
        Bq, Q = qb.shape[:2]
        kg, vg = take(ks, ib), take(vs, ib)
        kpos = ib[..., None] * NSA_BLOCK + jnp.arange(NSA_BLOCK, dtype=jnp.int32)
        dist = pb[None, None, :, None, None] - kpos
        mask = (okb[..., None] & (dist >= 0)).reshape(Bq, NSA_KV_HEADS, 1, Q, -1)
        bias = jnp.moveaxis(rbk[kv_ix, t5_bucket(dist)], -1, 2)
        s = jnp.einsum('bqkgd,bkqsjd->bkgqsj', qb, kg).astype(jnp.float32) + bias
        p = masked_softmax(s.reshape(Bq, NSA_KV_HEADS, NSA_GROUP, Q, -1), mask)
        return jnp.einsum('bkgqn,bkqnd->bqkgd', p.astype(dt), vg.reshape(Bq, NSA_KV_HEADS, Q, -1, HEAD_DIM))

    o_s = over_query_blocks(sel_block, (qg, idx, ok, qpos), (1, 2, 2, 0), T)
    kw = kv_new[:, :, 4:]
    if win_buf is None:
        real = kw
        k_all = jnp.pad(kw, ((0, 0), (NSA_WINDOW, 0), (0, 0), (0, 0), (0, 0)))
        span = NSA_WINDOW
    else:
        real = jnp.concatenate([win_buf.astype(dt), kw], axis=1)
        k_all = real
        span = win_buf.shape[1]
    k0 = pos0 - span

    def win_block(args):
        qb, pb = args
        Q = qb.shape[1]
        start = pb[0] - pos0
        kb = lax.dynamic_slice_in_dim(k_all, start, span + Q, axis=1)
        kpos = k0 + start + jnp.arange(span + Q, dtype=jnp.int32)
        dist = pb[:, None] - kpos[None, :]
        mask = (dist >= 0) & (dist < NSA_WINDOW) & (kpos >= 0)[None, :]
        s = jnp.einsum('bqkgd,bnkd->bkgqn', qb, kb[:, :, 0]).astype(jnp.float32) + jnp.transpose(rb[t5_bucket(dist)], (2, 3, 0, 1))
        p = masked_softmax(s, mask)
        return jnp.einsum('bkgqn,bnkd->bqkgd', p.astype(dt), kb[:, :, 1])

    o_w = over_query_blocks(win_block, (qg, qpos), (1, 0), T)
    new_win = real[:, real.shape[1] - min(NSA_WINDOW, real.shape[1]):]
    g = jax.nn.sigmoid(gate_logits.astype(jnp.float32)).astype(dt).reshape(B, T, 3, NSA_KV_HEADS, NSA_GROUP, 1)
    o = g[:, :, 0] * o_c + g[:, :, 1] * o_s + g[:, :, 2] * o_w
    return o.reshape(B, T, NSA_HEADS * HEAD_DIM), new_win


def conformer_conv(u, buf, dw, dw_b, ln_g, ln_b):
    a, gte = jnp.split(u, 2, axis=-1)
    y, new_buf = causal_dwconv(a * jax.nn.sigmoid(gte), buf, dw)
    y = layer_norm(y + dw_b.astype(y.dtype), ln_g, ln_b)
    return jax.nn.silu(y), new_buf


def retention(q, k, v, gate, S0, pos0, gn):
    B, T, _ = q.shape
    dt = q.dtype
    H, d = RET_HEADS, HEAD_DIM
    pos = pos0 + jnp.arange(T, dtype=jnp.int32)
    f = lambda a: a.astype(jnp.float32).reshape(B, T, H, d)
    q = rotary(f(q), pos)
    k = rotary(f(k), pos) * d ** -0.5
    v = f(v)
    C = math.gcd(T, RET_CHUNK)
    n = T // C
    lg = jnp.log1p(-jnp.exp2(-5.0 - jnp.arange(H, dtype=jnp.float32)))
    ch = lambda a: a.reshape(B, n, C, H, d).transpose(0, 3, 1, 2, 4)
    qc, kc, vc = ch(q), ch(k), ch(v)
    i = jnp.arange(C, dtype=jnp.float32)
    diff = i[:, None] - i[None, :]
    Dm = jnp.where(diff >= 0, jnp.exp(jnp.maximum(diff, 0.0)[None] * lg[:, None, None]), 0.0)
    att = jnp.einsum('bhncd,bhnsd->bhncs', qc, kc) * Dm[None, :, None]
    o = jnp.einsum('bhncs,bhnse->bhnce', att, vc)
    xi = jnp.exp((i[None, :] + 1.0) * lg[:, None])
    zeta = jnp.exp((C - 1.0 - i[None, :]) * lg[:, None])
    kv = jnp.einsum('bhncd,bhnce->nbhde', kc * zeta[None, :, None, :, None], vc)
    decay = jnp.exp(C * lg)[None, :, None, None]

    def step(S, kvn):
        return S * decay + kvn, S

    S_fin, S_prev = lax.scan(step, S0.astype(jnp.float32), kv)
    o = o + jnp.einsum('bhncd,nbhde->bhnce', qc * xi[None, :, None, :, None], S_prev)
    o = o.transpose(0, 2, 3, 1, 4).reshape(B, T, H, d)
    mu = jnp.mean(o, axis=-1, keepdims=True)
    var = jnp.mean(jnp.square(o - mu), axis=-1, keepdims=True)
    o = ((o - mu) * lax.rsqrt(var + EPS)).reshape(B, T, H * d) * gn.astype(jnp.float32)
    return (o * jax.nn.silu(gate.astype(jnp.float32))).astype(dt), S_fin


def gated_deltanet(qkv, z, a, b, conv_buf, S0, conv_w, A_log, dt_bias, norm_g):
    B, T, _ = qkv.shape
    dt = qkv.dtype
    H, d = GDN_HEADS, HEAD_DIM
    y, new_buf = causal_dwconv(qkv, conv_buf, conv_w)
    y = jax.nn.silu(y.astype(jnp.float32))
    q, k, v = [t.reshape(B, T, H, d) for t in jnp.split(y, 3, axis=-1)]
    l2 = lambda t: t * lax.rsqrt(jnp.sum(t * t, axis=-1, keepdims=True) + EPS)
    q = l2(q) * d ** -0.5
    k = l2(k)
    g = -jnp.exp(A_log.astype(jnp.float32)) * jax.nn.softplus(a.astype(jnp.float32) + dt_bias.astype(jnp.float32))
    beta = jax.nn.sigmoid(b.astype(jnp.float32))
    C = math.gcd(T, GDN_CHUNK)
    n = T // C
    ch = lambda t: t.reshape(B, n, C, H, d).transpose(0, 3, 1, 2, 4)
    chs = lambda t: t.reshape(B, n, C, H).transpose(0, 3, 1, 2)
    qc, kc, vc = ch(q), ch(k), ch(v)
    gc = jnp.cumsum(chs(g), axis=-1)
    bc = chs(beta)
    ii = jnp.arange(C)
    tri = ii[:, None] >= ii[None, :]
    strict = ii[:, None] > ii[None, :]
    Lm = jnp.exp(jnp.where(tri, gc[..., :, None] - gc[..., None, :], -jnp.inf))
    kb = kc * bc[..., None]
    M = jnp.where(strict, jnp.einsum('bhncd,bhnsd->bhncs', kb, kc) * Lm, 0.0)
    A = M + jnp.eye(C, dtype=jnp.float32)
    rhs = jnp.concatenate([vc * bc[..., None], kb * jnp.exp(gc)[..., None]], axis=-1)
    sol = lax.linalg.triangular_solve(A, rhs, left_side=True, lower=True, unit_diagonal=True)
    u, w = sol[..., :d], sol[..., d:]
    att = jnp.einsum('bhncd,bhnsd->bhncs', qc, kc) * Lm
    qd = qc * jnp.exp(gc)[..., None]
    kd = kc * jnp.exp(gc[..., -1:] - gc)[..., None]
    glast = jnp.exp(gc[..., -1])
    xs = tuple(jnp.moveaxis(t, 2, 0) for t in (u, w, att, qd, kd, glast))

    def step(S, xs_n):
        u_, w_, at_, qd_, kd_, gl_ = xs_n
        vn = u_ - jnp.einsum('bhcd,bhde->bhce', w_, S)
        o_ = jnp.einsum('bhcd,bhde->bhce', qd_, S) + jnp.einsum('bhcs,bhse->bhce', at_, vn)
        S = S * gl_[..., None, None] + jnp.einsum('bhcd,bhce->bhde', kd_, vn)
        return S, o_

    S_fin, o = lax.scan(step, S0.astype(jnp.float32), xs)
    o = o.transpose(1, 0, 3, 2, 4).reshape(B, T, H, d)
    o = o * lax.rsqrt(jnp.mean(o * o, axis=-1, keepdims=True) + EPS) * norm_g.astype(jnp.float32)
    o = o * jax.nn.silu(z.astype(jnp.float32).reshape(B, T, H, d))
    return o.reshape(B, T, H * d).astype(dt), new_buf, S_fin


def conv_ffn(h, buf, w_up, dw, w_down):
    gpre, val = jnp.split(h @ w_up, 2, axis=-1)
    gconv, new_buf = causal_dwconv(gpre, buf, dw)
    return (jax.nn.gelu(gconv) * val) @ w_down, new_buf


def trunk_layer(x, c, pos0, nsa_past, win_buf, conv_buf, ret_s, gdn_buf, gdn_s, ffn_buf, p, rel_bias):
    B, T, _ = x.shape
    mod = jax.nn.silu(c) @ p["w_ada"] + p["b_ada"]
    sh1, sc1, gt1, sh2, sc2, gt2 = [m[:, None, :] for m in jnp.split(mod, 6, axis=-1)]
    h = rms_norm(x, p["norms"][0]) * (1.0 + sc1) + sh1
    nq, nkv, ngt, ucv, ret, gdn, mg = split_cols(h @ p["w_in"], IN_SPLITS)
    nkv = nkv.reshape(B, T, 6, NSA_KV_HEADS, HEAD_DIM)
    o_nsa, new_win = nsa_attention(nq.reshape(B, T, NSA_HEADS, HEAD_DIM), nkv, ngt.reshape(B, T, 3, NSA_HEADS),
                                   nsa_past, win_buf, pos0, p["cmp_pool"], p["cmp_pe"], rel_bias)
    o_conv, new_conv = conformer_conv(ucv, conv_buf, p["conv_dw"], p["conv_dw_b"], p["conv_ln_g"], p["conv_ln_b"])
    rq, rk, rv, rg = split_cols(ret, (RET_W, RET_W, RET_W, RET_W))
    o_ret, new_ret = retention(rq, rk, rv, rg, ret_s, pos0, p["ret_gn"])
    gqkv, gz, ga, gb = split_cols(gdn, (3 * GDN_W, GDN_W, GDN_HEADS, GDN_HEADS))
    o_gdn, new_gdn_buf, new_gdn = gated_deltanet(gqkv, gz, ga, gb, gdn_buf, gdn_s, p["gdn_conv_w"],
                                                 p["gdn_A_log"], p["gdn_dt_bias"], p["gdn_norm"])
    branches = jnp.stack([o_nsa, o_conv, o_ret, o_gdn], axis=2)
    merged = jnp.einsum('btnw,nwd->btnd', branches, p["w_branch"])
    gates = jax.nn.sigmoid(mg.reshape(B, T, N_BRANCH, D_MODEL))
    mixed = jnp.einsum('btnd,btnd->btd', gates, merged) @ p["w_out"]
    x = x + gt1 * rms_norm(mixed, p["norms"][1])
    h2 = rms_norm(x, p["norms"][2]) * (1.0 + sc2) + sh2
    f, new_ffn = conv_ffn(h2, ffn_buf, p["ffn_up"], p["ffn_dw"], p["ffn_down"])
    x = x + gt2 * rms_norm(f, p["norms"][3])
    return x, (nkv[:, :, :4], new_win, new_conv, new_ret, new_gdn_buf, new_gdn, new_ffn)


def setup_inputs(seed: int = 0) -> dict:
    key = jax.random.key(seed)
    keys = iter(jax.random.split(key, 48))

    def nrm(shape, scale):
        return scale * jax.random.normal(next(keys), shape, jnp.float32)

    n_pages = PAST_LEN // PAGE_SIZE
    n_used = DEC_BATCH * n_pages
    n_pool = n_used + max(1, n_used // 4)
    page_table = jax.random.permutation(next(keys), n_pool)[:n_used].reshape(DEC_BATCH, n_pages).astype(jnp.int32)
    dt_init = jnp.exp(jax.random.uniform(next(keys), (DEPTH, GDN_HEADS), jnp.float32, math.log(1e-3), math.log(1e-1)))
    return {
        "x_prompt": nrm((BATCH, SEQ, D_MODEL), 1.0),
        "x_sample": nrm((DEC_BATCH, DEC_SEQ, D_MODEL), 1.0),
        "cache_nsa_kv": nrm((n_pool, DEPTH, PAGE_SIZE, 4, NSA_KV_HEADS, HEAD_DIM), 1.0),
        "cache_nsa_win": nrm((DEPTH, DEC_BATCH, min(NSA_WINDOW, PAST_LEN), 2, NSA_KV_HEADS, HEAD_DIM), 1.0),
        "state_conv": nrm((DEPTH, DEC_BATCH, CONV_WIDTH - 1, CONV_CH), 0.5),
        "state_ret": nrm((DEPTH, DEC_BATCH, RET_HEADS, HEAD_DIM, HEAD_DIM), 0.1),
        "state_gdn_conv": nrm((DEPTH, DEC_BATCH, GDN_CONV - 1, 3 * GDN_W), 1.0),
        "state_gdn": nrm((DEPTH, DEC_BATCH, GDN_HEADS, HEAD_DIM, HEAD_DIM), 0.1),
        "state_ffn_conv": nrm((DEPTH, DEC_BATCH, FFN_CONV - 1, D_FF), 1.0),
        "page_table": page_table,
        "c_prompt": nrm((BATCH, D_MODEL), 1.0),
        "c_sample": nrm((DEC_BATCH, D_MODEL), 1.0),
        "w_ada": nrm((DEPTH, D_MODEL, 6 * D_MODEL), 0.5 * D_MODEL ** -0.5),
        "b_ada": nrm((DEPTH, 6 * D_MODEL), 0.01),
        "norms": 1.0 + nrm((DEPTH, 4, D_MODEL), 0.05),
        "w_in": nrm((DEPTH, D_MODEL, IN_COLS), D_MODEL ** -0.5),
        "cmp_pool": (1.0 + nrm((DEPTH, 2, NSA_BLOCK), 0.1)) * NSA_BLOCK ** -0.5,
        "cmp_pe": nrm((DEPTH, 2, NSA_BLOCK, HEAD_DIM), 0.1),
        "rel_bias": nrm((NUM_BUCKETS, NSA_HEADS), 0.5),
        "conv_dw": nrm((DEPTH, CONV_WIDTH, CONV_CH), CONV_WIDTH ** -0.5),
        "conv_dw_b": nrm((DEPTH, CONV_CH), 0.01),
        "conv_ln_g": 1.0 + nrm((DEPTH, CONV_CH), 0.05),
        "conv_ln_b": nrm((DEPTH, CONV_CH), 0.01),
        "ret_gn": 1.0 + nrm((DEPTH, RET_W), 0.05),
        "gdn_conv_w": nrm((DEPTH, GDN_CONV, 3 * GDN_W), GDN_CONV ** -0.5),
        "gdn_A_log": jnp.log(jax.random.uniform(next(keys), (DEPTH, GDN_HEADS), jnp.float32, 1.0, 16.0)),
        "gdn_dt_bias": dt_init + jnp.log(-jnp.expm1(-dt_init)),
        "gdn_norm": 1.0 + nrm((DEPTH, HEAD_DIM), 0.05),
        "w_branch": nrm((DEPTH, N_BRANCH, BRANCH_W, D_MODEL), BRANCH_W ** -0.5),
        "w_out": nrm((DEPTH, D_MODEL, D_MODEL), D_MODEL ** -0.5),
        "ffn_up": nrm((DEPTH, D_MODEL, 2 * D_FF), D_MODEL ** -0.5),
        "ffn_dw": nrm((DEPTH, FFN_CONV, D_FF), FFN_CONV ** -0.5),
        "ffn_down": nrm((DEPTH, D_FF, D_MODEL), D_FF ** -0.5),
    }


def reference(x_prompt, x_sample, cache_nsa_kv, cache_nsa_win, state_conv, state_ret, state_gdn_conv, state_gdn,
              state_ffn_conv, page_table, c_prompt, c_sample, w_ada, b_ada, norms, w_in, cmp_pool, cmp_pe, rel_bias,
              conv_dw, conv_dw_b, conv_ln_g, conv_ln_b, ret_gn, gdn_conv_w, gdn_A_log, gdn_dt_bias, gdn_norm,
              w_branch, w_out, ffn_up, ffn_dw, ffn_down):
    B = x_prompt.shape[0]
    Bd = x_sample.shape[0]
    past = page_table.shape[1] * PAGE_SIZE
    layer_w = {"w_ada": w_ada, "b_ada": b_ada, "norms": norms, "w_in": w_in, "cmp_pool": cmp_pool,
               "cmp_pe": cmp_pe, "conv_dw": conv_dw, "conv_dw_b": conv_dw_b, "conv_ln_g": conv_ln_g,
               "conv_ln_b": conv_ln_b, "ret_gn": ret_gn, "gdn_conv_w": gdn_conv_w, "gdn_A_log": gdn_A_log,
               "gdn_dt_bias": gdn_dt_bias, "gdn_norm": gdn_norm, "w_branch": w_branch, "w_out": w_out,
               "ffn_up": ffn_up, "ffn_dw": ffn_dw, "ffn_down": ffn_down}
    yp, ys = x_prompt, x_sample
    st_p, st_s = [], []
    for l in range(DEPTH):
        p = {name: w[l] for name, w in layer_w.items()}
        yp, sp = trunk_layer(
            yp, c_prompt, 0, None, None,
            jnp.zeros((B, CONV_WIDTH - 1, CONV_CH), x_prompt.dtype),
            jnp.zeros((B, RET_HEADS, HEAD_DIM, HEAD_DIM), jnp.float32),
            jnp.zeros((B, GDN_CONV - 1, 3 * GDN_W), x_prompt.dtype),
            jnp.zeros((B, GDN_HEADS, HEAD_DIM, HEAD_DIM), jnp.float32),
            jnp.zeros((B, FFN_CONV - 1, D_FF), x_prompt.dtype),
            p, rel_bias)
        past_rows = cache_nsa_kv[page_table, l].reshape(Bd, past, 4, NSA_KV_HEADS, HEAD_DIM)
        ys, ss = trunk_layer(
            ys, c_sample, past, past_rows, cache_nsa_win[l], state_conv[l], state_ret[l],
            state_gdn_conv[l], state_gdn[l], state_ffn_conv[l], p, rel_bias)
        st_p.append(sp)
        st_s.append(ss)

    def stack(outs, i, axis):
        return jnp.stack([o[i] for o in outs], axis=axis)

    kv_p, kv_s = stack(st_p, 0, 1), stack(st_s, 0, 1)
    win_p, win_s = stack(st_p, 1, 0), stack(st_s, 1, 0)
    conv_p, conv_s = stack(st_p, 2, 0), stack(st_s, 2, 0)
    ret_p, ret_s = stack(st_p, 3, 0), stack(st_s, 3, 0)
    gdnc_p, gdnc_s = stack(st_p, 4, 0), stack(st_s, 4, 0)
    gdn_p, gdn_s = stack(st_p, 5, 0), stack(st_s, 5, 0)
    ffn_p, ffn_s = stack(st_p, 6, 0), stack(st_s, 6, 0)
    return (yp, ys, kv_p, kv_s, win_p, win_s, conv_p, conv_s, ret_p, ret_s, gdnc_p, gdnc_s, gdn_p, gdn_s, ffn_p, ffn_s)
```

```python
import functools
import math

import jax
import jax.numpy as jnp
import numpy as np
from jax import lax
from jax.experimental import pallas as pl
from jax.experimental.pallas import tpu as pltpu

D_MODEL = 1024
BATCH = 4
SEQ = 4096
DEPTH = 2
DEC_BATCH = 32
DEC_SEQ = 8
PAST_LEN = 16384
PAGE_SIZE = 128

HEAD_DIM = 64
NSA_HEADS = 4
NSA_KV_HEADS = 2
NSA_GROUP = NSA_HEADS // NSA_KV_HEADS
NSA_BLOCK = 64
NSA_TOPK = 16
NSA_WINDOW = 512
Q_BLOCK = 128
NUM_BUCKETS = 32
MAX_DISTANCE = 128
CONV_CH = D_MODEL // 4
CONV_WIDTH = 31
RET_HEADS = 4
RET_W = RET_HEADS * HEAD_DIM
RET_CHUNK = 64
ROPE_BASE = 10000.0
GDN_HEADS = 4
GDN_W = GDN_HEADS * HEAD_DIM
GDN_CONV = 4
GDN_CHUNK = 64
D_FF = 2816
FFN_CONV = 3
N_BRANCH = 4
BRANCH_W = NSA_HEADS * HEAD_DIM
EPS = 1e-6
NEG_INF = -1e30
IN_SPLITS = (NSA_HEADS * HEAD_DIM, 6 * NSA_KV_HEADS * HEAD_DIM, 3 * NSA_HEADS, 2 * CONV_CH, 4 * RET_W,
             4 * GDN_W + 2 * GDN_HEADS, N_BRANCH * D_MODEL)
IN_COLS = sum(IN_SPLITS)

LANES = 128
SUBLANES = 8
VMEM_LIMIT = 56 * 1024 * 1024


def _round_up(a, m):
    return -(-a // m) * m


def _mm_kernel(x_ref, w_ref, o_ref):
    o_ref[...] = jnp.dot(x_ref[...].astype(jnp.bfloat16), w_ref[...].astype(jnp.bfloat16),
                         preferred_element_type=jnp.float32)


def _pmm(x, w):
    M, K = x.shape
    N = w.shape[1]
    Mp = _round_up(M, SUBLANES)
    tm = min(512, Mp)
    Mp = _round_up(Mp, tm)
    tn = 512 if N > 512 else _round_up(N, LANES)
    Np = _round_up(N, tn)
    if Mp != M:
        x = jnp.pad(x, ((0, Mp - M), (0, 0)))
    if Np != N:
        w = jnp.pad(w, ((0, 0), (0, Np - N)))
    out = pl.pallas_call(
        _mm_kernel,
        grid=(Mp // tm, Np // tn),
        in_specs=[pl.BlockSpec((tm, K), lambda i, j: (i, 0)), pl.BlockSpec((K, tn), lambda i, j: (0, j))],
        out_specs=pl.BlockSpec((tm, tn), lambda i, j: (i, j)),
        out_shape=jax.ShapeDtypeStruct((Mp, Np), jnp.float32),
        compiler_params=pltpu.CompilerParams(dimension_semantics=("arbitrary", "arbitrary"),
                                             vmem_limit_bytes=VMEM_LIMIT),
    )(x, w)
    return out[:M, :N]


def split_cols(a, sizes):
    return jnp.split(a, [int(i) for i in np.cumsum(sizes)[:-1]], axis=-1)


def rms_norm(x, g):
    x32 = x.astype(jnp.float32)
    y = x32 * lax.rsqrt(jnp.mean(x32 * x32, axis=-1, keepdims=True) + EPS)
    return (y * g.astype(jnp.float32)).astype(x.dtype)


def layer_norm(x, g, b):
    x32 = x.astype(jnp.float32)
    mu = jnp.mean(x32, axis=-1, keepdims=True)
    var = jnp.mean(jnp.square(x32 - mu), axis=-1, keepdims=True)
    return ((x32 - mu) * lax.rsqrt(var + EPS) * g.astype(jnp.float32) + b.astype(jnp.float32)).astype(x.dtype)


def causal_dwconv(x, buf, w):
    xp = jnp.concatenate([buf.astype(x.dtype), x], axis=1)
    y = lax.conv_general_dilated(xp, w[:, None, :].astype(x.dtype), window_strides=(1,), padding='VALID',
                                 dimension_numbers=('NWC', 'WIO', 'NWC'), feature_group_count=x.shape[-1])
    return y, xp[:, xp.shape[1] - (w.shape[0] - 1):]


def t5_bucket(dist):
    n = jnp.maximum(dist, 0)
    exact = NUM_BUCKETS // 2
    large = exact + (jnp.log(jnp.maximum(n, 1).astype(jnp.float32) / exact) / math.log(MAX_DISTANCE / exact)
                     * (NUM_BUCKETS - exact)).astype(jnp.int32)
    return jnp.where(n < exact, n, jnp.minimum(large, NUM_BUCKETS - 1))


def masked_softmax(s, mask):
    return jax.nn.softmax(jnp.where(mask, s, NEG_INF), axis=-1) * mask


def over_query_blocks(fn, args, q_axes, T):
    if T <= Q_BLOCK or T % Q_BLOCK:
        return fn(args)
    nq = T // Q_BLOCK

    def split(a, ax):
        return jnp.moveaxis(a.reshape(a.shape[:ax] + (nq, Q_BLOCK) + a.shape[ax + 1:]), ax, 0)

    out = lax.map(fn, tuple(split(a, ax) for a, ax in zip(args, q_axes)))
    out = jnp.moveaxis(out, 0, 1)
    return out.reshape((out.shape[0], T) + out.shape[3:])


def rotary(x, pos):
    half = x.shape[-1] // 2
    inv = ROPE_BASE ** (-jnp.arange(half, dtype=jnp.float32) / half)
    ang = pos.astype(jnp.float32)[:, None] * inv[None, :]
    cos, sin = jnp.cos(ang)[None, :, None, :], jnp.sin(ang)[None, :, None, :]
    x1, x2 = x[..., :half], x[..., half:]
    return jnp.concatenate([x1 * cos - x2 * sin, x1 * sin + x2 * cos], axis=-1)


def nsa_attention(q, kv_new, gate_logits, past_rows, win_buf, pos0, cmp_pool, cmp_pe, rel_bias):
    B, T = q.shape[:2]
    dt = q.dtype
    qpos = pos0 + jnp.arange(T, dtype=jnp.int32)
    qg = (q * HEAD_DIM ** -0.5).reshape(B, T, NSA_KV_HEADS, NSA_GROUP, HEAD_DIM)
    rb = rel_bias.astype(jnp.float32).reshape(NUM_BUCKETS, NSA_KV_HEADS, NSA_GROUP)
    L = T if past_rows is None else past_rows.shape[1] + T
    nb = -(-L // NSA_BLOCK)
    new_rows = jnp.pad(kv_new[:, :, :4], ((0, 0), (0, nb * NSA_BLOCK - L), (0, 0), (0, 0), (0, 0)))
    full = new_rows if past_rows is None else jnp.concatenate([past_rows.astype(dt), new_rows], axis=1)
    blocks = full.reshape(B, nb, NSA_BLOCK, 4, NSA_KV_HEADS, HEAD_DIM)
    pe = jnp.transpose(cmp_pe, (1, 0, 2))[:, :, None, :].astype(dt)
    cmp = jnp.einsum('bnjckd,cj->bnckd', blocks[:, :, :, :2] + pe, cmp_pool.astype(dt))
    k_c, v_c = cmp[:, :, 0], cmp[:, :, 1]
    blk = jnp.arange(nb, dtype=jnp.int32)
    d_c = qpos[:, None] - (blk * NSA_BLOCK + NSA_BLOCK - 1)[None, :]
    s_c = jnp.einsum('btkgd,bnkd->bkgtn', qg, k_c).astype(jnp.float32) + jnp.transpose(rb[t5_bucket(d_c)], (2, 3, 0, 1))
    p_c = masked_softmax(s_c, d_c >= 0)
    o_c = jnp.einsum('bkgtn,bnkd->btkgd', p_c.astype(dt), v_c)
    cur = (qpos // NSA_BLOCK)[:, None]
    forced = (blk[None] == 0) | (blk[None] == cur) | (blk[None] == cur - 1)
    score = jnp.where(blk[None] <= cur, jnp.where(forced, 2.0, p_c.sum(axis=2)), -1.0)
    top_s, idx = lax.top_k(score, min(NSA_TOPK, nb))
    ok = top_s > -0.5
    ks = jnp.transpose(blocks[:, :, :, 2], (0, 3, 1, 2, 4))
    vs = jnp.transpose(blocks[:, :, :, 3], (0, 3, 1, 2, 4))
    take = jax.vmap(jax.vmap(lambda a, i: a[i]))
    kv_ix = jnp.arange(NSA_KV_HEADS)[None, :, None, None, None]
    rbk = jnp.transpose(rb, (1, 0, 2))

    def sel_block(args):
        qb, ib, okb, pb = args
        Bq, Q = qb.shape[:2]
        kg, vg = take(ks, ib), take(vs, ib)
        kpos = ib[..., None] * NSA_BLOCK + jnp.arange(NSA_BLOCK, dtype=jnp.int32)
        dist = pb[None, None, :, None, None] - kpos
        mask = (okb[..., None] & (dist >= 0)).reshape(Bq, NSA_KV_HEADS, 1, Q, -1)
        bias = jnp.moveaxis(rbk[kv_ix, t5_bucket(dist)], -1, 2)
        s = jnp.einsum('bqkgd,bkqsjd->bkgqsj', qb, kg).astype(jnp.float32) + bias
        p = masked_softmax(s.reshape(Bq, NSA_KV_HEADS, NSA_GROUP, Q, -1), mask)
        return jnp.einsum('bkgqn,bkqnd->bqkgd', p.astype(dt), vg.reshape(Bq, NSA_KV_HEADS, Q, -1, HEAD_DIM))

    o_s = over_query_blocks(sel_block, (qg, idx, ok, qpos), (1, 2, 2, 0), T)
    kw = kv_new[:, :, 4:]
    if win_buf is None:
        real = kw
        k_all = jnp.pad(kw, ((0, 0), (NSA_WINDOW, 0), (0, 0), (0, 0), (0, 0)))
        span = NSA_WINDOW
    else:
        real = jnp.concatenate([win_buf.astype(dt), kw], axis=1)
        k_all = real
        span = win_buf.shape[1]
    k0 = pos0 - span

    def win_block(args):
        qb, pb = args
        Q = qb.shape[1]
        start = pb[0] - pos0
        kb = lax.dynamic_slice_in_dim(k_all, start, span + Q, axis=1)
        kpos = k0 + start + jnp.arange(span + Q, dtype=jnp.int32)
        dist = pb[:, None] - kpos[None, :]
        mask = (dist >= 0) & (dist < NSA_WINDOW) & (kpos >= 0)[None, :]
        s = jnp.einsum('bqkgd,bnkd->bkgqn', qb, kb[:, :, 0]).astype(jnp.float32) + jnp.transpose(rb[t5_bucket(dist)], (2, 3, 0, 1))
        p = masked_softmax(s, mask)
        return jnp.einsum('bkgqn,bnkd->bqkgd', p.astype(dt), kb[:, :, 1])

    o_w = over_query_blocks(win_block, (qg, qpos), (1, 0), T)
    new_win = real[:, real.shape[1] - min(NSA_WINDOW, real.shape[1]):]
    g = jax.nn.sigmoid(gate_logits.astype(jnp.float32)).astype(dt).reshape(B, T, 3, NSA_KV_HEADS, NSA_GROUP, 1)
    o = g[:, :, 0] * o_c + g[:, :, 1] * o_s + g[:, :, 2] * o_w
    return o.reshape(B, T, NSA_HEADS * HEAD_DIM), new_win


def conformer_conv(u, buf, dw, dw_b, ln_g, ln_b):
    a, gte = jnp.split(u, 2, axis=-1)
    y, new_buf = causal_dwconv(a * jax.nn.sigmoid(gte), buf, dw)
    y = layer_norm(y + dw_b.astype(y.dtype), ln_g, ln_b)
    return jax.nn.silu(y), new_buf


def retention(q, k, v, gate, S0, pos0, gn):
    B, T, _ = q.shape
    dt = q.dtype
    H, d = RET_HEADS, HEAD_DIM
    pos = pos0 + jnp.arange(T, dtype=jnp.int32)
    f = lambda a: a.astype(jnp.float32).reshape(B, T, H, d)
    q = rotary(f(q), pos)
    k = rotary(f(k), pos) * d ** -0.5
    v = f(v)
    C = math.gcd(T, RET_CHUNK)
    n = T // C
    lg = jnp.log1p(-jnp.exp2(-5.0 - jnp.arange(H, dtype=jnp.float32)))
    ch = lambda a: a.reshape(B, n, C, H, d).transpose(0, 3, 1, 2, 4)
    qc, kc, vc = ch(q), ch(k), ch(v)
    i = jnp.arange(C, dtype=jnp.float32)
    diff = i[:, None] - i[None, :]
    Dm = jnp.where(diff >= 0, jnp.exp(jnp.maximum(diff, 0.0)[None] * lg[:, None, None]), 0.0)
    att = jnp.einsum('bhncd,bhnsd->bhncs', qc, kc) * Dm[None, :, None]
    o = jnp.einsum('bhncs,bhnse->bhnce', att, vc)
    xi = jnp.exp((i[None, :] + 1.0) * lg[:, None])
    zeta = jnp.exp((C - 1.0 - i[None, :]) * lg[:, None])
    kv = jnp.einsum('bhncd,bhnce->nbhde', kc * zeta[None, :, None, :, None], vc)
    decay = jnp.exp(C * lg)[None, :, None, None]

    def step(S, kvn):
        return S * decay + kvn, S

    S_fin, S_prev = lax.scan(step, S0.astype(jnp.float32), kv)
    o = o + jnp.einsum('bhncd,nbhde->bhnce', qc * xi[None, :, None, :, None], S_prev)
    o = o.transpose(0, 2, 3, 1, 4).reshape(B, T, H, d)
    mu = jnp.mean(o, axis=-1, keepdims=True)
    var = jnp.mean(jnp.square(o - mu), axis=-1, keepdims=True)
    o = ((o - mu) * lax.rsqrt(var + EPS)).reshape(B, T, H * d) * gn.astype(jnp.float32)
    return (o * jax.nn.silu(gate.astype(jnp.float32))).astype(dt), S_fin


def gated_deltanet(qkv, z, a, b, conv_buf, S0, conv_w, A_log, dt_bias, norm_g):
    B, T, _ = qkv.shape
    dt = qkv.dtype
    H, d = GDN_HEADS, HEAD_DIM
    y, new_buf = causal_dwconv(qkv, conv_buf, conv_w)
    y = jax.nn.silu(y.astype(jnp.float32))
    q, k, v = [t.reshape(B, T, H, d) for t in jnp.split(y, 3, axis=-1)]
    l2 = lambda t: t * lax.rsqrt(jnp.sum(t * t, axis=-1, keepdims=True) + EPS)
    q = l2(q) * d ** -0.5
    k = l2(k)
    g = -jnp.exp(A_log.astype(jnp.float32)) * jax.nn.softplus(a.astype(jnp.float32) + dt_bias.astype(jnp.float32))
    beta = jax.nn.sigmoid(b.astype(jnp.float32))
    C = math.gcd(T, GDN_CHUNK)
    n = T // C
    ch = lambda t: t.reshape(B, n, C, H, d).transpose(0, 3, 1, 2, 4)
    chs = lambda t: t.reshape(B, n, C, H).transpose(0, 3, 1, 2)
    qc, kc, vc = ch(q), ch(k), ch(v)
    gc = jnp.cumsum(chs(g), axis=-1)
    bc = chs(beta)
    ii = jnp.arange(C)
    tri = ii[:, None] >= ii[None, :]
    strict = ii[:, None] > ii[None, :]
    Lm = jnp.exp(jnp.where(tri, gc[..., :, None] - gc[..., None, :], -jnp.inf))
    kb = kc * bc[..., None]
    M = jnp.where(strict, jnp.einsum('bhncd,bhnsd->bhncs', kb, kc) * Lm, 0.0)
    A = M + jnp.eye(C, dtype=jnp.float32)
    rhs = jnp.concatenate([vc * bc[..., None], kb * jnp.exp(gc)[..., None]], axis=-1)
    sol = lax.linalg.triangular_solve(A, rhs, left_side=True, lower=True, unit_diagonal=True)
    u, w = sol[..., :d], sol[..., d:]
    att = jnp.einsum('bhncd,bhnsd->bhncs', qc, kc) * Lm
    qd = qc * jnp.exp(gc)[..., None]
    kd = kc * jnp.exp(gc[..., -1:] - gc)[..., None]
    glast = jnp.exp(gc[..., -1])
    xs = tuple(jnp.moveaxis(t, 2, 0) for t in (u, w, att, qd, kd, glast))

    def step(S, xs_n):
        u_, w_, at_, qd_, kd_, gl_ = xs_n
        vn = u_ - jnp.einsum('bhcd,bhde->bhce', w_, S)
        o_ = jnp.einsum('bhcd,bhde->bhce', qd_, S) + jnp.einsum('bhcs,bhse->bhce', at_, vn)
        S = S * gl_[..., None, None] + jnp.einsum('bhcd,bhce->bhde', kd_, vn)
        return S, o_

    S_fin, o = lax.scan(step, S0.astype(jnp.float32), xs)
    o = o.transpose(1, 0, 3, 2, 4).reshape(B, T, H, d)
    o = o * lax.rsqrt(jnp.mean(o * o, axis=-1, keepdims=True) + EPS) * norm_g.astype(jnp.float32)
    o = o * jax.nn.silu(z.astype(jnp.float32).reshape(B, T, H, d))
    return o.reshape(B, T, H * d).astype(dt), new_buf, S_fin


def conv_ffn(h, buf, w_up, dw, w_down):
    Bh, Th, _ = h.shape
    gpre, val = jnp.split(_pmm(h.reshape(Bh * Th, D_MODEL), w_up).reshape(Bh, Th, 2 * D_FF), 2, axis=-1)
    gconv, new_buf = causal_dwconv(gpre, buf, dw)
    return _pmm((jax.nn.gelu(gconv) * val).reshape(Bh * Th, D_FF), w_down).reshape(Bh, Th, D_MODEL), new_buf


def trunk_layer(x, c, pos0, nsa_past, win_buf, conv_buf, ret_s, gdn_buf, gdn_s, ffn_buf, p, rel_bias):
    B, T, _ = x.shape
    mod = _pmm(jax.nn.silu(c), p["w_ada"]) + p["b_ada"]
    sh1, sc1, gt1, sh2, sc2, gt2 = [m[:, None, :] for m in jnp.split(mod, 6, axis=-1)]
    h = rms_norm(x, p["norms"][0]) * (1.0 + sc1) + sh1
    nq, nkv, ngt, ucv, ret, gdn, mg = split_cols(
        _pmm(h.reshape(B * T, D_MODEL), p["w_in"]).reshape(B, T, IN_COLS), IN_SPLITS)
    nkv = nkv.reshape(B, T, 6, NSA_KV_HEADS, HEAD_DIM)
    o_nsa, new_win = nsa_attention(nq.reshape(B, T, NSA_HEADS, HEAD_DIM), nkv, ngt.reshape(B, T, 3, NSA_HEADS),
                                   nsa_past, win_buf, pos0, p["cmp_pool"], p["cmp_pe"], rel_bias)
    o_conv, new_conv = conformer_conv(ucv, conv_buf, p["conv_dw"], p["conv_dw_b"], p["conv_ln_g"], p["conv_ln_b"])
    rq, rk, rv, rg = split_cols(ret, (RET_W, RET_W, RET_W, RET_W))
    o_ret, new_ret = retention(rq, rk, rv, rg, ret_s, pos0, p["ret_gn"])
    gqkv, gz, ga, gb = split_cols(gdn, (3 * GDN_W, GDN_W, GDN_HEADS, GDN_HEADS))
    o_gdn, new_gdn_buf, new_gdn = gated_deltanet(gqkv, gz, ga, gb, gdn_buf, gdn_s, p["gdn_conv_w"],
                                                 p["gdn_A_log"], p["gdn_dt_bias"], p["gdn_norm"])
    branches = jnp.stack([o_nsa, o_conv, o_ret, o_gdn], axis=2)
    merged = jnp.stack([_pmm(branches[:, :, n].reshape(B * T, BRANCH_W), p["w_branch"][n]).reshape(B, T, D_MODEL)
                        for n in range(N_BRANCH)], axis=2)
    gates = jax.nn.sigmoid(mg.reshape(B, T, N_BRANCH, D_MODEL))
    mixed = _pmm(jnp.einsum('btnd,btnd->btd', gates, merged).reshape(B * T, D_MODEL),
                 p["w_out"]).reshape(B, T, D_MODEL)
    x = x + gt1 * rms_norm(mixed, p["norms"][1])
    h2 = rms_norm(x, p["norms"][2]) * (1.0 + sc2) + sh2
    f, new_ffn = conv_ffn(h2, ffn_buf, p["ffn_up"], p["ffn_dw"], p["ffn_down"])
    x = x + gt2 * rms_norm(f, p["norms"][3])
    return x, (nkv[:, :, :4], new_win, new_conv, new_ret, new_gdn_buf, new_gdn, new_ffn)


def kernel(x_prompt, x_sample, cache_nsa_kv, cache_nsa_win, state_conv, state_ret, state_gdn_conv, state_gdn,
           state_ffn_conv, page_table, c_prompt, c_sample, w_ada, b_ada, norms, w_in, cmp_pool, cmp_pe, rel_bias,
           conv_dw, conv_dw_b, conv_ln_g, conv_ln_b, ret_gn, gdn_conv_w, gdn_A_log, gdn_dt_bias, gdn_norm,
           w_branch, w_out, ffn_up, ffn_dw, ffn_down):
    B = x_prompt.shape[0]
    Bd = x_sample.shape[0]
    past = page_table.shape[1] * PAGE_SIZE
    layer_w = {"w_ada": w_ada, "b_ada": b_ada, "norms": norms, "w_in": w_in, "cmp_pool": cmp_pool,
               "cmp_pe": cmp_pe, "conv_dw": conv_dw, "conv_dw_b": conv_dw_b, "conv_ln_g": conv_ln_g,
               "conv_ln_b": conv_ln_b, "ret_gn": ret_gn, "gdn_conv_w": gdn_conv_w, "gdn_A_log": gdn_A_log,
               "gdn_dt_bias": gdn_dt_bias, "gdn_norm": gdn_norm, "w_branch": w_branch, "w_out": w_out,
               "ffn_up": ffn_up, "ffn_dw": ffn_dw, "ffn_down": ffn_down}
    yp, ys = x_prompt, x_sample
    st_p, st_s = [], []
    for l in range(DEPTH):
        p = {name: w[l] for name, w in layer_w.items()}
        yp, sp = trunk_layer(
            yp, c_prompt, 0, None, None,
            jnp.zeros((B, CONV_WIDTH - 1, CONV_CH), x_prompt.dtype),
            jnp.zeros((B, RET_HEADS, HEAD_DIM, HEAD_DIM), jnp.float32),
            jnp.zeros((B, GDN_CONV - 1, 3 * GDN_W), x_prompt.dtype),
            jnp.zeros((B, GDN_HEADS, HEAD_DIM, HEAD_DIM), jnp.float32),
            jnp.zeros((B, FFN_CONV - 1, D_FF), x_prompt.dtype),
            p, rel_bias)
        past_rows = cache_nsa_kv[page_table, l].reshape(Bd, past, 4, NSA_KV_HEADS, HEAD_DIM)
        ys, ss = trunk_layer(
            ys, c_sample, past, past_rows, cache_nsa_win[l], state_conv[l], state_ret[l],
            state_gdn_conv[l], state_gdn[l], state_ffn_conv[l], p, rel_bias)
        st_p.append(sp)
        st_s.append(ss)

    def stack(outs, i, axis):
        return jnp.stack([o[i] for o in outs], axis=axis)

    kv_p, kv_s = stack(st_p, 0, 1), stack(st_s, 0, 1)
    win_p, win_s = stack(st_p, 1, 0), stack(st_s, 1, 0)
    conv_p, conv_s = stack(st_p, 2, 0), stack(st_s, 2, 0)
    ret_p, ret_s = stack(st_p, 3, 0), stack(st_s, 3, 0)
    gdnc_p, gdnc_s = stack(st_p, 4, 0), stack(st_s, 4, 0)
    gdn_p, gdn_s = stack(st_p, 5, 0), stack(st_s, 5, 0)
    ffn_p, ffn_s = stack(st_p, 6, 0), stack(st_s, 6, 0)
    return (yp, ys, kv_p, kv_s, win_p, win_s, conv_p, conv_s, ret_p, ret_s, gdnc_p, gdnc_s, gdn_p, gdn_s, ffn_p, ffn_s)
```

```python
import functools
import math

import jax
import jax.numpy as jnp
import numpy as np
from jax import lax
from jax.experimental import pallas as pl
from jax.experimental.pallas import tpu as pltpu

D_MODEL = 1024
BATCH = 4
SEQ = 4096
DEPTH = 2
DEC_BATCH = 32
DEC_SEQ = 8
PAST_LEN = 16384
PAGE_SIZE = 128

HEAD_DIM = 64
NSA_HEADS = 4
NSA_KV_HEADS = 2
NSA_GROUP = NSA_HEADS // NSA_KV_HEADS
NSA_BLOCK = 64
NSA_TOPK = 16
NSA_WINDOW = 512
Q_BLOCK = 128
NUM_BUCKETS = 32
MAX_DISTANCE = 128
CONV_CH = D_MODEL // 4
CONV_WIDTH = 31
RET_HEADS = 4
RET_W = RET_HEADS * HEAD_DIM
RET_CHUNK = 64
ROPE_BASE = 10000.0
GDN_HEADS = 4
GDN_W = GDN_HEADS * HEAD_DIM
GDN_CONV = 4
GDN_CHUNK = 64
D_FF = 2816
FFN_CONV = 3
N_BRANCH = 4
BRANCH_W = NSA_HEADS * HEAD_DIM
EPS = 1e-6
NEG_INF = -1e30
IN_SPLITS = (NSA_HEADS * HEAD_DIM, 6 * NSA_KV_HEADS * HEAD_DIM, 3 * NSA_HEADS, 2 * CONV_CH, 4 * RET_W,
             4 * GDN_W + 2 * GDN_HEADS, N_BRANCH * D_MODEL)
IN_COLS = sum(IN_SPLITS)

LANES = 128
SUBLANES = 8
VMEM_LIMIT = 56 * 1024 * 1024


def _round_up(a, m):
    return -(-a // m) * m


def _mm_kernel(x_ref, w_ref, o_ref):
    o_ref[...] = jnp.dot(x_ref[...].astype(jnp.bfloat16), w_ref[...].astype(jnp.bfloat16),
                         preferred_element_type=jnp.float32)


def _pmm(x, w):
    M, K = x.shape
    N = w.shape[1]
    Mp = _round_up(M, SUBLANES)
    tm = min(512, Mp)
    Mp = _round_up(Mp, tm)
    tn = 512 if N > 512 else _round_up(N, LANES)
    Np = _round_up(N, tn)
    if Mp != M:
        x = jnp.pad(x, ((0, Mp - M), (0, 0)))
    if Np != N:
        w = jnp.pad(w, ((0, 0), (0, Np - N)))
    out = pl.pallas_call(
        _mm_kernel,
        grid=(Mp // tm, Np // tn),
        in_specs=[pl.BlockSpec((tm, K), lambda i, j: (i, 0)), pl.BlockSpec((K, tn), lambda i, j: (0, j))],
        out_specs=pl.BlockSpec((tm, tn), lambda i, j: (i, j)),
        out_shape=jax.ShapeDtypeStruct((Mp, Np), jnp.float32),
        compiler_params=pltpu.CompilerParams(dimension_semantics=("arbitrary", "arbitrary"),
                                             vmem_limit_bytes=VMEM_LIMIT),
    )(x, w)
    return out[:M, :N]


def split_cols(a, sizes):
    return jnp.split(a, [int(i) for i in np.cumsum(sizes)[:-1]], axis=-1)


def rms_norm(x, g):
    x32 = x.astype(jnp.float32)
    y = x32 * lax.rsqrt(jnp.mean(x32 * x32, axis=-1, keepdims=True) + EPS)
    return (y * g.astype(jnp.float32)).astype(x.dtype)


def layer_norm(x, g, b):
    x32 = x.astype(jnp.float32)
    mu = jnp.mean(x32, axis=-1, keepdims=True)
    var = jnp.mean(jnp.square(x32 - mu), axis=-1, keepdims=True)
    return ((x32 - mu) * lax.rsqrt(var + EPS) * g.astype(jnp.float32) + b.astype(jnp.float32)).astype(x.dtype)


def causal_dwconv(x, buf, w):
    xp = jnp.concatenate([buf.astype(x.dtype), x], axis=1)
    y = lax.conv_general_dilated(xp, w[:, None, :].astype(x.dtype), window_strides=(1,), padding='VALID',
                                 dimension_numbers=('NWC', 'WIO', 'NWC'), feature_group_count=x.shape[-1])
    return y, xp[:, xp.shape[1] - (w.shape[0] - 1):]


def t5_bucket(dist):
    n = jnp.maximum(dist, 0)
    exact = NUM_BUCKETS // 2
    large = exact + (jnp.log(jnp.maximum(n, 1).astype(jnp.float32) / exact) / math.log(MAX_DISTANCE / exact)
                     * (NUM_BUCKETS - exact)).astype(jnp.int32)
    return jnp.where(n < exact, n, jnp.minimum(large, NUM_BUCKETS - 1))


def masked_softmax(s, mask):
    return jax.nn.softmax(jnp.where(mask, s, NEG_INF), axis=-1) * mask


def over_query_blocks(fn, args, q_axes, T):
    if T <= Q_BLOCK or T % Q_BLOCK:
        return fn(args)
    nq = T // Q_BLOCK

    def split(a, ax):
        return jnp.moveaxis(a.reshape(a.shape[:ax] + (nq, Q_BLOCK) + a.shape[ax + 1:]), ax, 0)

    out = lax.map(fn, tuple(split(a, ax) for a, ax in zip(args, q_axes)))
    out = jnp.moveaxis(out, 0, 1)
    return out.reshape((out.shape[0], T) + out.shape[3:])


def rotary(x, pos):
    half = x.shape[-1] // 2
    inv = ROPE_BASE ** (-jnp.arange(half, dtype=jnp.float32) / half)
    ang = pos.astype(jnp.float32)[:, None] * inv[None, :]
    cos, sin = jnp.cos(ang)[None, :, None, :], jnp.sin(ang)[None, :, None, :]
    x1, x2 = x[..., :half], x[..., half:]
    return jnp.concatenate([x1 * cos - x2 * sin, x1 * sin + x2 * cos], axis=-1)


NSA_TQ = 128
BLOCKS_PER_TILE = NSA_TQ // NSA_BLOCK
WIN_TILES = NSA_WINDOW // NSA_TQ


def _t5_thresholds():
    n = np.arange(0, 2 * MAX_DISTANCE)
    exact = NUM_BUCKETS // 2
    large = exact + (np.log(np.maximum(n, 1).astype(np.float32) / np.float32(exact))
                     / np.float32(math.log(MAX_DISTANCE / exact)) * (NUM_BUCKETS - exact)).astype(np.int32)
    bucket = np.where(n < exact, n, np.minimum(large, NUM_BUCKETS - 1))
    return tuple(int(np.argmax(bucket >= k)) for k in range(1, NUM_BUCKETS))


_T5_THR = _t5_thresholds()


def _bias_from_dist(dist, rb_ref, h):
    v = jnp.full(dist.shape, rb_ref[NUM_BUCKETS - 1, h], jnp.float32)
    for k in range(NUM_BUCKETS - 2, -1, -1):
        v = jnp.where(dist < _T5_THR[k], rb_ref[k, h], v)
    return v


def _dot_nt(a, b):
    return lax.dot_general(a, b, (((1,), (1,)), ((), ())), preferred_element_type=jnp.float32)


def _flash_tile(carry, k_t, vt_t, qs, bias, keep):
    m, l, acc = carry
    s = _dot_nt(k_t, qs) + bias
    if keep is not None:
        s = jnp.where(keep, s, NEG_INF)
    m_new = jnp.maximum(m, jnp.max(s, axis=0, keepdims=True))
    p = jnp.exp(s - m_new)
    alpha = jnp.exp(m - m_new)
    l = alpha * l + jnp.sum(p, axis=0, keepdims=True)
    acc = alpha * acc + jnp.dot(vt_t, p.astype(jnp.bfloat16), preferred_element_type=jnp.float32)
    return m_new, l, acc


def _nsa_prompt_kernel(rb_ref, q_ref, kv_ref, gt_ref, poolt_ref, pe_ref, o_ref,
                       kc_ref, vc_ref, ksel_ref, vselt_ref, kwin_ref, vwint_ref, tbl_ref, score_ref, sel_ref):
    b = pl.program_id(0)
    qi = pl.program_id(1)
    T = kv_ref.shape[1]
    nb = T // NSA_BLOCK
    topk = min(NSA_TOPK, nb)
    TQ = NSA_TQ
    CH = 512
    f32, bf16 = jnp.float32, jnp.bfloat16

    @pl.when((b == 0) & (qi == 0))
    def _tables():
        jj = lax.broadcasted_iota(jnp.int32, (TQ, TQ), 0)
        tt = lax.broadcasted_iota(jnp.int32, (TQ, TQ), 1)
        d0 = tt - jj
        for h in range(NSA_HEADS):
            kvg, g = divmod(h, NSA_GROUP)
            lanes = slice(g * TQ, (g + 1) * TQ)
            far = rb_ref[NUM_BUCKETS - 1, h]
            tbl_ref[kvg, 0, :, lanes] = jnp.where(d0 >= 0, _bias_from_dist(d0, rb_ref, h), NEG_INF)
            tbl_ref[kvg, 1, :, lanes] = _bias_from_dist(d0 + TQ, rb_ref, h)
            tbl_ref[kvg, 2, :, lanes] = jnp.where(d0 < 0, far, NEG_INF)

    @pl.when(qi == 0)
    def _prologue():
        def chunk(i, carry):
            r = pl.multiple_of(i * CH, CH)
            rb8 = pl.multiple_of(i * (CH // NSA_BLOCK), CH // NSA_BLOCK)
            for kvg in range(NSA_KV_HEADS):
                def col(c):
                    lo = c * NSA_KV_HEADS * HEAD_DIM + kvg * HEAD_DIM
                    return kv_ref[0, pl.ds(r, CH), lo:lo + HEAD_DIM]
                for c, dst in ((0, kc_ref), (1, vc_ref)):
                    x = col(c).reshape(CH // NSA_BLOCK, NSA_BLOCK, HEAD_DIM) + pe_ref[c][None]
                    dst[kvg, pl.ds(rb8, CH // NSA_BLOCK), :] = jnp.sum(x * poolt_ref[:, c:c + 1][None], axis=1)
                ksel_ref[kvg, pl.ds(r, CH), :] = col(2).astype(bf16)
                vselt_ref[kvg, :, pl.ds(r, CH)] = col(3).T.astype(bf16)
                kwin_ref[kvg, pl.ds(r, CH), :] = col(4).astype(bf16)
                vwint_ref[kvg, :, pl.ds(r, CH)] = col(5).T.astype(bf16)
            return carry
        lax.fori_loop(0, T // CH, chunk, 0)

    q = q_ref[0]
    gates = jax.nn.sigmoid(gt_ref[0])
    n_io = lax.broadcasted_iota(jnp.int32, (nb, TQ), 0)
    t_io = lax.broadcasted_iota(jnp.int32, (nb, TQ), 1)
    lane2 = lax.broadcasted_iota(jnp.int32, (1, 2 * TQ), 1)
    dist_c = qi * TQ + t_io - (n_io * NSA_BLOCK + NSA_BLOCK - 1)
    vis_c = dist_c >= 0
    vis_c2 = jnp.concatenate([vis_c, vis_c], axis=1)
    cur = (qi * TQ + t_io) // NSA_BLOCK
    forced = (n_io == 0) | (n_io == cur) | (n_io == cur - 1)
    q0 = pl.multiple_of(qi * TQ, TQ)
    outs = []
    for kvg in range(NSA_KV_HEADS):
        base = kvg * NSA_GROUP * HEAD_DIM
        qs = jnp.concatenate([q[:, base + g * HEAD_DIM: base + (g + 1) * HEAD_DIM] for g in range(NSA_GROUP)],
                             axis=0)
        qs = (qs * HEAD_DIM ** -0.5).astype(bf16)
        far_row = jnp.where(lane2 < TQ, rb_ref[NUM_BUCKETS - 1, kvg * NSA_GROUP],
                            rb_ref[NUM_BUCKETS - 1, kvg * NSA_GROUP + 1])

        sc = _dot_nt(kc_ref[kvg].astype(bf16), qs)
        bias_c = jnp.concatenate([_bias_from_dist(dist_c, rb_ref, kvg * NSA_GROUP + g) for g in range(NSA_GROUP)],
                                 axis=1)
        sc = jnp.where(vis_c2, sc + bias_c, NEG_INF)
        e = jnp.exp(sc - jnp.max(sc, axis=0, keepdims=True))
        p_c = e / jnp.sum(e, axis=0, keepdims=True) * vis_c2.astype(f32)
        oc = jnp.dot(vc_ref[kvg].T.astype(bf16), p_c.astype(bf16), preferred_element_type=f32)

        score = jnp.where(n_io <= cur, jnp.where(forced, 2.0, p_c[:, :TQ] + p_c[:, TQ:]), -1.0)
        score_ref[kvg] = score

        def rank_body(mi, rank):
            row = score_ref[kvg, pl.ds(mi, 1), :]
            beats = (row > score) | ((row == score) & (mi < n_io))
            return rank + beats.astype(jnp.int32)

        rank = lax.fori_loop(0, BLOCKS_PER_TILE * (qi + 1), rank_body, jnp.zeros((nb, TQ), jnp.int32))
        sel_ref[kvg] = ((rank < topk) & (n_io <= cur)).astype(f32)

        def sel_keep(j):
            rows = [jnp.broadcast_to(sel_ref[kvg, pl.ds(BLOCKS_PER_TILE * j + u, 1), :], (NSA_BLOCK, TQ))
                    for u in range(BLOCKS_PER_TILE)]
            mm = jnp.concatenate(rows, axis=0)
            return jnp.concatenate([mm, mm], axis=1) > 0.5

        def tile(kref, vtref, j):
            r = pl.multiple_of(j * TQ, TQ)
            return kref[kvg, pl.ds(r, TQ), :], vtref[kvg, :, pl.ds(r, TQ)]

        init = (jnp.full((1, 2 * TQ), NEG_INF, f32), jnp.zeros((1, 2 * TQ), f32), jnp.zeros((HEAD_DIM, 2 * TQ), f32))

        k_t, vt_t = tile(ksel_ref, vselt_ref, qi)
        carry = _flash_tile(init, k_t, vt_t, qs, tbl_ref[kvg, 0], sel_keep(qi))
        jp = jnp.maximum(qi - 1, 0)
        k_t, vt_t = tile(ksel_ref, vselt_ref, jp)
        carry = _flash_tile(carry, k_t, vt_t, qs, tbl_ref[kvg, 1], sel_keep(jp) & (qi >= 1))

        def far_body(i, c):
            j = qi - 2 - i
            k_f, vt_f = tile(ksel_ref, vselt_ref, j)
            return _flash_tile(c, k_f, vt_f, qs, far_row, sel_keep(j))

        m_s, l_s, acc_s = lax.fori_loop(0, jnp.maximum(qi - 1, 0), far_body, carry)

        k_t, vt_t = tile(kwin_ref, vwint_ref, qi)
        carry = _flash_tile(init, k_t, vt_t, qs, tbl_ref[kvg, 0], None)
        for back in range(1, WIN_TILES + 1):
            jb = jnp.maximum(qi - back, 0)
            k_t, vt_t = tile(kwin_ref, vwint_ref, jb)
            bias = tbl_ref[kvg, 1] if back == 1 else (tbl_ref[kvg, 2] if back == WIN_TILES else far_row)
            carry = _flash_tile(carry, k_t, vt_t, qs, bias, qi >= back)
        m_w, l_w, acc_w = carry

        o_s = acc_s / l_s
        o_w = acc_w / l_w
        for g in range(NSA_GROUP):
            h = kvg * NSA_GROUP + g
            lanes = slice(g * TQ, (g + 1) * TQ)
            o = (gates[h:h + 1] * oc[:, lanes] + gates[NSA_HEADS + h:NSA_HEADS + h + 1] * o_s[:, lanes]
                 + gates[2 * NSA_HEADS + h:2 * NSA_HEADS + h + 1] * o_w[:, lanes])
            outs.append(o.T)
    o_ref[0] = jnp.concatenate(outs, axis=1)


def _nsa_prompt(q, kv, gate_logits, cmp_pool, cmp_pe, rel_bias):
    B, T, _ = q.shape
    nb = T // NSA_BLOCK
    assert T % 512 == 0 and nb % SUBLANES == 0
    TQ = NSA_TQ
    gt = jnp.transpose(gate_logits, (0, 2, 1))
    f32, bf16 = jnp.float32, jnp.bfloat16
    return pl.pallas_call(
        _nsa_prompt_kernel,
        grid=(B, T // TQ),
        in_specs=[
            pl.BlockSpec(memory_space=pltpu.SMEM),
            pl.BlockSpec((1, TQ, NSA_HEADS * HEAD_DIM), lambda b, i: (b, i, 0)),
            pl.BlockSpec((1, T, 6 * NSA_KV_HEADS * HEAD_DIM), lambda b, i: (b, 0, 0)),
            pl.BlockSpec((1, 3 * NSA_HEADS, TQ), lambda b, i: (b, 0, i)),
            pl.BlockSpec((NSA_BLOCK, 2), lambda b, i: (0, 0)),
            pl.BlockSpec((2, NSA_BLOCK, HEAD_DIM), lambda b, i: (0, 0, 0)),
        ],
        out_specs=pl.BlockSpec((1, TQ, NSA_HEADS * HEAD_DIM), lambda b, i: (b, i, 0)),
        out_shape=jax.ShapeDtypeStruct((B, T, NSA_HEADS * HEAD_DIM), f32),
        scratch_shapes=[
            pltpu.VMEM((NSA_KV_HEADS, nb, HEAD_DIM), f32),
            pltpu.VMEM((NSA_KV_HEADS, nb, HEAD_DIM), f32),
            pltpu.VMEM((NSA_KV_HEADS, T, HEAD_DIM), bf16),
            pltpu.VMEM((NSA_KV_HEADS, HEAD_DIM, T), bf16),
            pltpu.VMEM((NSA_KV_HEADS, T, HEAD_DIM), bf16),
            pltpu.VMEM((NSA_KV_HEADS, HEAD_DIM, T), bf16),
            pltpu.VMEM((NSA_KV_HEADS, 3, TQ, NSA_GROUP * TQ), f32),
            pltpu.VMEM((NSA_KV_HEADS, nb, TQ), f32),
            pltpu.VMEM((NSA_KV_HEADS, nb, TQ), f32),
        ],
        compiler_params=pltpu.CompilerParams(dimension_semantics=("arbitrary", "arbitrary"),
                                             vmem_limit_bytes=VMEM_LIMIT),
        name="nsa_prompt",
    )(rel_bias, q, kv, gt, jnp.transpose(cmp_pool), cmp_pe)


def nsa_attention(q, kv_new, gate_logits, past_rows, win_buf, pos0, cmp_pool, cmp_pe, rel_bias):
    B, T = q.shape[:2]
    if past_rows is None:
        o = _nsa_prompt(q.reshape(B, T, -1), kv_new.reshape(B, T, -1), gate_logits.reshape(B, T, -1),
                        cmp_pool, cmp_pe, rel_bias)
        return o, kv_new[:, T - min(NSA_WINDOW, T):, 4:]
    dt = q.dtype
    qpos = pos0 + jnp.arange(T, dtype=jnp.int32)
    qg = (q * HEAD_DIM ** -0.5).reshape(B, T, NSA_KV_HEADS, NSA_GROUP, HEAD_DIM)
    rb = rel_bias.astype(jnp.float32).reshape(NUM_BUCKETS, NSA_KV_HEADS, NSA_GROUP)
    L = T if past_rows is None else past_rows.shape[1] + T
    nb = -(-L // NSA_BLOCK)
    new_rows = jnp.pad(kv_new[:, :, :4], ((0, 0), (0, nb * NSA_BLOCK - L), (0, 0), (0, 0), (0, 0)))
    full = new_rows if past_rows is None else jnp.concatenate([past_rows.astype(dt), new_rows], axis=1)
    blocks = full.reshape(B, nb, NSA_BLOCK, 4, NSA_KV_HEADS, HEAD_DIM)
    pe = jnp.transpose(cmp_pe, (1, 0, 2))[:, :, None, :].astype(dt)
    cmp = jnp.einsum('bnjckd,cj->bnckd', blocks[:, :, :, :2] + pe, cmp_pool.astype(dt))
    k_c, v_c = cmp[:, :, 0], cmp[:, :, 1]
    blk = jnp.arange(nb, dtype=jnp.int32)
    d_c = qpos[:, None] - (blk * NSA_BLOCK + NSA_BLOCK - 1)[None, :]
    s_c = jnp.einsum('btkgd,bnkd->bkgtn', qg, k_c).astype(jnp.float32) + jnp.transpose(rb[t5_bucket(d_c)], (2, 3, 0, 1))
    p_c = masked_softmax(s_c, d_c >= 0)
    o_c = jnp.einsum('bkgtn,bnkd->btkgd', p_c.astype(dt), v_c)
    cur = (qpos // NSA_BLOCK)[:, None]
    forced = (blk[None] == 0) | (blk[None] == cur) | (blk[None] == cur - 1)
    score = jnp.where(blk[None] <= cur, jnp.where(forced, 2.0, p_c.sum(axis=2)), -1.0)
    top_s, idx = lax.top_k(score, min(NSA_TOPK, nb))
    ok = top_s > -0.5
    ks = jnp.transpose(blocks[:, :, :, 2], (0, 3, 1, 2, 4))
    vs = jnp.transpose(blocks[:, :, :, 3], (0, 3, 1, 2, 4))
    take = jax.vmap(jax.vmap(lambda a, i: a[i]))
    kv_ix = jnp.arange(NSA_KV_HEADS)[None, :, None, None, None]
    rbk = jnp.transpose(rb, (1, 0, 2))

    def sel_block(args):
        qb, ib, okb, pb = args
        Bq, Q = qb.shape[:2]
        kg, vg = take(ks, ib), take(vs, ib)
        kpos = ib[..., None] * NSA_BLOCK + jnp.arange(NSA_BLOCK, dtype=jnp.int32)
        dist = pb[None, None, :, None, None] - kpos
        mask = (okb[..., None] & (dist >= 0)).reshape(Bq, NSA_KV_HEADS, 1, Q, -1)
        bias = jnp.moveaxis(rbk[kv_ix, t5_bucket(dist)], -1, 2)
        s = jnp.einsum('bqkgd,bkqsjd->bkgqsj', qb, kg).astype(jnp.float32) + bias
        p = masked_softmax(s.reshape(Bq, NSA_KV_HEADS, NSA_GROUP, Q, -1), mask)
        return jnp.einsum('bkgqn,bkqnd->bqkgd', p.astype(dt), vg.reshape(Bq, NSA_KV_HEADS, Q, -1, HEAD_DIM))

    o_s = over_query_blocks(sel_block, (qg, idx, ok, qpos), (1, 2, 2, 0), T)
    kw = kv_new[:, :, 4:]
    if win_buf is None:
        real = kw
        k_all = jnp.pad(kw, ((0, 0), (NSA_WINDOW, 0), (0, 0), (0, 0), (0, 0)))
        span = NSA_WINDOW
    else:
        real = jnp.concatenate([win_buf.astype(dt), kw], axis=1)
        k_all = real
        span = win_buf.shape[1]
    k0 = pos0 - span

    def win_block(args):
        qb, pb = args
        Q = qb.shape[1]
        start = pb[0] - pos0
        kb = lax.dynamic_slice_in_dim(k_all, start, span + Q, axis=1)
        kpos = k0 + start + jnp.arange(span + Q, dtype=jnp.int32)
        dist = pb[:, None] - kpos[None, :]
        mask = (dist >= 0) & (dist < NSA_WINDOW) & (kpos >= 0)[None, :]
        s = jnp.einsum('bqkgd,bnkd->bkgqn', qb, kb[:, :, 0]).astype(jnp.float32) + jnp.transpose(rb[t5_bucket(dist)], (2, 3, 0, 1))
        p = masked_softmax(s, mask)
        return jnp.einsum('bkgqn,bnkd->bqkgd', p.astype(dt), kb[:, :, 1])

    o_w = over_query_blocks(win_block, (qg, qpos), (1, 0), T)
    new_win = real[:, real.shape[1] - min(NSA_WINDOW, real.shape[1]):]
    g = jax.nn.sigmoid(gate_logits.astype(jnp.float32)).astype(dt).reshape(B, T, 3, NSA_KV_HEADS, NSA_GROUP, 1)
    o = g[:, :, 0] * o_c + g[:, :, 1] * o_s + g[:, :, 2] * o_w
    return o.reshape(B, T, NSA_HEADS * HEAD_DIM), new_win


def conformer_conv(u, buf, dw, dw_b, ln_g, ln_b):
    a, gte = jnp.split(u, 2, axis=-1)
    y, new_buf = causal_dwconv(a * jax.nn.sigmoid(gte), buf, dw)
    y = layer_norm(y + dw_b.astype(y.dtype), ln_g, ln_b)
    return jax.nn.silu(y), new_buf


def retention(q, k, v, gate, S0, pos0, gn):
    B, T, _ = q.shape
    dt = q.dtype
    H, d = RET_HEADS, HEAD_DIM
    pos = pos0 + jnp.arange(T, dtype=jnp.int32)
    f = lambda a: a.astype(jnp.float32).reshape(B, T, H, d)
    q = rotary(f(q), pos)
    k = rotary(f(k), pos) * d ** -0.5
    v = f(v)
    C = math.gcd(T, RET_CHUNK)
    n = T // C
    lg = jnp.log1p(-jnp.exp2(-5.0 - jnp.arange(H, dtype=jnp.float32)))
    ch = lambda a: a.reshape(B, n, C, H, d).transpose(0, 3, 1, 2, 4)
    qc, kc, vc = ch(q), ch(k), ch(v)
    i = jnp.arange(C, dtype=jnp.float32)
    diff = i[:, None] - i[None, :]
    Dm = jnp.where(diff >= 0, jnp.exp(jnp.maximum(diff, 0.0)[None] * lg[:, None, None]), 0.0)
    att = jnp.einsum('bhncd,bhnsd->bhncs', qc, kc) * Dm[None, :, None]
    o = jnp.einsum('bhncs,bhnse->bhnce', att, vc)
    xi = jnp.exp((i[None, :] + 1.0) * lg[:, None])
    zeta = jnp.exp((C - 1.0 - i[None, :]) * lg[:, None])
    kv = jnp.einsum('bhncd,bhnce->nbhde', kc * zeta[None, :, None, :, None], vc)
    decay = jnp.exp(C * lg)[None, :, None, None]

    def step(S, kvn):
        return S * decay + kvn, S

    S_fin, S_prev = lax.scan(step, S0.astype(jnp.float32), kv)
    o = o + jnp.einsum('bhncd,nbhde->bhnce', qc * xi[None, :, None, :, None], S_prev)
    o = o.transpose(0, 2, 3, 1, 4).reshape(B, T, H, d)
    mu = jnp.mean(o, axis=-1, keepdims=True)
    var = jnp.mean(jnp.square(o - mu), axis=-1, keepdims=True)
    o = ((o - mu) * lax.rsqrt(var + EPS)).reshape(B, T, H * d) * gn.astype(jnp.float32)
    return (o * jax.nn.silu(gate.astype(jnp.float32))).astype(dt), S_fin


def gated_deltanet(qkv, z, a, b, conv_buf, S0, conv_w, A_log, dt_bias, norm_g):
    B, T, _ = qkv.shape
    dt = qkv.dtype
    H, d = GDN_HEADS, HEAD_DIM
    y, new_buf = causal_dwconv(qkv, conv_buf, conv_w)
    y = jax.nn.silu(y.astype(jnp.float32))
    q, k, v = [t.reshape(B, T, H, d) for t in jnp.split(y, 3, axis=-1)]
    l2 = lambda t: t * lax.rsqrt(jnp.sum(t * t, axis=-1, keepdims=True) + EPS)
    q = l2(q) * d ** -0.5
    k = l2(k)
    g = -jnp.exp(A_log.astype(jnp.float32)) * jax.nn.softplus(a.astype(jnp.float32) + dt_bias.astype(jnp.float32))
    beta = jax.nn.sigmoid(b.astype(jnp.float32))
    C = math.gcd(T, GDN_CHUNK)
    n = T // C
    ch = lambda t: t.reshape(B, n, C, H, d).transpose(0, 3, 1, 2, 4)
    chs = lambda t: t.reshape(B, n, C, H).transpose(0, 3, 1, 2)
    qc, kc, vc = ch(q), ch(k), ch(v)
    gc = jnp.cumsum(chs(g), axis=-1)
    bc = chs(beta)
    ii = jnp.arange(C)
    tri = ii[:, None] >= ii[None, :]
    strict = ii[:, None] > ii[None, :]
    Lm = jnp.exp(jnp.where(tri, gc[..., :, None] - gc[..., None, :], -jnp.inf))
    kb = kc * bc[..., None]
    M = jnp.where(strict, jnp.einsum('bhncd,bhnsd->bhncs', kb, kc) * Lm, 0.0)
    A = M + jnp.eye(C, dtype=jnp.float32)
    rhs = jnp.concatenate([vc * bc[..., None], kb * jnp.exp(gc)[..., None]], axis=-1)
    sol = lax.linalg.triangular_solve(A, rhs, left_side=True, lower=True, unit_diagonal=True)
    u, w = sol[..., :d], sol[..., d:]
    att = jnp.einsum('bhncd,bhnsd->bhncs', qc, kc) * Lm
    qd = qc * jnp.exp(gc)[..., None]
    kd = kc * jnp.exp(gc[..., -1:] - gc)[..., None]
    glast = jnp.exp(gc[..., -1])
    xs = tuple(jnp.moveaxis(t, 2, 0) for t in (u, w, att, qd, kd, glast))

    def step(S, xs_n):
        u_, w_, at_, qd_, kd_, gl_ = xs_n
        vn = u_ - jnp.einsum('bhcd,bhde->bhce', w_, S)
        o_ = jnp.einsum('bhcd,bhde->bhce', qd_, S) + jnp.einsum('bhcs,bhse->bhce', at_, vn)
        S = S * gl_[..., None, None] + jnp.einsum('bhcd,bhce->bhde', kd_, vn)
        return S, o_

    S_fin, o = lax.scan(step, S0.astype(jnp.float32), xs)
    o = o.transpose(1, 0, 3, 2, 4).reshape(B, T, H, d)
    o = o * lax.rsqrt(jnp.mean(o * o, axis=-1, keepdims=True) + EPS) * norm_g.astype(jnp.float32)
    o = o * jax.nn.silu(z.astype(jnp.float32).reshape(B, T, H, d))
    return o.reshape(B, T, H * d).astype(dt), new_buf, S_fin


def conv_ffn(h, buf, w_up, dw, w_down):
    Bh, Th, _ = h.shape
    gpre, val = jnp.split(_pmm(h.reshape(Bh * Th, D_MODEL), w_up).reshape(Bh, Th, 2 * D_FF), 2, axis=-1)
    gconv, new_buf = causal_dwconv(gpre, buf, dw)
    return _pmm((jax.nn.gelu(gconv) * val).reshape(Bh * Th, D_FF), w_down).reshape(Bh, Th, D_MODEL), new_buf


def trunk_layer(x, c, pos0, nsa_past, win_buf, conv_buf, ret_s, gdn_buf, gdn_s, ffn_buf, p, rel_bias):
    B, T, _ = x.shape
    mod = _pmm(jax.nn.silu(c), p["w_ada"]) + p["b_ada"]
    sh1, sc1, gt1, sh2, sc2, gt2 = [m[:, None, :] for m in jnp.split(mod, 6, axis=-1)]
    h = rms_norm(x, p["norms"][0]) * (1.0 + sc1) + sh1
    nq, nkv, ngt, ucv, ret, gdn, mg = split_cols(
        _pmm(h.reshape(B * T, D_MODEL), p["w_in"]).reshape(B, T, IN_COLS), IN_SPLITS)
    nkv = nkv.reshape(B, T, 6, NSA_KV_HEADS, HEAD_DIM)
    o_nsa, new_win = nsa_attention(nq.reshape(B, T, NSA_HEADS, HEAD_DIM), nkv, ngt.reshape(B, T, 3, NSA_HEADS),
                                   nsa_past, win_buf, pos0, p["cmp_pool"], p["cmp_pe"], rel_bias)
    o_conv, new_conv = conformer_conv(ucv, conv_buf, p["conv_dw"], p["conv_dw_b"], p["conv_ln_g"], p["conv_ln_b"])
    rq, rk, rv, rg = split_cols(ret, (RET_W, RET_W, RET_W, RET_W))
    o_ret, new_ret = retention(rq, rk, rv, rg, ret_s, pos0, p["ret_gn"])
    gqkv, gz, ga, gb = split_cols(gdn, (3 * GDN_W, GDN_W, GDN_HEADS, GDN_HEADS))
    o_gdn, new_gdn_buf, new_gdn = gated_deltanet(gqkv, gz, ga, gb, gdn_buf, gdn_s, p["gdn_conv_w"],
                                                 p["gdn_A_log"], p["gdn_dt_bias"], p["gdn_norm"])
    branches = jnp.stack([o_nsa, o_conv, o_ret, o_gdn], axis=2)
    merged = jnp.stack([_pmm(branches[:, :, n].reshape(B * T, BRANCH_W), p["w_branch"][n]).reshape(B, T, D_MODEL)
                        for n in range(N_BRANCH)], axis=2)
    gates = jax.nn.sigmoid(mg.reshape(B, T, N_BRANCH, D_MODEL))
    mixed = _pmm(jnp.einsum('btnd,btnd->btd', gates, merged).reshape(B * T, D_MODEL),
                 p["w_out"]).reshape(B, T, D_MODEL)
    x = x + gt1 * rms_norm(mixed, p["norms"][1])
    h2 = rms_norm(x, p["norms"][2]) * (1.0 + sc2) + sh2
    f, new_ffn = conv_ffn(h2, ffn_buf, p["ffn_up"], p["ffn_dw"], p["ffn_down"])
    x = x + gt2 * rms_norm(f, p["norms"][3])
    return x, (nkv[:, :, :4], new_win, new_conv, new_ret, new_gdn_buf, new_gdn, new_ffn)


def kernel(x_prompt, x_sample, cache_nsa_kv, cache_nsa_win, state_conv, state_ret, state_gdn_conv, state_gdn,
           state_ffn_conv, page_table, c_prompt, c_sample, w_ada, b_ada, norms, w_in, cmp_pool, cmp_pe, rel_bias,
           conv_dw, conv_dw_b, conv_ln_g, conv_ln_b, ret_gn, gdn_conv_w, gdn_A_log, gdn_dt_bias, gdn_norm,
           w_branch, w_out, ffn_up, ffn_dw, ffn_down):
    B = x_prompt.shape[0]
    Bd = x_sample.shape[0]
    past = page_table.shape[1] * PAGE_SIZE
    layer_w = {"w_ada": w_ada, "b_ada": b_ada, "norms": norms, "w_in": w_in, "cmp_pool": cmp_pool,
               "cmp_pe": cmp_pe, "conv_dw": conv_dw, "conv_dw_b": conv_dw_b, "conv_ln_g": conv_ln_g,
               "conv_ln_b": conv_ln_b, "ret_gn": ret_gn, "gdn_conv_w": gdn_conv_w, "gdn_A_log": gdn_A_log,
               "gdn_dt_bias": gdn_dt_bias, "gdn_norm": gdn_norm, "w_branch": w_branch, "w_out": w_out,
               "ffn_up": ffn_up, "ffn_dw": ffn_dw, "ffn_down": ffn_down}
    yp, ys = x_prompt, x_sample
    st_p, st_s = [], []
    for l in range(DEPTH):
        p = {name: w[l] for name, w in layer_w.items()}
        yp, sp = trunk_layer(
            yp, c_prompt, 0, None, None,
            jnp.zeros((B, CONV_WIDTH - 1, CONV_CH), x_prompt.dtype),
            jnp.zeros((B, RET_HEADS, HEAD_DIM, HEAD_DIM), jnp.float32),
            jnp.zeros((B, GDN_CONV - 1, 3 * GDN_W), x_prompt.dtype),
            jnp.zeros((B, GDN_HEADS, HEAD_DIM, HEAD_DIM), jnp.float32),
            jnp.zeros((B, FFN_CONV - 1, D_FF), x_prompt.dtype),
            p, rel_bias)
        past_rows = cache_nsa_kv[page_table, l].reshape(Bd, past, 4, NSA_KV_HEADS, HEAD_DIM)
        ys, ss = trunk_layer(
            ys, c_sample, past, past_rows, cache_nsa_win[l], state_conv[l], state_ret[l],
            state_gdn_conv[l], state_gdn[l], state_ffn_conv[l], p, rel_bias)
        st_p.append(sp)
        st_s.append(ss)

    def stack(outs, i, axis):
        return jnp.stack([o[i] for o in outs], axis=axis)

    kv_p, kv_s = stack(st_p, 0, 1), stack(st_s, 0, 1)
    win_p, win_s = stack(st_p, 1, 0), stack(st_s, 1, 0)
    conv_p, conv_s = stack(st_p, 2, 0), stack(st_s, 2, 0)
    ret_p, ret_s = stack(st_p, 3, 0), stack(st_s, 3, 0)
    gdnc_p, gdnc_s = stack(st_p, 4, 0), stack(st_s, 4, 0)
    gdn_p, gdn_s = stack(st_p, 5, 0), stack(st_s, 5, 0)
    ffn_p, ffn_s = stack(st_p, 6, 0), stack(st_s, 6, 0)
    return (yp, ys, kv_p, kv_s, win_p, win_s, conv_p, conv_s, ret_p, ret_s, gdnc_p, gdnc_s, gdn_p, gdn_s, ffn_p, ffn_s)
```

```python
import functools
import math

import jax
import jax.numpy as jnp
import numpy as np
from jax import lax
from jax.experimental import pallas as pl
from jax.experimental.pallas import tpu as pltpu

D_MODEL = 1024
BATCH = 4
SEQ = 4096
DEPTH = 2
DEC_BATCH = 32
DEC_SEQ = 8
PAST_LEN = 16384
PAGE_SIZE = 128

HEAD_DIM = 64
NSA_HEADS = 4
NSA_KV_HEADS = 2
NSA_GROUP = NSA_HEADS // NSA_KV_HEADS
NSA_BLOCK = 64
NSA_TOPK = 16
NSA_WINDOW = 512
Q_BLOCK = 128
NUM_BUCKETS = 32
MAX_DISTANCE = 128
CONV_CH = D_MODEL // 4
CONV_WIDTH = 31
RET_HEADS = 4
RET_W = RET_HEADS * HEAD_DIM
RET_CHUNK = 64
ROPE_BASE = 10000.0
GDN_HEADS = 4
GDN_W = GDN_HEADS * HEAD_DIM
GDN_CONV = 4
GDN_CHUNK = 64
D_FF = 2816
FFN_CONV = 3
N_BRANCH = 4
BRANCH_W = NSA_HEADS * HEAD_DIM
EPS = 1e-6
NEG_INF = -1e30
IN_SPLITS = (NSA_HEADS * HEAD_DIM, 6 * NSA_KV_HEADS * HEAD_DIM, 3 * NSA_HEADS, 2 * CONV_CH, 4 * RET_W,
             4 * GDN_W + 2 * GDN_HEADS, N_BRANCH * D_MODEL)
IN_COLS = sum(IN_SPLITS)

LANES = 128
SUBLANES = 8
VMEM_LIMIT = 56 * 1024 * 1024


def _round_up(a, m):
    return -(-a // m) * m


def _mm_kernel(x_ref, w_ref, o_ref):
    o_ref[...] = jnp.dot(x_ref[...].astype(jnp.bfloat16), w_ref[...].astype(jnp.bfloat16),
                         preferred_element_type=jnp.float32)


def _pmm(x, w):
    M, K = x.shape
    N = w.shape[1]
    Mp = _round_up(M, SUBLANES)
    tm = min(512, Mp)
    Mp = _round_up(Mp, tm)
    tn = 512 if N > 512 else _round_up(N, LANES)
    Np = _round_up(N, tn)
    if Mp != M:
        x = jnp.pad(x, ((0, Mp - M), (0, 0)))
    if Np != N:
        w = jnp.pad(w, ((0, 0), (0, Np - N)))
    out = pl.pallas_call(
        _mm_kernel,
        grid=(Mp // tm, Np // tn),
        in_specs=[pl.BlockSpec((tm, K), lambda i, j: (i, 0)), pl.BlockSpec((K, tn), lambda i, j: (0, j))],
        out_specs=pl.BlockSpec((tm, tn), lambda i, j: (i, j)),
        out_shape=jax.ShapeDtypeStruct((Mp, Np), jnp.float32),
        compiler_params=pltpu.CompilerParams(dimension_semantics=("arbitrary", "arbitrary"),
                                             vmem_limit_bytes=VMEM_LIMIT),
    )(x, w)
    return out[:M, :N]


def split_cols(a, sizes):
    return jnp.split(a, [int(i) for i in np.cumsum(sizes)[:-1]], axis=-1)


def rms_norm(x, g):
    x32 = x.astype(jnp.float32)
    y = x32 * lax.rsqrt(jnp.mean(x32 * x32, axis=-1, keepdims=True) + EPS)
    return (y * g.astype(jnp.float32)).astype(x.dtype)


def layer_norm(x, g, b):
    x32 = x.astype(jnp.float32)
    mu = jnp.mean(x32, axis=-1, keepdims=True)
    var = jnp.mean(jnp.square(x32 - mu), axis=-1, keepdims=True)
    return ((x32 - mu) * lax.rsqrt(var + EPS) * g.astype(jnp.float32) + b.astype(jnp.float32)).astype(x.dtype)


def causal_dwconv(x, buf, w):
    xp = jnp.concatenate([buf.astype(x.dtype), x], axis=1)
    y = lax.conv_general_dilated(xp, w[:, None, :].astype(x.dtype), window_strides=(1,), padding='VALID',
                                 dimension_numbers=('NWC', 'WIO', 'NWC'), feature_group_count=x.shape[-1])
    return y, xp[:, xp.shape[1] - (w.shape[0] - 1):]


def t5_bucket(dist):
    n = jnp.maximum(dist, 0)
    exact = NUM_BUCKETS // 2
    large = exact + (jnp.log(jnp.maximum(n, 1).astype(jnp.float32) / exact) / math.log(MAX_DISTANCE / exact)
                     * (NUM_BUCKETS - exact)).astype(jnp.int32)
    return jnp.where(n < exact, n, jnp.minimum(large, NUM_BUCKETS - 1))


def masked_softmax(s, mask):
    return jax.nn.softmax(jnp.where(mask, s, NEG_INF), axis=-1) * mask


def over_query_blocks(fn, args, q_axes, T):
    if T <= Q_BLOCK or T % Q_BLOCK:
        return fn(args)
    nq = T // Q_BLOCK

    def split(a, ax):
        return jnp.moveaxis(a.reshape(a.shape[:ax] + (nq, Q_BLOCK) + a.shape[ax + 1:]), ax, 0)

    out = lax.map(fn, tuple(split(a, ax) for a, ax in zip(args, q_axes)))
    out = jnp.moveaxis(out, 0, 1)
    return out.reshape((out.shape[0], T) + out.shape[3:])


def rotary(x, pos):
    half = x.shape[-1] // 2
    inv = ROPE_BASE ** (-jnp.arange(half, dtype=jnp.float32) / half)
    ang = pos.astype(jnp.float32)[:, None] * inv[None, :]
    cos, sin = jnp.cos(ang)[None, :, None, :], jnp.sin(ang)[None, :, None, :]
    x1, x2 = x[..., :half], x[..., half:]
    return jnp.concatenate([x1 * cos - x2 * sin, x1 * sin + x2 * cos], axis=-1)


NSA_TQ = 128
BLOCKS_PER_TILE = NSA_TQ // NSA_BLOCK
WIN_TILES = NSA_WINDOW // NSA_TQ


def _t5_thresholds():
    n = np.arange(0, 2 * MAX_DISTANCE)
    exact = NUM_BUCKETS // 2
    large = exact + (np.log(np.maximum(n, 1).astype(np.float32) / np.float32(exact))
                     / np.float32(math.log(MAX_DISTANCE / exact)) * (NUM_BUCKETS - exact)).astype(np.int32)
    bucket = np.where(n < exact, n, np.minimum(large, NUM_BUCKETS - 1))
    return tuple(int(np.argmax(bucket >= k)) for k in range(1, NUM_BUCKETS))


_T5_THR = _t5_thresholds()


def _bias_from_dist(dist, rb_ref, h):
    v = jnp.full(dist.shape, rb_ref[NUM_BUCKETS - 1, h], jnp.float32)
    for k in range(NUM_BUCKETS - 2, -1, -1):
        v = jnp.where(dist < _T5_THR[k], rb_ref[k, h], v)
    return v


def _dot_nt(a, b):
    return lax.dot_general(a, b, (((1,), (1,)), ((), ())), preferred_element_type=jnp.float32)


def _flash_tile(carry, k_t, vt_t, qs, bias, keep):
    m, l, acc = carry
    s = _dot_nt(k_t, qs) + bias
    if keep is not None:
        s = jnp.where(keep, s, NEG_INF)
    m_new = jnp.maximum(m, jnp.max(s, axis=0, keepdims=True))
    p = jnp.exp(s - m_new)
    alpha = jnp.exp(m - m_new)
    l = alpha * l + jnp.sum(p, axis=0, keepdims=True)
    acc = alpha * acc + jnp.dot(vt_t, p.astype(jnp.bfloat16), preferred_element_type=jnp.float32)
    return m_new, l, acc


def _nsa_prompt_kernel(rb_ref, q_ref, kv_ref, gt_ref, poolt_ref, pe_ref, o_ref,
                       kc_ref, vc_ref, ksel_ref, vselt_ref, kwin_ref, vwint_ref, tbl_ref, score_ref, sel_ref):
    b = pl.program_id(0)
    qi = pl.program_id(1)
    T = kv_ref.shape[1]
    nb = T // NSA_BLOCK
    topk = min(NSA_TOPK, nb)
    TQ = NSA_TQ
    CH = 512
    f32, bf16 = jnp.float32, jnp.bfloat16

    @pl.when((b == 0) & (qi == 0))
    def _tables():
        jj = lax.broadcasted_iota(jnp.int32, (TQ, TQ), 0)
        tt = lax.broadcasted_iota(jnp.int32, (TQ, TQ), 1)
        d0 = tt - jj
        for h in range(NSA_HEADS):
            kvg, g = divmod(h, NSA_GROUP)
            lanes = slice(g * TQ, (g + 1) * TQ)
            far = rb_ref[NUM_BUCKETS - 1, h]
            tbl_ref[kvg, 0, :, lanes] = jnp.where(d0 >= 0, _bias_from_dist(d0, rb_ref, h), NEG_INF)
            tbl_ref[kvg, 1, :, lanes] = _bias_from_dist(d0 + TQ, rb_ref, h)
            tbl_ref[kvg, 2, :, lanes] = jnp.where(d0 < 0, far, NEG_INF)

    @pl.when(qi == 0)
    def _prologue():
        def chunk(i, carry):
            r = pl.multiple_of(i * CH, CH)
            rb8 = pl.multiple_of(i * (CH // NSA_BLOCK), CH // NSA_BLOCK)
            for kvg in range(NSA_KV_HEADS):
                def col(c):
                    lo = c * NSA_KV_HEADS * HEAD_DIM + kvg * HEAD_DIM
                    return kv_ref[0, pl.ds(r, CH), lo:lo + HEAD_DIM]
                for c, dst in ((0, kc_ref), (1, vc_ref)):
                    x = col(c).reshape(CH // NSA_BLOCK, NSA_BLOCK, HEAD_DIM) + pe_ref[c][None]
                    dst[kvg, pl.ds(rb8, CH // NSA_BLOCK), :] = jnp.sum(x * poolt_ref[:, c:c + 1][None], axis=1)
                ksel_ref[kvg, pl.ds(r, CH), :] = col(2).astype(bf16)
                vselt_ref[kvg, :, pl.ds(r, CH)] = col(3).T.astype(bf16)
                kwin_ref[kvg, pl.ds(r, CH), :] = col(4).astype(bf16)
                vwint_ref[kvg, :, pl.ds(r, CH)] = col(5).T.astype(bf16)
            return carry
        lax.fori_loop(0, T // CH, chunk, 0)

    q = q_ref[0]
    gates = jax.nn.sigmoid(gt_ref[0])
    n_io = lax.broadcasted_iota(jnp.int32, (nb, TQ), 0)
    t_io = lax.broadcasted_iota(jnp.int32, (nb, TQ), 1)
    lane2 = lax.broadcasted_iota(jnp.int32, (1, 2 * TQ), 1)
    dist_c = qi * TQ + t_io - (n_io * NSA_BLOCK + NSA_BLOCK - 1)
    vis_c = dist_c >= 0
    vis_c2 = jnp.concatenate([vis_c, vis_c], axis=1)
    cur = (qi * TQ + t_io) // NSA_BLOCK
    forced = (n_io == 0) | (n_io == cur) | (n_io == cur - 1)
    q0 = pl.multiple_of(qi * TQ, TQ)
    outs = []
    for kvg in range(NSA_KV_HEADS):
        base = kvg * NSA_GROUP * HEAD_DIM
        qs = jnp.concatenate([q[:, base + g * HEAD_DIM: base + (g + 1) * HEAD_DIM] for g in range(NSA_GROUP)],
                             axis=0)
        qs = (qs * HEAD_DIM ** -0.5).astype(bf16)
        far_row = jnp.where(lane2 < TQ, rb_ref[NUM_BUCKETS - 1, kvg * NSA_GROUP],
                            rb_ref[NUM_BUCKETS - 1, kvg * NSA_GROUP + 1])

        sc = _dot_nt(kc_ref[kvg].astype(bf16), qs)
        bias_c = jnp.concatenate([_bias_from_dist(dist_c, rb_ref, kvg * NSA_GROUP + g) for g in range(NSA_GROUP)],
                                 axis=1)
        sc = jnp.where(vis_c2, sc + bias_c, NEG_INF)
        e = jnp.exp(sc - jnp.max(sc, axis=0, keepdims=True))
        p_c = e / jnp.sum(e, axis=0, keepdims=True) * vis_c2.astype(f32)
        oc = jnp.dot(vc_ref[kvg].T.astype(bf16), p_c.astype(bf16), preferred_element_type=f32)

        score = jnp.where(n_io <= cur, jnp.where(forced, 2.0, p_c[:, :TQ] + p_c[:, TQ:]), -1.0)
        score_ref[kvg] = score

        def rank_body(mi, rank):
            row = score_ref[kvg, pl.ds(mi, 1), :]
            beats = (row > score) | ((row == score) & (mi < n_io))
            return rank + beats.astype(jnp.int32)

        rank = lax.fori_loop(0, BLOCKS_PER_TILE * (qi + 1), rank_body, jnp.zeros((nb, TQ), jnp.int32))
        sel_ref[kvg] = ((rank < topk) & (n_io <= cur)).astype(f32)

        def sel_keep(j):
            rows = [jnp.broadcast_to(sel_ref[kvg, pl.ds(BLOCKS_PER_TILE * j + u, 1), :], (NSA_BLOCK, TQ))
                    for u in range(BLOCKS_PER_TILE)]
            mm = jnp.concatenate(rows, axis=0)
            return jnp.concatenate([mm, mm], axis=1) > 0.5

        def tile(kref, vtref, j):
            r = pl.multiple_of(j * TQ, TQ)
            return kref[kvg, pl.ds(r, TQ), :], vtref[kvg, :, pl.ds(r, TQ)]

        init = (jnp.full((1, 2 * TQ), NEG_INF, f32), jnp.zeros((1, 2 * TQ), f32), jnp.zeros((HEAD_DIM, 2 * TQ), f32))

        k_t, vt_t = tile(ksel_ref, vselt_ref, qi)
        carry = _flash_tile(init, k_t, vt_t, qs, tbl_ref[kvg, 0], sel_keep(qi))
        jp = jnp.maximum(qi - 1, 0)
        k_t, vt_t = tile(ksel_ref, vselt_ref, jp)
        carry = _flash_tile(carry, k_t, vt_t, qs, tbl_ref[kvg, 1], sel_keep(jp) & (qi >= 1))

        def far_body(i, c):
            j = qi - 2 - i
            k_f, vt_f = tile(ksel_ref, vselt_ref, j)
            return _flash_tile(c, k_f, vt_f, qs, far_row, sel_keep(j))

        m_s, l_s, acc_s = lax.fori_loop(0, jnp.maximum(qi - 1, 0), far_body, carry)

        k_t, vt_t = tile(kwin_ref, vwint_ref, qi)
        carry = _flash_tile(init, k_t, vt_t, qs, tbl_ref[kvg, 0], None)
        for back in range(1, WIN_TILES + 1):
            jb = jnp.maximum(qi - back, 0)
            k_t, vt_t = tile(kwin_ref, vwint_ref, jb)
            bias = tbl_ref[kvg, 1] if back == 1 else (tbl_ref[kvg, 2] if back == WIN_TILES else far_row)
            carry = _flash_tile(carry, k_t, vt_t, qs, bias, qi >= back)
        m_w, l_w, acc_w = carry

        o_s = acc_s / l_s
        o_w = acc_w / l_w
        for g in range(NSA_GROUP):
            h = kvg * NSA_GROUP + g
            lanes = slice(g * TQ, (g + 1) * TQ)
            o = (gates[h:h + 1] * oc[:, lanes] + gates[NSA_HEADS + h:NSA_HEADS + h + 1] * o_s[:, lanes]
                 + gates[2 * NSA_HEADS + h:2 * NSA_HEADS + h + 1] * o_w[:, lanes])
            outs.append(o.T)
    o_ref[0] = jnp.concatenate(outs, axis=1)


def _nsa_prompt(q, q_blk, kv, kv_blk, gate_logits, cmp_pool, cmp_pe, rel_bias):
    B, T, _ = q.shape
    nb = T // NSA_BLOCK
    assert T % 512 == 0 and nb % SUBLANES == 0
    TQ = NSA_TQ
    gt = jnp.transpose(gate_logits, (0, 2, 1))
    f32, bf16 = jnp.float32, jnp.bfloat16
    return pl.pallas_call(
        _nsa_prompt_kernel,
        grid=(B, T // TQ),
        in_specs=[
            pl.BlockSpec(memory_space=pltpu.SMEM),
            pl.BlockSpec((1, TQ, NSA_HEADS * HEAD_DIM), lambda b, i: (b, i, q_blk)),
            pl.BlockSpec((1, T, 6 * NSA_KV_HEADS * HEAD_DIM), lambda b, i: (b, 0, kv_blk)),
            pl.BlockSpec((1, 3 * NSA_HEADS, TQ), lambda b, i: (b, 0, i)),
            pl.BlockSpec((NSA_BLOCK, 2), lambda b, i: (0, 0)),
            pl.BlockSpec((2, NSA_BLOCK, HEAD_DIM), lambda b, i: (0, 0, 0)),
        ],
        out_specs=pl.BlockSpec((1, TQ, NSA_HEADS * HEAD_DIM), lambda b, i: (b, i, 0)),
        out_shape=jax.ShapeDtypeStruct((B, T, NSA_HEADS * HEAD_DIM), f32),
        scratch_shapes=[
            pltpu.VMEM((NSA_KV_HEADS, nb, HEAD_DIM), f32),
            pltpu.VMEM((NSA_KV_HEADS, nb, HEAD_DIM), f32),
            pltpu.VMEM((NSA_KV_HEADS, T, HEAD_DIM), bf16),
            pltpu.VMEM((NSA_KV_HEADS, HEAD_DIM, T), bf16),
            pltpu.VMEM((NSA_KV_HEADS, T, HEAD_DIM), bf16),
            pltpu.VMEM((NSA_KV_HEADS, HEAD_DIM, T), bf16),
            pltpu.VMEM((NSA_KV_HEADS, 3, TQ, NSA_GROUP * TQ), f32),
            pltpu.VMEM((NSA_KV_HEADS, nb, TQ), f32),
            pltpu.VMEM((NSA_KV_HEADS, nb, TQ), f32),
        ],
        compiler_params=pltpu.CompilerParams(dimension_semantics=("arbitrary", "arbitrary"),
                                             vmem_limit_bytes=VMEM_LIMIT),
        name="nsa_prompt",
    )(rel_bias, q, kv, gt, jnp.transpose(cmp_pool), cmp_pe)


PAGES_PER_STEP = 8
SEL_TILE = 2048
KVW = NSA_KV_HEADS * HEAD_DIM


def _nsa_sample_kernel(pt_ref, rb_ref, *refs):
    PPS = PAGES_PER_STEP
    pages = refs[:PPS]
    q_ref, kvn_ref, gt_ref, win_ref, poolw_ref, pefull_ref, o_ref, cmp_ref, ksel_ref, vsel_ref = refs[PPS:]
    s_id = pl.program_id(1)
    f32, bf16 = jnp.float32, jnp.bfloat16
    Tn = q_ref.shape[1]
    past = ksel_ref.shape[0]
    nbp = past // NSA_BLOCK
    Wb = win_ref.shape[1]
    bpp = PAGE_SIZE // NSA_BLOCK
    R = NSA_HEADS * Tn

    rows = []
    for k in range(PPS):
        x = pages[k][0, 0, :, 0:2 * KVW].reshape(bpp, NSA_BLOCK, 2 * KVW)
        rows.append(jnp.sum((x + pefull_ref[...][None]) * poolw_ref[...][None], axis=1))
        r = pl.multiple_of((s_id * PPS + k) * PAGE_SIZE, PAGE_SIZE)
        ksel_ref[pl.ds(r, PAGE_SIZE), :] = pages[k][0, 0, :, 2 * KVW:3 * KVW].astype(bf16)
        vsel_ref[pl.ds(r, PAGE_SIZE), :] = pages[k][0, 0, :, 3 * KVW:4 * KVW].astype(bf16)
    cmp_ref[pl.ds(pl.multiple_of(s_id * PPS * bpp, PPS * bpp), PPS * bpp), :] = jnp.concatenate(rows, axis=0)

    @pl.when(s_id == pl.num_programs(1) - 1)
    def _attend():
        def per_head(fn):
            return jnp.concatenate([fn(h) for h in range(NSA_HEADS)], axis=0)

        q = q_ref[0] * HEAD_DIM ** -0.5
        zero = jnp.zeros((Tn, HEAD_DIM), f32)

        def q_rows(h):
            qh = q[:, h * HEAD_DIM:(h + 1) * HEAD_DIM]
            return jnp.concatenate([qh, zero] if h < NSA_GROUP else [zero, qh], axis=1)

        q2 = per_head(q_rows).astype(bf16)
        kvn = kvn_ref[0]
        pad = jnp.zeros((LANES - Tn, KVW), f32)

        def new_rows(c):
            return jnp.concatenate([kvn[:, c * KVW:(c + 1) * KVW], pad], axis=0).astype(bf16)

        tn_io = lax.broadcasted_iota(jnp.int32, (Tn, LANES), 0)
        jn_io = lax.broadcasted_iota(jnp.int32, (Tn, LANES), 1)
        d_new = tn_io - jn_io
        keep_new = per_head(lambda h: (d_new >= 0) & (jn_io < Tn))
        bias_new = per_head(lambda h: _bias_from_dist(d_new, rb_ref, h))
        far = per_head(lambda h: jnp.full((Tn, 1), rb_ref[NUM_BUCKETS - 1, h], f32))

        kc = cmp_ref[:, 0:KVW].astype(bf16)
        vc = cmp_ref[:, KVW:2 * KVW].astype(bf16)
        n_io = lax.broadcasted_iota(jnp.int32, (Tn, nbp), 1)
        t_io = lax.broadcasted_iota(jnp.int32, (Tn, nbp), 0)
        dist_c = past + t_io - (n_io * NSA_BLOCK + NSA_BLOCK - 1)
        sc = _dot_nt(q2, kc) + per_head(lambda h: _bias_from_dist(dist_c, rb_ref, h))
        e = jnp.exp(sc - jnp.max(sc, axis=1, keepdims=True))
        p_c = e / jnp.sum(e, axis=1, keepdims=True)
        oc = jnp.dot(p_c.astype(bf16), vc, preferred_element_type=f32)

        topk = min(NSA_TOPK, nbp + 1)
        m_io = lax.broadcasted_iota(jnp.int32, (nbp, nbp), 0)
        c_io = lax.broadcasted_iota(jnp.int32, (nbp, nbp), 1)
        lower = m_io < c_io
        forced = (n_io == 0) | (n_io == nbp - 1)
        sels = []
        for kvg in range(NSA_KV_HEADS):
            r0 = kvg * NSA_GROUP * Tn
            score = jnp.where(forced, 2.0, p_c[r0:r0 + Tn] + p_c[r0 + Tn:r0 + 2 * Tn])
            score_t = jnp.concatenate([score, jnp.zeros((LANES - Tn, nbp), f32)], axis=0).T
            ranks = []
            for t in range(Tn):
                colb = jnp.broadcast_to(score_t[:, t:t + 1], (nbp, nbp))
                rowb = jnp.broadcast_to(score[t:t + 1, :], (nbp, nbp))
                beats = (colb > rowb) | ((colb == rowb) & lower)
                ranks.append(jnp.sum(beats.astype(f32), axis=0, keepdims=True))
            sel = (jnp.concatenate(ranks, axis=0) < topk - 1).astype(f32)
            sels += [sel] * NSA_GROUP
        sel_rows = jnp.concatenate(sels, axis=0).astype(bf16)

        bpt = SEL_TILE // NSA_BLOCK
        expand = (lax.broadcasted_iota(jnp.int32, (bpt, SEL_TILE), 0)
                  == lax.broadcasted_iota(jnp.int32, (bpt, SEL_TILE), 1) // NSA_BLOCK).astype(bf16)
        d_last = LANES + tn_io - jn_io
        near = per_head(lambda h: _bias_from_dist(d_last, rb_ref, h))
        m = jnp.full((R, 1), NEG_INF, f32)
        l = jnp.zeros((R, 1), f32)
        acc = jnp.zeros((R, KVW), f32)

        def flash(carry, s, v_t):
            m, l, acc = carry
            m_new = jnp.maximum(m, jnp.max(s, axis=1, keepdims=True))
            p = jnp.exp(s - m_new)
            alpha = jnp.exp(m - m_new)
            return (m_new, alpha * l + jnp.sum(p, axis=1, keepdims=True),
                    alpha * acc + jnp.dot(p.astype(bf16), v_t, preferred_element_type=f32))

        carry = (m, l, acc)
        ntile = past // SEL_TILE
        for j in range(ntile):
            k_t = ksel_ref[j * SEL_TILE:(j + 1) * SEL_TILE, :]
            v_t = vsel_ref[j * SEL_TILE:(j + 1) * SEL_TILE, :]
            keep = jnp.dot(sel_rows[:, j * bpt:(j + 1) * bpt], expand, preferred_element_type=f32) > 0.5
            if j == ntile - 1:
                bias = jnp.concatenate([jnp.broadcast_to(far, (R, SEL_TILE - LANES)), near], axis=1)
            else:
                bias = far
            carry = flash(carry, jnp.where(keep, _dot_nt(q2, k_t) + bias, NEG_INF), v_t)
        s_new = jnp.where(keep_new, _dot_nt(q2, new_rows(2)) + bias_new, NEG_INF)
        m, l, acc = flash(carry, s_new, new_rows(3))
        o_s = acc / l

        win = win_ref[0]
        tw_io = lax.broadcasted_iota(jnp.int32, (Tn, Wb), 0)
        cw_io = lax.broadcasted_iota(jnp.int32, (Tn, Wb), 1)
        d_w = Wb + tw_io - cw_io
        s_w = jnp.where(per_head(lambda h: d_w < NSA_WINDOW),
                        _dot_nt(q2, win[:, 0:KVW].astype(bf16)) + per_head(lambda h: _bias_from_dist(d_w, rb_ref, h)),
                        NEG_INF)
        s_wn = jnp.where(keep_new, _dot_nt(q2, new_rows(4)) + bias_new, NEG_INF)
        s_all = jnp.concatenate([s_w, s_wn], axis=1)
        e = jnp.exp(s_all - jnp.max(s_all, axis=1, keepdims=True))
        p_w = (e / jnp.sum(e, axis=1, keepdims=True)).astype(bf16)
        v_all = jnp.concatenate([win[:, KVW:2 * KVW].astype(bf16), new_rows(5)], axis=0)
        o_w = jnp.dot(p_w, v_all, preferred_element_type=f32)

        gates = jax.nn.sigmoid(gt_ref[0])
        outs = []
        for h in range(NSA_HEADS):
            rs = slice(h * Tn, (h + 1) * Tn)
            cs = slice((h // NSA_GROUP) * HEAD_DIM, (h // NSA_GROUP + 1) * HEAD_DIM)
            outs.append(gates[:, h:h + 1] * oc[rs, cs] + gates[:, NSA_HEADS + h:NSA_HEADS + h + 1] * o_s[rs, cs]
                        + gates[:, 2 * NSA_HEADS + h:2 * NSA_HEADS + h + 1] * o_w[rs, cs])
        o_ref[0] = jnp.concatenate(outs, axis=1)


def _nsa_sample(q, kv, gate_logits, cache, layer, page_table, win_buf, cmp_pool, cmp_pe, rel_bias):
    B, Tn, _ = q.shape
    npages = page_table.shape[1]
    past = npages * PAGE_SIZE
    Wb = win_buf.shape[1]
    PPS = PAGES_PER_STEP
    assert npages % PPS == 0 and past % SEL_TILE == 0 and Tn == SUBLANES and (past // NSA_BLOCK) % LANES == 0
    f32 = jnp.float32
    cache2 = cache.reshape(cache.shape[0], cache.shape[1], PAGE_SIZE, 4 * KVW)
    poolw = jnp.concatenate([jnp.broadcast_to(cmp_pool[c][:, None], (NSA_BLOCK, KVW)) for c in range(2)], axis=1)
    pefull = jnp.concatenate([cmp_pe[c] for c in range(2) for _ in range(NSA_KV_HEADS)], axis=1)

    def page_spec(k):
        return pl.BlockSpec((1, 1, PAGE_SIZE, 4 * KVW), lambda b, s, pt: (pt[b, s * PPS + k], layer, 0, 0))

    grid_spec = pltpu.PrefetchScalarGridSpec(
        num_scalar_prefetch=1,
        grid=(B, npages // PPS),
        in_specs=[pl.BlockSpec(memory_space=pltpu.SMEM)] + [page_spec(k) for k in range(PPS)] + [
            pl.BlockSpec((1, Tn, NSA_HEADS * HEAD_DIM), lambda b, s, pt: (b, 0, 0)),
            pl.BlockSpec((1, Tn, 6 * KVW), lambda b, s, pt: (b, 0, 0)),
            pl.BlockSpec((1, Tn, 3 * NSA_HEADS), lambda b, s, pt: (b, 0, 0)),
            pl.BlockSpec((1, Wb, 2 * KVW), lambda b, s, pt: (b, 0, 0)),
            pl.BlockSpec((NSA_BLOCK, 2 * KVW), lambda b, s, pt: (0, 0)),
            pl.BlockSpec((NSA_BLOCK, 2 * KVW), lambda b, s, pt: (0, 0)),
        ],
        out_specs=pl.BlockSpec((1, Tn, NSA_HEADS * HEAD_DIM), lambda b, s, pt: (b, 0, 0)),
        scratch_shapes=[
            pltpu.VMEM((past // NSA_BLOCK, 2 * KVW), f32),
            pltpu.VMEM((past, KVW), jnp.bfloat16),
            pltpu.VMEM((past, KVW), jnp.bfloat16),
        ],
    )
    return pl.pallas_call(
        _nsa_sample_kernel,
        grid_spec=grid_spec,
        out_shape=jax.ShapeDtypeStruct((B, Tn, NSA_HEADS * HEAD_DIM), f32),
        compiler_params=pltpu.CompilerParams(dimension_semantics=("arbitrary", "arbitrary"),
                                             vmem_limit_bytes=VMEM_LIMIT),
        name="nsa_sample",
    )(page_table, rel_bias, *([cache2] * PPS), q, kv, gate_logits, win_buf.reshape(B, Wb, 2 * KVW), poolw, pefull)


def nsa_attention(q, kv_new, gate_logits, past_rows, win_buf, pos0, cmp_pool, cmp_pe, rel_bias):
    B, T = q.shape[:2]
    if past_rows is None:
        o = _nsa_prompt(q.reshape(B, T, -1), 0, kv_new.reshape(B, T, -1), 0, gate_logits.reshape(B, T, -1),
                        cmp_pool, cmp_pe, rel_bias)
        return o, kv_new[:, T - min(NSA_WINDOW, T):, 4:]
    if isinstance(past_rows, tuple):
        cache, layer, page_table = past_rows
        o = _nsa_sample(q.reshape(B, T, -1), kv_new.reshape(B, T, -1), gate_logits.reshape(B, T, -1),
                        cache, layer, page_table, win_buf, cmp_pool, cmp_pe, rel_bias)
        real = jnp.concatenate([win_buf, kv_new[:, :, 4:]], axis=1)
        return o, real[:, real.shape[1] - min(NSA_WINDOW, real.shape[1]):]
    dt = q.dtype
    qpos = pos0 + jnp.arange(T, dtype=jnp.int32)
    qg = (q * HEAD_DIM ** -0.5).reshape(B, T, NSA_KV_HEADS, NSA_GROUP, HEAD_DIM)
    rb = rel_bias.astype(jnp.float32).reshape(NUM_BUCKETS, NSA_KV_HEADS, NSA_GROUP)
    L = T if past_rows is None else past_rows.shape[1] + T
    nb = -(-L // NSA_BLOCK)
    new_rows = jnp.pad(kv_new[:, :, :4], ((0, 0), (0, nb * NSA_BLOCK - L), (0, 0), (0, 0), (0, 0)))
    full = new_rows if past_rows is None else jnp.concatenate([past_rows.astype(dt), new_rows], axis=1)
    blocks = full.reshape(B, nb, NSA_BLOCK, 4, NSA_KV_HEADS, HEAD_DIM)
    pe = jnp.transpose(cmp_pe, (1, 0, 2))[:, :, None, :].astype(dt)
    cmp = jnp.einsum('bnjckd,cj->bnckd', blocks[:, :, :, :2] + pe, cmp_pool.astype(dt))
    k_c, v_c = cmp[:, :, 0], cmp[:, :, 1]
    blk = jnp.arange(nb, dtype=jnp.int32)
    d_c = qpos[:, None] - (blk * NSA_BLOCK + NSA_BLOCK - 1)[None, :]
    s_c = jnp.einsum('btkgd,bnkd->bkgtn', qg, k_c).astype(jnp.float32) + jnp.transpose(rb[t5_bucket(d_c)], (2, 3, 0, 1))
    p_c = masked_softmax(s_c, d_c >= 0)
    o_c = jnp.einsum('bkgtn,bnkd->btkgd', p_c.astype(dt), v_c)
    cur = (qpos // NSA_BLOCK)[:, None]
    forced = (blk[None] == 0) | (blk[None] == cur) | (blk[None] == cur - 1)
    score = jnp.where(blk[None] <= cur, jnp.where(forced, 2.0, p_c.sum(axis=2)), -1.0)
    top_s, idx = lax.top_k(score, min(NSA_TOPK, nb))
    ok = top_s > -0.5
    ks = jnp.transpose(blocks[:, :, :, 2], (0, 3, 1, 2, 4))
    vs = jnp.transpose(blocks[:, :, :, 3], (0, 3, 1, 2, 4))
    take = jax.vmap(jax.vmap(lambda a, i: a[i]))
    kv_ix = jnp.arange(NSA_KV_HEADS)[None, :, None, None, None]
    rbk = jnp.transpose(rb, (1, 0, 2))

    def sel_block(args):
        qb, ib, okb, pb = args
        Bq, Q = qb.shape[:2]
        kg, vg = take(ks, ib), take(vs, ib)
        kpos = ib[..., None] * NSA_BLOCK + jnp.arange(NSA_BLOCK, dtype=jnp.int32)
        dist = pb[None, None, :, None, None] - kpos
        mask = (okb[..., None] & (dist >= 0)).reshape(Bq, NSA_KV_HEADS, 1, Q, -1)
        bias = jnp.moveaxis(rbk[kv_ix, t5_bucket(dist)], -1, 2)
        s = jnp.einsum('bqkgd,bkqsjd->bkgqsj', qb, kg).astype(jnp.float32) + bias
        p = masked_softmax(s.reshape(Bq, NSA_KV_HEADS, NSA_GROUP, Q, -1), mask)
        return jnp.einsum('bkgqn,bkqnd->bqkgd', p.astype(dt), vg.reshape(Bq, NSA_KV_HEADS, Q, -1, HEAD_DIM))

    o_s = over_query_blocks(sel_block, (qg, idx, ok, qpos), (1, 2, 2, 0), T)
    kw = kv_new[:, :, 4:]
    if win_buf is None:
        real = kw
        k_all = jnp.pad(kw, ((0, 0), (NSA_WINDOW, 0), (0, 0), (0, 0), (0, 0)))
        span = NSA_WINDOW
    else:
        real = jnp.concatenate([win_buf.astype(dt), kw], axis=1)
        k_all = real
        span = win_buf.shape[1]
    k0 = pos0 - span

    def win_block(args):
        qb, pb = args
        Q = qb.shape[1]
        start = pb[0] - pos0
        kb = lax.dynamic_slice_in_dim(k_all, start, span + Q, axis=1)
        kpos = k0 + start + jnp.arange(span + Q, dtype=jnp.int32)
        dist = pb[:, None] - kpos[None, :]
        mask = (dist >= 0) & (dist < NSA_WINDOW) & (kpos >= 0)[None, :]
        s = jnp.einsum('bqkgd,bnkd->bkgqn', qb, kb[:, :, 0]).astype(jnp.float32) + jnp.transpose(rb[t5_bucket(dist)], (2, 3, 0, 1))
        p = masked_softmax(s, mask)
        return jnp.einsum('bkgqn,bnkd->bqkgd', p.astype(dt), kb[:, :, 1])

    o_w = over_query_blocks(win_block, (qg, qpos), (1, 0), T)
    new_win = real[:, real.shape[1] - min(NSA_WINDOW, real.shape[1]):]
    g = jax.nn.sigmoid(gate_logits.astype(jnp.float32)).astype(dt).reshape(B, T, 3, NSA_KV_HEADS, NSA_GROUP, 1)
    o = g[:, :, 0] * o_c + g[:, :, 1] * o_s + g[:, :, 2] * o_w
    return o.reshape(B, T, NSA_HEADS * HEAD_DIM), new_win


def conformer_conv(u, buf, dw, dw_b, ln_g, ln_b):
    a, gte = jnp.split(u, 2, axis=-1)
    y, new_buf = causal_dwconv(a * jax.nn.sigmoid(gte), buf, dw)
    y = layer_norm(y + dw_b.astype(y.dtype), ln_g, ln_b)
    return jax.nn.silu(y), new_buf


def retention(q, k, v, gate, S0, pos0, gn):
    B, T, _ = q.shape
    dt = q.dtype
    H, d = RET_HEADS, HEAD_DIM
    pos = pos0 + jnp.arange(T, dtype=jnp.int32)
    f = lambda a: a.astype(jnp.float32).reshape(B, T, H, d)
    q = rotary(f(q), pos)
    k = rotary(f(k), pos) * d ** -0.5
    v = f(v)
    C = math.gcd(T, RET_CHUNK)
    n = T // C
    lg = jnp.log1p(-jnp.exp2(-5.0 - jnp.arange(H, dtype=jnp.float32)))
    ch = lambda a: a.reshape(B, n, C, H, d).transpose(0, 3, 1, 2, 4)
    qc, kc, vc = ch(q), ch(k), ch(v)
    i = jnp.arange(C, dtype=jnp.float32)
    diff = i[:, None] - i[None, :]
    Dm = jnp.where(diff >= 0, jnp.exp(jnp.maximum(diff, 0.0)[None] * lg[:, None, None]), 0.0)
    att = jnp.einsum('bhncd,bhnsd->bhncs', qc, kc) * Dm[None, :, None]
    o = jnp.einsum('bhncs,bhnse->bhnce', att, vc)
    xi = jnp.exp((i[None, :] + 1.0) * lg[:, None])
    zeta = jnp.exp((C - 1.0 - i[None, :]) * lg[:, None])
    kv = jnp.einsum('bhncd,bhnce->nbhde', kc * zeta[None, :, None, :, None], vc)
    decay = jnp.exp(C * lg)[None, :, None, None]

    def step(S, kvn):
        return S * decay + kvn, S

    S_fin, S_prev = lax.scan(step, S0.astype(jnp.float32), kv)
    o = o + jnp.einsum('bhncd,nbhde->bhnce', qc * xi[None, :, None, :, None], S_prev)
    o = o.transpose(0, 2, 3, 1, 4).reshape(B, T, H, d)
    mu = jnp.mean(o, axis=-1, keepdims=True)
    var = jnp.mean(jnp.square(o - mu), axis=-1, keepdims=True)
    o = ((o - mu) * lax.rsqrt(var + EPS)).reshape(B, T, H * d) * gn.astype(jnp.float32)
    return (o * jax.nn.silu(gate.astype(jnp.float32))).astype(dt), S_fin


def gated_deltanet(qkv, z, a, b, conv_buf, S0, conv_w, A_log, dt_bias, norm_g):
    B, T, _ = qkv.shape
    dt = qkv.dtype
    H, d = GDN_HEADS, HEAD_DIM
    y, new_buf = causal_dwconv(qkv, conv_buf, conv_w)
    y = jax.nn.silu(y.astype(jnp.float32))
    q, k, v = [t.reshape(B, T, H, d) for t in jnp.split(y, 3, axis=-1)]
    l2 = lambda t: t * lax.rsqrt(jnp.sum(t * t, axis=-1, keepdims=True) + EPS)
    q = l2(q) * d ** -0.5
    k = l2(k)
    g = -jnp.exp(A_log.astype(jnp.float32)) * jax.nn.softplus(a.astype(jnp.float32) + dt_bias.astype(jnp.float32))
    beta = jax.nn.sigmoid(b.astype(jnp.float32))
    C = math.gcd(T, GDN_CHUNK)
    n = T // C
    ch = lambda t: t.reshape(B, n, C, H, d).transpose(0, 3, 1, 2, 4)
    chs = lambda t: t.reshape(B, n, C, H).transpose(0, 3, 1, 2)
    qc, kc, vc = ch(q), ch(k), ch(v)
    gc = jnp.cumsum(chs(g), axis=-1)
    bc = chs(beta)
    ii = jnp.arange(C)
    tri = ii[:, None] >= ii[None, :]
    strict = ii[:, None] > ii[None, :]
    Lm = jnp.exp(jnp.where(tri, gc[..., :, None] - gc[..., None, :], -jnp.inf))
    kb = kc * bc[..., None]
    M = jnp.where(strict, jnp.einsum('bhncd,bhnsd->bhncs', kb, kc) * Lm, 0.0)
    A = M + jnp.eye(C, dtype=jnp.float32)
    rhs = jnp.concatenate([vc * bc[..., None], kb * jnp.exp(gc)[..., None]], axis=-1)
    sol = lax.linalg.triangular_solve(A, rhs, left_side=True, lower=True, unit_diagonal=True)
    u, w = sol[..., :d], sol[..., d:]
    att = jnp.einsum('bhncd,bhnsd->bhncs', qc, kc) * Lm
    qd = qc * jnp.exp(gc)[..., None]
    kd = kc * jnp.exp(gc[..., -1:] - gc)[..., None]
    glast = jnp.exp(gc[..., -1])
    xs = tuple(jnp.moveaxis(t, 2, 0) for t in (u, w, att, qd, kd, glast))

    def step(S, xs_n):
        u_, w_, at_, qd_, kd_, gl_ = xs_n
        vn = u_ - jnp.einsum('bhcd,bhde->bhce', w_, S)
        o_ = jnp.einsum('bhcd,bhde->bhce', qd_, S) + jnp.einsum('bhcs,bhse->bhce', at_, vn)
        S = S * gl_[..., None, None] + jnp.einsum('bhcd,bhce->bhde', kd_, vn)
        return S, o_

    S_fin, o = lax.scan(step, S0.astype(jnp.float32), xs)
    o = o.transpose(1, 0, 3, 2, 4).reshape(B, T, H, d)
    o = o * lax.rsqrt(jnp.mean(o * o, axis=-1, keepdims=True) + EPS) * norm_g.astype(jnp.float32)
    o = o * jax.nn.silu(z.astype(jnp.float32).reshape(B, T, H, d))
    return o.reshape(B, T, H * d).astype(dt), new_buf, S_fin


def conv_ffn(h, buf, w_up, dw, w_down):
    Bh, Th, _ = h.shape
    gpre, val = jnp.split(_pmm(h.reshape(Bh * Th, D_MODEL), w_up).reshape(Bh, Th, 2 * D_FF), 2, axis=-1)
    gconv, new_buf = causal_dwconv(gpre, buf, dw)
    return _pmm((jax.nn.gelu(gconv) * val).reshape(Bh * Th, D_FF), w_down).reshape(Bh, Th, D_MODEL), new_buf


PROJ_KV, PROJ_Q, PROJ_UCV, PROJ_RET, PROJ_GQKV, PROJ_GZ, PROJ_MISC = 0, 768, 1024, 1536, 2560, 3328, 3584
PROJ_COLS = PROJ_MISC + LANES
PROJ_TM = 512
MERGE_TM = 256
FFN_TM = 256


def _prep_w_in(w_in):
    o = [int(v) for v in np.cumsum((0,) + IN_SPLITS)]
    small = o[5] + 4 * GDN_W
    pieces = [w_in[:, o[1]:o[2]], w_in[:, o[0]:o[1]], w_in[:, o[3]:o[4]], w_in[:, o[4]:o[5]], w_in[:, o[5]:small],
              w_in[:, o[2]:o[3]], w_in[:, small:o[6]]]
    used = sum(pc.shape[1] for pc in pieces)
    pieces.append(jnp.zeros((D_MODEL, PROJ_COLS - used), w_in.dtype))
    return jnp.concatenate(pieces, axis=1).astype(jnp.bfloat16), w_in[:, o[6]:].astype(jnp.bfloat16)


def _modulated_norm(x, g, sc, sh):
    y = x * lax.rsqrt(jnp.mean(x * x, axis=-1, keepdims=True) + EPS)
    return (y * g) * (1.0 + sc) + sh


def _resident(shape):
    return pl.BlockSpec(shape, lambda *_: (0,) * len(shape), pipeline_mode=pl.Buffered(1))


def _mod_spec(mod, rows_per_group, tm):
    if mod.shape[1] == 1:
        return pl.BlockSpec((1, 1, D_MODEL), lambda i, *_: (i // (rows_per_group // tm), 0, 0))
    return pl.BlockSpec((1, tm, D_MODEL), lambda i, *_: (0, i, 0))


def _proj_kernel(x_ref, g_ref, sc_ref, sh_ref, w_ref, o_ref):
    h = _modulated_norm(x_ref[...], g_ref[...], sc_ref[0], sh_ref[0]).astype(jnp.bfloat16)
    o_ref[...] = jnp.dot(h, w_ref[...], preferred_element_type=jnp.float32)


def _proj(x2, g, sc, sh, w, rows_per_group, tm):
    M = x2.shape[0]
    N = w.shape[1]
    return pl.pallas_call(
        _proj_kernel,
        grid=(M // tm,),
        in_specs=[pl.BlockSpec((tm, D_MODEL), lambda i: (i, 0)), _resident((1, D_MODEL)),
                  _mod_spec(sc, rows_per_group, tm), _mod_spec(sh, rows_per_group, tm), _resident((D_MODEL, N))],
        out_specs=pl.BlockSpec((tm, N), lambda i: (i, 0)),
        out_shape=jax.ShapeDtypeStruct((M, N), jnp.float32),
        compiler_params=pltpu.CompilerParams(dimension_semantics=("arbitrary",), vmem_limit_bytes=VMEM_LIMIT),
        name="in_proj",
    )(x2, g, sc, sh, w)


def _merge_kernel(x_ref, g0_ref, sc_ref, sh_ref, gt_ref, g1_ref, b0_ref, b1_ref, b2_ref, b3_ref,
                  wmg_ref, wbr_ref, wout_ref, o_ref):
    f32, bf16 = jnp.float32, jnp.bfloat16
    x = x_ref[...]
    h = _modulated_norm(x, g0_ref[...], sc_ref[0], sh_ref[0]).astype(bf16)
    acc = jnp.zeros(x.shape, f32)
    for n, b_ref in enumerate((b0_ref, b1_ref, b2_ref, b3_ref)):
        gate = jax.nn.sigmoid(jnp.dot(h, wmg_ref[:, n * D_MODEL:(n + 1) * D_MODEL], preferred_element_type=f32))
        acc = acc + gate * jnp.dot(b_ref[...].astype(bf16), wbr_ref[n], preferred_element_type=f32)
    mixed = jnp.dot(acc.astype(bf16), wout_ref[...], preferred_element_type=f32)
    y = mixed * lax.rsqrt(jnp.mean(mixed * mixed, axis=-1, keepdims=True) + EPS) * g1_ref[...]
    o_ref[...] = x + gt_ref[0] * y


def _merge(x2, g0, sc, sh, gt, g1, branches, wmg, wbr, wout, rows_per_group, tm):
    M = x2.shape[0]
    row = lambda w: pl.BlockSpec((tm, w), lambda i: (i, 0))
    mspec = _mod_spec(sc, rows_per_group, tm)
    return pl.pallas_call(
        _merge_kernel,
        grid=(M // tm,),
        in_specs=[row(D_MODEL), _resident((1, D_MODEL)), mspec, mspec, mspec, _resident((1, D_MODEL))]
                 + [row(BRANCH_W)] * N_BRANCH
                 + [_resident(wmg.shape), _resident(wbr.shape), _resident(wout.shape)],
        out_specs=row(D_MODEL),
        out_shape=jax.ShapeDtypeStruct((M, D_MODEL), jnp.float32),
        compiler_params=pltpu.CompilerParams(dimension_semantics=("arbitrary",), vmem_limit_bytes=VMEM_LIMIT),
        name="merge",
    )(x2, g0, sc, sh, gt, g1, *branches, wmg, wbr, wout)


def _ffn_kernel(x_ref, g2_ref, sc_ref, sh_ref, gt_ref, g3_ref, buf_ref, wup_ref, dw_ref, wdn_ref,
                o_ref, st_ref, gp_ref):
    f32, bf16 = jnp.float32, jnp.bfloat16
    tm = x_ref.shape[1]
    HALO = SUBLANES

    @pl.when(pl.program_id(1) == 0)
    def _():
        gp_ref[0:HALO, :] = buf_ref[0]

    x = x_ref[0]
    h = _modulated_norm(x, g2_ref[...], sc_ref[0], sh_ref[0]).astype(bf16)
    gp_ref[HALO:HALO + tm, :] = jnp.dot(h, wup_ref[:, 0:D_FF], preferred_element_type=f32)
    val = jnp.dot(h, wup_ref[:, D_FF:2 * D_FF], preferred_element_type=f32)
    gconv = (dw_ref[2:3, :] * gp_ref[HALO:HALO + tm, :] + dw_ref[1:2, :] * gp_ref[HALO - 1:HALO - 1 + tm, :]
             + dw_ref[0:1, :] * gp_ref[HALO - 2:HALO - 2 + tm, :])
    a = (jax.nn.gelu(gconv) * val).astype(bf16)
    f = jnp.dot(a, wdn_ref[...], preferred_element_type=f32)
    y = f * lax.rsqrt(jnp.mean(f * f, axis=-1, keepdims=True) + EPS) * g3_ref[...]
    o_ref[0] = x + gt_ref[0] * y
    tail = gp_ref[tm:tm + HALO, :]
    gp_ref[0:HALO, :] = tail
    st_ref[0] = tail


def _ffn(x3, g2, sc, sh, gt, g3, buf, wup, dw, wdn, tm):
    B, T, _ = x3.shape
    buf8 = jnp.pad(buf, ((0, 0), (SUBLANES - (FFN_CONV - 1), 0), (0, 0)))
    mspec = pl.BlockSpec((1, 1, D_MODEL), lambda b, i: (b, 0, 0))
    y, st = pl.pallas_call(
        _ffn_kernel,
        grid=(B, T // tm),
        in_specs=[pl.BlockSpec((1, tm, D_MODEL), lambda b, i: (b, i, 0)), _resident((1, D_MODEL)), mspec, mspec, mspec,
                  _resident((1, D_MODEL)), pl.BlockSpec((1, SUBLANES, D_FF), lambda b, i: (b, 0, 0)),
                  _resident(wup.shape), _resident(dw.shape), _resident(wdn.shape)],
        out_specs=[pl.BlockSpec((1, tm, D_MODEL), lambda b, i: (b, i, 0)),
                   pl.BlockSpec((1, SUBLANES, D_FF), lambda b, i: (b, 0, 0))],
        out_shape=[jax.ShapeDtypeStruct((B, T, D_MODEL), jnp.float32),
                   jax.ShapeDtypeStruct((B, SUBLANES, D_FF), jnp.float32)],
        scratch_shapes=[pltpu.VMEM((tm + SUBLANES, D_FF), jnp.float32)],
        compiler_params=pltpu.CompilerParams(dimension_semantics=("arbitrary", "arbitrary"),
                                             vmem_limit_bytes=VMEM_LIMIT),
        name="conv_ffn",
    )(x3, g2, sc, sh, gt, g3, buf8, wup, dw, wdn)
    return y, st[:, SUBLANES - (FFN_CONV - 1):]


def _prep_layer(p):
    w_main, w_mg = _prep_w_in(p["w_in"])
    q = dict(p)
    q.update(w_main=w_main, w_mg=w_mg, w_branch_b=p["w_branch"].astype(jnp.bfloat16),
             w_out_b=p["w_out"].astype(jnp.bfloat16), ffn_up_b=p["ffn_up"].astype(jnp.bfloat16),
             ffn_down_b=p["ffn_down"].astype(jnp.bfloat16))
    return q


def trunk_layer(x, c, pos0, nsa_past, win_buf, conv_buf, ret_s, gdn_buf, gdn_s, ffn_buf, p, rel_bias):
    B, T, _ = x.shape
    M = B * T
    mod = _pmm(jax.nn.silu(c), p["w_ada"]) + p["b_ada"]
    per_row = T < MERGE_TM
    if per_row:
        sh1, sc1, gt1, sh2, sc2, gt2 = [jnp.repeat(m, T, axis=0)[None] for m in jnp.split(mod, 6, axis=-1)]
    else:
        sh1, sc1, gt1, sh2, sc2, gt2 = [m[:, None, :] for m in jnp.split(mod, 6, axis=-1)]
    norms = p["norms"][:, None, :]
    x2 = x.reshape(M, D_MODEL)
    P = _proj(x2, norms[0], sc1, sh1, p["w_main"], T, M if per_row else PROJ_TM).reshape(B, T, PROJ_COLS)
    nkv = P[:, :, PROJ_KV:PROJ_KV + 6 * KVW]
    ngt = P[:, :, PROJ_MISC:PROJ_MISC + 3 * NSA_HEADS]
    ga = P[:, :, PROJ_MISC + 3 * NSA_HEADS:PROJ_MISC + 3 * NSA_HEADS + GDN_HEADS]
    gb = P[:, :, PROJ_MISC + 3 * NSA_HEADS + GDN_HEADS:PROJ_MISC + 3 * NSA_HEADS + 2 * GDN_HEADS]
    kw = nkv[:, :, 4 * KVW:].reshape(B, T, 2, NSA_KV_HEADS, HEAD_DIM)
    if nsa_past is None:
        o_nsa = _nsa_prompt(P, PROJ_Q // (NSA_HEADS * HEAD_DIM), P, PROJ_KV // (6 * KVW), ngt,
                            p["cmp_pool"], p["cmp_pe"], rel_bias)
        new_win = kw[:, T - min(NSA_WINDOW, T):]
    else:
        cache, layer, page_table = nsa_past
        o_nsa = _nsa_sample(P[:, :, PROJ_Q:PROJ_Q + NSA_HEADS * HEAD_DIM], nkv, ngt, cache, layer, page_table,
                            win_buf, p["cmp_pool"], p["cmp_pe"], rel_bias)
        real = jnp.concatenate([win_buf, kw], axis=1)
        new_win = real[:, real.shape[1] - min(NSA_WINDOW, real.shape[1]):]
    o_conv, new_conv = conformer_conv(P[:, :, PROJ_UCV:PROJ_UCV + 2 * CONV_CH], conv_buf, p["conv_dw"],
                                      p["conv_dw_b"], p["conv_ln_g"], p["conv_ln_b"])
    rq, rk, rv, rg = [P[:, :, PROJ_RET + i * RET_W:PROJ_RET + (i + 1) * RET_W] for i in range(4)]
    o_ret, new_ret = retention(rq, rk, rv, rg, ret_s, pos0, p["ret_gn"])
    o_gdn, new_gdn_buf, new_gdn = gated_deltanet(P[:, :, PROJ_GQKV:PROJ_GQKV + 3 * GDN_W],
                                                 P[:, :, PROJ_GZ:PROJ_GZ + GDN_W], ga, gb, gdn_buf, gdn_s,
                                                 p["gdn_conv_w"], p["gdn_A_log"], p["gdn_dt_bias"], p["gdn_norm"])
    branches = [o.reshape(M, BRANCH_W) for o in (o_nsa, o_conv, o_ret, o_gdn)]
    x1 = _merge(x2, norms[0], sc1, sh1, gt1, norms[1], branches, p["w_mg"], p["w_branch_b"], p["w_out_b"],
                T, M if per_row else MERGE_TM).reshape(B, T, D_MODEL)
    if per_row:
        s2, h2, g2 = sc2.reshape(B, T, D_MODEL), sh2.reshape(B, T, D_MODEL), gt2.reshape(B, T, D_MODEL)
        f, new_ffn = conv_ffn(rms_norm(x1, p["norms"][2]) * (1.0 + s2) + h2, ffn_buf, p["ffn_up"], p["ffn_dw"],
                              p["ffn_down"])
        x_out = x1 + g2 * rms_norm(f, p["norms"][3])
    else:
        x_out, new_ffn = _ffn(x1, norms[2], sc2, sh2, gt2, norms[3], ffn_buf, p["ffn_up_b"], p["ffn_dw"],
                              p["ffn_down_b"], FFN_TM)
    kv_rows = nkv[:, :, :4 * KVW].reshape(B, T, 4, NSA_KV_HEADS, HEAD_DIM)
    return x_out, (kv_rows, new_win, new_conv, new_ret, new_gdn_buf, new_gdn, new_ffn)


def kernel(x_prompt, x_sample, cache_nsa_kv, cache_nsa_win, state_conv, state_ret, state_gdn_conv, state_gdn,
           state_ffn_conv, page_table, c_prompt, c_sample, w_ada, b_ada, norms, w_in, cmp_pool, cmp_pe, rel_bias,
           conv_dw, conv_dw_b, conv_ln_g, conv_ln_b, ret_gn, gdn_conv_w, gdn_A_log, gdn_dt_bias, gdn_norm,
           w_branch, w_out, ffn_up, ffn_dw, ffn_down):
    B = x_prompt.shape[0]
    Bd = x_sample.shape[0]
    past = page_table.shape[1] * PAGE_SIZE
    layer_w = {"w_ada": w_ada, "b_ada": b_ada, "norms": norms, "w_in": w_in, "cmp_pool": cmp_pool,
               "cmp_pe": cmp_pe, "conv_dw": conv_dw, "conv_dw_b": conv_dw_b, "conv_ln_g": conv_ln_g,
               "conv_ln_b": conv_ln_b, "ret_gn": ret_gn, "gdn_conv_w": gdn_conv_w, "gdn_A_log": gdn_A_log,
               "gdn_dt_bias": gdn_dt_bias, "gdn_norm": gdn_norm, "w_branch": w_branch, "w_out": w_out,
               "ffn_up": ffn_up, "ffn_dw": ffn_dw, "ffn_down": ffn_down}
    yp, ys = x_prompt, x_sample
    st_p, st_s = [], []
    for l in range(DEPTH):
        p = _prep_layer({name: w[l] for name, w in layer_w.items()})
        yp, sp = trunk_layer(
            yp, c_prompt, 0, None, None,
            jnp.zeros((B, CONV_WIDTH - 1, CONV_CH), x_prompt.dtype),
            jnp.zeros((B, RET_HEADS, HEAD_DIM, HEAD_DIM), jnp.float32),
            jnp.zeros((B, GDN_CONV - 1, 3 * GDN_W), x_prompt.dtype),
            jnp.zeros((B, GDN_HEADS, HEAD_DIM, HEAD_DIM), jnp.float32),
            jnp.zeros((B, FFN_CONV - 1, D_FF), x_prompt.dtype),
            p, rel_bias)
        ys, ss = trunk_layer(
            ys, c_sample, past, (cache_nsa_kv, l, page_table), cache_nsa_win[l], state_conv[l], state_ret[l],
            state_gdn_conv[l], state_gdn[l], state_ffn_conv[l], p, rel_bias)
        st_p.append(sp)
        st_s.append(ss)

    def stack(outs, i, axis):
        return jnp.stack([o[i] for o in outs], axis=axis)

    kv_p, kv_s = stack(st_p, 0, 1), stack(st_s, 0, 1)
    win_p, win_s = stack(st_p, 1, 0), stack(st_s, 1, 0)
    conv_p, conv_s = stack(st_p, 2, 0), stack(st_s, 2, 0)
    ret_p, ret_s = stack(st_p, 3, 0), stack(st_s, 3, 0)
    gdnc_p, gdnc_s = stack(st_p, 4, 0), stack(st_s, 4, 0)
    gdn_p, gdn_s = stack(st_p, 5, 0), stack(st_s, 5, 0)
    ffn_p, ffn_s = stack(st_p, 6, 0), stack(st_s, 6, 0)
    return (yp, ys, kv_p, kv_s, win_p, win_s, conv_p, conv_s, ret_p, ret_s, gdnc_p, gdnc_s, gdn_p, gdn_s, ffn_p, ffn_s)
```

```python
import functools
import math

import jax
import jax.numpy as jnp
import numpy as np
from jax import lax
from jax.experimental import pallas as pl
from jax.experimental.pallas import tpu as pltpu

D_MODEL = 1024
BATCH = 4
SEQ = 4096
DEPTH = 2
DEC_BATCH = 32
DEC_SEQ = 8
PAST_LEN = 16384
PAGE_SIZE = 128

HEAD_DIM = 64
NSA_HEADS = 4
NSA_KV_HEADS = 2
NSA_GROUP = NSA_HEADS // NSA_KV_HEADS
NSA_BLOCK = 64
NSA_TOPK = 16
NSA_WINDOW = 512
Q_BLOCK = 128
NUM_BUCKETS = 32
MAX_DISTANCE = 128
CONV_CH = D_MODEL // 4
CONV_WIDTH = 31
RET_HEADS = 4
RET_W = RET_HEADS * HEAD_DIM
RET_CHUNK = 64
ROPE_BASE = 10000.0
GDN_HEADS = 4
GDN_W = GDN_HEADS * HEAD_DIM
GDN_CONV = 4
GDN_CHUNK = 64
D_FF = 2816
FFN_CONV = 3
N_BRANCH = 4
BRANCH_W = NSA_HEADS * HEAD_DIM
EPS = 1e-6
NEG_INF = -1e30
IN_SPLITS = (NSA_HEADS * HEAD_DIM, 6 * NSA_KV_HEADS * HEAD_DIM, 3 * NSA_HEADS, 2 * CONV_CH, 4 * RET_W,
             4 * GDN_W + 2 * GDN_HEADS, N_BRANCH * D_MODEL)
IN_COLS = sum(IN_SPLITS)

LANES = 128
SUBLANES = 8
VMEM_LIMIT = 56 * 1024 * 1024


def _round_up(a, m):
    return -(-a // m) * m


def _mm_kernel(x_ref, w_ref, o_ref):
    o_ref[...] = jnp.dot(x_ref[...].astype(jnp.bfloat16), w_ref[...].astype(jnp.bfloat16),
                         preferred_element_type=jnp.float32)


def _pmm(x, w):
    M, K = x.shape
    N = w.shape[1]
    Mp = _round_up(M, SUBLANES)
    tm = min(512, Mp)
    Mp = _round_up(Mp, tm)
    tn = 512 if N > 512 else _round_up(N, LANES)
    Np = _round_up(N, tn)
    if Mp != M:
        x = jnp.pad(x, ((0, Mp - M), (0, 0)))
    if Np != N:
        w = jnp.pad(w, ((0, 0), (0, Np - N)))
    out = pl.pallas_call(
        _mm_kernel,
        grid=(Mp // tm, Np // tn),
        in_specs=[pl.BlockSpec((tm, K), lambda i, j: (i, 0)), pl.BlockSpec((K, tn), lambda i, j: (0, j))],
        out_specs=pl.BlockSpec((tm, tn), lambda i, j: (i, j)),
        out_shape=jax.ShapeDtypeStruct((Mp, Np), jnp.float32),
        compiler_params=pltpu.CompilerParams(dimension_semantics=("arbitrary", "arbitrary"),
                                             vmem_limit_bytes=VMEM_LIMIT),
    )(x, w)
    return out[:M, :N]


def split_cols(a, sizes):
    return jnp.split(a, [int(i) for i in np.cumsum(sizes)[:-1]], axis=-1)


def rms_norm(x, g):
    x32 = x.astype(jnp.float32)
    y = x32 * lax.rsqrt(jnp.mean(x32 * x32, axis=-1, keepdims=True) + EPS)
    return (y * g.astype(jnp.float32)).astype(x.dtype)


def layer_norm(x, g, b):
    x32 = x.astype(jnp.float32)
    mu = jnp.mean(x32, axis=-1, keepdims=True)
    var = jnp.mean(jnp.square(x32 - mu), axis=-1, keepdims=True)
    return ((x32 - mu) * lax.rsqrt(var + EPS) * g.astype(jnp.float32) + b.astype(jnp.float32)).astype(x.dtype)


def causal_dwconv(x, buf, w):
    xp = jnp.concatenate([buf.astype(x.dtype), x], axis=1)
    y = lax.conv_general_dilated(xp, w[:, None, :].astype(x.dtype), window_strides=(1,), padding='VALID',
                                 dimension_numbers=('NWC', 'WIO', 'NWC'), feature_group_count=x.shape[-1])
    return y, xp[:, xp.shape[1] - (w.shape[0] - 1):]


def t5_bucket(dist):
    n = jnp.maximum(dist, 0)
    exact = NUM_BUCKETS // 2
    large = exact + (jnp.log(jnp.maximum(n, 1).astype(jnp.float32) / exact) / math.log(MAX_DISTANCE / exact)
                     * (NUM_BUCKETS - exact)).astype(jnp.int32)
    return jnp.where(n < exact, n, jnp.minimum(large, NUM_BUCKETS - 1))


def masked_softmax(s, mask):
    return jax.nn.softmax(jnp.where(mask, s, NEG_INF), axis=-1) * mask


def over_query_blocks(fn, args, q_axes, T):
    if T <= Q_BLOCK or T % Q_BLOCK:
        return fn(args)
    nq = T // Q_BLOCK

    def split(a, ax):
        return jnp.moveaxis(a.reshape(a.shape[:ax] + (nq, Q_BLOCK) + a.shape[ax + 1:]), ax, 0)

    out = lax.map(fn, tuple(split(a, ax) for a, ax in zip(args, q_axes)))
    out = jnp.moveaxis(out, 0, 1)
    return out.reshape((out.shape[0], T) + out.shape[3:])


def rotary(x, pos):
    half = x.shape[-1] // 2
    inv = ROPE_BASE ** (-jnp.arange(half, dtype=jnp.float32) / half)
    ang = pos.astype(jnp.float32)[:, None] * inv[None, :]
    cos, sin = jnp.cos(ang)[None, :, None, :], jnp.sin(ang)[None, :, None, :]
    x1, x2 = x[..., :half], x[..., half:]
    return jnp.concatenate([x1 * cos - x2 * sin, x1 * sin + x2 * cos], axis=-1)


NSA_TQ = 128
BLOCKS_PER_TILE = NSA_TQ // NSA_BLOCK
WIN_TILES = NSA_WINDOW // NSA_TQ


def _t5_thresholds():
    n = np.arange(0, 2 * MAX_DISTANCE)
    exact = NUM_BUCKETS // 2
    large = exact + (np.log(np.maximum(n, 1).astype(np.float32) / np.float32(exact))
                     / np.float32(math.log(MAX_DISTANCE / exact)) * (NUM_BUCKETS - exact)).astype(np.int32)
    bucket = np.where(n < exact, n, np.minimum(large, NUM_BUCKETS - 1))
    return tuple(int(np.argmax(bucket >= k)) for k in range(1, NUM_BUCKETS))


_T5_THR = _t5_thresholds()


def _bias_from_dist(dist, rb_ref, h):
    v = jnp.full(dist.shape, rb_ref[NUM_BUCKETS - 1, h], jnp.float32)
    for k in range(NUM_BUCKETS - 2, -1, -1):
        v = jnp.where(dist < _T5_THR[k], rb_ref[k, h], v)
    return v


def _dot_nt(a, b):
    return lax.dot_general(a, b, (((1,), (1,)), ((), ())), preferred_element_type=jnp.float32)


def _flash_tile(carry, k_t, vt_t, qs, bias, keep):
    m, l, acc = carry
    s = _dot_nt(k_t, qs) + bias
    if keep is not None:
        s = jnp.where(keep, s, NEG_INF)
    m_new = jnp.maximum(m, jnp.max(s, axis=0, keepdims=True))
    p = jnp.exp(s - m_new)
    alpha = jnp.exp(m - m_new)
    l = alpha * l + jnp.sum(p, axis=0, keepdims=True)
    acc = alpha * acc + jnp.dot(vt_t, p.astype(jnp.bfloat16), preferred_element_type=jnp.float32)
    return m_new, l, acc


def _nsa_prompt_kernel(rb_ref, q_ref, kv_ref, gt_ref, poolt_ref, pe_ref, o_ref,
                       kc_ref, vc_ref, ksel_ref, vselt_ref, kwin_ref, vwint_ref, tbl_ref, score_ref, sel_ref):
    b = pl.program_id(0)
    qi = pl.program_id(1)
    T = kv_ref.shape[1]
    nb = T // NSA_BLOCK
    topk = min(NSA_TOPK, nb)
    TQ = NSA_TQ
    CH = 512
    f32, bf16 = jnp.float32, jnp.bfloat16

    @pl.when((b == 0) & (qi == 0))
    def _tables():
        jj = lax.broadcasted_iota(jnp.int32, (TQ, TQ), 0)
        tt = lax.broadcasted_iota(jnp.int32, (TQ, TQ), 1)
        d0 = tt - jj
        for h in range(NSA_HEADS):
            kvg, g = divmod(h, NSA_GROUP)
            lanes = slice(g * TQ, (g + 1) * TQ)
            far = rb_ref[NUM_BUCKETS - 1, h]
            tbl_ref[kvg, 0, :, lanes] = jnp.where(d0 >= 0, _bias_from_dist(d0, rb_ref, h), NEG_INF)
            tbl_ref[kvg, 1, :, lanes] = _bias_from_dist(d0 + TQ, rb_ref, h)
            tbl_ref[kvg, 2, :, lanes] = jnp.where(d0 < 0, far, NEG_INF)

    @pl.when(qi == 0)
    def _prologue():
        def chunk(i, carry):
            r = pl.multiple_of(i * CH, CH)
            rb8 = pl.multiple_of(i * (CH // NSA_BLOCK), CH // NSA_BLOCK)
            for kvg in range(NSA_KV_HEADS):
                def col(c):
                    lo = c * NSA_KV_HEADS * HEAD_DIM + kvg * HEAD_DIM
                    return kv_ref[0, pl.ds(r, CH), lo:lo + HEAD_DIM]
                for c, dst in ((0, kc_ref), (1, vc_ref)):
                    x = col(c).reshape(CH // NSA_BLOCK, NSA_BLOCK, HEAD_DIM) + pe_ref[c][None]
                    dst[kvg, pl.ds(rb8, CH // NSA_BLOCK), :] = jnp.sum(x * poolt_ref[:, c:c + 1][None], axis=1)
                ksel_ref[kvg, pl.ds(r, CH), :] = col(2).astype(bf16)
                vselt_ref[kvg, :, pl.ds(r, CH)] = col(3).T.astype(bf16)
                kwin_ref[kvg, pl.ds(r, CH), :] = col(4).astype(bf16)
                vwint_ref[kvg, :, pl.ds(r, CH)] = col(5).T.astype(bf16)
            return carry
        lax.fori_loop(0, T // CH, chunk, 0)

    q = q_ref[0]
    gates = jax.nn.sigmoid(gt_ref[0])
    n_io = lax.broadcasted_iota(jnp.int32, (nb, TQ), 0)
    t_io = lax.broadcasted_iota(jnp.int32, (nb, TQ), 1)
    lane2 = lax.broadcasted_iota(jnp.int32, (1, 2 * TQ), 1)
    dist_c = qi * TQ + t_io - (n_io * NSA_BLOCK + NSA_BLOCK - 1)
    vis_c = dist_c >= 0
    vis_c2 = jnp.concatenate([vis_c, vis_c], axis=1)
    cur = (qi * TQ + t_io) // NSA_BLOCK
    forced = (n_io == 0) | (n_io == cur) | (n_io == cur - 1)
    q0 = pl.multiple_of(qi * TQ, TQ)
    outs = []
    for kvg in range(NSA_KV_HEADS):
        base = kvg * NSA_GROUP * HEAD_DIM
        qs = jnp.concatenate([q[:, base + g * HEAD_DIM: base + (g + 1) * HEAD_DIM] for g in range(NSA_GROUP)],
                             axis=0)
        qs = (qs * HEAD_DIM ** -0.5).astype(bf16)
        far_row = jnp.where(lane2 < TQ, rb_ref[NUM_BUCKETS - 1, kvg * NSA_GROUP],
                            rb_ref[NUM_BUCKETS - 1, kvg * NSA_GROUP + 1])

        sc = _dot_nt(kc_ref[kvg].astype(bf16), qs)
        bias_c = jnp.concatenate([_bias_from_dist(dist_c, rb_ref, kvg * NSA_GROUP + g) for g in range(NSA_GROUP)],
                                 axis=1)
        sc = jnp.where(vis_c2, sc + bias_c, NEG_INF)
        e = jnp.exp(sc - jnp.max(sc, axis=0, keepdims=True))
        p_c = e / jnp.sum(e, axis=0, keepdims=True) * vis_c2.astype(f32)
        oc = jnp.dot(vc_ref[kvg].T.astype(bf16), p_c.astype(bf16), preferred_element_type=f32)

        score = jnp.where(n_io <= cur, jnp.where(forced, 2.0, p_c[:, :TQ] + p_c[:, TQ:]), -1.0)
        score_ref[kvg] = score

        def rank_body(mi, rank):
            row = score_ref[kvg, pl.ds(mi, 1), :]
            beats = (row > score) | ((row == score) & (mi < n_io))
            return rank + beats.astype(jnp.int32)

        rank = lax.fori_loop(0, BLOCKS_PER_TILE * (qi + 1), rank_body, jnp.zeros((nb, TQ), jnp.int32))
        sel_ref[kvg] = ((rank < topk) & (n_io <= cur)).astype(f32)

        def sel_keep(j):
            rows = [jnp.broadcast_to(sel_ref[kvg, pl.ds(BLOCKS_PER_TILE * j + u, 1), :], (NSA_BLOCK, TQ))
                    for u in range(BLOCKS_PER_TILE)]
            mm = jnp.concatenate(rows, axis=0)
            return jnp.concatenate([mm, mm], axis=1) > 0.5

        def tile(kref, vtref, j):
            r = pl.multiple_of(j * TQ, TQ)
            return kref[kvg, pl.ds(r, TQ), :], vtref[kvg, :, pl.ds(r, TQ)]

        init = (jnp.full((1, 2 * TQ), NEG_INF, f32), jnp.zeros((1, 2 * TQ), f32), jnp.zeros((HEAD_DIM, 2 * TQ), f32))

        k_t, vt_t = tile(ksel_ref, vselt_ref, qi)
        carry = _flash_tile(init, k_t, vt_t, qs, tbl_ref[kvg, 0], sel_keep(qi))
        jp = jnp.maximum(qi - 1, 0)
        k_t, vt_t = tile(ksel_ref, vselt_ref, jp)
        carry = _flash_tile(carry, k_t, vt_t, qs, tbl_ref[kvg, 1], sel_keep(jp) & (qi >= 1))

        def far_body(i, c):
            j = qi - 2 - i
            k_f, vt_f = tile(ksel_ref, vselt_ref, j)
            return _flash_tile(c, k_f, vt_f, qs, far_row, sel_keep(j))

        m_s, l_s, acc_s = lax.fori_loop(0, jnp.maximum(qi - 1, 0), far_body, carry)

        k_t, vt_t = tile(kwin_ref, vwint_ref, qi)
        carry = _flash_tile(init, k_t, vt_t, qs, tbl_ref[kvg, 0], None)
        for back in range(1, WIN_TILES + 1):
            jb = jnp.maximum(qi - back, 0)
            k_t, vt_t = tile(kwin_ref, vwint_ref, jb)
            bias = tbl_ref[kvg, 1] if back == 1 else (tbl_ref[kvg, 2] if back == WIN_TILES else far_row)
            carry = _flash_tile(carry, k_t, vt_t, qs, bias, qi >= back)
        m_w, l_w, acc_w = carry

        o_s = acc_s / l_s
        o_w = acc_w / l_w
        for g in range(NSA_GROUP):
            h = kvg * NSA_GROUP + g
            lanes = slice(g * TQ, (g + 1) * TQ)
            o = (gates[h:h + 1] * oc[:, lanes] + gates[NSA_HEADS + h:NSA_HEADS + h + 1] * o_s[:, lanes]
                 + gates[2 * NSA_HEADS + h:2 * NSA_HEADS + h + 1] * o_w[:, lanes])
            outs.append(o.T)
    o_ref[0] = jnp.concatenate(outs, axis=1)


def _nsa_prompt(q, q_blk, kv, kv_blk, gate_logits, cmp_pool, cmp_pe, rel_bias):
    B, T, _ = q.shape
    nb = T // NSA_BLOCK
    assert T % 512 == 0 and nb % SUBLANES == 0
    TQ = NSA_TQ
    gt = jnp.transpose(gate_logits, (0, 2, 1))
    f32, bf16 = jnp.float32, jnp.bfloat16
    return pl.pallas_call(
        _nsa_prompt_kernel,
        grid=(B, T // TQ),
        in_specs=[
            pl.BlockSpec(memory_space=pltpu.SMEM),
            pl.BlockSpec((1, TQ, NSA_HEADS * HEAD_DIM), lambda b, i: (b, i, q_blk)),
            pl.BlockSpec((1, T, 6 * NSA_KV_HEADS * HEAD_DIM), lambda b, i: (b, 0, kv_blk)),
            pl.BlockSpec((1, 3 * NSA_HEADS, TQ), lambda b, i: (b, 0, i)),
            pl.BlockSpec((NSA_BLOCK, 2), lambda b, i: (0, 0)),
            pl.BlockSpec((2, NSA_BLOCK, HEAD_DIM), lambda b, i: (0, 0, 0)),
        ],
        out_specs=pl.BlockSpec((1, TQ, NSA_HEADS * HEAD_DIM), lambda b, i: (b, i, 0)),
        out_shape=jax.ShapeDtypeStruct((B, T, NSA_HEADS * HEAD_DIM), f32),
        scratch_shapes=[
            pltpu.VMEM((NSA_KV_HEADS, nb, HEAD_DIM), f32),
            pltpu.VMEM((NSA_KV_HEADS, nb, HEAD_DIM), f32),
            pltpu.VMEM((NSA_KV_HEADS, T, HEAD_DIM), bf16),
            pltpu.VMEM((NSA_KV_HEADS, HEAD_DIM, T), bf16),
            pltpu.VMEM((NSA_KV_HEADS, T, HEAD_DIM), bf16),
            pltpu.VMEM((NSA_KV_HEADS, HEAD_DIM, T), bf16),
            pltpu.VMEM((NSA_KV_HEADS, 3, TQ, NSA_GROUP * TQ), f32),
            pltpu.VMEM((NSA_KV_HEADS, nb, TQ), f32),
            pltpu.VMEM((NSA_KV_HEADS, nb, TQ), f32),
        ],
        compiler_params=pltpu.CompilerParams(dimension_semantics=("arbitrary", "arbitrary"),
                                             vmem_limit_bytes=VMEM_LIMIT),
        name="nsa_prompt",
    )(rel_bias, q, kv, gt, jnp.transpose(cmp_pool), cmp_pe)


PAGES_PER_STEP = 8
SEL_TILE = 2048
KVW = NSA_KV_HEADS * HEAD_DIM


def _nsa_sample_kernel(pt_ref, rb_ref, *refs):
    PPS = PAGES_PER_STEP
    pages = refs[:PPS]
    q_ref, kvn_ref, gt_ref, win_ref, poolm_ref, cconst_ref, o_ref, cmp_ref, kselt_ref, vselt_ref = refs[PPS:]
    s_id = pl.program_id(1)
    f32, bf16 = jnp.float32, jnp.bfloat16
    Tn = q_ref.shape[1]
    past = kselt_ref.shape[1]
    nbp = past // NSA_BLOCK
    Wb = win_ref.shape[3]
    bpp = PAGE_SIZE // NSA_BLOCK
    R = NSA_HEADS * Tn

    rows = []
    for k in range(PPS):
        parts = []
        for c in range(2):
            x = pages[k][0, 0, c]
            x_hi = x.astype(bf16)
            x_lo = (x - x_hi.astype(f32)).astype(bf16)
            a = _dot_nt(poolm_ref[c], x_hi)
            b = _dot_nt(poolm_ref[c, 0:SUBLANES], x_lo)
            parts.append(a[0:bpp] + a[SUBLANES:SUBLANES + bpp] + b[0:bpp] + cconst_ref[c:c + 1])
        rows.append(jnp.concatenate(parts, axis=1))
        r = pl.multiple_of((s_id * PPS + k) * PAGE_SIZE, PAGE_SIZE)
        kselt_ref[:, pl.ds(r, PAGE_SIZE)] = pages[k][0, 0, 2].astype(bf16)
        vselt_ref[:, pl.ds(r, PAGE_SIZE)] = pages[k][0, 0, 3].astype(bf16)
    cmp_ref[pl.ds(pl.multiple_of(s_id * PPS * bpp, PPS * bpp), PPS * bpp), :] = jnp.concatenate(rows, axis=0)

    @pl.when(s_id == pl.num_programs(1) - 1)
    def _attend():
        def per_head(fn):
            return jnp.concatenate([fn(h) for h in range(NSA_HEADS)], axis=0)

        q = q_ref[0] * HEAD_DIM ** -0.5
        zero = jnp.zeros((Tn, HEAD_DIM), f32)

        def q_rows(h):
            qh = q[:, h * HEAD_DIM:(h + 1) * HEAD_DIM]
            return jnp.concatenate([qh, zero] if h < NSA_GROUP else [zero, qh], axis=1)

        q2 = per_head(q_rows).astype(bf16)
        kvn = kvn_ref[0]
        pad = jnp.zeros((LANES - Tn, KVW), f32)

        def new_rows(c):
            return jnp.concatenate([kvn[:, c * KVW:(c + 1) * KVW], pad], axis=0).astype(bf16)

        tn_io = lax.broadcasted_iota(jnp.int32, (Tn, LANES), 0)
        jn_io = lax.broadcasted_iota(jnp.int32, (Tn, LANES), 1)
        d_new = tn_io - jn_io
        keep_new = per_head(lambda h: (d_new >= 0) & (jn_io < Tn))
        bias_new = per_head(lambda h: _bias_from_dist(d_new, rb_ref, h))
        far = per_head(lambda h: jnp.full((Tn, 1), rb_ref[NUM_BUCKETS - 1, h], f32))

        kc = cmp_ref[:, 0:KVW].astype(bf16)
        vc = cmp_ref[:, KVW:2 * KVW].astype(bf16)
        n_io = lax.broadcasted_iota(jnp.int32, (Tn, nbp), 1)
        t_io = lax.broadcasted_iota(jnp.int32, (Tn, nbp), 0)
        dist_c = past + t_io - (n_io * NSA_BLOCK + NSA_BLOCK - 1)
        sc = _dot_nt(q2, kc) + per_head(lambda h: _bias_from_dist(dist_c, rb_ref, h))
        e = jnp.exp(sc - jnp.max(sc, axis=1, keepdims=True))
        p_c = e / jnp.sum(e, axis=1, keepdims=True)
        oc = jnp.dot(p_c.astype(bf16), vc, preferred_element_type=f32)

        topk = min(NSA_TOPK, nbp + 1)
        m_io = lax.broadcasted_iota(jnp.int32, (nbp, nbp), 0)
        c_io = lax.broadcasted_iota(jnp.int32, (nbp, nbp), 1)
        lower = m_io < c_io
        forced = (n_io == 0) | (n_io == nbp - 1)
        sels = []
        for kvg in range(NSA_KV_HEADS):
            r0 = kvg * NSA_GROUP * Tn
            score = jnp.where(forced, 2.0, p_c[r0:r0 + Tn] + p_c[r0 + Tn:r0 + 2 * Tn])
            score_t = jnp.concatenate([score, jnp.zeros((LANES - Tn, nbp), f32)], axis=0).T
            ranks = []
            for t in range(Tn):
                colb = jnp.broadcast_to(score_t[:, t:t + 1], (nbp, nbp))
                rowb = jnp.broadcast_to(score[t:t + 1, :], (nbp, nbp))
                beats = (colb > rowb) | ((colb == rowb) & lower)
                ranks.append(jnp.sum(beats.astype(f32), axis=0, keepdims=True))
            sel = (jnp.concatenate(ranks, axis=0) < topk - 1).astype(f32)
            sels += [sel] * NSA_GROUP
        sel_rows = jnp.concatenate(sels, axis=0).astype(bf16)

        bpt = SEL_TILE // NSA_BLOCK
        expand = (lax.broadcasted_iota(jnp.int32, (bpt, SEL_TILE), 0)
                  == lax.broadcasted_iota(jnp.int32, (bpt, SEL_TILE), 1) // NSA_BLOCK).astype(bf16)
        d_last = LANES + tn_io - jn_io
        near = per_head(lambda h: _bias_from_dist(d_last, rb_ref, h))
        m = jnp.full((R, 1), NEG_INF, f32)
        l = jnp.zeros((R, 1), f32)
        acc = jnp.zeros((R, KVW), f32)

        def flash(carry, s, v_t, v_feature_major):
            m, l, acc = carry
            m_new = jnp.maximum(m, jnp.max(s, axis=1, keepdims=True))
            p = jnp.exp(s - m_new)
            alpha = jnp.exp(m - m_new)
            pb = p.astype(bf16)
            pv = _dot_nt(pb, v_t) if v_feature_major else jnp.dot(pb, v_t, preferred_element_type=f32)
            return m_new, alpha * l + jnp.sum(p, axis=1, keepdims=True), alpha * acc + pv

        carry = (m, l, acc)
        ntile = past // SEL_TILE
        for j in range(ntile):
            k_t = kselt_ref[:, j * SEL_TILE:(j + 1) * SEL_TILE]
            v_t = vselt_ref[:, j * SEL_TILE:(j + 1) * SEL_TILE]
            keep = jnp.dot(sel_rows[:, j * bpt:(j + 1) * bpt], expand, preferred_element_type=f32) > 0.5
            if j == ntile - 1:
                bias = jnp.concatenate([jnp.broadcast_to(far, (R, SEL_TILE - LANES)), near], axis=1)
            else:
                bias = far
            s_t = jnp.dot(q2, k_t, preferred_element_type=f32)
            carry = flash(carry, jnp.where(keep, s_t + bias, NEG_INF), v_t, True)
        s_new = jnp.where(keep_new, _dot_nt(q2, new_rows(2)) + bias_new, NEG_INF)
        m, l, acc = flash(carry, s_new, new_rows(3), False)
        o_s = acc / l

        tw_io = lax.broadcasted_iota(jnp.int32, (Tn, Wb), 0)
        cw_io = lax.broadcasted_iota(jnp.int32, (Tn, Wb), 1)
        d_w = Wb + tw_io - cw_io
        s_w = jnp.where(per_head(lambda h: d_w < NSA_WINDOW),
                        jnp.dot(q2, win_ref[0, 0].astype(bf16), preferred_element_type=f32)
                        + per_head(lambda h: _bias_from_dist(d_w, rb_ref, h)), NEG_INF)
        s_wn = jnp.where(keep_new, _dot_nt(q2, new_rows(4)) + bias_new, NEG_INF)
        s_all = jnp.concatenate([s_w, s_wn], axis=1)
        e = jnp.exp(s_all - jnp.max(s_all, axis=1, keepdims=True))
        p_w = (e / jnp.sum(e, axis=1, keepdims=True)).astype(bf16)
        o_w = _dot_nt(p_w[:, :Wb], win_ref[0, 1].astype(bf16)) + jnp.dot(p_w[:, Wb:], new_rows(5),
                                                                          preferred_element_type=f32)

        gates = jax.nn.sigmoid(gt_ref[0])
        outs = []
        for h in range(NSA_HEADS):
            rs = slice(h * Tn, (h + 1) * Tn)
            cs = slice((h // NSA_GROUP) * HEAD_DIM, (h // NSA_GROUP + 1) * HEAD_DIM)
            outs.append(gates[:, h:h + 1] * oc[rs, cs] + gates[:, NSA_HEADS + h:NSA_HEADS + h + 1] * o_s[rs, cs]
                        + gates[:, 2 * NSA_HEADS + h:2 * NSA_HEADS + h + 1] * o_w[rs, cs])
        o_ref[0] = jnp.concatenate(outs, axis=1)


def _nsa_sample(q, kv, gate_logits, cache, layer, page_table, win_buf, cmp_pool, cmp_pe, rel_bias):
    B, Tn, _ = q.shape
    npages = page_table.shape[1]
    past = npages * PAGE_SIZE
    Wb = win_buf.shape[1]
    PPS = PAGES_PER_STEP
    assert npages % PPS == 0 and past % SEL_TILE == 0 and Tn == SUBLANES and (past // NSA_BLOCK) % LANES == 0
    f32, bf16 = jnp.float32, jnp.bfloat16
    bpp = PAGE_SIZE // NSA_BLOCK
    cache_t = jnp.transpose(cache, (0, 1, 3, 4, 5, 2)).reshape(cache.shape[0], cache.shape[1], 4, KVW, PAGE_SIZE)
    win_t = jnp.transpose(win_buf, (0, 2, 3, 4, 1)).reshape(B, 2, KVW, Wb)
    r_io = np.arange(PAGE_SIZE)
    onehot = jnp.asarray((r_io[None, :] // NSA_BLOCK == np.arange(SUBLANES)[:, None]), f32)
    pool_full = onehot[None] * jnp.tile(cmp_pool, (1, bpp))[:, None, :]
    pool_hi = pool_full.astype(bf16)
    pool_lo = (pool_full - pool_hi.astype(f32)).astype(bf16)
    poolm = jnp.concatenate([pool_hi, pool_lo], axis=1)
    cconst = jnp.tile(jnp.sum(cmp_pool[:, :, None] * cmp_pe, axis=1), (1, NSA_KV_HEADS))

    def page_spec(k):
        return pl.BlockSpec((1, 1, 4, KVW, PAGE_SIZE), lambda b, s, pt: (pt[b, s * PPS + k], layer, 0, 0, 0))

    grid_spec = pltpu.PrefetchScalarGridSpec(
        num_scalar_prefetch=1,
        grid=(B, npages // PPS),
        in_specs=[pl.BlockSpec(memory_space=pltpu.SMEM)] + [page_spec(k) for k in range(PPS)] + [
            pl.BlockSpec((1, Tn, NSA_HEADS * HEAD_DIM), lambda b, s, pt: (b, 0, 0)),
            pl.BlockSpec((1, Tn, 6 * KVW), lambda b, s, pt: (b, 0, 0)),
            pl.BlockSpec((1, Tn, 3 * NSA_HEADS), lambda b, s, pt: (b, 0, 0)),
            pl.BlockSpec((1, 2, KVW, Wb), lambda b, s, pt: (b, 0, 0, 0)),
            pl.BlockSpec((2, 2 * SUBLANES, PAGE_SIZE), lambda b, s, pt: (0, 0, 0)),
            pl.BlockSpec((2, KVW), lambda b, s, pt: (0, 0)),
        ],
        out_specs=pl.BlockSpec((1, Tn, NSA_HEADS * HEAD_DIM), lambda b, s, pt: (b, 0, 0)),
        scratch_shapes=[
            pltpu.VMEM((past // NSA_BLOCK, 2 * KVW), f32),
            pltpu.VMEM((KVW, past), bf16),
            pltpu.VMEM((KVW, past), bf16),
        ],
    )
    return pl.pallas_call(
        _nsa_sample_kernel,
        grid_spec=grid_spec,
        out_shape=jax.ShapeDtypeStruct((B, Tn, NSA_HEADS * HEAD_DIM), f32),
        compiler_params=pltpu.CompilerParams(dimension_semantics=("arbitrary", "arbitrary"),
                                             vmem_limit_bytes=VMEM_LIMIT),
        name="nsa_sample",
    )(page_table, rel_bias, *([cache_t] * PPS), q, kv, gate_logits, win_t, poolm, cconst)


def nsa_attention(q, kv_new, gate_logits, past_rows, win_buf, pos0, cmp_pool, cmp_pe, rel_bias):
    B, T = q.shape[:2]
    if past_rows is None:
        o = _nsa_prompt(q.reshape(B, T, -1), 0, kv_new.reshape(B, T, -1), 0, gate_logits.reshape(B, T, -1),
                        cmp_pool, cmp_pe, rel_bias)
        return o, kv_new[:, T - min(NSA_WINDOW, T):, 4:]
    if isinstance(past_rows, tuple):
        cache, layer, page_table = past_rows
        o = _nsa_sample(q.reshape(B, T, -1), kv_new.reshape(B, T, -1), gate_logits.reshape(B, T, -1),
                        cache, layer, page_table, win_buf, cmp_pool, cmp_pe, rel_bias)
        real = jnp.concatenate([win_buf, kv_new[:, :, 4:]], axis=1)
        return o, real[:, real.shape[1] - min(NSA_WINDOW, real.shape[1]):]
    dt = q.dtype
    qpos = pos0 + jnp.arange(T, dtype=jnp.int32)
    qg = (q * HEAD_DIM ** -0.5).reshape(B, T, NSA_KV_HEADS, NSA_GROUP, HEAD_DIM)
    rb = rel_bias.astype(jnp.float32).reshape(NUM_BUCKETS, NSA_KV_HEADS, NSA_GROUP)
    L = T if past_rows is None else past_rows.shape[1] + T
    nb = -(-L // NSA_BLOCK)
    new_rows = jnp.pad(kv_new[:, :, :4], ((0, 0), (0, nb * NSA_BLOCK - L), (0, 0), (0, 0), (0, 0)))
    full = new_rows if past_rows is None else jnp.concatenate([past_rows.astype(dt), new_rows], axis=1)
    blocks = full.reshape(B, nb, NSA_BLOCK, 4, NSA_KV_HEADS, HEAD_DIM)
    pe = jnp.transpose(cmp_pe, (1, 0, 2))[:, :, None, :].astype(dt)
    cmp = jnp.einsum('bnjckd,cj->bnckd', blocks[:, :, :, :2] + pe, cmp_pool.astype(dt))
    k_c, v_c = cmp[:, :, 0], cmp[:, :, 1]
    blk = jnp.arange(nb, dtype=jnp.int32)
    d_c = qpos[:, None] - (blk * NSA_BLOCK + NSA_BLOCK - 1)[None, :]
    s_c = jnp.einsum('btkgd,bnkd->bkgtn', qg, k_c).astype(jnp.float32) + jnp.transpose(rb[t5_bucket(d_c)], (2, 3, 0, 1))
    p_c = masked_softmax(s_c, d_c >= 0)
    o_c = jnp.einsum('bkgtn,bnkd->btkgd', p_c.astype(dt), v_c)
    cur = (qpos // NSA_BLOCK)[:, None]
    forced = (blk[None] == 0) | (blk[None] == cur) | (blk[None] == cur - 1)
    score = jnp.where(blk[None] <= cur, jnp.where(forced, 2.0, p_c.sum(axis=2)), -1.0)
    top_s, idx = lax.top_k(score, min(NSA_TOPK, nb))
    ok = top_s > -0.5
    ks = jnp.transpose(blocks[:, :, :, 2], (0, 3, 1, 2, 4))
    vs = jnp.transpose(blocks[:, :, :, 3], (0, 3, 1, 2, 4))
    take = jax.vmap(jax.vmap(lambda a, i: a[i]))
    kv_ix = jnp.arange(NSA_KV_HEADS)[None, :, None, None, None]
    rbk = jnp.transpose(rb, (1, 0, 2))

    def sel_block(args):
        qb, ib, okb, pb = args
        Bq, Q = qb.shape[:2]
        kg, vg = take(ks, ib), take(vs, ib)
        kpos = ib[..., None] * NSA_BLOCK + jnp.arange(NSA_BLOCK, dtype=jnp.int32)
        dist = pb[None, None, :, None, None] - kpos
        mask = (okb[..., None] & (dist >= 0)).reshape(Bq, NSA_KV_HEADS, 1, Q, -1)
        bias = jnp.moveaxis(rbk[kv_ix, t5_bucket(dist)], -1, 2)
        s = jnp.einsum('bqkgd,bkqsjd->bkgqsj', qb, kg).astype(jnp.float32) + bias
        p = masked_softmax(s.reshape(Bq, NSA_KV_HEADS, NSA_GROUP, Q, -1), mask)
        return jnp.einsum('bkgqn,bkqnd->bqkgd', p.astype(dt), vg.reshape(Bq, NSA_KV_HEADS, Q, -1, HEAD_DIM))

    o_s = over_query_blocks(sel_block, (qg, idx, ok, qpos), (1, 2, 2, 0), T)
    kw = kv_new[:, :, 4:]
    if win_buf is None:
        real = kw
        k_all = jnp.pad(kw, ((0, 0), (NSA_WINDOW, 0), (0, 0), (0, 0), (0, 0)))
        span = NSA_WINDOW
    else:
        real = jnp.concatenate([win_buf.astype(dt), kw], axis=1)
        k_all = real
        span = win_buf.shape[1]
    k0 = pos0 - span

    def win_block(args):
        qb, pb = args
        Q = qb.shape[1]
        start = pb[0] - pos0
        kb = lax.dynamic_slice_in_dim(k_all, start, span + Q, axis=1)
        kpos = k0 + start + jnp.arange(span + Q, dtype=jnp.int32)
        dist = pb[:, None] - kpos[None, :]
        mask = (dist >= 0) & (dist < NSA_WINDOW) & (kpos >= 0)[None, :]
        s = jnp.einsum('bqkgd,bnkd->bkgqn', qb, kb[:, :, 0]).astype(jnp.float32) + jnp.transpose(rb[t5_bucket(dist)], (2, 3, 0, 1))
        p = masked_softmax(s, mask)
        return jnp.einsum('bkgqn,bnkd->bqkgd', p.astype(dt), kb[:, :, 1])

    o_w = over_query_blocks(win_block, (qg, qpos), (1, 0), T)
    new_win = real[:, real.shape[1] - min(NSA_WINDOW, real.shape[1]):]
    g = jax.nn.sigmoid(gate_logits.astype(jnp.float32)).astype(dt).reshape(B, T, 3, NSA_KV_HEADS, NSA_GROUP, 1)
    o = g[:, :, 0] * o_c + g[:, :, 1] * o_s + g[:, :, 2] * o_w
    return o.reshape(B, T, NSA_HEADS * HEAD_DIM), new_win


def conformer_conv(u, buf, dw, dw_b, ln_g, ln_b):
    a, gte = jnp.split(u, 2, axis=-1)
    y, new_buf = causal_dwconv(a * jax.nn.sigmoid(gte), buf, dw)
    y = layer_norm(y + dw_b.astype(y.dtype), ln_g, ln_b)
    return jax.nn.silu(y), new_buf


def retention(q, k, v, gate, S0, pos0, gn):
    B, T, _ = q.shape
    dt = q.dtype
    H, d = RET_HEADS, HEAD_DIM
    pos = pos0 + jnp.arange(T, dtype=jnp.int32)
    f = lambda a: a.astype(jnp.float32).reshape(B, T, H, d)
    q = rotary(f(q), pos)
    k = rotary(f(k), pos) * d ** -0.5
    v = f(v)
    C = math.gcd(T, RET_CHUNK)
    n = T // C
    lg = jnp.log1p(-jnp.exp2(-5.0 - jnp.arange(H, dtype=jnp.float32)))
    ch = lambda a: a.reshape(B, n, C, H, d).transpose(0, 3, 1, 2, 4)
    qc, kc, vc = ch(q), ch(k), ch(v)
    i = jnp.arange(C, dtype=jnp.float32)
    diff = i[:, None] - i[None, :]
    Dm = jnp.where(diff >= 0, jnp.exp(jnp.maximum(diff, 0.0)[None] * lg[:, None, None]), 0.0)
    att = jnp.einsum('bhncd,bhnsd->bhncs', qc, kc) * Dm[None, :, None]
    o = jnp.einsum('bhncs,bhnse->bhnce', att, vc)
    xi = jnp.exp((i[None, :] + 1.0) * lg[:, None])
    zeta = jnp.exp((C - 1.0 - i[None, :]) * lg[:, None])
    kv = jnp.einsum('bhncd,bhnce->nbhde', kc * zeta[None, :, None, :, None], vc)
    decay = jnp.exp(C * lg)[None, :, None, None]

    def step(S, kvn):
        return S * decay + kvn, S

    S_fin, S_prev = lax.scan(step, S0.astype(jnp.float32), kv)
    o = o + jnp.einsum('bhncd,nbhde->bhnce', qc * xi[None, :, None, :, None], S_prev)
    o = o.transpose(0, 2, 3, 1, 4).reshape(B, T, H, d)
    mu = jnp.mean(o, axis=-1, keepdims=True)
    var = jnp.mean(jnp.square(o - mu), axis=-1, keepdims=True)
    o = ((o - mu) * lax.rsqrt(var + EPS)).reshape(B, T, H * d) * gn.astype(jnp.float32)
    return (o * jax.nn.silu(gate.astype(jnp.float32))).astype(dt), S_fin


def gated_deltanet(qkv, z, a, b, conv_buf, S0, conv_w, A_log, dt_bias, norm_g):
    B, T, _ = qkv.shape
    dt = qkv.dtype
    H, d = GDN_HEADS, HEAD_DIM
    y, new_buf = causal_dwconv(qkv, conv_buf, conv_w)
    y = jax.nn.silu(y.astype(jnp.float32))
    q, k, v = [t.reshape(B, T, H, d) for t in jnp.split(y, 3, axis=-1)]
    l2 = lambda t: t * lax.rsqrt(jnp.sum(t * t, axis=-1, keepdims=True) + EPS)
    q = l2(q) * d ** -0.5
    k = l2(k)
    g = -jnp.exp(A_log.astype(jnp.float32)) * jax.nn.softplus(a.astype(jnp.float32) + dt_bias.astype(jnp.float32))
    beta = jax.nn.sigmoid(b.astype(jnp.float32))
    C = math.gcd(T, GDN_CHUNK)
    n = T // C
    ch = lambda t: t.reshape(B, n, C, H, d).transpose(0, 3, 1, 2, 4)
    chs = lambda t: t.reshape(B, n, C, H).transpose(0, 3, 1, 2)
    qc, kc, vc = ch(q), ch(k), ch(v)
    gc = jnp.cumsum(chs(g), axis=-1)
    bc = chs(beta)
    ii = jnp.arange(C)
    tri = ii[:, None] >= ii[None, :]
    strict = ii[:, None] > ii[None, :]
    Lm = jnp.exp(jnp.where(tri, gc[..., :, None] - gc[..., None, :], -jnp.inf))
    kb = kc * bc[..., None]
    M = jnp.where(strict, jnp.einsum('bhncd,bhnsd->bhncs', kb, kc) * Lm, 0.0)
    A = M + jnp.eye(C, dtype=jnp.float32)
    rhs = jnp.concatenate([vc * bc[..., None], kb * jnp.exp(gc)[..., None]], axis=-1)
    sol = lax.linalg.triangular_solve(A, rhs, left_side=True, lower=True, unit_diagonal=True)
    u, w = sol[..., :d], sol[..., d:]
    att = jnp.einsum('bhncd,bhnsd->bhncs', qc, kc) * Lm
    qd = qc * jnp.exp(gc)[..., None]
    kd = kc * jnp.exp(gc[..., -1:] - gc)[..., None]
    glast = jnp.exp(gc[..., -1])
    xs = tuple(jnp.moveaxis(t, 2, 0) for t in (u, w, att, qd, kd, glast))

    def step(S, xs_n):
        u_, w_, at_, qd_, kd_, gl_ = xs_n
        vn = u_ - jnp.einsum('bhcd,bhde->bhce', w_, S)
        o_ = jnp.einsum('bhcd,bhde->bhce', qd_, S) + jnp.einsum('bhcs,bhse->bhce', at_, vn)
        S = S * gl_[..., None, None] + jnp.einsum('bhcd,bhce->bhde', kd_, vn)
        return S, o_

    S_fin, o = lax.scan(step, S0.astype(jnp.float32), xs)
    o = o.transpose(1, 0, 3, 2, 4).reshape(B, T, H, d)
    o = o * lax.rsqrt(jnp.mean(o * o, axis=-1, keepdims=True) + EPS) * norm_g.astype(jnp.float32)
    o = o * jax.nn.silu(z.astype(jnp.float32).reshape(B, T, H, d))
    return o.reshape(B, T, H * d).astype(dt), new_buf, S_fin


def conv_ffn(h, buf, w_up, dw, w_down):
    Bh, Th, _ = h.shape
    gpre, val = jnp.split(_pmm(h.reshape(Bh * Th, D_MODEL), w_up).reshape(Bh, Th, 2 * D_FF), 2, axis=-1)
    gconv, new_buf = causal_dwconv(gpre, buf, dw)
    return _pmm((jax.nn.gelu(gconv) * val).reshape(Bh * Th, D_FF), w_down).reshape(Bh, Th, D_MODEL), new_buf


PROJ_KV, PROJ_Q, PROJ_UCV, PROJ_RET, PROJ_GQKV, PROJ_GZ, PROJ_MISC = 0, 768, 1024, 1536, 2560, 3328, 3584
PROJ_COLS = PROJ_MISC + LANES
PROJ_TM = 512
MERGE_TM = 256
FFN_TM = 256


def _prep_w_in(w_in):
    o = [int(v) for v in np.cumsum((0,) + IN_SPLITS)]
    small = o[5] + 4 * GDN_W
    pieces = [w_in[:, o[1]:o[2]], w_in[:, o[0]:o[1]], w_in[:, o[3]:o[4]], w_in[:, o[4]:o[5]], w_in[:, o[5]:small],
              w_in[:, o[2]:o[3]], w_in[:, small:o[6]]]
    used = sum(pc.shape[1] for pc in pieces)
    pieces.append(jnp.zeros((D_MODEL, PROJ_COLS - used), w_in.dtype))
    return jnp.concatenate(pieces, axis=1).astype(jnp.bfloat16), w_in[:, o[6]:].astype(jnp.bfloat16)


def _modulated_norm(x, g, sc, sh):
    y = x * lax.rsqrt(jnp.mean(x * x, axis=-1, keepdims=True) + EPS)
    return (y * g) * (1.0 + sc) + sh


def _resident(shape):
    return pl.BlockSpec(shape, lambda *_: (0,) * len(shape), pipeline_mode=pl.Buffered(1))


def _mod_spec(mod, rows_per_group, tm):
    if mod.shape[1] == 1:
        return pl.BlockSpec((1, 1, D_MODEL), lambda i, *_: (i // (rows_per_group // tm), 0, 0))
    return pl.BlockSpec((1, tm, D_MODEL), lambda i, *_: (0, i, 0))


def _proj_kernel(x_ref, g_ref, sc_ref, sh_ref, w_ref, o_ref):
    h = _modulated_norm(x_ref[...], g_ref[...], sc_ref[0], sh_ref[0]).astype(jnp.bfloat16)
    o_ref[...] = jnp.dot(h, w_ref[...], preferred_element_type=jnp.float32)


def _proj(x2, g, sc, sh, w, rows_per_group, tm):
    M = x2.shape[0]
    N = w.shape[1]
    return pl.pallas_call(
        _proj_kernel,
        grid=(M // tm,),
        in_specs=[pl.BlockSpec((tm, D_MODEL), lambda i: (i, 0)), _resident((1, D_MODEL)),
                  _mod_spec(sc, rows_per_group, tm), _mod_spec(sh, rows_per_group, tm), _resident((D_MODEL, N))],
        out_specs=pl.BlockSpec((tm, N), lambda i: (i, 0)),
        out_shape=jax.ShapeDtypeStruct((M, N), jnp.float32),
        compiler_params=pltpu.CompilerParams(dimension_semantics=("arbitrary",), vmem_limit_bytes=VMEM_LIMIT),
        name="in_proj",
    )(x2, g, sc, sh, w)


def _merge_kernel(x_ref, g0_ref, sc_ref, sh_ref, gt_ref, g1_ref, b0_ref, b1_ref, b2_ref, b3_ref,
                  wmg_ref, wbr_ref, wout_ref, o_ref):
    f32, bf16 = jnp.float32, jnp.bfloat16
    x = x_ref[...]
    h = _modulated_norm(x, g0_ref[...], sc_ref[0], sh_ref[0]).astype(bf16)
    acc = jnp.zeros(x.shape, f32)
    for n, b_ref in enumerate((b0_ref, b1_ref, b2_ref, b3_ref)):
        gate = jax.nn.sigmoid(jnp.dot(h, wmg_ref[:, n * D_MODEL:(n + 1) * D_MODEL], preferred_element_type=f32))
        acc = acc + gate * jnp.dot(b_ref[...].astype(bf16), wbr_ref[n], preferred_element_type=f32)
    mixed = jnp.dot(acc.astype(bf16), wout_ref[...], preferred_element_type=f32)
    y = mixed * lax.rsqrt(jnp.mean(mixed * mixed, axis=-1, keepdims=True) + EPS) * g1_ref[...]
    o_ref[...] = x + gt_ref[0] * y


def _merge(x2, g0, sc, sh, gt, g1, branches, wmg, wbr, wout, rows_per_group, tm):
    M = x2.shape[0]
    row = lambda w: pl.BlockSpec((tm, w), lambda i: (i, 0))
    mspec = _mod_spec(sc, rows_per_group, tm)
    return pl.pallas_call(
        _merge_kernel,
        grid=(M // tm,),
        in_specs=[row(D_MODEL), _resident((1, D_MODEL)), mspec, mspec, mspec, _resident((1, D_MODEL))]
                 + [row(BRANCH_W)] * N_BRANCH
                 + [_resident(wmg.shape), _resident(wbr.shape), _resident(wout.shape)],
        out_specs=row(D_MODEL),
        out_shape=jax.ShapeDtypeStruct((M, D_MODEL), jnp.float32),
        compiler_params=pltpu.CompilerParams(dimension_semantics=("arbitrary",), vmem_limit_bytes=VMEM_LIMIT),
        name="merge",
    )(x2, g0, sc, sh, gt, g1, *branches, wmg, wbr, wout)


def _ffn_kernel(x_ref, g2_ref, sc_ref, sh_ref, gt_ref, g3_ref, buf_ref, wup_ref, dw_ref, wdn_ref,
                o_ref, st_ref, gp_ref):
    f32, bf16 = jnp.float32, jnp.bfloat16
    tm = x_ref.shape[1]
    HALO = SUBLANES

    @pl.when(pl.program_id(1) == 0)
    def _():
        gp_ref[0:HALO, :] = buf_ref[0]

    x = x_ref[0]
    h = _modulated_norm(x, g2_ref[...], sc_ref[0], sh_ref[0]).astype(bf16)
    gp_ref[HALO:HALO + tm, :] = jnp.dot(h, wup_ref[:, 0:D_FF], preferred_element_type=f32)
    val = jnp.dot(h, wup_ref[:, D_FF:2 * D_FF], preferred_element_type=f32)
    gconv = (dw_ref[2:3, :] * gp_ref[HALO:HALO + tm, :] + dw_ref[1:2, :] * gp_ref[HALO - 1:HALO - 1 + tm, :]
             + dw_ref[0:1, :] * gp_ref[HALO - 2:HALO - 2 + tm, :])
    a = (jax.nn.gelu(gconv) * val).astype(bf16)
    f = jnp.dot(a, wdn_ref[...], preferred_element_type=f32)
    y = f * lax.rsqrt(jnp.mean(f * f, axis=-1, keepdims=True) + EPS) * g3_ref[...]
    o_ref[0] = x + gt_ref[0] * y
    tail = gp_ref[tm:tm + HALO, :]
    gp_ref[0:HALO, :] = tail
    st_ref[0] = tail


def _ffn(x3, g2, sc, sh, gt, g3, buf, wup, dw, wdn, tm):
    B, T, _ = x3.shape
    buf8 = jnp.pad(buf, ((0, 0), (SUBLANES - (FFN_CONV - 1), 0), (0, 0)))
    mspec = pl.BlockSpec((1, 1, D_MODEL), lambda b, i: (b, 0, 0))
    y, st = pl.pallas_call(
        _ffn_kernel,
        grid=(B, T // tm),
        in_specs=[pl.BlockSpec((1, tm, D_MODEL), lambda b, i: (b, i, 0)), _resident((1, D_MODEL)), mspec, mspec, mspec,
                  _resident((1, D_MODEL)), pl.BlockSpec((1, SUBLANES, D_FF), lambda b, i: (b, 0, 0)),
                  _resident(wup.shape), _resident(dw.shape), _resident(wdn.shape)],
        out_specs=[pl.BlockSpec((1, tm, D_MODEL), lambda b, i: (b, i, 0)),
                   pl.BlockSpec((1, SUBLANES, D_FF), lambda b, i: (b, 0, 0))],
        out_shape=[jax.ShapeDtypeStruct((B, T, D_MODEL), jnp.float32),
                   jax.ShapeDtypeStruct((B, SUBLANES, D_FF), jnp.float32)],
        scratch_shapes=[pltpu.VMEM((tm + SUBLANES, D_FF), jnp.float32)],
        compiler_params=pltpu.CompilerParams(dimension_semantics=("arbitrary", "arbitrary"),
                                             vmem_limit_bytes=VMEM_LIMIT),
        name="conv_ffn",
    )(x3, g2, sc, sh, gt, g3, buf8, wup, dw, wdn)
    return y, st[:, SUBLANES - (FFN_CONV - 1):]


CONF_HALO = 32
CONF_TM = 512


def _conformer_kernel(u_ref, buf_ref, dw_ref, dwb_ref, lng_ref, lnb_ref, o_ref, st_ref, xp_ref):
    tm = u_ref.shape[1]
    first = CONF_HALO - (CONV_WIDTH - 1)

    @pl.when(pl.program_id(1) == 0)
    def _():
        xp_ref[0:CONF_HALO, :] = buf_ref[0]

    u = u_ref[0]
    xp_ref[CONF_HALO:CONF_HALO + tm, :] = u[:, :CONV_CH] * jax.nn.sigmoid(u[:, CONV_CH:])
    acc = jnp.zeros((tm, CONV_CH), jnp.float32)
    for k in range(CONV_WIDTH):
        acc = acc + dw_ref[k:k + 1, :] * xp_ref[first + k:first + k + tm, :]
    y = acc + dwb_ref[...]
    mu = jnp.mean(y, axis=-1, keepdims=True)
    var = jnp.mean(jnp.square(y - mu), axis=-1, keepdims=True)
    yn = (y - mu) * lax.rsqrt(var + EPS) * lng_ref[...] + lnb_ref[...]
    o_ref[0] = yn * jax.nn.sigmoid(yn)
    tail = xp_ref[tm:tm + CONF_HALO, :]
    xp_ref[0:CONF_HALO, :] = tail
    st_ref[0] = tail


def _conformer(P, col_blk, buf, dw, dw_b, ln_g, ln_b, tm):
    B, T, _ = P.shape
    bufp = jnp.pad(buf, ((0, 0), (CONF_HALO - (CONV_WIDTH - 1), 0), (0, 0)))
    vec = lambda: _resident((1, CONV_CH))
    o, st = pl.pallas_call(
        _conformer_kernel,
        grid=(B, T // tm),
        in_specs=[pl.BlockSpec((1, tm, 2 * CONV_CH), lambda b, i: (b, i, col_blk)),
                  pl.BlockSpec((1, CONF_HALO, CONV_CH), lambda b, i: (b, 0, 0)),
                  _resident((CONV_WIDTH, CONV_CH)), vec(), vec(), vec()],
        out_specs=[pl.BlockSpec((1, tm, CONV_CH), lambda b, i: (b, i, 0)),
                   pl.BlockSpec((1, CONF_HALO, CONV_CH), lambda b, i: (b, 0, 0))],
        out_shape=[jax.ShapeDtypeStruct((B, T, CONV_CH), jnp.float32),
                   jax.ShapeDtypeStruct((B, CONF_HALO, CONV_CH), jnp.float32)],
        scratch_shapes=[pltpu.VMEM((tm + CONF_HALO, CONV_CH), jnp.float32)],
        compiler_params=pltpu.CompilerParams(dimension_semantics=("arbitrary", "arbitrary"),
                                             vmem_limit_bytes=VMEM_LIMIT),
        name="conformer",
    )(P, bufp, dw, dw_b[None], ln_g[None], ln_b[None])
    return o, st[:, CONF_HALO - (CONV_WIDTH - 1):]


RET_LOG_GAMMA = tuple(math.log1p(-2.0 ** (-5 - h)) for h in range(RET_HEADS))
RET_TM = 128


def _retention_kernel(q_ref, k_ref, v_ref, g_ref, cos_ref, sin_ref, s0_ref, gn_ref, o_ref, sfin_ref,
                      s_ref, dec_ref, *, n_valid):
    f32, bf16 = jnp.float32, jnp.bfloat16
    C = q_ref.shape[1]
    W = RET_W
    half = HEAD_DIM // 2

    @pl.when(pl.program_id(1) == 0)
    def _():
        s_ref[...] = s0_ref[0]

    @pl.when((pl.program_id(0) == 0) & (pl.program_id(1) == 0))
    def _():
        ii = lax.broadcasted_iota(jnp.int32, (C, C), 0)
        jj = lax.broadcasted_iota(jnp.int32, (C, C), 1)
        d = (ii - jj).astype(f32)
        for h in range(RET_HEADS):
            dec_ref[h] = jnp.where((ii >= jj) & (jj < n_valid), jnp.exp(jnp.maximum(d, 0.0) * RET_LOG_GAMMA[h]), 0.0)

    lane = lax.broadcasted_iota(jnp.int32, (C, W), 1)
    low = lane % HEAD_DIM < half
    cos = cos_ref[...]
    sin = sin_ref[...]

    def rot(x):
        other = jnp.where(low, pltpu.roll(x, W - half, axis=1), pltpu.roll(x, half, axis=1))
        return x * cos + other * sin

    q = rot(q_ref[0])
    k = rot(k_ref[0]) * HEAD_DIM ** -0.5
    v = v_ref[0]
    row = lax.broadcasted_iota(jnp.int32, (C, HEAD_DIM), 0)
    rowf = row.astype(f32)
    valid = row < n_valid
    outs = []
    for h in range(RET_HEADS):
        cs = slice(h * HEAD_DIM, (h + 1) * HEAD_DIM)
        lg = RET_LOG_GAMMA[h]
        qh, kh, vh = q[:, cs], k[:, cs], v[:, cs].astype(bf16)
        s_old = s_ref[h]
        att = _dot_nt(qh.astype(bf16), kh.astype(bf16)) * dec_ref[h]
        o = (jnp.dot(att.astype(bf16), vh, preferred_element_type=f32)
             + jnp.dot((qh * jnp.exp((rowf + 1.0) * lg)).astype(bf16), s_old.astype(bf16), preferred_element_type=f32))
        kz = jnp.where(valid, kh * jnp.exp((n_valid - 1.0 - rowf) * lg), 0.0)
        s_ref[h] = s_old * math.exp(n_valid * lg) + jnp.dot(kz.T.astype(bf16), vh, preferred_element_type=f32)
        mu = jnp.mean(o, axis=-1, keepdims=True)
        var = jnp.mean(jnp.square(o - mu), axis=-1, keepdims=True)
        outs.append((o - mu) * lax.rsqrt(var + EPS))
    g = g_ref[0]
    o_ref[0] = jnp.concatenate(outs, axis=1) * gn_ref[...] * (g * jax.nn.sigmoid(g))

    @pl.when(pl.program_id(1) == pl.num_programs(1) - 1)
    def _():
        sfin_ref[0] = s_ref[...]


def _retention(P, col_blk0, S0, pos, gn, chunk, n_valid):
    B, T, _ = P.shape
    half = HEAD_DIM // 2
    inv = ROPE_BASE ** (-jnp.arange(half, dtype=jnp.float32) / half)
    ang = pos.astype(jnp.float32)[:, None] * inv[None, :]
    cos, sin = jnp.cos(ang), jnp.sin(ang)
    cosf = jnp.tile(jnp.concatenate([cos, cos], axis=1), (1, RET_HEADS))
    sinf = jnp.tile(jnp.concatenate([-sin, sin], axis=1), (1, RET_HEADS))
    col = lambda j: pl.BlockSpec((1, chunk, RET_W), lambda b, i: (b, i, col_blk0 + j))
    tab = pl.BlockSpec((chunk, RET_W), lambda b, i: (i, 0))
    state = pl.BlockSpec((1, RET_HEADS, HEAD_DIM, HEAD_DIM), lambda b, i: (b, 0, 0, 0))
    return pl.pallas_call(
        functools.partial(_retention_kernel, n_valid=n_valid),
        grid=(B, T // chunk),
        in_specs=[col(0), col(1), col(2), col(3), tab, tab, state, _resident((1, RET_W))],
        out_specs=[pl.BlockSpec((1, chunk, RET_W), lambda b, i: (b, i, 0)), state],
        out_shape=[jax.ShapeDtypeStruct((B, T, RET_W), jnp.float32),
                   jax.ShapeDtypeStruct((B, RET_HEADS, HEAD_DIM, HEAD_DIM), jnp.float32)],
        scratch_shapes=[pltpu.VMEM((RET_HEADS, HEAD_DIM, HEAD_DIM), jnp.float32),
                        pltpu.VMEM((RET_HEADS, chunk, chunk), jnp.float32)],
        compiler_params=pltpu.CompilerParams(dimension_semantics=("arbitrary", "arbitrary"),
                                             vmem_limit_bytes=VMEM_LIMIT),
        name="retention",
    )(P, P, P, P, cosf, sinf, S0, gn[None])


GDN_RT = 2 * GDN_CHUNK
GDN_HALO = SUBLANES
GDN_A_LANE = 3 * NSA_HEADS
GDN_B_LANE = GDN_A_LANE + GDN_HEADS


def _mm1(a, b):
    return jnp.dot(a.astype(jnp.bfloat16), b.astype(jnp.bfloat16), preferred_element_type=jnp.float32)


def _mm3(a, b):
    f32, bf16 = jnp.float32, jnp.bfloat16
    ah, bh = a.astype(bf16), b.astype(bf16)
    al, bl = (a - ah.astype(f32)).astype(bf16), (b - bh.astype(f32)).astype(bf16)
    d = lambda x, y: jnp.dot(x, y, preferred_element_type=f32)
    return d(ah, bh) + (d(ah, bl) + d(al, bh))


def _segment_cumsum(x, axis, seg):
    idx = lax.broadcasted_iota(jnp.int32, x.shape, axis) % seg
    s = 1
    while s < seg:
        x = x + jnp.where(idx >= s, pltpu.roll(x, s, axis=axis), 0.0)
        s *= 2
    return x


def _gdn_kernel(q_ref, k_ref, v_ref, z_ref, ab_ref, abt_ref, buf_ref, s0_ref, cw_ref, ng_ref, alane_ref, dlane_ref,
                acol_ref, dcol_ref, o_ref, sfin_ref, s_ref, xp_ref, *, n_valid):
    f32 = jnp.float32
    RT = q_ref.shape[1]
    C = GDN_CHUNK
    W = GDN_W
    first = GDN_HALO - (GDN_CONV - 1)

    @pl.when(pl.program_id(1) == 0)
    def _():
        s_ref[...] = s0_ref[0]
        xp_ref[0:GDN_HALO, :] = buf_ref[0]

    xp_ref[GDN_HALO:GDN_HALO + RT, 0:W] = q_ref[0]
    xp_ref[GDN_HALO:GDN_HALO + RT, W:2 * W] = k_ref[0]
    xp_ref[GDN_HALO:GDN_HALO + RT, 2 * W:3 * W] = v_ref[0]
    y = jnp.zeros((RT, 3 * W), f32)
    for t in range(GDN_CONV):
        y = y + cw_ref[t:t + 1, :] * xp_ref[first + t:first + t + RT, :]
    y = y * jax.nn.sigmoid(y)
    xp_ref[0:GDN_HALO, :] = xp_ref[RT:RT + GDN_HALO, :]

    ab = ab_ref[0]
    g_lanes = -jnp.exp(alane_ref[...]) * jax.nn.softplus(ab + dlane_ref[...])
    beta_lanes = jax.nn.sigmoid(ab)
    g_rows = -jnp.exp(acol_ref[...]) * jax.nn.softplus(abt_ref[0] + dcol_ref[...])
    if n_valid < C:
        g_lanes = jnp.where(lax.broadcasted_iota(jnp.int32, g_lanes.shape, 0) % C < n_valid, g_lanes, 0.0)
        beta_lanes = jnp.where(lax.broadcasted_iota(jnp.int32, g_lanes.shape, 0) % C < n_valid, beta_lanes, 0.0)
        g_rows = jnp.where(lax.broadcasted_iota(jnp.int32, g_rows.shape, 1) % C < n_valid, g_rows, 0.0)
    gc_lanes = _segment_cumsum(g_lanes, 0, C)
    gc_rows = _segment_cumsum(g_rows, 1, C)

    ii = lax.broadcasted_iota(jnp.int32, (C, C), 0)
    jj = lax.broadcasted_iota(jnp.int32, (C, C), 1)
    tri = ii >= jj
    strict = ii > jj
    eye = (ii == jj).astype(f32)
    z = z_ref[0]
    bf16 = jnp.bfloat16
    nch = RT // C
    pairs = [(c, h) for c in range(nch) for h in range(GDN_HEADS)]
    qs, ks, ms, atts, rhss, gcs = {}, {}, {}, {}, {}, {}
    for c, h in pairs:
        rs = slice(c * C, (c + 1) * C)
        qh = y[rs, h * HEAD_DIM:(h + 1) * HEAD_DIM]
        kh = y[rs, W + h * HEAD_DIM:W + (h + 1) * HEAD_DIM]
        vh = y[rs, 2 * W + h * HEAD_DIM:2 * W + (h + 1) * HEAD_DIM]
        qh = qh * lax.rsqrt(jnp.sum(qh * qh, axis=-1, keepdims=True) + EPS) * HEAD_DIM ** -0.5
        kh = kh * lax.rsqrt(jnp.sum(kh * kh, axis=-1, keepdims=True) + EPS)
        gc_col = gc_lanes[rs, GDN_A_LANE + h:GDN_A_LANE + h + 1]
        beta = beta_lanes[rs, GDN_B_LANE + h:GDN_B_LANE + h + 1]
        gc_row = gc_rows[h:h + 1, c * C:(c + 1) * C]
        e_col = jnp.exp(gc_col)
        lm = jnp.exp(jnp.where(tri, gc_col - gc_row, NEG_INF))
        kb = kh * beta
        ms[c, h] = jnp.where(strict, _dot_nt(kb.astype(bf16), kh.astype(bf16)) * lm, 0.0)
        atts[c, h] = _dot_nt(qh.astype(bf16), kh.astype(bf16)) * lm
        rhss[c, h] = jnp.concatenate([vh * beta, kb * e_col], axis=1)
        qs[c, h], ks[c, h], gcs[c, h] = qh * e_col, kh, gc_col
    pw = {p: -ms[p] for p in pairs}
    xs = {p: eye + pw[p] for p in pairs}
    for _ in range(int(math.log2(C)) - 1):
        pw = {p: _mm1(pw[p], pw[p]) for p in pairs}
        xs = {p: xs[p] + _mm1(xs[p], pw[p]) for p in pairs}
    res = {p: eye - (xs[p] + _mm3(ms[p], xs[p])) for p in pairs}
    xs = {p: xs[p] + _mm1(xs[p], res[p]) for p in pairs}
    sols = {p: _mm3(xs[p], rhss[p]) for p in pairs}
    state = [s_ref[h] for h in range(GDN_HEADS)]
    outs = {}
    for c in range(nch):
        heads = range(GDN_HEADS)
        g_last = [gcs[c, h][C - 1:C, :] for h in heads]
        vn = [sols[c, h][:, :HEAD_DIM] - _mm1(sols[c, h][:, HEAD_DIM:], state[h]) for h in heads]
        o_in = [_mm1(qs[c, h], state[h]) for h in heads]
        kd_t = [(ks[c, h] * jnp.exp(g_last[h] - gcs[c, h])).T for h in heads]
        for h in heads:
            outs[c, h] = o_in[h] + _mm1(atts[c, h], vn[h])
        state = [state[h] * jnp.exp(g_last[h]) + _mm1(kd_t[h], vn[h]) for h in heads]
    for h in range(GDN_HEADS):
        s_ref[h] = state[h]
    row_outs = []
    for c in range(nch):
        head_outs = []
        for h in range(GDN_HEADS):
            o = outs[c, h]
            o = o * lax.rsqrt(jnp.mean(o * o, axis=-1, keepdims=True) + EPS) * ng_ref[...]
            zh = z[c * C:(c + 1) * C, h * HEAD_DIM:(h + 1) * HEAD_DIM]
            head_outs.append(o * (zh * jax.nn.sigmoid(zh)))
        row_outs.append(jnp.concatenate(head_outs, axis=1))
    o_ref[0] = jnp.concatenate(row_outs, axis=0)

    @pl.when(pl.program_id(1) == pl.num_programs(1) - 1)
    def _():
        sfin_ref[0] = s_ref[...]


def _gdn(P, blk_q, blk_misc, ga, conv_buf, S0, conv_w, A_log, dt_bias, norm_g, rt, n_valid):
    B, T, _ = P.shape
    f32 = jnp.float32
    abt = jnp.pad(jnp.transpose(ga, (0, 2, 1)), ((0, 0), (0, SUBLANES - GDN_HEADS), (0, 0)))
    buf8 = jnp.pad(conv_buf, ((0, 0), (GDN_HALO - (GDN_CONV - 1), 0), (0, 0)))
    lane_vec = lambda v: jnp.zeros((1, LANES), f32).at[0, GDN_A_LANE:GDN_A_LANE + GDN_HEADS].set(v)
    col_vec = lambda v: jnp.zeros((SUBLANES, 1), f32).at[0:GDN_HEADS, 0].set(v)
    col = lambda j: pl.BlockSpec((1, rt, GDN_W), lambda b, i: (b, i, blk_q + j))
    state = pl.BlockSpec((1, GDN_HEADS, HEAD_DIM, HEAD_DIM), lambda b, i: (b, 0, 0, 0))
    return pl.pallas_call(
        functools.partial(_gdn_kernel, n_valid=n_valid),
        grid=(B, T // rt),
        in_specs=[col(0), col(1), col(2), col(3),
                  pl.BlockSpec((1, rt, LANES), lambda b, i: (b, i, blk_misc)),
                  pl.BlockSpec((1, SUBLANES, rt), lambda b, i: (b, 0, i)),
                  pl.BlockSpec((1, GDN_HALO, 3 * GDN_W), lambda b, i: (b, 0, 0)), state,
                  _resident((GDN_CONV, 3 * GDN_W)), _resident((1, HEAD_DIM)),
                  _resident((1, LANES)), _resident((1, LANES)), _resident((SUBLANES, 1)), _resident((SUBLANES, 1))],
        out_specs=[pl.BlockSpec((1, rt, GDN_W), lambda b, i: (b, i, 0)), state],
        out_shape=[jax.ShapeDtypeStruct((B, T, GDN_W), f32),
                   jax.ShapeDtypeStruct((B, GDN_HEADS, HEAD_DIM, HEAD_DIM), f32)],
        scratch_shapes=[pltpu.VMEM((GDN_HEADS, HEAD_DIM, HEAD_DIM), f32),
                        pltpu.VMEM((rt + GDN_HALO, 3 * GDN_W), f32)],
        compiler_params=pltpu.CompilerParams(dimension_semantics=("arbitrary", "arbitrary"),
                                             vmem_limit_bytes=VMEM_LIMIT),
        name="gated_deltanet",
    )(P, P, P, P, P, abt, buf8, S0, conv_w, norm_g[None], lane_vec(A_log), lane_vec(dt_bias),
      col_vec(A_log), col_vec(dt_bias))


def _prep_layer(p):
    w_main, w_mg = _prep_w_in(p["w_in"])
    q = dict(p)
    q.update(w_main=w_main, w_mg=w_mg, w_branch_b=p["w_branch"].astype(jnp.bfloat16),
             w_out_b=p["w_out"].astype(jnp.bfloat16), ffn_up_b=p["ffn_up"].astype(jnp.bfloat16),
             ffn_down_b=p["ffn_down"].astype(jnp.bfloat16))
    return q


def trunk_layer(x, c, pos0, nsa_past, win_buf, conv_buf, ret_s, gdn_buf, gdn_s, ffn_buf, p, rel_bias):
    B, T, _ = x.shape
    M = B * T
    mod = _pmm(jax.nn.silu(c), p["w_ada"]) + p["b_ada"]
    per_row = T < MERGE_TM
    if per_row:
        sh1, sc1, gt1, sh2, sc2, gt2 = [jnp.repeat(m, T, axis=0)[None] for m in jnp.split(mod, 6, axis=-1)]
    else:
        sh1, sc1, gt1, sh2, sc2, gt2 = [m[:, None, :] for m in jnp.split(mod, 6, axis=-1)]
    norms = p["norms"][:, None, :]
    x2 = x.reshape(M, D_MODEL)
    P = _proj(x2, norms[0], sc1, sh1, p["w_main"], T, M if per_row else PROJ_TM).reshape(B, T, PROJ_COLS)
    nkv = P[:, :, PROJ_KV:PROJ_KV + 6 * KVW]
    ngt = P[:, :, PROJ_MISC:PROJ_MISC + 3 * NSA_HEADS]
    ga = P[:, :, PROJ_MISC + 3 * NSA_HEADS:PROJ_MISC + 3 * NSA_HEADS + GDN_HEADS]
    gb = P[:, :, PROJ_MISC + 3 * NSA_HEADS + GDN_HEADS:PROJ_MISC + 3 * NSA_HEADS + 2 * GDN_HEADS]
    kw = nkv[:, :, 4 * KVW:].reshape(B, T, 2, NSA_KV_HEADS, HEAD_DIM)
    if nsa_past is None:
        o_nsa = _nsa_prompt(P, PROJ_Q // (NSA_HEADS * HEAD_DIM), P, PROJ_KV // (6 * KVW), ngt,
                            p["cmp_pool"], p["cmp_pe"], rel_bias)
        new_win = kw[:, T - min(NSA_WINDOW, T):]
    else:
        cache, layer, page_table = nsa_past
        o_nsa = _nsa_sample(P[:, :, PROJ_Q:PROJ_Q + NSA_HEADS * HEAD_DIM], nkv, ngt, cache, layer, page_table,
                            win_buf, p["cmp_pool"], p["cmp_pe"], rel_bias)
        real = jnp.concatenate([win_buf, kw], axis=1)
        new_win = real[:, real.shape[1] - min(NSA_WINDOW, real.shape[1]):]
    o_conv, new_conv = _conformer(P, PROJ_UCV // (2 * CONV_CH), conv_buf, p["conv_dw"], p["conv_dw_b"],
                                  p["conv_ln_g"], p["conv_ln_b"], T if per_row else CONF_TM)
    if per_row:
        Pr = jnp.pad(P[:, :, PROJ_RET:PROJ_RET + 4 * RET_W], ((0, 0), (0, RET_CHUNK - T), (0, 0)))
        o_ret, new_ret = _retention(Pr, 0, ret_s, pos0 + jnp.arange(RET_CHUNK, dtype=jnp.int32), p["ret_gn"],
                                    RET_CHUNK, T)
        o_ret = o_ret[:, :T]
    else:
        o_ret, new_ret = _retention(P, PROJ_RET // RET_W, ret_s, pos0 + jnp.arange(T, dtype=jnp.int32), p["ret_gn"],
                                    RET_TM, RET_TM)
    gdn_w = (p["gdn_conv_w"], p["gdn_A_log"], p["gdn_dt_bias"], p["gdn_norm"])
    if per_row:
        pad_rows = ((0, 0), (0, GDN_CHUNK - T), (0, 0))
        Pg = jnp.pad(P[:, :, PROJ_GQKV:PROJ_COLS], pad_rows)
        o_gdn, new_gdn = _gdn(Pg, 0, (PROJ_MISC - PROJ_GQKV) // LANES, jnp.pad(ga, pad_rows), gdn_buf, gdn_s, *gdn_w,
                              GDN_CHUNK, T)
        o_gdn = o_gdn[:, :T]
    else:
        o_gdn, new_gdn = _gdn(P, PROJ_GQKV // GDN_W, PROJ_MISC // LANES, ga, gdn_buf, gdn_s, *gdn_w,
                              GDN_RT, GDN_CHUNK)
    new_gdn_buf = jnp.concatenate([gdn_buf, P[:, :, PROJ_GQKV:PROJ_GQKV + 3 * GDN_W]], axis=1)[:, T:]
    branches = [o.reshape(M, BRANCH_W) for o in (o_nsa, o_conv, o_ret, o_gdn)]
    x1 = _merge(x2, norms[0], sc1, sh1, gt1, norms[1], branches, p["w_mg"], p["w_branch_b"], p["w_out_b"],
                T, M if per_row else MERGE_TM).reshape(B, T, D_MODEL)
    if per_row:
        s2, h2, g2 = sc2.reshape(B, T, D_MODEL), sh2.reshape(B, T, D_MODEL), gt2.reshape(B, T, D_MODEL)
        f, new_ffn = conv_ffn(rms_norm(x1, p["norms"][2]) * (1.0 + s2) + h2, ffn_buf, p["ffn_up"], p["ffn_dw"],
                              p["ffn_down"])
        x_out = x1 + g2 * rms_norm(f, p["norms"][3])
    else:
        x_out, new_ffn = _ffn(x1, norms[2], sc2, sh2, gt2, norms[3], ffn_buf, p["ffn_up_b"], p["ffn_dw"],
                              p["ffn_down_b"], FFN_TM)
    kv_rows = nkv[:, :, :4 * KVW].reshape(B, T, 4, NSA_KV_HEADS, HEAD_DIM)
    return x_out, (kv_rows, new_win, new_conv, new_ret, new_gdn_buf, new_gdn, new_ffn)


def kernel(x_prompt, x_sample, cache_nsa_kv, cache_nsa_win, state_conv, state_ret, state_gdn_conv, state_gdn,
           state_ffn_conv, page_table, c_prompt, c_sample, w_ada, b_ada, norms, w_in, cmp_pool, cmp_pe, rel_bias,
           conv_dw, conv_dw_b, conv_ln_g, conv_ln_b, ret_gn, gdn_conv_w, gdn_A_log, gdn_dt_bias, gdn_norm,
           w_branch, w_out, ffn_up, ffn_dw, ffn_down):
    B = x_prompt.shape[0]
    Bd = x_sample.shape[0]
    past = page_table.shape[1] * PAGE_SIZE
    layer_w = {"w_ada": w_ada, "b_ada": b_ada, "norms": norms, "w_in": w_in, "cmp_pool": cmp_pool,
               "cmp_pe": cmp_pe, "conv_dw": conv_dw, "conv_dw_b": conv_dw_b, "conv_ln_g": conv_ln_g,
               "conv_ln_b": conv_ln_b, "ret_gn": ret_gn, "gdn_conv_w": gdn_conv_w, "gdn_A_log": gdn_A_log,
               "gdn_dt_bias": gdn_dt_bias, "gdn_norm": gdn_norm, "w_branch": w_branch, "w_out": w_out,
               "ffn_up": ffn_up, "ffn_dw": ffn_dw, "ffn_down": ffn_down}
    yp, ys = x_prompt, x_sample
    st_p, st_s = [], []
    for l in range(DEPTH):
        p = _prep_layer({name: w[l] for name, w in layer_w.items()})
        yp, sp = trunk_layer(
            yp, c_prompt, 0, None, None,
            jnp.zeros((B, CONV_WIDTH - 1, CONV_CH), x_prompt.dtype),
            jnp.zeros((B, RET_HEADS, HEAD_DIM, HEAD_DIM), jnp.float32),
            jnp.zeros((B, GDN_CONV - 1, 3 * GDN_W), x_prompt.dtype),
            jnp.zeros((B, GDN_HEADS, HEAD_DIM, HEAD_DIM), jnp.float32),
            jnp.zeros((B, FFN_CONV - 1, D_FF), x_prompt.dtype),
            p, rel_bias)
        ys, ss = trunk_layer(
            ys, c_sample, past, (cache_nsa_kv, l, page_table), cache_nsa_win[l], state_conv[l], state_ret[l],
            state_gdn_conv[l], state_gdn[l], state_ffn_conv[l], p, rel_bias)
        st_p.append(sp)
        st_s.append(ss)

    def stack(outs, i, axis):
        return jnp.stack([o[i] for o in outs], axis=axis)

    kv_p, kv_s = stack(st_p, 0, 1), stack(st_s, 0, 1)
    win_p, win_s = stack(st_p, 1, 0), stack(st_s, 1, 0)
    conv_p, conv_s = stack(st_p, 2, 0), stack(st_s, 2, 0)
    ret_p, ret_s = stack(st_p, 3, 0), stack(st_s, 3, 0)
    gdnc_p, gdnc_s = stack(st_p, 4, 0), stack(st_s, 4, 0)
    gdn_p, gdn_s = stack(st_p, 5, 0), stack(st_s, 5, 0)
    ffn_p, ffn_s = stack(st_p, 6, 0), stack(st_s, 6, 0)
    return (yp, ys, kv_p, kv_s, win_p, win_s, conv_p, conv_s, ret_p, ret_s, gdnc_p, gdnc_s, gdn_p, gdn_s, ffn_p, ffn_s)
```

```python
import functools
import math

import jax
import jax.numpy as jnp
import numpy as np
from jax import lax
from jax.experimental import pallas as pl
from jax.experimental.pallas import tpu as pltpu

D_MODEL = 1024
BATCH = 4
SEQ = 4096
DEPTH = 2
DEC_BATCH = 32
DEC_SEQ = 8
PAST_LEN = 16384
PAGE_SIZE = 128

HEAD_DIM = 64
NSA_HEADS = 4
NSA_KV_HEADS = 2
NSA_GROUP = NSA_HEADS // NSA_KV_HEADS
NSA_BLOCK = 64
NSA_TOPK = 16
NSA_WINDOW = 512
Q_BLOCK = 128
NUM_BUCKETS = 32
MAX_DISTANCE = 128
CONV_CH = D_MODEL // 4
CONV_WIDTH = 31
RET_HEADS = 4
RET_W = RET_HEADS * HEAD_DIM
RET_CHUNK = 64
ROPE_BASE = 10000.0
GDN_HEADS = 4
GDN_W = GDN_HEADS * HEAD_DIM
GDN_CONV = 4
GDN_CHUNK = 64
D_FF = 2816
FFN_CONV = 3
N_BRANCH = 4
BRANCH_W = NSA_HEADS * HEAD_DIM
EPS = 1e-6
NEG_INF = -1e30
IN_SPLITS = (NSA_HEADS * HEAD_DIM, 6 * NSA_KV_HEADS * HEAD_DIM, 3 * NSA_HEADS, 2 * CONV_CH, 4 * RET_W,
             4 * GDN_W + 2 * GDN_HEADS, N_BRANCH * D_MODEL)
IN_COLS = sum(IN_SPLITS)

LANES = 128
SUBLANES = 8
VMEM_LIMIT = 56 * 1024 * 1024


def _round_up(a, m):
    return -(-a // m) * m


def _mm_kernel(x_ref, w_ref, o_ref):
    o_ref[...] = jnp.dot(x_ref[...].astype(jnp.bfloat16), w_ref[...].astype(jnp.bfloat16),
                         preferred_element_type=jnp.float32)


def _pmm(x, w):
    M, K = x.shape
    N = w.shape[1]
    Mp = _round_up(M, SUBLANES)
    tm = min(512, Mp)
    Mp = _round_up(Mp, tm)
    tn = 512 if N > 512 else _round_up(N, LANES)
    Np = _round_up(N, tn)
    if Mp != M:
        x = jnp.pad(x, ((0, Mp - M), (0, 0)))
    if Np != N:
        w = jnp.pad(w, ((0, 0), (0, Np - N)))
    out = pl.pallas_call(
        _mm_kernel,
        grid=(Mp // tm, Np // tn),
        in_specs=[pl.BlockSpec((tm, K), lambda i, j: (i, 0)), pl.BlockSpec((K, tn), lambda i, j: (0, j))],
        out_specs=pl.BlockSpec((tm, tn), lambda i, j: (i, j)),
        out_shape=jax.ShapeDtypeStruct((Mp, Np), jnp.float32),
        compiler_params=pltpu.CompilerParams(dimension_semantics=("arbitrary", "arbitrary"),
                                             vmem_limit_bytes=VMEM_LIMIT),
    )(x, w)
    return out[:M, :N]


def split_cols(a, sizes):
    return jnp.split(a, [int(i) for i in np.cumsum(sizes)[:-1]], axis=-1)


def rms_norm(x, g):
    x32 = x.astype(jnp.float32)
    y = x32 * lax.rsqrt(jnp.mean(x32 * x32, axis=-1, keepdims=True) + EPS)
    return (y * g.astype(jnp.float32)).astype(x.dtype)


def layer_norm(x, g, b):
    x32 = x.astype(jnp.float32)
    mu = jnp.mean(x32, axis=-1, keepdims=True)
    var = jnp.mean(jnp.square(x32 - mu), axis=-1, keepdims=True)
    return ((x32 - mu) * lax.rsqrt(var + EPS) * g.astype(jnp.float32) + b.astype(jnp.float32)).astype(x.dtype)


def causal_dwconv(x, buf, w):
    xp = jnp.concatenate([buf.astype(x.dtype), x], axis=1)
    y = lax.conv_general_dilated(xp, w[:, None, :].astype(x.dtype), window_strides=(1,), padding='VALID',
                                 dimension_numbers=('NWC', 'WIO', 'NWC'), feature_group_count=x.shape[-1])
    return y, xp[:, xp.shape[1] - (w.shape[0] - 1):]


def t5_bucket(dist):
    n = jnp.maximum(dist, 0)
    exact = NUM_BUCKETS // 2
    large = exact + (jnp.log(jnp.maximum(n, 1).astype(jnp.float32) / exact) / math.log(MAX_DISTANCE / exact)
                     * (NUM_BUCKETS - exact)).astype(jnp.int32)
    return jnp.where(n < exact, n, jnp.minimum(large, NUM_BUCKETS - 1))


def masked_softmax(s, mask):
    return jax.nn.softmax(jnp.where(mask, s, NEG_INF), axis=-1) * mask


def over_query_blocks(fn, args, q_axes, T):
    if T <= Q_BLOCK or T % Q_BLOCK:
        return fn(args)
    nq = T // Q_BLOCK

    def split(a, ax):
        return jnp.moveaxis(a.reshape(a.shape[:ax] + (nq, Q_BLOCK) + a.shape[ax + 1:]), ax, 0)

    out = lax.map(fn, tuple(split(a, ax) for a, ax in zip(args, q_axes)))
    out = jnp.moveaxis(out, 0, 1)
    return out.reshape((out.shape[0], T) + out.shape[3:])


def rotary(x, pos):
    half = x.shape[-1] // 2
    inv = ROPE_BASE ** (-jnp.arange(half, dtype=jnp.float32) / half)
    ang = pos.astype(jnp.float32)[:, None] * inv[None, :]
    cos, sin = jnp.cos(ang)[None, :, None, :], jnp.sin(ang)[None, :, None, :]
    x1, x2 = x[..., :half], x[..., half:]
    return jnp.concatenate([x1 * cos - x2 * sin, x1 * sin + x2 * cos], axis=-1)


NSA_TQ = 128
BLOCKS_PER_TILE = NSA_TQ // NSA_BLOCK
WIN_TILES = NSA_WINDOW // NSA_TQ


def _t5_thresholds():
    n = np.arange(0, 2 * MAX_DISTANCE)
    exact = NUM_BUCKETS // 2
    large = exact + (np.log(np.maximum(n, 1).astype(np.float32) / np.float32(exact))
                     / np.float32(math.log(MAX_DISTANCE / exact)) * (NUM_BUCKETS - exact)).astype(np.int32)
    bucket = np.where(n < exact, n, np.minimum(large, NUM_BUCKETS - 1))
    return tuple(int(np.argmax(bucket >= k)) for k in range(1, NUM_BUCKETS))


_T5_THR = _t5_thresholds()


def _bias_from_dist(dist, rb_ref, h):
    v = jnp.full(dist.shape, rb_ref[NUM_BUCKETS - 1, h], jnp.float32)
    for k in range(NUM_BUCKETS - 2, -1, -1):
        v = jnp.where(dist < _T5_THR[k], rb_ref[k, h], v)
    return v


def _dot_nt(a, b):
    return lax.dot_general(a, b, (((1,), (1,)), ((), ())), preferred_element_type=jnp.float32)


def _flash_tile(carry, k_t, vt_t, qs, bias, keep):
    m, l, acc = carry
    s = _dot_nt(k_t, qs) + bias
    if keep is not None:
        s = jnp.where(keep, s, NEG_INF)
    m_new = jnp.maximum(m, jnp.max(s, axis=0, keepdims=True))
    p = jnp.exp(s - m_new)
    alpha = jnp.exp(m - m_new)
    l = alpha * l + jnp.sum(p, axis=0, keepdims=True)
    acc = alpha * acc + jnp.dot(vt_t, p.astype(jnp.bfloat16), preferred_element_type=jnp.float32)
    return m_new, l, acc


def _nsa_prompt_kernel_tiled(rb_ref, q_ref, kv_ref, gt_ref, poolt_ref, pe_ref, o_ref,
                             kc_ref, vc_ref, ksel_ref, vselt_ref, kwin_ref, vwint_ref, tbl_ref, score_ref, sel_ref):
    b = pl.program_id(0)
    qi = pl.program_id(1)
    T = kv_ref.shape[1]
    nb = T // NSA_BLOCK
    topk = min(NSA_TOPK, nb)
    TQ = NSA_TQ
    CH = 512
    f32, bf16 = jnp.float32, jnp.bfloat16

    @pl.when((b == 0) & (qi == 0))
    def _tables():
        jj = lax.broadcasted_iota(jnp.int32, (TQ, TQ), 0)
        tt = lax.broadcasted_iota(jnp.int32, (TQ, TQ), 1)
        d0 = tt - jj
        for h in range(NSA_HEADS):
            kvg, g = divmod(h, NSA_GROUP)
            lanes = slice(g * TQ, (g + 1) * TQ)
            far = rb_ref[NUM_BUCKETS - 1, h]
            tbl_ref[kvg, 0, :, lanes] = jnp.where(d0 >= 0, _bias_from_dist(d0, rb_ref, h), NEG_INF)
            tbl_ref[kvg, 1, :, lanes] = _bias_from_dist(d0 + TQ, rb_ref, h)
            tbl_ref[kvg, 2, :, lanes] = jnp.where(d0 < 0, far, NEG_INF)

    @pl.when(qi == 0)
    def _prologue():
        def chunk(i, carry):
            r = pl.multiple_of(i * CH, CH)
            rb8 = pl.multiple_of(i * (CH // NSA_BLOCK), CH // NSA_BLOCK)
            for kvg in range(NSA_KV_HEADS):
                def col(c):
                    lo = c * NSA_KV_HEADS * HEAD_DIM + kvg * HEAD_DIM
                    return kv_ref[0, pl.ds(r, CH), lo:lo + HEAD_DIM]
                for c, dst in ((0, kc_ref), (1, vc_ref)):
                    x = col(c).reshape(CH // NSA_BLOCK, NSA_BLOCK, HEAD_DIM) + pe_ref[c][None]
                    dst[kvg, pl.ds(rb8, CH // NSA_BLOCK), :] = jnp.sum(x * poolt_ref[:, c:c + 1][None], axis=1)
                ksel_ref[kvg, pl.ds(r, CH), :] = col(2).astype(bf16)
                vselt_ref[kvg, :, pl.ds(r, CH)] = col(3).T.astype(bf16)
                kwin_ref[kvg, pl.ds(r, CH), :] = col(4).astype(bf16)
                vwint_ref[kvg, :, pl.ds(r, CH)] = col(5).T.astype(bf16)
            return carry
        lax.fori_loop(0, T // CH, chunk, 0)

    q = q_ref[0]
    gates = jax.nn.sigmoid(gt_ref[0])
    n_io = lax.broadcasted_iota(jnp.int32, (nb, TQ), 0)
    t_io = lax.broadcasted_iota(jnp.int32, (nb, TQ), 1)
    lane2 = lax.broadcasted_iota(jnp.int32, (1, 2 * TQ), 1)
    dist_c = qi * TQ + t_io - (n_io * NSA_BLOCK + NSA_BLOCK - 1)
    vis_c = dist_c >= 0
    vis_c2 = jnp.concatenate([vis_c, vis_c], axis=1)
    cur = (qi * TQ + t_io) // NSA_BLOCK
    forced = (n_io == 0) | (n_io == cur) | (n_io == cur - 1)
    q0 = pl.multiple_of(qi * TQ, TQ)
    outs = []
    for kvg in range(NSA_KV_HEADS):
        base = kvg * NSA_GROUP * HEAD_DIM
        qs = jnp.concatenate([q[:, base + g * HEAD_DIM: base + (g + 1) * HEAD_DIM] for g in range(NSA_GROUP)],
                             axis=0)
        qs = (qs * HEAD_DIM ** -0.5).astype(bf16)
        far_row = jnp.where(lane2 < TQ, rb_ref[NUM_BUCKETS - 1, kvg * NSA_GROUP],
                            rb_ref[NUM_BUCKETS - 1, kvg * NSA_GROUP + 1])

        sc = _dot_nt(kc_ref[kvg].astype(bf16), qs)
        bias_c = jnp.concatenate([_bias_from_dist(dist_c, rb_ref, kvg * NSA_GROUP + g) for g in range(NSA_GROUP)],
                                 axis=1)
        sc = jnp.where(vis_c2, sc + bias_c, NEG_INF)
        e = jnp.exp(sc - jnp.max(sc, axis=0, keepdims=True))
        p_c = e / jnp.sum(e, axis=0, keepdims=True) * vis_c2.astype(f32)
        oc = jnp.dot(vc_ref[kvg].T.astype(bf16), p_c.astype(bf16), preferred_element_type=f32)

        score = jnp.where(n_io <= cur, jnp.where(forced, 2.0, p_c[:, :TQ] + p_c[:, TQ:]), -1.0)
        score_ref[kvg] = score

        def rank_body(mi, rank):
            row = score_ref[kvg, pl.ds(mi, 1), :]
            beats = (row > score) | ((row == score) & (mi < n_io))
            return rank + beats.astype(jnp.int32)

        rank = lax.fori_loop(0, BLOCKS_PER_TILE * (qi + 1), rank_body, jnp.zeros((nb, TQ), jnp.int32))
        sel_ref[kvg] = ((rank < topk) & (n_io <= cur)).astype(f32)

        def sel_keep(j):
            rows = [jnp.broadcast_to(sel_ref[kvg, pl.ds(BLOCKS_PER_TILE * j + u, 1), :], (NSA_BLOCK, TQ))
                    for u in range(BLOCKS_PER_TILE)]
            mm = jnp.concatenate(rows, axis=0)
            return jnp.concatenate([mm, mm], axis=1) > 0.5

        def tile(kref, vtref, j):
            r = pl.multiple_of(j * TQ, TQ)
            return kref[kvg, pl.ds(r, TQ), :], vtref[kvg, :, pl.ds(r, TQ)]

        init = (jnp.full((1, 2 * TQ), NEG_INF, f32), jnp.zeros((1, 2 * TQ), f32), jnp.zeros((HEAD_DIM, 2 * TQ), f32))

        k_t, vt_t = tile(ksel_ref, vselt_ref, qi)
        carry = _flash_tile(init, k_t, vt_t, qs, tbl_ref[kvg, 0], sel_keep(qi))
        jp = jnp.maximum(qi - 1, 0)
        k_t, vt_t = tile(ksel_ref, vselt_ref, jp)
        carry = _flash_tile(carry, k_t, vt_t, qs, tbl_ref[kvg, 1], sel_keep(jp) & (qi >= 1))

        def far_body(i, c):
            j = qi - 2 - i
            k_f, vt_f = tile(ksel_ref, vselt_ref, j)
            return _flash_tile(c, k_f, vt_f, qs, far_row, sel_keep(j))

        m_s, l_s, acc_s = lax.fori_loop(0, jnp.maximum(qi - 1, 0), far_body, carry)

        k_t, vt_t = tile(kwin_ref, vwint_ref, qi)
        carry = _flash_tile(init, k_t, vt_t, qs, tbl_ref[kvg, 0], None)
        for back in range(1, WIN_TILES + 1):
            jb = jnp.maximum(qi - back, 0)
            k_t, vt_t = tile(kwin_ref, vwint_ref, jb)
            bias = tbl_ref[kvg, 1] if back == 1 else (tbl_ref[kvg, 2] if back == WIN_TILES else far_row)
            carry = _flash_tile(carry, k_t, vt_t, qs, bias, qi >= back)
        m_w, l_w, acc_w = carry

        o_s = acc_s / l_s
        o_w = acc_w / l_w
        for g in range(NSA_GROUP):
            h = kvg * NSA_GROUP + g
            lanes = slice(g * TQ, (g + 1) * TQ)
            o = (gates[h:h + 1] * oc[:, lanes] + gates[NSA_HEADS + h:NSA_HEADS + h + 1] * o_s[:, lanes]
                 + gates[2 * NSA_HEADS + h:2 * NSA_HEADS + h + 1] * o_w[:, lanes])
            outs.append(o.T)
    o_ref[0] = jnp.concatenate(outs, axis=1)


FAR_GROUP = 4
NEAR_ROWS = 2 * NSA_TQ
WIN_ROWS = NSA_WINDOW + NSA_TQ


def _nsa_prompt_kernel(rb_ref, q_ref, kv_ref, gt_ref, poolt_ref, pe_ref, o_ref,
                       kc_ref, vc_ref, ksel_ref, vselt_ref, kwin_ref, vwint_ref, near_ref, wtbl_ref, score_ref,
                       sel_ref):
    b = pl.program_id(0)
    qi = pl.program_id(1)
    T = kv_ref.shape[1]
    nb = T // NSA_BLOCK
    topk = min(NSA_TOPK, nb)
    TQ = NSA_TQ
    CH = 512
    f32, bf16 = jnp.float32, jnp.bfloat16
    kvs = range(NSA_KV_HEADS)

    @pl.when((b == 0) & (qi == 0))
    def _tables():
        for kvg in kvs:
            ksel_ref[kvg, 0:TQ, :] = jnp.zeros((TQ, HEAD_DIM), bf16)
            vselt_ref[kvg, :, 0:TQ] = jnp.zeros((HEAD_DIM, TQ), bf16)
            kwin_ref[kvg, 0:NSA_WINDOW, :] = jnp.zeros((NSA_WINDOW, HEAD_DIM), bf16)
            vwint_ref[kvg, :, 0:NSA_WINDOW] = jnp.zeros((HEAD_DIM, NSA_WINDOW), bf16)
        d_near = (lax.broadcasted_iota(jnp.int32, (NEAR_ROWS, TQ), 1) + TQ
                  - lax.broadcasted_iota(jnp.int32, (NEAR_ROWS, TQ), 0))
        d_win = (lax.broadcasted_iota(jnp.int32, (WIN_ROWS, TQ), 1) + NSA_WINDOW
                 - lax.broadcasted_iota(jnp.int32, (WIN_ROWS, TQ), 0))
        for h in range(NSA_HEADS):
            kvg, g = divmod(h, NSA_GROUP)
            lanes = slice(g * TQ, (g + 1) * TQ)
            near_ref[kvg, :, lanes] = jnp.where(d_near >= 0, _bias_from_dist(d_near, rb_ref, h), NEG_INF)
            wtbl_ref[kvg, :, lanes] = jnp.where((d_win >= 0) & (d_win < NSA_WINDOW),
                                                _bias_from_dist(d_win, rb_ref, h), NEG_INF)

    @pl.when(qi == 0)
    def _prologue():
        def chunk(i, carry):
            r = pl.multiple_of(i * CH, CH)
            rs = pl.multiple_of(i * CH + TQ, TQ)
            rw = pl.multiple_of(i * CH + NSA_WINDOW, TQ)
            rb8 = pl.multiple_of(i * (CH // NSA_BLOCK), CH // NSA_BLOCK)
            for kvg in kvs:
                def col(c):
                    lo = c * NSA_KV_HEADS * HEAD_DIM + kvg * HEAD_DIM
                    return kv_ref[0, pl.ds(r, CH), lo:lo + HEAD_DIM]
                for c, dst in ((0, kc_ref), (1, vc_ref)):
                    x = col(c).reshape(CH // NSA_BLOCK, NSA_BLOCK, HEAD_DIM) + pe_ref[c][None]
                    dst[kvg, pl.ds(rb8, CH // NSA_BLOCK), :] = jnp.sum(x * poolt_ref[:, c:c + 1][None], axis=1)
                ksel_ref[kvg, pl.ds(rs, CH), :] = col(2).astype(bf16)
                vselt_ref[kvg, :, pl.ds(rs, CH)] = col(3).T.astype(bf16)
                kwin_ref[kvg, pl.ds(rw, CH), :] = col(4).astype(bf16)
                vwint_ref[kvg, :, pl.ds(rw, CH)] = col(5).T.astype(bf16)
            return carry
        lax.fori_loop(0, T // CH, chunk, 0)

    q = q_ref[0]
    gates = jax.nn.sigmoid(gt_ref[0])
    n_io = lax.broadcasted_iota(jnp.int32, (nb, TQ), 0)
    t_io = lax.broadcasted_iota(jnp.int32, (nb, TQ), 1)
    lane2 = lax.broadcasted_iota(jnp.int32, (1, 2 * TQ), 1)
    dist_c = qi * TQ + t_io - (n_io * NSA_BLOCK + NSA_BLOCK - 1)
    vis_c = dist_c >= 0
    vis_c2 = jnp.concatenate([vis_c, vis_c], axis=1)
    cur = (qi * TQ + t_io) // NSA_BLOCK
    forced = (n_io == 0) | (n_io == cur) | (n_io == cur - 1)
    q0 = pl.multiple_of(qi * TQ, TQ)

    qs, far_row, oc, score = [], [], [], []
    for kvg in kvs:
        base = kvg * NSA_GROUP * HEAD_DIM
        qk = jnp.concatenate([q[:, base + g * HEAD_DIM: base + (g + 1) * HEAD_DIM] for g in range(NSA_GROUP)], axis=0)
        qs.append((qk * HEAD_DIM ** -0.5).astype(bf16))
        far_row.append(jnp.where(lane2 < TQ, rb_ref[NUM_BUCKETS - 1, kvg * NSA_GROUP],
                                 rb_ref[NUM_BUCKETS - 1, kvg * NSA_GROUP + 1]))
        sc = _dot_nt(kc_ref[kvg].astype(bf16), qs[kvg])
        bias_c = jnp.concatenate([_bias_from_dist(dist_c, rb_ref, kvg * NSA_GROUP + g) for g in range(NSA_GROUP)],
                                 axis=1)
        sc = jnp.where(vis_c2, sc + bias_c, NEG_INF)
        e = jnp.exp(sc - jnp.max(sc, axis=0, keepdims=True))
        p_c = e / jnp.sum(e, axis=0, keepdims=True) * vis_c2.astype(f32)
        oc.append(jnp.dot(vc_ref[kvg].T.astype(bf16), p_c.astype(bf16), preferred_element_type=f32))
        score.append(jnp.where(n_io <= cur, jnp.where(forced, 2.0, p_c[:, :TQ] + p_c[:, TQ:]), -1.0))
        score_ref[kvg] = score[kvg]

    def rank_body(mi, ranks):
        out = []
        for kvg in kvs:
            row = score_ref[kvg, pl.ds(mi, 1), :]
            beats = (row > score[kvg]) | ((row == score[kvg]) & (mi < n_io))
            out.append(ranks[kvg] + beats.astype(jnp.int32))
        return tuple(out)

    ranks = lax.fori_loop(0, BLOCKS_PER_TILE * (qi + 1), rank_body,
                          tuple(jnp.zeros((nb, TQ), jnp.int32) for _ in kvs))
    for kvg in kvs:
        sel_ref[kvg] = ((ranks[kvg] < topk) & (n_io <= cur)).astype(f32)

    def keep_rows(kvg, blk0, nblk, limit):
        rows = []
        for u in range(nblk):
            blk = blk0 + u
            ok = (blk >= 0) & (blk < limit)
            row = sel_ref[kvg, pl.ds(jnp.clip(blk, 0, nb - 1), 1), :]
            rows.append(jnp.broadcast_to(jnp.where(ok, row, 0.0), (NSA_BLOCK, TQ)))
        mm = jnp.concatenate(rows, axis=0)
        return jnp.concatenate([mm, mm], axis=1) > 0.5

    carries = []
    for kvg in kvs:
        k_n = ksel_ref[kvg, pl.ds(q0, NEAR_ROWS), :]
        vt_n = vselt_ref[kvg, :, pl.ds(q0, NEAR_ROWS)]
        keep = keep_rows(kvg, BLOCKS_PER_TILE * (qi - 1), 2 * BLOCKS_PER_TILE, nb)
        s = jnp.where(keep, _dot_nt(k_n, qs[kvg]) + near_ref[kvg], NEG_INF)
        m = jnp.max(s, axis=0, keepdims=True)
        p = jnp.exp(s - m)
        carries.append((m, jnp.sum(p, axis=0, keepdims=True),
                        jnp.dot(vt_n, p.astype(bf16), preferred_element_type=f32)))

    n_far = jnp.maximum(qi - 1, 0)
    rows_far = FAR_GROUP * TQ

    def far_body(i, cs):
        out = []
        for kvg in kvs:
            r = pl.multiple_of(TQ + i * rows_far, TQ)
            keep = keep_rows(kvg, i * FAR_GROUP * BLOCKS_PER_TILE, FAR_GROUP * BLOCKS_PER_TILE,
                             n_far * BLOCKS_PER_TILE)
            out.append(_flash_tile(cs[kvg], ksel_ref[kvg, pl.ds(r, rows_far), :],
                                   vselt_ref[kvg, :, pl.ds(r, rows_far)], qs[kvg], far_row[kvg], keep))
        return tuple(out)

    carries = lax.fori_loop(0, (n_far + FAR_GROUP - 1) // FAR_GROUP, far_body, tuple(carries))

    w_io = lax.broadcasted_iota(jnp.int32, (WIN_ROWS, 2 * TQ), 0)
    outs = []
    for kvg in kvs:
        m_s, l_s, acc_s = carries[kvg]
        o_s = acc_s / l_s
        k_w = kwin_ref[kvg, pl.ds(q0, WIN_ROWS), :]
        vt_w = vwint_ref[kvg, :, pl.ds(q0, WIN_ROWS)]
        s = jnp.where(w_io >= NSA_WINDOW - q0, _dot_nt(k_w, qs[kvg]) + wtbl_ref[kvg], NEG_INF)
        e = jnp.exp(s - jnp.max(s, axis=0, keepdims=True))
        o_w = (jnp.dot(vt_w, e.astype(bf16), preferred_element_type=f32) / jnp.sum(e, axis=0, keepdims=True))
        for g in range(NSA_GROUP):
            h = kvg * NSA_GROUP + g
            lanes = slice(g * TQ, (g + 1) * TQ)
            o = (gates[h:h + 1] * oc[kvg][:, lanes] + gates[NSA_HEADS + h:NSA_HEADS + h + 1] * o_s[:, lanes]
                 + gates[2 * NSA_HEADS + h:2 * NSA_HEADS + h + 1] * o_w[:, lanes])
            outs.append(o.T)
    o_ref[0] = jnp.concatenate(outs, axis=1)


def _nsa_prompt(q, q_blk, kv, kv_blk, gate_logits, cmp_pool, cmp_pe, rel_bias):
    B, T, _ = q.shape
    nb = T // NSA_BLOCK
    assert T % 512 == 0 and nb % SUBLANES == 0
    TQ = NSA_TQ
    gt = jnp.transpose(gate_logits, (0, 2, 1))
    f32, bf16 = jnp.float32, jnp.bfloat16
    return pl.pallas_call(
        _nsa_prompt_kernel,
        grid=(B, T // TQ),
        in_specs=[
            pl.BlockSpec(memory_space=pltpu.SMEM),
            pl.BlockSpec((1, TQ, NSA_HEADS * HEAD_DIM), lambda b, i: (b, i, q_blk)),
            pl.BlockSpec((1, T, 6 * NSA_KV_HEADS * HEAD_DIM), lambda b, i: (b, 0, kv_blk)),
            pl.BlockSpec((1, 3 * NSA_HEADS, TQ), lambda b, i: (b, 0, i)),
            pl.BlockSpec((NSA_BLOCK, 2), lambda b, i: (0, 0)),
            pl.BlockSpec((2, NSA_BLOCK, HEAD_DIM), lambda b, i: (0, 0, 0)),
        ],
        out_specs=pl.BlockSpec((1, TQ, NSA_HEADS * HEAD_DIM), lambda b, i: (b, i, 0)),
        out_shape=jax.ShapeDtypeStruct((B, T, NSA_HEADS * HEAD_DIM), f32),
        scratch_shapes=[
            pltpu.VMEM((NSA_KV_HEADS, nb, HEAD_DIM), f32),
            pltpu.VMEM((NSA_KV_HEADS, nb, HEAD_DIM), f32),
            pltpu.VMEM((NSA_KV_HEADS, T + TQ, HEAD_DIM), bf16),
            pltpu.VMEM((NSA_KV_HEADS, HEAD_DIM, T + TQ), bf16),
            pltpu.VMEM((NSA_KV_HEADS, T + NSA_WINDOW, HEAD_DIM), bf16),
            pltpu.VMEM((NSA_KV_HEADS, HEAD_DIM, T + NSA_WINDOW), bf16),
            pltpu.VMEM((NSA_KV_HEADS, NEAR_ROWS, NSA_GROUP * TQ), f32),
            pltpu.VMEM((NSA_KV_HEADS, WIN_ROWS, NSA_GROUP * TQ), f32),
            pltpu.VMEM((NSA_KV_HEADS, nb, TQ), f32),
            pltpu.VMEM((NSA_KV_HEADS, nb, TQ), f32),
        ],
        compiler_params=pltpu.CompilerParams(dimension_semantics=("arbitrary", "arbitrary"),
                                             vmem_limit_bytes=VMEM_LIMIT),
        name="nsa_prompt",
    )(rel_bias, q, kv, gt, jnp.transpose(cmp_pool), cmp_pe)


PAGES_PER_STEP = 8
SEL_TILE = 2048
KVW = NSA_KV_HEADS * HEAD_DIM


def _nsa_sample_kernel(pt_ref, rb_ref, *refs):
    PPS = PAGES_PER_STEP
    pages = refs[:PPS]
    q_ref, kvn_ref, gt_ref, win_ref, poolm_ref, cconst_ref, o_ref, cmp_ref, kselt_ref, vselt_ref = refs[PPS:]
    s_id = pl.program_id(1)
    f32, bf16 = jnp.float32, jnp.bfloat16
    Tn = q_ref.shape[1]
    past = kselt_ref.shape[1]
    nbp = past // NSA_BLOCK
    Wb = win_ref.shape[3]
    bpp = PAGE_SIZE // NSA_BLOCK
    R = NSA_HEADS * Tn

    rows = []
    for k in range(PPS):
        parts = []
        for c in range(2):
            x = pages[k][0, 0, c]
            x_hi = x.astype(bf16)
            x_lo = (x - x_hi.astype(f32)).astype(bf16)
            a = _dot_nt(poolm_ref[c], x_hi)
            b = _dot_nt(poolm_ref[c, 0:SUBLANES], x_lo)
            parts.append(a[0:bpp] + a[SUBLANES:SUBLANES + bpp] + b[0:bpp] + cconst_ref[c:c + 1])
        rows.append(jnp.concatenate(parts, axis=1))
        r = pl.multiple_of((s_id * PPS + k) * PAGE_SIZE, PAGE_SIZE)
        kselt_ref[:, pl.ds(r, PAGE_SIZE)] = pages[k][0, 0, 2].astype(bf16)
        vselt_ref[:, pl.ds(r, PAGE_SIZE)] = pages[k][0, 0, 3].astype(bf16)
    cmp_ref[pl.ds(pl.multiple_of(s_id * PPS * bpp, PPS * bpp), PPS * bpp), :] = jnp.concatenate(rows, axis=0)

    @pl.when(s_id == pl.num_programs(1) - 1)
    def _attend():
        def per_head(fn):
            return jnp.concatenate([fn(h) for h in range(NSA_HEADS)], axis=0)

        q = q_ref[0] * HEAD_DIM ** -0.5
        zero = jnp.zeros((Tn, HEAD_DIM), f32)

        def q_rows(h):
            qh = q[:, h * HEAD_DIM:(h + 1) * HEAD_DIM]
            return jnp.concatenate([qh, zero] if h < NSA_GROUP else [zero, qh], axis=1)

        q2 = per_head(q_rows).astype(bf16)
        kvn = kvn_ref[0]
        pad = jnp.zeros((LANES - Tn, KVW), f32)

        def new_rows(c):
            return jnp.concatenate([kvn[:, c * KVW:(c + 1) * KVW], pad], axis=0).astype(bf16)

        tn_io = lax.broadcasted_iota(jnp.int32, (Tn, LANES), 0)
        jn_io = lax.broadcasted_iota(jnp.int32, (Tn, LANES), 1)
        d_new = tn_io - jn_io
        keep_new = per_head(lambda h: (d_new >= 0) & (jn_io < Tn))
        bias_new = per_head(lambda h: _bias_from_dist(d_new, rb_ref, h))
        far = per_head(lambda h: jnp.full((Tn, 1), rb_ref[NUM_BUCKETS - 1, h], f32))

        kc = cmp_ref[:, 0:KVW].astype(bf16)
        vc = cmp_ref[:, KVW:2 * KVW].astype(bf16)
        n_io = lax.broadcasted_iota(jnp.int32, (Tn, nbp), 1)
        t_io = lax.broadcasted_iota(jnp.int32, (Tn, nbp), 0)
        dist_c = past + t_io - (n_io * NSA_BLOCK + NSA_BLOCK - 1)
        sc = _dot_nt(q2, kc) + per_head(lambda h: _bias_from_dist(dist_c, rb_ref, h))
        e = jnp.exp(sc - jnp.max(sc, axis=1, keepdims=True))
        p_c = e / jnp.sum(e, axis=1, keepdims=True)
        oc = jnp.dot(p_c.astype(bf16), vc, preferred_element_type=f32)

        topk = min(NSA_TOPK, nbp + 1)
        m_io = lax.broadcasted_iota(jnp.int32, (nbp, nbp), 0)
        c_io = lax.broadcasted_iota(jnp.int32, (nbp, nbp), 1)
        lower = m_io < c_io
        forced = (n_io == 0) | (n_io == nbp - 1)
        sels = []
        for kvg in range(NSA_KV_HEADS):
            r0 = kvg * NSA_GROUP * Tn
            score = jnp.where(forced, 2.0, p_c[r0:r0 + Tn] + p_c[r0 + Tn:r0 + 2 * Tn])
            score_t = jnp.concatenate([score, jnp.zeros((LANES - Tn, nbp), f32)], axis=0).T
            ranks = []
            for t in range(Tn):
                colb = jnp.broadcast_to(score_t[:, t:t + 1], (nbp, nbp))
                rowb = jnp.broadcast_to(score[t:t + 1, :], (nbp, nbp))
                beats = (colb > rowb) | ((colb == rowb) & lower)
                ranks.append(jnp.sum(beats.astype(f32), axis=0, keepdims=True))
            sel = (jnp.concatenate(ranks, axis=0) < topk - 1).astype(f32)
            sels += [sel] * NSA_GROUP
        sel_rows = jnp.concatenate(sels, axis=0).astype(bf16)

        bpt = SEL_TILE // NSA_BLOCK
        expand = (lax.broadcasted_iota(jnp.int32, (bpt, SEL_TILE), 0)
                  == lax.broadcasted_iota(jnp.int32, (bpt, SEL_TILE), 1) // NSA_BLOCK).astype(bf16)
        d_last = LANES + tn_io - jn_io
        near = per_head(lambda h: _bias_from_dist(d_last, rb_ref, h))
        m = jnp.full((R, 1), NEG_INF, f32)
        l = jnp.zeros((R, 1), f32)
        acc = jnp.zeros((R, KVW), f32)

        def flash(carry, s, v_t, v_feature_major):
            m, l, acc = carry
            m_new = jnp.maximum(m, jnp.max(s, axis=1, keepdims=True))
            p = jnp.exp(s - m_new)
            alpha = jnp.exp(m - m_new)
            pb = p.astype(bf16)
            pv = _dot_nt(pb, v_t) if v_feature_major else jnp.dot(pb, v_t, preferred_element_type=f32)
            return m_new, alpha * l + jnp.sum(p, axis=1, keepdims=True), alpha * acc + pv

        carry = (m, l, acc)
        ntile = past // SEL_TILE
        for j in range(ntile):
            k_t = kselt_ref[:, j * SEL_TILE:(j + 1) * SEL_TILE]
            v_t = vselt_ref[:, j * SEL_TILE:(j + 1) * SEL_TILE]
            keep = jnp.dot(sel_rows[:, j * bpt:(j + 1) * bpt], expand, preferred_element_type=f32) > 0.5
            if j == ntile - 1:
                bias = jnp.concatenate([jnp.broadcast_to(far, (R, SEL_TILE - LANES)), near], axis=1)
            else:
                bias = far
            s_t = jnp.dot(q2, k_t, preferred_element_type=f32)
            carry = flash(carry, jnp.where(keep, s_t + bias, NEG_INF), v_t, True)
        s_new = jnp.where(keep_new, _dot_nt(q2, new_rows(2)) + bias_new, NEG_INF)
        m, l, acc = flash(carry, s_new, new_rows(3), False)
        o_s = acc / l

        tw_io = lax.broadcasted_iota(jnp.int32, (Tn, Wb), 0)
        cw_io = lax.broadcasted_iota(jnp.int32, (Tn, Wb), 1)
        d_w = Wb + tw_io - cw_io
        s_w = jnp.where(per_head(lambda h: d_w < NSA_WINDOW),
                        jnp.dot(q2, win_ref[0, 0].astype(bf16), preferred_element_type=f32)
                        + per_head(lambda h: _bias_from_dist(d_w, rb_ref, h)), NEG_INF)
        s_wn = jnp.where(keep_new, _dot_nt(q2, new_rows(4)) + bias_new, NEG_INF)
        s_all = jnp.concatenate([s_w, s_wn], axis=1)
        e = jnp.exp(s_all - jnp.max(s_all, axis=1, keepdims=True))
        p_w = (e / jnp.sum(e, axis=1, keepdims=True)).astype(bf16)
        o_w = _dot_nt(p_w[:, :Wb], win_ref[0, 1].astype(bf16)) + jnp.dot(p_w[:, Wb:], new_rows(5),
                                                                          preferred_element_type=f32)

        gates = jax.nn.sigmoid(gt_ref[0])
        outs = []
        for h in range(NSA_HEADS):
            rs = slice(h * Tn, (h + 1) * Tn)
            cs = slice((h // NSA_GROUP) * HEAD_DIM, (h // NSA_GROUP + 1) * HEAD_DIM)
            outs.append(gates[:, h:h + 1] * oc[rs, cs] + gates[:, NSA_HEADS + h:NSA_HEADS + h + 1] * o_s[rs, cs]
                        + gates[:, 2 * NSA_HEADS + h:2 * NSA_HEADS + h + 1] * o_w[rs, cs])
        o_ref[0] = jnp.concatenate(outs, axis=1)


def _nsa_sample(q, kv, gate_logits, cache, layer, page_table, win_buf, cmp_pool, cmp_pe, rel_bias):
    B, Tn, _ = q.shape
    npages = page_table.shape[1]
    past = npages * PAGE_SIZE
    Wb = win_buf.shape[1]
    PPS = PAGES_PER_STEP
    assert npages % PPS == 0 and past % SEL_TILE == 0 and Tn == SUBLANES and (past // NSA_BLOCK) % LANES == 0
    f32, bf16 = jnp.float32, jnp.bfloat16
    bpp = PAGE_SIZE // NSA_BLOCK
    cache_t = jnp.transpose(cache, (0, 1, 3, 4, 5, 2)).reshape(cache.shape[0], cache.shape[1], 4, KVW, PAGE_SIZE)
    win_t = jnp.transpose(win_buf, (0, 2, 3, 4, 1)).reshape(B, 2, KVW, Wb)
    r_io = np.arange(PAGE_SIZE)
    onehot = jnp.asarray((r_io[None, :] // NSA_BLOCK == np.arange(SUBLANES)[:, None]), f32)
    pool_full = onehot[None] * jnp.tile(cmp_pool, (1, bpp))[:, None, :]
    pool_hi = pool_full.astype(bf16)
    pool_lo = (pool_full - pool_hi.astype(f32)).astype(bf16)
    poolm = jnp.concatenate([pool_hi, pool_lo], axis=1)
    cconst = jnp.tile(jnp.sum(cmp_pool[:, :, None] * cmp_pe, axis=1), (1, NSA_KV_HEADS))

    def page_spec(k):
        return pl.BlockSpec((1, 1, 4, KVW, PAGE_SIZE), lambda b, s, pt: (pt[b, s * PPS + k], layer, 0, 0, 0))

    grid_spec = pltpu.PrefetchScalarGridSpec(
        num_scalar_prefetch=1,
        grid=(B, npages // PPS),
        in_specs=[pl.BlockSpec(memory_space=pltpu.SMEM)] + [page_spec(k) for k in range(PPS)] + [
            pl.BlockSpec((1, Tn, NSA_HEADS * HEAD_DIM), lambda b, s, pt: (b, 0, 0)),
            pl.BlockSpec((1, Tn, 6 * KVW), lambda b, s, pt: (b, 0, 0)),
            pl.BlockSpec((1, Tn, 3 * NSA_HEADS), lambda b, s, pt: (b, 0, 0)),
            pl.BlockSpec((1, 2, KVW, Wb), lambda b, s, pt: (b, 0, 0, 0)),
            pl.BlockSpec((2, 2 * SUBLANES, PAGE_SIZE), lambda b, s, pt: (0, 0, 0)),
            pl.BlockSpec((2, KVW), lambda b, s, pt: (0, 0)),
        ],
        out_specs=pl.BlockSpec((1, Tn, NSA_HEADS * HEAD_DIM), lambda b, s, pt: (b, 0, 0)),
        scratch_shapes=[
            pltpu.VMEM((past // NSA_BLOCK, 2 * KVW), f32),
            pltpu.VMEM((KVW, past), bf16),
            pltpu.VMEM((KVW, past), bf16),
        ],
    )
    return pl.pallas_call(
        _nsa_sample_kernel,
        grid_spec=grid_spec,
        out_shape=jax.ShapeDtypeStruct((B, Tn, NSA_HEADS * HEAD_DIM), f32),
        compiler_params=pltpu.CompilerParams(dimension_semantics=("arbitrary", "arbitrary"),
                                             vmem_limit_bytes=VMEM_LIMIT),
        name="nsa_sample",
    )(page_table, rel_bias, *([cache_t] * PPS), q, kv, gate_logits, win_t, poolm, cconst)


def nsa_attention(q, kv_new, gate_logits, past_rows, win_buf, pos0, cmp_pool, cmp_pe, rel_bias):
    B, T = q.shape[:2]
    if past_rows is None:
        o = _nsa_prompt(q.reshape(B, T, -1), 0, kv_new.reshape(B, T, -1), 0, gate_logits.reshape(B, T, -1),
                        cmp_pool, cmp_pe, rel_bias)
        return o, kv_new[:, T - min(NSA_WINDOW, T):, 4:]
    if isinstance(past_rows, tuple):
        cache, layer, page_table = past_rows
        o = _nsa_sample(q.reshape(B, T, -1), kv_new.reshape(B, T, -1), gate_logits.reshape(B, T, -1),
                        cache, layer, page_table, win_buf, cmp_pool, cmp_pe, rel_bias)
        real = jnp.concatenate([win_buf, kv_new[:, :, 4:]], axis=1)
        return o, real[:, real.shape[1] - min(NSA_WINDOW, real.shape[1]):]
    dt = q.dtype
    qpos = pos0 + jnp.arange(T, dtype=jnp.int32)
    qg = (q * HEAD_DIM ** -0.5).reshape(B, T, NSA_KV_HEADS, NSA_GROUP, HEAD_DIM)
    rb = rel_bias.astype(jnp.float32).reshape(NUM_BUCKETS, NSA_KV_HEADS, NSA_GROUP)
    L = T if past_rows is None else past_rows.shape[1] + T
    nb = -(-L // NSA_BLOCK)
    new_rows = jnp.pad(kv_new[:, :, :4], ((0, 0), (0, nb * NSA_BLOCK - L), (0, 0), (0, 0), (0, 0)))
    full = new_rows if past_rows is None else jnp.concatenate([past_rows.astype(dt), new_rows], axis=1)
    blocks = full.reshape(B, nb, NSA_BLOCK, 4, NSA_KV_HEADS, HEAD_DIM)
    pe = jnp.transpose(cmp_pe, (1, 0, 2))[:, :, None, :].astype(dt)
    cmp = jnp.einsum('bnjckd,cj->bnckd', blocks[:, :, :, :2] + pe, cmp_pool.astype(dt))
    k_c, v_c = cmp[:, :, 0], cmp[:, :, 1]
    blk = jnp.arange(nb, dtype=jnp.int32)
    d_c = qpos[:, None] - (blk * NSA_BLOCK + NSA_BLOCK - 1)[None, :]
    s_c = jnp.einsum('btkgd,bnkd->bkgtn', qg, k_c).astype(jnp.float32) + jnp.transpose(rb[t5_bucket(d_c)], (2, 3, 0, 1))
    p_c = masked_softmax(s_c, d_c >= 0)
    o_c = jnp.einsum('bkgtn,bnkd->btkgd', p_c.astype(dt), v_c)
    cur = (qpos // NSA_BLOCK)[:, None]
    forced = (blk[None] == 0) | (blk[None] == cur) | (blk[None] == cur - 1)
    score = jnp.where(blk[None] <= cur, jnp.where(forced, 2.0, p_c.sum(axis=2)), -1.0)
    top_s, idx = lax.top_k(score, min(NSA_TOPK, nb))
    ok = top_s > -0.5
    ks = jnp.transpose(blocks[:, :, :, 2], (0, 3, 1, 2, 4))
    vs = jnp.transpose(blocks[:, :, :, 3], (0, 3, 1, 2, 4))
    take = jax.vmap(jax.vmap(lambda a, i: a[i]))
    kv_ix = jnp.arange(NSA_KV_HEADS)[None, :, None, None, None]
    rbk = jnp.transpose(rb, (1, 0, 2))

    def sel_block(args):
        qb, ib, okb, pb = args
        Bq, Q = qb.shape[:2]
        kg, vg = take(ks, ib), take(vs, ib)
        kpos = ib[..., None] * NSA_BLOCK + jnp.arange(NSA_BLOCK, dtype=jnp.int32)
        dist = pb[None, None, :, None, None] - kpos
        mask = (okb[..., None] & (dist >= 0)).reshape(Bq, NSA_KV_HEADS, 1, Q, -1)
        bias = jnp.moveaxis(rbk[kv_ix, t5_bucket(dist)], -1, 2)
        s = jnp.einsum('bqkgd,bkqsjd->bkgqsj', qb, kg).astype(jnp.float32) + bias
        p = masked_softmax(s.reshape(Bq, NSA_KV_HEADS, NSA_GROUP, Q, -1), mask)
        return jnp.einsum('bkgqn,bkqnd->bqkgd', p.astype(dt), vg.reshape(Bq, NSA_KV_HEADS, Q, -1, HEAD_DIM))

    o_s = over_query_blocks(sel_block, (qg, idx, ok, qpos), (1, 2, 2, 0), T)
    kw = kv_new[:, :, 4:]
    if win_buf is None:
        real = kw
        k_all = jnp.pad(kw, ((0, 0), (NSA_WINDOW, 0), (0, 0), (0, 0), (0, 0)))
        span = NSA_WINDOW
    else:
        real = jnp.concatenate([win_buf.astype(dt), kw], axis=1)
        k_all = real
        span = win_buf.shape[1]
    k0 = pos0 - span

    def win_block(args):
        qb, pb = args
        Q = qb.shape[1]
        start = pb[0] - pos0
        kb = lax.dynamic_slice_in_dim(k_all, start, span + Q, axis=1)
        kpos = k0 + start + jnp.arange(span + Q, dtype=jnp.int32)
        dist = pb[:, None] - kpos[None, :]
        mask = (dist >= 0) & (dist < NSA_WINDOW) & (kpos >= 0)[None, :]
        s = jnp.einsum('bqkgd,bnkd->bkgqn', qb, kb[:, :, 0]).astype(jnp.float32) + jnp.transpose(rb[t5_bucket(dist)], (2, 3, 0, 1))
        p = masked_softmax(s, mask)
        return jnp.einsum('bkgqn,bnkd->bqkgd', p.astype(dt), kb[:, :, 1])

    o_w = over_query_blocks(win_block, (qg, qpos), (1, 0), T)
    new_win = real[:, real.shape[1] - min(NSA_WINDOW, real.shape[1]):]
    g = jax.nn.sigmoid(gate_logits.astype(jnp.float32)).astype(dt).reshape(B, T, 3, NSA_KV_HEADS, NSA_GROUP, 1)
    o = g[:, :, 0] * o_c + g[:, :, 1] * o_s + g[:, :, 2] * o_w
    return o.reshape(B, T, NSA_HEADS * HEAD_DIM), new_win


def conformer_conv(u, buf, dw, dw_b, ln_g, ln_b):
    a, gte = jnp.split(u, 2, axis=-1)
    y, new_buf = causal_dwconv(a * jax.nn.sigmoid(gte), buf, dw)
    y = layer_norm(y + dw_b.astype(y.dtype), ln_g, ln_b)
    return jax.nn.silu(y), new_buf


def retention(q, k, v, gate, S0, pos0, gn):
    B, T, _ = q.shape
    dt = q.dtype
    H, d = RET_HEADS, HEAD_DIM
    pos = pos0 + jnp.arange(T, dtype=jnp.int32)
    f = lambda a: a.astype(jnp.float32).reshape(B, T, H, d)
    q = rotary(f(q), pos)
    k = rotary(f(k), pos) * d ** -0.5
    v = f(v)
    C = math.gcd(T, RET_CHUNK)
    n = T // C
    lg = jnp.log1p(-jnp.exp2(-5.0 - jnp.arange(H, dtype=jnp.float32)))
    ch = lambda a: a.reshape(B, n, C, H, d).transpose(0, 3, 1, 2, 4)
    qc, kc, vc = ch(q), ch(k), ch(v)
    i = jnp.arange(C, dtype=jnp.float32)
    diff = i[:, None] - i[None, :]
    Dm = jnp.where(diff >= 0, jnp.exp(jnp.maximum(diff, 0.0)[None] * lg[:, None, None]), 0.0)
    att = jnp.einsum('bhncd,bhnsd->bhncs', qc, kc) * Dm[None, :, None]
    o = jnp.einsum('bhncs,bhnse->bhnce', att, vc)
    xi = jnp.exp((i[None, :] + 1.0) * lg[:, None])
    zeta = jnp.exp((C - 1.0 - i[None, :]) * lg[:, None])
    kv = jnp.einsum('bhncd,bhnce->nbhde', kc * zeta[None, :, None, :, None], vc)
    decay = jnp.exp(C * lg)[None, :, None, None]

    def step(S, kvn):
        return S * decay + kvn, S

    S_fin, S_prev = lax.scan(step, S0.astype(jnp.float32), kv)
    o = o + jnp.einsum('bhncd,nbhde->bhnce', qc * xi[None, :, None, :, None], S_prev)
    o = o.transpose(0, 2, 3, 1, 4).reshape(B, T, H, d)
    mu = jnp.mean(o, axis=-1, keepdims=True)
    var = jnp.mean(jnp.square(o - mu), axis=-1, keepdims=True)
    o = ((o - mu) * lax.rsqrt(var + EPS)).reshape(B, T, H * d) * gn.astype(jnp.float32)
    return (o * jax.nn.silu(gate.astype(jnp.float32))).astype(dt), S_fin


def gated_deltanet(qkv, z, a, b, conv_buf, S0, conv_w, A_log, dt_bias, norm_g):
    B, T, _ = qkv.shape
    dt = qkv.dtype
    H, d = GDN_HEADS, HEAD_DIM
    y, new_buf = causal_dwconv(qkv, conv_buf, conv_w)
    y = jax.nn.silu(y.astype(jnp.float32))
    q, k, v = [t.reshape(B, T, H, d) for t in jnp.split(y, 3, axis=-1)]
    l2 = lambda t: t * lax.rsqrt(jnp.sum(t * t, axis=-1, keepdims=True) + EPS)
    q = l2(q) * d ** -0.5
    k = l2(k)
    g = -jnp.exp(A_log.astype(jnp.float32)) * jax.nn.softplus(a.astype(jnp.float32) + dt_bias.astype(jnp.float32))
    beta = jax.nn.sigmoid(b.astype(jnp.float32))
    C = math.gcd(T, GDN_CHUNK)
    n = T // C
    ch = lambda t: t.reshape(B, n, C, H, d).transpose(0, 3, 1, 2, 4)
    chs = lambda t: t.reshape(B, n, C, H).transpose(0, 3, 1, 2)
    qc, kc, vc = ch(q), ch(k), ch(v)
    gc = jnp.cumsum(chs(g), axis=-1)
    bc = chs(beta)
    ii = jnp.arange(C)
    tri = ii[:, None] >= ii[None, :]
    strict = ii[:, None] > ii[None, :]
    Lm = jnp.exp(jnp.where(tri, gc[..., :, None] - gc[..., None, :], -jnp.inf))
    kb = kc * bc[..., None]
    M = jnp.where(strict, jnp.einsum('bhncd,bhnsd->bhncs', kb, kc) * Lm, 0.0)
    A = M + jnp.eye(C, dtype=jnp.float32)
    rhs = jnp.concatenate([vc * bc[..., None], kb * jnp.exp(gc)[..., None]], axis=-1)
    sol = lax.linalg.triangular_solve(A, rhs, left_side=True, lower=True, unit_diagonal=True)
    u, w = sol[..., :d], sol[..., d:]
    att = jnp.einsum('bhncd,bhnsd->bhncs', qc, kc) * Lm
    qd = qc * jnp.exp(gc)[..., None]
    kd = kc * jnp.exp(gc[..., -1:] - gc)[..., None]
    glast = jnp.exp(gc[..., -1])
    xs = tuple(jnp.moveaxis(t, 2, 0) for t in (u, w, att, qd, kd, glast))

    def step(S, xs_n):
        u_, w_, at_, qd_, kd_, gl_ = xs_n
        vn = u_ - jnp.einsum('bhcd,bhde->bhce', w_, S)
        o_ = jnp.einsum('bhcd,bhde->bhce', qd_, S) + jnp.einsum('bhcs,bhse->bhce', at_, vn)
        S = S * gl_[..., None, None] + jnp.einsum('bhcd,bhce->bhde', kd_, vn)
        return S, o_

    S_fin, o = lax.scan(step, S0.astype(jnp.float32), xs)
    o = o.transpose(1, 0, 3, 2, 4).reshape(B, T, H, d)
    o = o * lax.rsqrt(jnp.mean(o * o, axis=-1, keepdims=True) + EPS) * norm_g.astype(jnp.float32)
    o = o * jax.nn.silu(z.astype(jnp.float32).reshape(B, T, H, d))
    return o.reshape(B, T, H * d).astype(dt), new_buf, S_fin


def conv_ffn(h, buf, w_up, dw, w_down):
    Bh, Th, _ = h.shape
    gpre, val = jnp.split(_pmm(h.reshape(Bh * Th, D_MODEL), w_up).reshape(Bh, Th, 2 * D_FF), 2, axis=-1)
    gconv, new_buf = causal_dwconv(gpre, buf, dw)
    return _pmm((jax.nn.gelu(gconv) * val).reshape(Bh * Th, D_FF), w_down).reshape(Bh, Th, D_MODEL), new_buf


PROJ_KV, PROJ_Q, PROJ_UCV, PROJ_RET, PROJ_GQKV, PROJ_GZ, PROJ_MISC = 0, 768, 1024, 1536, 2560, 3328, 3584
PROJ_COLS = PROJ_MISC + LANES
PROJ_TM = 512
MERGE_TM = 256
FFN_TM = 256


def _prep_w_in(w_in):
    o = [int(v) for v in np.cumsum((0,) + IN_SPLITS)]
    small = o[5] + 4 * GDN_W
    pieces = [w_in[:, o[1]:o[2]], w_in[:, o[0]:o[1]], w_in[:, o[3]:o[4]], w_in[:, o[4]:o[5]], w_in[:, o[5]:small],
              w_in[:, o[2]:o[3]], w_in[:, small:o[6]]]
    used = sum(pc.shape[1] for pc in pieces)
    pieces.append(jnp.zeros((D_MODEL, PROJ_COLS - used), w_in.dtype))
    return jnp.concatenate(pieces, axis=1), w_in[:, o[6]:]


def _modulated_norm(x, g, sc, sh):
    y = x * lax.rsqrt(jnp.mean(x * x, axis=-1, keepdims=True) + EPS)
    return (y * g) * (1.0 + sc) + sh


def _resident(shape):
    return pl.BlockSpec(shape, lambda *_: (0,) * len(shape), pipeline_mode=pl.Buffered(1))


def _mod_spec(mod, rows_per_group, tm):
    if mod.shape[1] == 1:
        return pl.BlockSpec((1, 1, D_MODEL), lambda i, *_: (i // (rows_per_group // tm), 0, 0))
    return pl.BlockSpec((1, tm, D_MODEL), lambda i, *_: (0, i, 0))


def _proj_kernel(x_ref, g_ref, sc_ref, sh_ref, w_ref, o_ref, wb_ref):
    @pl.when(pl.program_id(0) == 0)
    def _():
        wb_ref[...] = w_ref[...].astype(jnp.bfloat16)

    h = _modulated_norm(x_ref[...], g_ref[...], sc_ref[0], sh_ref[0]).astype(jnp.bfloat16)
    o_ref[...] = jnp.dot(h, wb_ref[...], preferred_element_type=jnp.float32)


def _proj(x2, g, sc, sh, w, rows_per_group, tm):
    M = x2.shape[0]
    N = w.shape[1]
    return pl.pallas_call(
        _proj_kernel,
        grid=(M // tm,),
        in_specs=[pl.BlockSpec((tm, D_MODEL), lambda i: (i, 0)), _resident((1, D_MODEL)),
                  _mod_spec(sc, rows_per_group, tm), _mod_spec(sh, rows_per_group, tm), _resident((D_MODEL, N))],
        out_specs=pl.BlockSpec((tm, N), lambda i: (i, 0)),
        out_shape=jax.ShapeDtypeStruct((M, N), jnp.float32),
        scratch_shapes=[pltpu.VMEM((D_MODEL, N), jnp.bfloat16)],
        compiler_params=pltpu.CompilerParams(dimension_semantics=("arbitrary",), vmem_limit_bytes=VMEM_LIMIT),
        name="in_proj",
    )(x2, g, sc, sh, w)


def _merge_kernel(x_ref, g0_ref, sc_ref, sh_ref, gt_ref, g1_ref, b0_ref, b1_ref, b2_ref, b3_ref,
                  wmg_ref, wbr_ref, wout_ref, o_ref, wmgb_ref):
    f32, bf16 = jnp.float32, jnp.bfloat16

    @pl.when(pl.program_id(0) == 0)
    def _():
        wmgb_ref[...] = wmg_ref[...].astype(bf16)

    x = x_ref[...]
    h = _modulated_norm(x, g0_ref[...], sc_ref[0], sh_ref[0]).astype(bf16)
    acc = jnp.zeros(x.shape, f32)
    for n, b_ref in enumerate((b0_ref, b1_ref, b2_ref, b3_ref)):
        gate = jax.nn.sigmoid(jnp.dot(h, wmgb_ref[:, n * D_MODEL:(n + 1) * D_MODEL], preferred_element_type=f32))
        acc = acc + gate * jnp.dot(b_ref[...].astype(bf16), wbr_ref[n], preferred_element_type=f32)
    mixed = jnp.dot(acc.astype(bf16), wout_ref[...], preferred_element_type=f32)
    y = mixed * lax.rsqrt(jnp.mean(mixed * mixed, axis=-1, keepdims=True) + EPS) * g1_ref[...]
    o_ref[...] = x + gt_ref[0] * y


def _merge(x2, g0, sc, sh, gt, g1, branches, wmg, wbr, wout, rows_per_group, tm):
    M = x2.shape[0]
    row = lambda w: pl.BlockSpec((tm, w), lambda i: (i, 0))
    mspec = _mod_spec(sc, rows_per_group, tm)
    return pl.pallas_call(
        _merge_kernel,
        grid=(M // tm,),
        in_specs=[row(D_MODEL), _resident((1, D_MODEL)), mspec, mspec, mspec, _resident((1, D_MODEL))]
                 + [row(BRANCH_W)] * N_BRANCH
                 + [_resident(wmg.shape), _resident(wbr.shape), _resident(wout.shape)],
        out_specs=row(D_MODEL),
        out_shape=jax.ShapeDtypeStruct((M, D_MODEL), jnp.float32),
        scratch_shapes=[pltpu.VMEM(wmg.shape, jnp.bfloat16)],
        compiler_params=pltpu.CompilerParams(dimension_semantics=("arbitrary",), vmem_limit_bytes=VMEM_LIMIT),
        name="merge",
    )(x2, g0, sc, sh, gt, g1, *branches, wmg, wbr, wout)


def _ffn_kernel(x_ref, g2_ref, sc_ref, sh_ref, gt_ref, g3_ref, buf_ref, wup_ref, dw_ref, wdn_ref,
                o_ref, st_ref, gp_ref):
    f32, bf16 = jnp.float32, jnp.bfloat16
    tm = x_ref.shape[1]
    HALO = SUBLANES

    @pl.when(pl.program_id(1) == 0)
    def _():
        gp_ref[0:HALO, :] = buf_ref[0]

    x = x_ref[0]
    h = _modulated_norm(x, g2_ref[...], sc_ref[0], sh_ref[0]).astype(bf16)
    gp_ref[HALO:HALO + tm, :] = jnp.dot(h, wup_ref[:, 0:D_FF], preferred_element_type=f32)
    val = jnp.dot(h, wup_ref[:, D_FF:2 * D_FF], preferred_element_type=f32)
    gconv = (dw_ref[2:3, :] * gp_ref[HALO:HALO + tm, :] + dw_ref[1:2, :] * gp_ref[HALO - 1:HALO - 1 + tm, :]
             + dw_ref[0:1, :] * gp_ref[HALO - 2:HALO - 2 + tm, :])
    a = (jax.nn.gelu(gconv) * val).astype(bf16)
    f = jnp.dot(a, wdn_ref[...], preferred_element_type=f32)
    y = f * lax.rsqrt(jnp.mean(f * f, axis=-1, keepdims=True) + EPS) * g3_ref[...]
    o_ref[0] = x + gt_ref[0] * y
    tail = gp_ref[tm:tm + HALO, :]
    gp_ref[0:HALO, :] = tail
    st_ref[0] = tail


def _ffn(x3, g2, sc, sh, gt, g3, buf, wup, dw, wdn, tm):
    B, T, _ = x3.shape
    buf8 = jnp.pad(buf, ((0, 0), (SUBLANES - (FFN_CONV - 1), 0), (0, 0)))
    mspec = pl.BlockSpec((1, 1, D_MODEL), lambda b, i: (b, 0, 0))
    y, st = pl.pallas_call(
        _ffn_kernel,
        grid=(B, T // tm),
        in_specs=[pl.BlockSpec((1, tm, D_MODEL), lambda b, i: (b, i, 0)), _resident((1, D_MODEL)), mspec, mspec, mspec,
                  _resident((1, D_MODEL)), pl.BlockSpec((1, SUBLANES, D_FF), lambda b, i: (b, 0, 0)),
                  _resident(wup.shape), _resident(dw.shape), _resident(wdn.shape)],
        out_specs=[pl.BlockSpec((1, tm, D_MODEL), lambda b, i: (b, i, 0)),
                   pl.BlockSpec((1, SUBLANES, D_FF), lambda b, i: (b, 0, 0))],
        out_shape=[jax.ShapeDtypeStruct((B, T, D_MODEL), jnp.float32),
                   jax.ShapeDtypeStruct((B, SUBLANES, D_FF), jnp.float32)],
        scratch_shapes=[pltpu.VMEM((tm + SUBLANES, D_FF), jnp.float32)],
        compiler_params=pltpu.CompilerParams(dimension_semantics=("arbitrary", "arbitrary"),
                                             vmem_limit_bytes=VMEM_LIMIT),
        name="conv_ffn",
    )(x3, g2, sc, sh, gt, g3, buf8, wup, dw, wdn)
    return y, st[:, SUBLANES - (FFN_CONV - 1):]


CONF_HALO = 32
CONF_TM = 512


def _conformer_kernel(u_ref, buf_ref, dw_ref, dwb_ref, lng_ref, lnb_ref, o_ref, st_ref, xp_ref):
    tm = u_ref.shape[1]
    first = CONF_HALO - (CONV_WIDTH - 1)

    @pl.when(pl.program_id(1) == 0)
    def _():
        xp_ref[0:CONF_HALO, :] = buf_ref[0]

    u = u_ref[0]
    xp_ref[CONF_HALO:CONF_HALO + tm, :] = u[:, :CONV_CH] * jax.nn.sigmoid(u[:, CONV_CH:])
    acc = jnp.zeros((tm, CONV_CH), jnp.float32)
    for k in range(CONV_WIDTH):
        acc = acc + dw_ref[k:k + 1, :] * xp_ref[first + k:first + k + tm, :]
    y = acc + dwb_ref[...]
    mu = jnp.mean(y, axis=-1, keepdims=True)
    var = jnp.mean(jnp.square(y - mu), axis=-1, keepdims=True)
    yn = (y - mu) * lax.rsqrt(var + EPS) * lng_ref[...] + lnb_ref[...]
    o_ref[0] = yn * jax.nn.sigmoid(yn)
    tail = xp_ref[tm:tm + CONF_HALO, :]
    xp_ref[0:CONF_HALO, :] = tail
    st_ref[0] = tail


def _conformer(P, col_blk, buf, dw, dw_b, ln_g, ln_b, tm):
    B, T, _ = P.shape
    bufp = jnp.pad(buf, ((0, 0), (CONF_HALO - (CONV_WIDTH - 1), 0), (0, 0)))
    vec = lambda: _resident((1, CONV_CH))
    o, st = pl.pallas_call(
        _conformer_kernel,
        grid=(B, T // tm),
        in_specs=[pl.BlockSpec((1, tm, 2 * CONV_CH), lambda b, i: (b, i, col_blk)),
                  pl.BlockSpec((1, CONF_HALO, CONV_CH), lambda b, i: (b, 0, 0)),
                  _resident((CONV_WIDTH, CONV_CH)), vec(), vec(), vec()],
        out_specs=[pl.BlockSpec((1, tm, CONV_CH), lambda b, i: (b, i, 0)),
                   pl.BlockSpec((1, CONF_HALO, CONV_CH), lambda b, i: (b, 0, 0))],
        out_shape=[jax.ShapeDtypeStruct((B, T, CONV_CH), jnp.float32),
                   jax.ShapeDtypeStruct((B, CONF_HALO, CONV_CH), jnp.float32)],
        scratch_shapes=[pltpu.VMEM((tm + CONF_HALO, CONV_CH), jnp.float32)],
        compiler_params=pltpu.CompilerParams(dimension_semantics=("arbitrary", "arbitrary"),
                                             vmem_limit_bytes=VMEM_LIMIT),
        name="conformer",
    )(P, bufp, dw, dw_b[None], ln_g[None], ln_b[None])
    return o, st[:, CONF_HALO - (CONV_WIDTH - 1):]


RET_LOG_GAMMA = tuple(math.log1p(-2.0 ** (-5 - h)) for h in range(RET_HEADS))
RET_TM = 128


def _retention_kernel(q_ref, k_ref, v_ref, g_ref, cos_ref, sin_ref, s0_ref, gn_ref, o_ref, sfin_ref,
                      s_ref, dec_ref, *, n_valid):
    f32, bf16 = jnp.float32, jnp.bfloat16
    C = q_ref.shape[1]
    W = RET_W
    half = HEAD_DIM // 2

    @pl.when(pl.program_id(1) == 0)
    def _():
        s_ref[...] = s0_ref[0]

    @pl.when((pl.program_id(0) == 0) & (pl.program_id(1) == 0))
    def _():
        ii = lax.broadcasted_iota(jnp.int32, (C, C), 0)
        jj = lax.broadcasted_iota(jnp.int32, (C, C), 1)
        d = (ii - jj).astype(f32)
        for h in range(RET_HEADS):
            dec_ref[h] = jnp.where((ii >= jj) & (jj < n_valid), jnp.exp(jnp.maximum(d, 0.0) * RET_LOG_GAMMA[h]), 0.0)

    lane = lax.broadcasted_iota(jnp.int32, (C, W), 1)
    low = lane % HEAD_DIM < half
    cos = cos_ref[...]
    sin = sin_ref[...]

    def rot(x):
        other = jnp.where(low, pltpu.roll(x, W - half, axis=1), pltpu.roll(x, half, axis=1))
        return x * cos + other * sin

    q = rot(q_ref[0])
    k = rot(k_ref[0]) * HEAD_DIM ** -0.5
    v = v_ref[0]
    row = lax.broadcasted_iota(jnp.int32, (C, HEAD_DIM), 0)
    rowf = row.astype(f32)
    valid = row < n_valid
    outs = []
    for h in range(RET_HEADS):
        cs = slice(h * HEAD_DIM, (h + 1) * HEAD_DIM)
        lg = RET_LOG_GAMMA[h]
        qh, kh, vh = q[:, cs], k[:, cs], v[:, cs].astype(bf16)
        s_old = s_ref[h]
        att = _dot_nt(qh.astype(bf16), kh.astype(bf16)) * dec_ref[h]
        o = (jnp.dot(att.astype(bf16), vh, preferred_element_type=f32)
             + jnp.dot((qh * jnp.exp((rowf + 1.0) * lg)).astype(bf16), s_old.astype(bf16), preferred_element_type=f32))
        kz = jnp.where(valid, kh * jnp.exp((n_valid - 1.0 - rowf) * lg), 0.0)
        s_ref[h] = s_old * math.exp(n_valid * lg) + jnp.dot(kz.T.astype(bf16), vh, preferred_element_type=f32)
        mu = jnp.mean(o, axis=-1, keepdims=True)
        var = jnp.mean(jnp.square(o - mu), axis=-1, keepdims=True)
        outs.append((o - mu) * lax.rsqrt(var + EPS))
    g = g_ref[0]
    o_ref[0] = jnp.concatenate(outs, axis=1) * gn_ref[...] * (g * jax.nn.sigmoid(g))

    @pl.when(pl.program_id(1) == pl.num_programs(1) - 1)
    def _():
        sfin_ref[0] = s_ref[...]


def _retention(P, col_blk0, S0, pos, gn, chunk, n_valid):
    B, T, _ = P.shape
    half = HEAD_DIM // 2
    inv = ROPE_BASE ** (-jnp.arange(half, dtype=jnp.float32) / half)
    ang = pos.astype(jnp.float32)[:, None] * inv[None, :]
    cos, sin = jnp.cos(ang), jnp.sin(ang)
    cosf = jnp.tile(jnp.concatenate([cos, cos], axis=1), (1, RET_HEADS))
    sinf = jnp.tile(jnp.concatenate([-sin, sin], axis=1), (1, RET_HEADS))
    col = lambda j: pl.BlockSpec((1, chunk, RET_W), lambda b, i: (b, i, col_blk0 + j))
    tab = pl.BlockSpec((chunk, RET_W), lambda b, i: (i, 0))
    state = pl.BlockSpec((1, RET_HEADS, HEAD_DIM, HEAD_DIM), lambda b, i: (b, 0, 0, 0))
    return pl.pallas_call(
        functools.partial(_retention_kernel, n_valid=n_valid),
        grid=(B, T // chunk),
        in_specs=[col(0), col(1), col(2), col(3), tab, tab, state, _resident((1, RET_W))],
        out_specs=[pl.BlockSpec((1, chunk, RET_W), lambda b, i: (b, i, 0)), state],
        out_shape=[jax.ShapeDtypeStruct((B, T, RET_W), jnp.float32),
                   jax.ShapeDtypeStruct((B, RET_HEADS, HEAD_DIM, HEAD_DIM), jnp.float32)],
        scratch_shapes=[pltpu.VMEM((RET_HEADS, HEAD_DIM, HEAD_DIM), jnp.float32),
                        pltpu.VMEM((RET_HEADS, chunk, chunk), jnp.float32)],
        compiler_params=pltpu.CompilerParams(dimension_semantics=("arbitrary", "arbitrary"),
                                             vmem_limit_bytes=VMEM_LIMIT),
        name="retention",
    )(P, P, P, P, cosf, sinf, S0, gn[None])


GDN_RT = 2 * GDN_CHUNK
GDN_HALO = SUBLANES
GDN_A_LANE = 3 * NSA_HEADS
GDN_B_LANE = GDN_A_LANE + GDN_HEADS


def _mm1(a, b):
    return jnp.dot(a.astype(jnp.bfloat16), b.astype(jnp.bfloat16), preferred_element_type=jnp.float32)


def _mm3(a, b):
    f32, bf16 = jnp.float32, jnp.bfloat16
    ah, bh = a.astype(bf16), b.astype(bf16)
    al, bl = (a - ah.astype(f32)).astype(bf16), (b - bh.astype(f32)).astype(bf16)
    d = lambda x, y: jnp.dot(x, y, preferred_element_type=f32)
    return d(ah, bh) + (d(ah, bl) + d(al, bh))


def _segment_cumsum(x, axis, seg):
    idx = lax.broadcasted_iota(jnp.int32, x.shape, axis) % seg
    s = 1
    while s < seg:
        x = x + jnp.where(idx >= s, pltpu.roll(x, s, axis=axis), 0.0)
        s *= 2
    return x


def _gdn_kernel(q_ref, k_ref, v_ref, z_ref, ab_ref, abt_ref, buf_ref, s0_ref, cw_ref, ng_ref, alane_ref, dlane_ref,
                acol_ref, dcol_ref, o_ref, sfin_ref, s_ref, xp_ref, *, n_valid):
    f32 = jnp.float32
    RT = q_ref.shape[1]
    C = GDN_CHUNK
    W = GDN_W
    first = GDN_HALO - (GDN_CONV - 1)

    @pl.when(pl.program_id(1) == 0)
    def _():
        s_ref[...] = s0_ref[0]
        xp_ref[0:GDN_HALO, :] = buf_ref[0]

    xp_ref[GDN_HALO:GDN_HALO + RT, 0:W] = q_ref[0]
    xp_ref[GDN_HALO:GDN_HALO + RT, W:2 * W] = k_ref[0]
    xp_ref[GDN_HALO:GDN_HALO + RT, 2 * W:3 * W] = v_ref[0]
    y = jnp.zeros((RT, 3 * W), f32)
    for t in range(GDN_CONV):
        y = y + cw_ref[t:t + 1, :] * xp_ref[first + t:first + t + RT, :]
    y = y * jax.nn.sigmoid(y)
    xp_ref[0:GDN_HALO, :] = xp_ref[RT:RT + GDN_HALO, :]

    ab = ab_ref[0]
    g_lanes = -jnp.exp(alane_ref[...]) * jax.nn.softplus(ab + dlane_ref[...])
    beta_lanes = jax.nn.sigmoid(ab)
    g_rows = -jnp.exp(acol_ref[...]) * jax.nn.softplus(abt_ref[0] + dcol_ref[...])
    if n_valid < C:
        g_lanes = jnp.where(lax.broadcasted_iota(jnp.int32, g_lanes.shape, 0) % C < n_valid, g_lanes, 0.0)
        beta_lanes = jnp.where(lax.broadcasted_iota(jnp.int32, g_lanes.shape, 0) % C < n_valid, beta_lanes, 0.0)
        g_rows = jnp.where(lax.broadcasted_iota(jnp.int32, g_rows.shape, 1) % C < n_valid, g_rows, 0.0)
    gc_lanes = _segment_cumsum(g_lanes, 0, C)
    gc_rows = _segment_cumsum(g_rows, 1, C)

    ii = lax.broadcasted_iota(jnp.int32, (C, C), 0)
    jj = lax.broadcasted_iota(jnp.int32, (C, C), 1)
    tri = ii >= jj
    strict = ii > jj
    eye = (ii == jj).astype(f32)
    z = z_ref[0]
    bf16 = jnp.bfloat16
    nch = RT // C
    pairs = [(c, h) for c in range(nch) for h in range(GDN_HEADS)]
    qs, ks, ms, atts, rhss, gcs = {}, {}, {}, {}, {}, {}
    for c, h in pairs:
        rs = slice(c * C, (c + 1) * C)
        qh = y[rs, h * HEAD_DIM:(h + 1) * HEAD_DIM]
        kh = y[rs, W + h * HEAD_DIM:W + (h + 1) * HEAD_DIM]
        vh = y[rs, 2 * W + h * HEAD_DIM:2 * W + (h + 1) * HEAD_DIM]
        qh = qh * lax.rsqrt(jnp.sum(qh * qh, axis=-1, keepdims=True) + EPS) * HEAD_DIM ** -0.5
        kh = kh * lax.rsqrt(jnp.sum(kh * kh, axis=-1, keepdims=True) + EPS)
        gc_col = gc_lanes[rs, GDN_A_LANE + h:GDN_A_LANE + h + 1]
        beta = beta_lanes[rs, GDN_B_LANE + h:GDN_B_LANE + h + 1]
        gc_row = gc_rows[h:h + 1, c * C:(c + 1) * C]
        e_col = jnp.exp(gc_col)
        lm = jnp.exp(jnp.where(tri, gc_col - gc_row, NEG_INF))
        kb = kh * beta
        ms[c, h] = jnp.where(strict, _dot_nt(kb.astype(bf16), kh.astype(bf16)) * lm, 0.0)
        atts[c, h] = _dot_nt(qh.astype(bf16), kh.astype(bf16)) * lm
        rhss[c, h] = jnp.concatenate([vh * beta, kb * e_col], axis=1)
        qs[c, h], ks[c, h], gcs[c, h] = qh * e_col, kh, gc_col
    pw = {p: -ms[p] for p in pairs}
    xs = {p: eye + pw[p] for p in pairs}
    for _ in range(int(math.log2(C)) - 1):
        pw = {p: _mm1(pw[p], pw[p]) for p in pairs}
        xs = {p: xs[p] + _mm1(xs[p], pw[p]) for p in pairs}
    res = {p: eye - (xs[p] + _mm3(ms[p], xs[p])) for p in pairs}
    xs = {p: xs[p] + _mm1(xs[p], res[p]) for p in pairs}
    sols = {p: _mm3(xs[p], rhss[p]) for p in pairs}
    state = [s_ref[h] for h in range(GDN_HEADS)]
    outs = {}
    for c in range(nch):
        heads = range(GDN_HEADS)
        g_last = [gcs[c, h][C - 1:C, :] for h in heads]
        vn = [sols[c, h][:, :HEAD_DIM] - _mm1(sols[c, h][:, HEAD_DIM:], state[h]) for h in heads]
        o_in = [_mm1(qs[c, h], state[h]) for h in heads]
        kd_t = [(ks[c, h] * jnp.exp(g_last[h] - gcs[c, h])).T for h in heads]
        for h in heads:
            outs[c, h] = o_in[h] + _mm1(atts[c, h], vn[h])
        state = [state[h] * jnp.exp(g_last[h]) + _mm1(kd_t[h], vn[h]) for h in heads]
    for h in range(GDN_HEADS):
        s_ref[h] = state[h]
    row_outs = []
    for c in range(nch):
        head_outs = []
        for h in range(GDN_HEADS):
            o = outs[c, h]
            o = o * lax.rsqrt(jnp.mean(o * o, axis=-1, keepdims=True) + EPS) * ng_ref[...]
            zh = z[c * C:(c + 1) * C, h * HEAD_DIM:(h + 1) * HEAD_DIM]
            head_outs.append(o * (zh * jax.nn.sigmoid(zh)))
        row_outs.append(jnp.concatenate(head_outs, axis=1))
    o_ref[0] = jnp.concatenate(row_outs, axis=0)

    @pl.when(pl.program_id(1) == pl.num_programs(1) - 1)
    def _():
        sfin_ref[0] = s_ref[...]


def _gdn(P, blk_q, blk_misc, ga, conv_buf, S0, conv_w, A_log, dt_bias, norm_g, rt, n_valid):
    B, T, _ = P.shape
    f32 = jnp.float32
    abt = jnp.pad(jnp.transpose(ga, (0, 2, 1)), ((0, 0), (0, SUBLANES - GDN_HEADS), (0, 0)))
    buf8 = jnp.pad(conv_buf, ((0, 0), (GDN_HALO - (GDN_CONV - 1), 0), (0, 0)))
    lane_vec = lambda v: jnp.zeros((1, LANES), f32).at[0, GDN_A_LANE:GDN_A_LANE + GDN_HEADS].set(v)
    col_vec = lambda v: jnp.zeros((SUBLANES, 1), f32).at[0:GDN_HEADS, 0].set(v)
    col = lambda j: pl.BlockSpec((1, rt, GDN_W), lambda b, i: (b, i, blk_q + j))
    state = pl.BlockSpec((1, GDN_HEADS, HEAD_DIM, HEAD_DIM), lambda b, i: (b, 0, 0, 0))
    return pl.pallas_call(
        functools.partial(_gdn_kernel, n_valid=n_valid),
        grid=(B, T // rt),
        in_specs=[col(0), col(1), col(2), col(3),
                  pl.BlockSpec((1, rt, LANES), lambda b, i: (b, i, blk_misc)),
                  pl.BlockSpec((1, SUBLANES, rt), lambda b, i: (b, 0, i)),
                  pl.BlockSpec((1, GDN_HALO, 3 * GDN_W), lambda b, i: (b, 0, 0)), state,
                  _resident((GDN_CONV, 3 * GDN_W)), _resident((1, HEAD_DIM)),
                  _resident((1, LANES)), _resident((1, LANES)), _resident((SUBLANES, 1)), _resident((SUBLANES, 1))],
        out_specs=[pl.BlockSpec((1, rt, GDN_W), lambda b, i: (b, i, 0)), state],
        out_shape=[jax.ShapeDtypeStruct((B, T, GDN_W), f32),
                   jax.ShapeDtypeStruct((B, GDN_HEADS, HEAD_DIM, HEAD_DIM), f32)],
        scratch_shapes=[pltpu.VMEM((GDN_HEADS, HEAD_DIM, HEAD_DIM), f32),
                        pltpu.VMEM((rt + GDN_HALO, 3 * GDN_W), f32)],
        compiler_params=pltpu.CompilerParams(dimension_semantics=("arbitrary", "arbitrary"),
                                             vmem_limit_bytes=VMEM_LIMIT),
        name="gated_deltanet",
    )(P, P, P, P, P, abt, buf8, S0, conv_w, norm_g[None], lane_vec(A_log), lane_vec(dt_bias),
      col_vec(A_log), col_vec(dt_bias))


def _prep_layer(p):
    w_main, w_mg = _prep_w_in(p["w_in"])
    q = dict(p)
    q.update(w_main=w_main, w_mg=w_mg, w_branch_b=p["w_branch"].astype(jnp.bfloat16),
             w_out_b=p["w_out"].astype(jnp.bfloat16), ffn_up_b=p["ffn_up"].astype(jnp.bfloat16),
             ffn_down_b=p["ffn_down"].astype(jnp.bfloat16))
    return q


def trunk_layer(x, c, pos0, nsa_past, win_buf, conv_buf, ret_s, gdn_buf, gdn_s, ffn_buf, p, rel_bias):
    B, T, _ = x.shape
    M = B * T
    mod = _pmm(jax.nn.silu(c), p["w_ada"]) + p["b_ada"]
    per_row = T < MERGE_TM
    if per_row:
        sh1, sc1, gt1, sh2, sc2, gt2 = [jnp.repeat(m, T, axis=0)[None] for m in jnp.split(mod, 6, axis=-1)]
    else:
        sh1, sc1, gt1, sh2, sc2, gt2 = [m[:, None, :] for m in jnp.split(mod, 6, axis=-1)]
    norms = p["norms"][:, None, :]
    x2 = x.reshape(M, D_MODEL)
    P = _proj(x2, norms[0], sc1, sh1, p["w_main"], T, M if per_row else PROJ_TM).reshape(B, T, PROJ_COLS)
    nkv = P[:, :, PROJ_KV:PROJ_KV + 6 * KVW]
    ngt = P[:, :, PROJ_MISC:PROJ_MISC + 3 * NSA_HEADS]
    ga = P[:, :, PROJ_MISC + 3 * NSA_HEADS:PROJ_MISC + 3 * NSA_HEADS + GDN_HEADS]
    gb = P[:, :, PROJ_MISC + 3 * NSA_HEADS + GDN_HEADS:PROJ_MISC + 3 * NSA_HEADS + 2 * GDN_HEADS]
    kw = nkv[:, :, 4 * KVW:].reshape(B, T, 2, NSA_KV_HEADS, HEAD_DIM)
    if nsa_past is None:
        o_nsa = _nsa_prompt(P, PROJ_Q // (NSA_HEADS * HEAD_DIM), P, PROJ_KV // (6 * KVW), ngt,
                            p["cmp_pool"], p["cmp_pe"], rel_bias)
        new_win = kw[:, T - min(NSA_WINDOW, T):]
    else:
        cache, layer, page_table = nsa_past
        o_nsa = _nsa_sample(P[:, :, PROJ_Q:PROJ_Q + NSA_HEADS * HEAD_DIM], nkv, ngt, cache, layer, page_table,
                            win_buf, p["cmp_pool"], p["cmp_pe"], rel_bias)
        real = jnp.concatenate([win_buf, kw], axis=1)
        new_win = real[:, real.shape[1] - min(NSA_WINDOW, real.shape[1]):]
    o_conv, new_conv = _conformer(P, PROJ_UCV // (2 * CONV_CH), conv_buf, p["conv_dw"], p["conv_dw_b"],
                                  p["conv_ln_g"], p["conv_ln_b"], T if per_row else CONF_TM)
    if per_row:
        Pr = jnp.pad(P[:, :, PROJ_RET:PROJ_RET + 4 * RET_W], ((0, 0), (0, RET_CHUNK - T), (0, 0)))
        o_ret, new_ret = _retention(Pr, 0, ret_s, pos0 + jnp.arange(RET_CHUNK, dtype=jnp.int32), p["ret_gn"],
                                    RET_CHUNK, T)
        o_ret = o_ret[:, :T]
    else:
        o_ret, new_ret = _retention(P, PROJ_RET // RET_W, ret_s, pos0 + jnp.arange(T, dtype=jnp.int32), p["ret_gn"],
                                    RET_TM, RET_TM)
    gdn_w = (p["gdn_conv_w"], p["gdn_A_log"], p["gdn_dt_bias"], p["gdn_norm"])
    if per_row:
        pad_rows = ((0, 0), (0, GDN_CHUNK - T), (0, 0))
        Pg = jnp.pad(P[:, :, PROJ_GQKV:PROJ_COLS], pad_rows)
        o_gdn, new_gdn = _gdn(Pg, 0, (PROJ_MISC - PROJ_GQKV) // LANES, jnp.pad(ga, pad_rows), gdn_buf, gdn_s, *gdn_w,
                              GDN_CHUNK, T)
        o_gdn = o_gdn[:, :T]
    else:
        o_gdn, new_gdn = _gdn(P, PROJ_GQKV // GDN_W, PROJ_MISC // LANES, ga, gdn_buf, gdn_s, *gdn_w,
                              GDN_RT, GDN_CHUNK)
    new_gdn_buf = jnp.concatenate([gdn_buf, P[:, :, PROJ_GQKV:PROJ_GQKV + 3 * GDN_W]], axis=1)[:, T:]
    branches = [o.reshape(M, BRANCH_W) for o in (o_nsa, o_conv, o_ret, o_gdn)]
    x1 = _merge(x2, norms[0], sc1, sh1, gt1, norms[1], branches, p["w_mg"], p["w_branch_b"], p["w_out_b"],
                T, M if per_row else MERGE_TM).reshape(B, T, D_MODEL)
    if per_row:
        s2, h2, g2 = sc2.reshape(B, T, D_MODEL), sh2.reshape(B, T, D_MODEL), gt2.reshape(B, T, D_MODEL)
        f, new_ffn = conv_ffn(rms_norm(x1, p["norms"][2]) * (1.0 + s2) + h2, ffn_buf, p["ffn_up"], p["ffn_dw"],
                              p["ffn_down"])
        x_out = x1 + g2 * rms_norm(f, p["norms"][3])
    else:
        x_out, new_ffn = _ffn(x1, norms[2], sc2, sh2, gt2, norms[3], ffn_buf, p["ffn_up_b"], p["ffn_dw"],
                              p["ffn_down_b"], FFN_TM)
    kv_rows = nkv[:, :, :4 * KVW].reshape(B, T, 4, NSA_KV_HEADS, HEAD_DIM)
    return x_out, (kv_rows, new_win, new_conv, new_ret, new_gdn_buf, new_gdn, new_ffn)


def kernel(x_prompt, x_sample, cache_nsa_kv, cache_nsa_win, state_conv, state_ret, state_gdn_conv, state_gdn,
           state_ffn_conv, page_table, c_prompt, c_sample, w_ada, b_ada, norms, w_in, cmp_pool, cmp_pe, rel_bias,
           conv_dw, conv_dw_b, conv_ln_g, conv_ln_b, ret_gn, gdn_conv_w, gdn_A_log, gdn_dt_bias, gdn_norm,
           w_branch, w_out, ffn_up, ffn_dw, ffn_down):
    B = x_prompt.shape[0]
    Bd = x_sample.shape[0]
    past = page_table.shape[1] * PAGE_SIZE
    layer_w = {"w_ada": w_ada, "b_ada": b_ada, "norms": norms, "w_in": w_in, "cmp_pool": cmp_pool,
               "cmp_pe": cmp_pe, "conv_dw": conv_dw, "conv_dw_b": conv_dw_b, "conv_ln_g": conv_ln_g,
               "conv_ln_b": conv_ln_b, "ret_gn": ret_gn, "gdn_conv_w": gdn_conv_w, "gdn_A_log": gdn_A_log,
               "gdn_dt_bias": gdn_dt_bias, "gdn_norm": gdn_norm, "w_branch": w_branch, "w_out": w_out,
               "ffn_up": ffn_up, "ffn_dw": ffn_dw, "ffn_down": ffn_down}
    yp, ys = x_prompt, x_sample
    st_p, st_s = [], []
    for l in range(DEPTH):
        p = _prep_layer({name: w[l] for name, w in layer_w.items()})
        yp, sp = trunk_layer(
            yp, c_prompt, 0, None, None,
            jnp.zeros((B, CONV_WIDTH - 1, CONV_CH), x_prompt.dtype),
            jnp.zeros((B, RET_HEADS, HEAD_DIM, HEAD_DIM), jnp.float32),
            jnp.zeros((B, GDN_CONV - 1, 3 * GDN_W), x_prompt.dtype),
            jnp.zeros((B, GDN_HEADS, HEAD_DIM, HEAD_DIM), jnp.float32),
            jnp.zeros((B, FFN_CONV - 1, D_FF), x_prompt.dtype),
            p, rel_bias)
        ys, ss = trunk_layer(
            ys, c_sample, past, (cache_nsa_kv, l, page_table), cache_nsa_win[l], state_conv[l], state_ret[l],
            state_gdn_conv[l], state_gdn[l], state_ffn_conv[l], p, rel_bias)
        st_p.append(sp)
        st_s.append(ss)

    def stack(outs, i, axis):
        return jnp.stack([o[i] for o in outs], axis=axis)

    kv_p, kv_s = stack(st_p, 0, 1), stack(st_s, 0, 1)
    win_p, win_s = stack(st_p, 1, 0), stack(st_s, 1, 0)
    conv_p, conv_s = stack(st_p, 2, 0), stack(st_s, 2, 0)
    ret_p, ret_s = stack(st_p, 3, 0), stack(st_s, 3, 0)
    gdnc_p, gdnc_s = stack(st_p, 4, 0), stack(st_s, 4, 0)
    gdn_p, gdn_s = stack(st_p, 5, 0), stack(st_s, 5, 0)
    ffn_p, ffn_s = stack(st_p, 6, 0), stack(st_s, 6, 0)
    return (yp, ys, kv_p, kv_s, win_p, win_s, conv_p, conv_s, ret_p, ret_s, gdnc_p, gdnc_s, gdn_p, gdn_s, ffn_p, ffn_s)
```

```python
import functools
import math

import jax
import jax.numpy as jnp
import numpy as np
from jax import lax
from jax.experimental import pallas as pl
from jax.experimental.pallas import tpu as pltpu

D_MODEL = 1024
BATCH = 4
SEQ = 4096
DEPTH = 2
DEC_BATCH = 32
DEC_SEQ = 8
PAST_LEN = 16384
PAGE_SIZE = 128

HEAD_DIM = 64
NSA_HEADS = 4
NSA_KV_HEADS = 2
NSA_GROUP = NSA_HEADS // NSA_KV_HEADS
NSA_BLOCK = 64
NSA_TOPK = 16
NSA_WINDOW = 512
Q_BLOCK = 128
NUM_BUCKETS = 32
MAX_DISTANCE = 128
CONV_CH = D_MODEL // 4
CONV_WIDTH = 31
RET_HEADS = 4
RET_W = RET_HEADS * HEAD_DIM
RET_CHUNK = 64
ROPE_BASE = 10000.0
GDN_HEADS = 4
GDN_W = GDN_HEADS * HEAD_DIM
GDN_CONV = 4
GDN_CHUNK = 64
D_FF = 2816
FFN_CONV = 3
N_BRANCH = 4
BRANCH_W = NSA_HEADS * HEAD_DIM
EPS = 1e-6
NEG_INF = -1e30
IN_SPLITS = (NSA_HEADS * HEAD_DIM, 6 * NSA_KV_HEADS * HEAD_DIM, 3 * NSA_HEADS, 2 * CONV_CH, 4 * RET_W,
             4 * GDN_W + 2 * GDN_HEADS, N_BRANCH * D_MODEL)
IN_COLS = sum(IN_SPLITS)

LANES = 128
SUBLANES = 8
VMEM_LIMIT = 56 * 1024 * 1024


def _round_up(a, m):
    return -(-a // m) * m


def _mm_kernel(x_ref, w_ref, o_ref):
    o_ref[...] = jnp.dot(x_ref[...].astype(jnp.bfloat16), w_ref[...].astype(jnp.bfloat16),
                         preferred_element_type=jnp.float32)


def _pmm(x, w):
    M, K = x.shape
    N = w.shape[1]
    Mp = _round_up(M, SUBLANES)
    tm = min(512, Mp)
    Mp = _round_up(Mp, tm)
    tn = 512 if N > 512 else _round_up(N, LANES)
    Np = _round_up(N, tn)
    if Mp != M:
        x = jnp.pad(x, ((0, Mp - M), (0, 0)))
    if Np != N:
        w = jnp.pad(w, ((0, 0), (0, Np - N)))
    out = pl.pallas_call(
        _mm_kernel,
        grid=(Mp // tm, Np // tn),
        in_specs=[pl.BlockSpec((tm, K), lambda i, j: (i, 0)), pl.BlockSpec((K, tn), lambda i, j: (0, j))],
        out_specs=pl.BlockSpec((tm, tn), lambda i, j: (i, j)),
        out_shape=jax.ShapeDtypeStruct((Mp, Np), jnp.float32),
        compiler_params=pltpu.CompilerParams(dimension_semantics=("arbitrary", "arbitrary"),
                                             vmem_limit_bytes=VMEM_LIMIT),
    )(x, w)
    return out[:M, :N]


def split_cols(a, sizes):
    return jnp.split(a, [int(i) for i in np.cumsum(sizes)[:-1]], axis=-1)


def rms_norm(x, g):
    x32 = x.astype(jnp.float32)
    y = x32 * lax.rsqrt(jnp.mean(x32 * x32, axis=-1, keepdims=True) + EPS)
    return (y * g.astype(jnp.float32)).astype(x.dtype)


def layer_norm(x, g, b):
    x32 = x.astype(jnp.float32)
    mu = jnp.mean(x32, axis=-1, keepdims=True)
    var = jnp.mean(jnp.square(x32 - mu), axis=-1, keepdims=True)
    return ((x32 - mu) * lax.rsqrt(var + EPS) * g.astype(jnp.float32) + b.astype(jnp.float32)).astype(x.dtype)


def causal_dwconv(x, buf, w):
    xp = jnp.concatenate([buf.astype(x.dtype), x], axis=1)
    y = lax.conv_general_dilated(xp, w[:, None, :].astype(x.dtype), window_strides=(1,), padding='VALID',
                                 dimension_numbers=('NWC', 'WIO', 'NWC'), feature_group_count=x.shape[-1])
    return y, xp[:, xp.shape[1] - (w.shape[0] - 1):]


def t5_bucket(dist):
    n = jnp.maximum(dist, 0)
    exact = NUM_BUCKETS // 2
    large = exact + (jnp.log(jnp.maximum(n, 1).astype(jnp.float32) / exact) / math.log(MAX_DISTANCE / exact)
                     * (NUM_BUCKETS - exact)).astype(jnp.int32)
    return jnp.where(n < exact, n, jnp.minimum(large, NUM_BUCKETS - 1))


def masked_softmax(s, mask):
    return jax.nn.softmax(jnp.where(mask, s, NEG_INF), axis=-1) * mask


def over_query_blocks(fn, args, q_axes, T):
    if T <= Q_BLOCK or T % Q_BLOCK:
        return fn(args)
    nq = T // Q_BLOCK

    def split(a, ax):
        return jnp.moveaxis(a.reshape(a.shape[:ax] + (nq, Q_BLOCK) + a.shape[ax + 1:]), ax, 0)

    out = lax.map(fn, tuple(split(a, ax) for a, ax in zip(args, q_axes)))
    out = jnp.moveaxis(out, 0, 1)
    return out.reshape((out.shape[0], T) + out.shape[3:])


def rotary(x, pos):
    half = x.shape[-1] // 2
    inv = ROPE_BASE ** (-jnp.arange(half, dtype=jnp.float32) / half)
    ang = pos.astype(jnp.float32)[:, None] * inv[None, :]
    cos, sin = jnp.cos(ang)[None, :, None, :], jnp.sin(ang)[None, :, None, :]
    x1, x2 = x[..., :half], x[..., half:]
    return jnp.concatenate([x1 * cos - x2 * sin, x1 * sin + x2 * cos], axis=-1)


NSA_TQ = 128
BLOCKS_PER_TILE = NSA_TQ // NSA_BLOCK
WIN_TILES = NSA_WINDOW // NSA_TQ


def _t5_thresholds():
    n = np.arange(0, 2 * MAX_DISTANCE)
    exact = NUM_BUCKETS // 2
    large = exact + (np.log(np.maximum(n, 1).astype(np.float32) / np.float32(exact))
                     / np.float32(math.log(MAX_DISTANCE / exact)) * (NUM_BUCKETS - exact)).astype(np.int32)
    bucket = np.where(n < exact, n, np.minimum(large, NUM_BUCKETS - 1))
    return tuple(int(np.argmax(bucket >= k)) for k in range(1, NUM_BUCKETS))


_T5_THR = _t5_thresholds()


def _bias_from_dist(dist, rb_ref, h):
    v = jnp.full(dist.shape, rb_ref[NUM_BUCKETS - 1, h], jnp.float32)
    for k in range(NUM_BUCKETS - 2, -1, -1):
        v = jnp.where(dist < _T5_THR[k], rb_ref[k, h], v)
    return v


def _dot_nt(a, b):
    return lax.dot_general(a, b, (((1,), (1,)), ((), ())), preferred_element_type=jnp.float32)


def _flash_tile(carry, k_t, vt_t, qs, bias, keep):
    m, l, acc = carry
    s = _dot_nt(k_t, qs) + bias
    if keep is not None:
        s = jnp.where(keep, s, NEG_INF)
    m_new = jnp.maximum(m, jnp.max(s, axis=0, keepdims=True))
    p = jnp.exp(s - m_new)
    alpha = jnp.exp(m - m_new)
    l = alpha * l + jnp.sum(p, axis=0, keepdims=True)
    acc = alpha * acc + jnp.dot(vt_t, p.astype(jnp.bfloat16), preferred_element_type=jnp.float32)
    return m_new, l, acc


def _nsa_prompt_kernel_tiled(rb_ref, q_ref, kv_ref, gt_ref, poolt_ref, pe_ref, o_ref,
                             kc_ref, vc_ref, ksel_ref, vselt_ref, kwin_ref, vwint_ref, tbl_ref, score_ref, sel_ref):
    b = pl.program_id(0)
    qi = pl.program_id(1)
    T = kv_ref.shape[1]
    nb = T // NSA_BLOCK
    topk = min(NSA_TOPK, nb)
    TQ = NSA_TQ
    CH = 512
    f32, bf16 = jnp.float32, jnp.bfloat16

    @pl.when((b == 0) & (qi == 0))
    def _tables():
        jj = lax.broadcasted_iota(jnp.int32, (TQ, TQ), 0)
        tt = lax.broadcasted_iota(jnp.int32, (TQ, TQ), 1)
        d0 = tt - jj
        for h in range(NSA_HEADS):
            kvg, g = divmod(h, NSA_GROUP)
            lanes = slice(g * TQ, (g + 1) * TQ)
            far = rb_ref[NUM_BUCKETS - 1, h]
            tbl_ref[kvg, 0, :, lanes] = jnp.where(d0 >= 0, _bias_from_dist(d0, rb_ref, h), NEG_INF)
            tbl_ref[kvg, 1, :, lanes] = _bias_from_dist(d0 + TQ, rb_ref, h)
            tbl_ref[kvg, 2, :, lanes] = jnp.where(d0 < 0, far, NEG_INF)

    @pl.when(qi == 0)
    def _prologue():
        def chunk(i, carry):
            r = pl.multiple_of(i * CH, CH)
            rb8 = pl.multiple_of(i * (CH // NSA_BLOCK), CH // NSA_BLOCK)
            for kvg in range(NSA_KV_HEADS):
                def col(c):
                    lo = c * NSA_KV_HEADS * HEAD_DIM + kvg * HEAD_DIM
                    return kv_ref[0, pl.ds(r, CH), lo:lo + HEAD_DIM]
                for c, dst in ((0, kc_ref), (1, vc_ref)):
                    x = col(c).reshape(CH // NSA_BLOCK, NSA_BLOCK, HEAD_DIM) + pe_ref[c][None]
                    dst[kvg, pl.ds(rb8, CH // NSA_BLOCK), :] = jnp.sum(x * poolt_ref[:, c:c + 1][None], axis=1)
                ksel_ref[kvg, pl.ds(r, CH), :] = col(2).astype(bf16)
                vselt_ref[kvg, :, pl.ds(r, CH)] = col(3).T.astype(bf16)
                kwin_ref[kvg, pl.ds(r, CH), :] = col(4).astype(bf16)
                vwint_ref[kvg, :, pl.ds(r, CH)] = col(5).T.astype(bf16)
            return carry
        lax.fori_loop(0, T // CH, chunk, 0)

    q = q_ref[0]
    gates = jax.nn.sigmoid(gt_ref[0])
    n_io = lax.broadcasted_iota(jnp.int32, (nb, TQ), 0)
    t_io = lax.broadcasted_iota(jnp.int32, (nb, TQ), 1)
    lane2 = lax.broadcasted_iota(jnp.int32, (1, 2 * TQ), 1)
    dist_c = qi * TQ + t_io - (n_io * NSA_BLOCK + NSA_BLOCK - 1)
    vis_c = dist_c >= 0
    vis_c2 = jnp.concatenate([vis_c, vis_c], axis=1)
    cur = (qi * TQ + t_io) // NSA_BLOCK
    forced = (n_io == 0) | (n_io == cur) | (n_io == cur - 1)
    q0 = pl.multiple_of(qi * TQ, TQ)
    outs = []
    for kvg in range(NSA_KV_HEADS):
        base = kvg * NSA_GROUP * HEAD_DIM
        qs = jnp.concatenate([q[:, base + g * HEAD_DIM: base + (g + 1) * HEAD_DIM] for g in range(NSA_GROUP)],
                             axis=0)
        qs = (qs * HEAD_DIM ** -0.5).astype(bf16)
        far_row = jnp.where(lane2 < TQ, rb_ref[NUM_BUCKETS - 1, kvg * NSA_GROUP],
                            rb_ref[NUM_BUCKETS - 1, kvg * NSA_GROUP + 1])

        sc = _dot_nt(kc_ref[kvg].astype(bf16), qs)
        bias_c = jnp.concatenate([_bias_from_dist(dist_c, rb_ref, kvg * NSA_GROUP + g) for g in range(NSA_GROUP)],
                                 axis=1)
        sc = jnp.where(vis_c2, sc + bias_c, NEG_INF)
        e = jnp.exp(sc - jnp.max(sc, axis=0, keepdims=True))
        p_c = e / jnp.sum(e, axis=0, keepdims=True) * vis_c2.astype(f32)
        oc = jnp.dot(vc_ref[kvg].T.astype(bf16), p_c.astype(bf16), preferred_element_type=f32)

        score = jnp.where(n_io <= cur, jnp.where(forced, 2.0, p_c[:, :TQ] + p_c[:, TQ:]), -1.0)
        score_ref[kvg] = score

        def rank_body(mi, rank):
            row = score_ref[kvg, pl.ds(mi, 1), :]
            beats = (row > score) | ((row == score) & (mi < n_io))
            return rank + beats.astype(jnp.int32)

        rank = lax.fori_loop(0, BLOCKS_PER_TILE * (qi + 1), rank_body, jnp.zeros((nb, TQ), jnp.int32))
        sel_ref[kvg] = ((rank < topk) & (n_io <= cur)).astype(f32)

        def sel_keep(j):
            rows = [jnp.broadcast_to(sel_ref[kvg, pl.ds(BLOCKS_PER_TILE * j + u, 1), :], (NSA_BLOCK, TQ))
                    for u in range(BLOCKS_PER_TILE)]
            mm = jnp.concatenate(rows, axis=0)
            return jnp.concatenate([mm, mm], axis=1) > 0.5

        def tile(kref, vtref, j):
            r = pl.multiple_of(j * TQ, TQ)
            return kref[kvg, pl.ds(r, TQ), :], vtref[kvg, :, pl.ds(r, TQ)]

        init = (jnp.full((1, 2 * TQ), NEG_INF, f32), jnp.zeros((1, 2 * TQ), f32), jnp.zeros((HEAD_DIM, 2 * TQ), f32))

        k_t, vt_t = tile(ksel_ref, vselt_ref, qi)
        carry = _flash_tile(init, k_t, vt_t, qs, tbl_ref[kvg, 0], sel_keep(qi))
        jp = jnp.maximum(qi - 1, 0)
        k_t, vt_t = tile(ksel_ref, vselt_ref, jp)
        carry = _flash_tile(carry, k_t, vt_t, qs, tbl_ref[kvg, 1], sel_keep(jp) & (qi >= 1))

        def far_body(i, c):
            j = qi - 2 - i
            k_f, vt_f = tile(ksel_ref, vselt_ref, j)
            return _flash_tile(c, k_f, vt_f, qs, far_row, sel_keep(j))

        m_s, l_s, acc_s = lax.fori_loop(0, jnp.maximum(qi - 1, 0), far_body, carry)

        k_t, vt_t = tile(kwin_ref, vwint_ref, qi)
        carry = _flash_tile(init, k_t, vt_t, qs, tbl_ref[kvg, 0], None)
        for back in range(1, WIN_TILES + 1):
            jb = jnp.maximum(qi - back, 0)
            k_t, vt_t = tile(kwin_ref, vwint_ref, jb)
            bias = tbl_ref[kvg, 1] if back == 1 else (tbl_ref[kvg, 2] if back == WIN_TILES else far_row)
            carry = _flash_tile(carry, k_t, vt_t, qs, bias, qi >= back)
        m_w, l_w, acc_w = carry

        o_s = acc_s / l_s
        o_w = acc_w / l_w
        for g in range(NSA_GROUP):
            h = kvg * NSA_GROUP + g
            lanes = slice(g * TQ, (g + 1) * TQ)
            o = (gates[h:h + 1] * oc[:, lanes] + gates[NSA_HEADS + h:NSA_HEADS + h + 1] * o_s[:, lanes]
                 + gates[2 * NSA_HEADS + h:2 * NSA_HEADS + h + 1] * o_w[:, lanes])
            outs.append(o.T)
    o_ref[0] = jnp.concatenate(outs, axis=1)


FAR_GROUP = 8
NEAR_ROWS = 2 * NSA_TQ
WIN_ROWS = NSA_WINDOW + NSA_TQ


def _nsa_prompt_kernel(rb_ref, q_ref, kv_ref, gt_ref, poolt_ref, pe_ref, o_ref,
                       kc_ref, vc_ref, ksel_ref, vselt_ref, kwin_ref, vwint_ref, near_ref, wtbl_ref, score_ref,
                       sel_ref):
    b = pl.program_id(0)
    qi = pl.program_id(1)
    T = kv_ref.shape[1]
    nb = T // NSA_BLOCK
    topk = min(NSA_TOPK, nb)
    TQ = NSA_TQ
    CH = 512
    f32, bf16 = jnp.float32, jnp.bfloat16
    kvs = range(NSA_KV_HEADS)

    @pl.when((b == 0) & (qi == 0))
    def _tables():
        for kvg in kvs:
            ksel_ref[kvg, 0:TQ, :] = jnp.zeros((TQ, HEAD_DIM), bf16)
            vselt_ref[kvg, :, 0:TQ] = jnp.zeros((HEAD_DIM, TQ), bf16)
            kwin_ref[kvg, 0:NSA_WINDOW, :] = jnp.zeros((NSA_WINDOW, HEAD_DIM), bf16)
            vwint_ref[kvg, :, 0:NSA_WINDOW] = jnp.zeros((HEAD_DIM, NSA_WINDOW), bf16)
        d_near = (lax.broadcasted_iota(jnp.int32, (NEAR_ROWS, TQ), 1) + TQ
                  - lax.broadcasted_iota(jnp.int32, (NEAR_ROWS, TQ), 0))
        d_win = (lax.broadcasted_iota(jnp.int32, (WIN_ROWS, TQ), 1) + NSA_WINDOW
                 - lax.broadcasted_iota(jnp.int32, (WIN_ROWS, TQ), 0))
        for h in range(NSA_HEADS):
            kvg, g = divmod(h, NSA_GROUP)
            lanes = slice(g * TQ, (g + 1) * TQ)
            near_ref[kvg, :, lanes] = jnp.where(d_near >= 0, _bias_from_dist(d_near, rb_ref, h), NEG_INF)
            wtbl_ref[kvg, :, lanes] = jnp.where((d_win >= 0) & (d_win < NSA_WINDOW),
                                                _bias_from_dist(d_win, rb_ref, h), NEG_INF)

    @pl.when(qi == 0)
    def _prologue():
        def chunk(i, carry):
            r = pl.multiple_of(i * CH, CH)
            rs = pl.multiple_of(i * CH + TQ, TQ)
            rw = pl.multiple_of(i * CH + NSA_WINDOW, TQ)
            rb8 = pl.multiple_of(i * (CH // NSA_BLOCK), CH // NSA_BLOCK)
            for kvg in kvs:
                def col(c):
                    lo = c * NSA_KV_HEADS * HEAD_DIM + kvg * HEAD_DIM
                    return kv_ref[0, pl.ds(r, CH), lo:lo + HEAD_DIM]
                for c, dst in ((0, kc_ref), (1, vc_ref)):
                    x = col(c).reshape(CH // NSA_BLOCK, NSA_BLOCK, HEAD_DIM) + pe_ref[c][None]
                    dst[kvg, pl.ds(rb8, CH // NSA_BLOCK), :] = jnp.sum(x * poolt_ref[:, c:c + 1][None], axis=1)
                ksel_ref[kvg, pl.ds(rs, CH), :] = col(2).astype(bf16)
                vselt_ref[kvg, :, pl.ds(rs, CH)] = col(3).T.astype(bf16)
                kwin_ref[kvg, pl.ds(rw, CH), :] = col(4).astype(bf16)
                vwint_ref[kvg, :, pl.ds(rw, CH)] = col(5).T.astype(bf16)
            return carry
        lax.fori_loop(0, T // CH, chunk, 0)

    q = q_ref[0]
    gates = jax.nn.sigmoid(gt_ref[0])
    n_io = lax.broadcasted_iota(jnp.int32, (nb, TQ), 0)
    t_io = lax.broadcasted_iota(jnp.int32, (nb, TQ), 1)
    lane2 = lax.broadcasted_iota(jnp.int32, (1, 2 * TQ), 1)
    dist_c = qi * TQ + t_io - (n_io * NSA_BLOCK + NSA_BLOCK - 1)
    vis_c = dist_c >= 0
    vis_c2 = jnp.concatenate([vis_c, vis_c], axis=1)
    cur = (qi * TQ + t_io) // NSA_BLOCK
    forced = (n_io == 0) | (n_io == cur) | (n_io == cur - 1)
    q0 = pl.multiple_of(qi * TQ, TQ)

    qs, far_row, oc, score = [], [], [], []
    for kvg in kvs:
        base = kvg * NSA_GROUP * HEAD_DIM
        qk = jnp.concatenate([q[:, base + g * HEAD_DIM: base + (g + 1) * HEAD_DIM] for g in range(NSA_GROUP)], axis=0)
        qs.append((qk * HEAD_DIM ** -0.5).astype(bf16))
        far_row.append(jnp.where(lane2 < TQ, rb_ref[NUM_BUCKETS - 1, kvg * NSA_GROUP],
                                 rb_ref[NUM_BUCKETS - 1, kvg * NSA_GROUP + 1]))
        sc = _dot_nt(kc_ref[kvg].astype(bf16), qs[kvg])
        bias_c = jnp.concatenate([_bias_from_dist(dist_c, rb_ref, kvg * NSA_GROUP + g) for g in range(NSA_GROUP)],
                                 axis=1)
        sc = jnp.where(vis_c2, sc + bias_c, NEG_INF)
        e = jnp.exp(sc - jnp.max(sc, axis=0, keepdims=True))
        p_c = e / jnp.sum(e, axis=0, keepdims=True) * vis_c2.astype(f32)
        oc.append(jnp.dot(vc_ref[kvg].T.astype(bf16), p_c.astype(bf16), preferred_element_type=f32))
        score.append(jnp.where(n_io <= cur, jnp.where(forced, 2.0, p_c[:, :TQ] + p_c[:, TQ:]), -1.0))
        score_ref[kvg] = score[kvg]

    def rank_body(mi, ranks):
        out = []
        for kvg in kvs:
            row = score_ref[kvg, pl.ds(mi, 1), :]
            beats = (row > score[kvg]) | ((row == score[kvg]) & (mi < n_io))
            out.append(ranks[kvg] + beats.astype(jnp.int32))
        return tuple(out)

    ranks = lax.fori_loop(0, BLOCKS_PER_TILE * (qi + 1), rank_body,
                          tuple(jnp.zeros((nb, TQ), jnp.int32) for _ in kvs))
    for kvg in kvs:
        sel_ref[kvg] = ((ranks[kvg] < topk) & (n_io <= cur)).astype(f32)

    def keep_rows(kvg, blk0, nblk, limit):
        rows = []
        for u in range(nblk):
            blk = blk0 + u
            ok = (blk >= 0) & (blk < limit)
            row = sel_ref[kvg, pl.ds(jnp.clip(blk, 0, nb - 1), 1), :]
            rows.append(jnp.broadcast_to(jnp.where(ok, row, 0.0), (NSA_BLOCK, TQ)))
        mm = jnp.concatenate(rows, axis=0)
        return jnp.concatenate([mm, mm], axis=1) > 0.5

    carries = []
    for kvg in kvs:
        k_n = ksel_ref[kvg, pl.ds(q0, NEAR_ROWS), :]
        vt_n = vselt_ref[kvg, :, pl.ds(q0, NEAR_ROWS)]
        keep = keep_rows(kvg, BLOCKS_PER_TILE * (qi - 1), 2 * BLOCKS_PER_TILE, nb)
        s = jnp.where(keep, _dot_nt(k_n, qs[kvg]) + near_ref[kvg], NEG_INF)
        m = jnp.max(s, axis=0, keepdims=True)
        p = jnp.exp(s - m)
        carries.append((m, jnp.sum(p, axis=0, keepdims=True),
                        jnp.dot(vt_n, p.astype(bf16), preferred_element_type=f32)))

    n_far = jnp.maximum(qi - 1, 0)
    rows_far = FAR_GROUP * TQ

    def far_body(i, cs):
        out = []
        for kvg in kvs:
            r = pl.multiple_of(TQ + i * rows_far, TQ)
            keep = keep_rows(kvg, i * FAR_GROUP * BLOCKS_PER_TILE, FAR_GROUP * BLOCKS_PER_TILE,
                             n_far * BLOCKS_PER_TILE)
            out.append(_flash_tile(cs[kvg], ksel_ref[kvg, pl.ds(r, rows_far), :],
                                   vselt_ref[kvg, :, pl.ds(r, rows_far)], qs[kvg], far_row[kvg], keep))
        return tuple(out)

    carries = lax.fori_loop(0, (n_far + FAR_GROUP - 1) // FAR_GROUP, far_body, tuple(carries))

    w_io = lax.broadcasted_iota(jnp.int32, (WIN_ROWS, 2 * TQ), 0)
    outs = []
    for kvg in kvs:
        m_s, l_s, acc_s = carries[kvg]
        o_s = acc_s / l_s
        k_w = kwin_ref[kvg, pl.ds(q0, WIN_ROWS), :]
        vt_w = vwint_ref[kvg, :, pl.ds(q0, WIN_ROWS)]
        s = jnp.where(w_io >= NSA_WINDOW - q0, _dot_nt(k_w, qs[kvg]) + wtbl_ref[kvg], NEG_INF)
        e = jnp.exp(s - jnp.max(s, axis=0, keepdims=True))
        o_w = (jnp.dot(vt_w, e.astype(bf16), preferred_element_type=f32) / jnp.sum(e, axis=0, keepdims=True))
        for g in range(NSA_GROUP):
            h = kvg * NSA_GROUP + g
            lanes = slice(g * TQ, (g + 1) * TQ)
            o = (gates[h:h + 1] * oc[kvg][:, lanes] + gates[NSA_HEADS + h:NSA_HEADS + h + 1] * o_s[:, lanes]
                 + gates[2 * NSA_HEADS + h:2 * NSA_HEADS + h + 1] * o_w[:, lanes])
            outs.append(o.T)
    o_ref[0] = jnp.concatenate(outs, axis=1)


def _nsa_prompt(q, q_blk, kv, kv_blk, gate_logits, cmp_pool, cmp_pe, rel_bias):
    B, T, _ = q.shape
    nb = T // NSA_BLOCK
    assert T % 512 == 0 and nb % SUBLANES == 0 and (T // NSA_TQ) % FAR_GROUP == 0
    TQ = NSA_TQ
    gt = jnp.transpose(gate_logits, (0, 2, 1))
    f32, bf16 = jnp.float32, jnp.bfloat16
    return pl.pallas_call(
        _nsa_prompt_kernel,
        grid=(B, T // TQ),
        in_specs=[
            pl.BlockSpec(memory_space=pltpu.SMEM),
            pl.BlockSpec((1, TQ, NSA_HEADS * HEAD_DIM), lambda b, i: (b, i, q_blk)),
            pl.BlockSpec((1, T, 6 * NSA_KV_HEADS * HEAD_DIM), lambda b, i: (b, 0, kv_blk)),
            pl.BlockSpec((1, 3 * NSA_HEADS, TQ), lambda b, i: (b, 0, i)),
            pl.BlockSpec((NSA_BLOCK, 2), lambda b, i: (0, 0)),
            pl.BlockSpec((2, NSA_BLOCK, HEAD_DIM), lambda b, i: (0, 0, 0)),
        ],
        out_specs=pl.BlockSpec((1, TQ, NSA_HEADS * HEAD_DIM), lambda b, i: (b, i, 0)),
        out_shape=jax.ShapeDtypeStruct((B, T, NSA_HEADS * HEAD_DIM), f32),
        scratch_shapes=[
            pltpu.VMEM((NSA_KV_HEADS, nb, HEAD_DIM), f32),
            pltpu.VMEM((NSA_KV_HEADS, nb, HEAD_DIM), f32),
            pltpu.VMEM((NSA_KV_HEADS, T + TQ, HEAD_DIM), bf16),
            pltpu.VMEM((NSA_KV_HEADS, HEAD_DIM, T + TQ), bf16),
            pltpu.VMEM((NSA_KV_HEADS, T + NSA_WINDOW, HEAD_DIM), bf16),
            pltpu.VMEM((NSA_KV_HEADS, HEAD_DIM, T + NSA_WINDOW), bf16),
            pltpu.VMEM((NSA_KV_HEADS, NEAR_ROWS, NSA_GROUP * TQ), f32),
            pltpu.VMEM((NSA_KV_HEADS, WIN_ROWS, NSA_GROUP * TQ), f32),
            pltpu.VMEM((NSA_KV_HEADS, nb, TQ), f32),
            pltpu.VMEM((NSA_KV_HEADS, nb, TQ), f32),
        ],
        compiler_params=pltpu.CompilerParams(dimension_semantics=("arbitrary", "arbitrary"),
                                             vmem_limit_bytes=VMEM_LIMIT),
        name="nsa_prompt",
    )(rel_bias, q, kv, gt, jnp.transpose(cmp_pool), cmp_pe)


PAGES_PER_STEP = 16
SEL_TILE = 2048
KVW = NSA_KV_HEADS * HEAD_DIM


def _nsa_sample_kernel(pt_ref, rb_ref, *refs):
    PPS = PAGES_PER_STEP
    pages = refs[:PPS]
    q_ref, kvn_ref, gt_ref, win_ref, poolm_ref, cconst_ref, o_ref, cmp_ref, kselt_ref, vselt_ref = refs[PPS:]
    s_id = pl.program_id(1)
    f32, bf16 = jnp.float32, jnp.bfloat16
    Tn = q_ref.shape[1]
    past = kselt_ref.shape[1]
    nbp = past // NSA_BLOCK
    Wb = win_ref.shape[3]
    bpp = PAGE_SIZE // NSA_BLOCK
    R = NSA_HEADS * Tn

    rows = []
    for k in range(PPS):
        parts = []
        for c in range(2):
            a = _dot_nt(poolm_ref[c], pages[k][0, 0, c].astype(bf16))
            parts.append(a[0:bpp] + a[SUBLANES:SUBLANES + bpp] + cconst_ref[c:c + 1])
        rows.append(jnp.concatenate(parts, axis=1))
        r = pl.multiple_of((s_id * PPS + k) * PAGE_SIZE, PAGE_SIZE)
        kselt_ref[:, pl.ds(r, PAGE_SIZE)] = pages[k][0, 0, 2].astype(bf16)
        vselt_ref[:, pl.ds(r, PAGE_SIZE)] = pages[k][0, 0, 3].astype(bf16)
    cmp_ref[pl.ds(pl.multiple_of(s_id * PPS * bpp, PPS * bpp), PPS * bpp), :] = jnp.concatenate(rows, axis=0)

    @pl.when(s_id == pl.num_programs(1) - 1)
    def _attend():
        def per_head(fn):
            return jnp.concatenate([fn(h) for h in range(NSA_HEADS)], axis=0)

        q = q_ref[0] * HEAD_DIM ** -0.5
        zero = jnp.zeros((Tn, HEAD_DIM), f32)

        def q_rows(h):
            qh = q[:, h * HEAD_DIM:(h + 1) * HEAD_DIM]
            return jnp.concatenate([qh, zero] if h < NSA_GROUP else [zero, qh], axis=1)

        q2 = per_head(q_rows).astype(bf16)
        kvn = kvn_ref[0]
        pad = jnp.zeros((LANES - Tn, KVW), f32)

        def new_rows(c):
            return jnp.concatenate([kvn[:, c * KVW:(c + 1) * KVW], pad], axis=0).astype(bf16)

        tn_io = lax.broadcasted_iota(jnp.int32, (Tn, LANES), 0)
        jn_io = lax.broadcasted_iota(jnp.int32, (Tn, LANES), 1)
        d_new = tn_io - jn_io
        keep_new = per_head(lambda h: (d_new >= 0) & (jn_io < Tn))
        bias_new = per_head(lambda h: _bias_from_dist(d_new, rb_ref, h))
        far = per_head(lambda h: jnp.full((Tn, 1), rb_ref[NUM_BUCKETS - 1, h], f32))

        kc = cmp_ref[:, 0:KVW].astype(bf16)
        vc = cmp_ref[:, KVW:2 * KVW].astype(bf16)
        n_io = lax.broadcasted_iota(jnp.int32, (Tn, nbp), 1)
        t_io = lax.broadcasted_iota(jnp.int32, (Tn, nbp), 0)
        dist_c = past + t_io - (n_io * NSA_BLOCK + NSA_BLOCK - 1)
        sc = _dot_nt(q2, kc) + per_head(lambda h: _bias_from_dist(dist_c, rb_ref, h))
        e = jnp.exp(sc - jnp.max(sc, axis=1, keepdims=True))
        p_c = e / jnp.sum(e, axis=1, keepdims=True)
        oc = jnp.dot(p_c.astype(bf16), vc, preferred_element_type=f32)

        topk = min(NSA_TOPK, nbp + 1)
        m_io = lax.broadcasted_iota(jnp.int32, (nbp, nbp), 0)
        c_io = lax.broadcasted_iota(jnp.int32, (nbp, nbp), 1)
        lower = m_io < c_io
        forced = (n_io == 0) | (n_io == nbp - 1)
        sels = []
        for kvg in range(NSA_KV_HEADS):
            r0 = kvg * NSA_GROUP * Tn
            score = jnp.where(forced, 2.0, p_c[r0:r0 + Tn] + p_c[r0 + Tn:r0 + 2 * Tn])
            score_t = jnp.concatenate([score, jnp.zeros((LANES - Tn, nbp), f32)], axis=0).T
            ranks = []
            for t in range(Tn):
                colb = jnp.broadcast_to(score_t[:, t:t + 1], (nbp, nbp))
                rowb = jnp.broadcast_to(score[t:t + 1, :], (nbp, nbp))
                beats = (colb > rowb) | ((colb == rowb) & lower)
                ranks.append(jnp.sum(beats.astype(f32), axis=0, keepdims=True))
            sel = (jnp.concatenate(ranks, axis=0) < topk - 1).astype(f32)
            sels += [sel] * NSA_GROUP
        sel_rows = jnp.concatenate(sels, axis=0).astype(bf16)

        bpt = SEL_TILE // NSA_BLOCK
        expand = (lax.broadcasted_iota(jnp.int32, (bpt, SEL_TILE), 0)
                  == lax.broadcasted_iota(jnp.int32, (bpt, SEL_TILE), 1) // NSA_BLOCK).astype(bf16)
        d_last = LANES + tn_io - jn_io
        near = per_head(lambda h: _bias_from_dist(d_last, rb_ref, h))
        m = jnp.full((R, 1), NEG_INF, f32)
        l = jnp.zeros((R, 1), f32)
        acc = jnp.zeros((R, KVW), f32)

        def flash(carry, s, v_t, v_feature_major):
            m, l, acc = carry
            m_new = jnp.maximum(m, jnp.max(s, axis=1, keepdims=True))
            p = jnp.exp(s - m_new)
            alpha = jnp.exp(m - m_new)
            pb = p.astype(bf16)
            pv = _dot_nt(pb, v_t) if v_feature_major else jnp.dot(pb, v_t, preferred_element_type=f32)
            return m_new, alpha * l + jnp.sum(p, axis=1, keepdims=True), alpha * acc + pv

        carry = (m, l, acc)
        ntile = past // SEL_TILE
        for j in range(ntile):
            k_t = kselt_ref[:, j * SEL_TILE:(j + 1) * SEL_TILE]
            v_t = vselt_ref[:, j * SEL_TILE:(j + 1) * SEL_TILE]
            keep = jnp.dot(sel_rows[:, j * bpt:(j + 1) * bpt], expand, preferred_element_type=f32) > 0.5
            if j == ntile - 1:
                bias = jnp.concatenate([jnp.broadcast_to(far, (R, SEL_TILE - LANES)), near], axis=1)
            else:
                bias = far
            s_t = jnp.dot(q2, k_t, preferred_element_type=f32)
            carry = flash(carry, jnp.where(keep, s_t + bias, NEG_INF), v_t, True)
        s_new = jnp.where(keep_new, _dot_nt(q2, new_rows(2)) + bias_new, NEG_INF)
        m, l, acc = flash(carry, s_new, new_rows(3), False)
        o_s = acc / l

        tw_io = lax.broadcasted_iota(jnp.int32, (Tn, Wb), 0)
        cw_io = lax.broadcasted_iota(jnp.int32, (Tn, Wb), 1)
        d_w = Wb + tw_io - cw_io
        s_w = jnp.where(per_head(lambda h: d_w < NSA_WINDOW),
                        jnp.dot(q2, win_ref[0, 0].astype(bf16), preferred_element_type=f32)
                        + per_head(lambda h: _bias_from_dist(d_w, rb_ref, h)), NEG_INF)
        s_wn = jnp.where(keep_new, _dot_nt(q2, new_rows(4)) + bias_new, NEG_INF)
        s_all = jnp.concatenate([s_w, s_wn], axis=1)
        e = jnp.exp(s_all - jnp.max(s_all, axis=1, keepdims=True))
        p_w = (e / jnp.sum(e, axis=1, keepdims=True)).astype(bf16)
        o_w = _dot_nt(p_w[:, :Wb], win_ref[0, 1].astype(bf16)) + jnp.dot(p_w[:, Wb:], new_rows(5),
                                                                          preferred_element_type=f32)

        gates = jax.nn.sigmoid(gt_ref[0])
        outs = []
        for h in range(NSA_HEADS):
            rs = slice(h * Tn, (h + 1) * Tn)
            cs = slice((h // NSA_GROUP) * HEAD_DIM, (h // NSA_GROUP + 1) * HEAD_DIM)
            outs.append(gates[:, h:h + 1] * oc[rs, cs] + gates[:, NSA_HEADS + h:NSA_HEADS + h + 1] * o_s[rs, cs]
                        + gates[:, 2 * NSA_HEADS + h:2 * NSA_HEADS + h + 1] * o_w[rs, cs])
        o_ref[0] = jnp.concatenate(outs, axis=1)


def _nsa_sample(q, kv, gate_logits, cache, layer, page_table, win_buf, cmp_pool, cmp_pe, rel_bias):
    B, Tn, _ = q.shape
    npages = page_table.shape[1]
    past = npages * PAGE_SIZE
    Wb = win_buf.shape[1]
    PPS = PAGES_PER_STEP
    assert npages % PPS == 0 and past % SEL_TILE == 0 and Tn == SUBLANES and (past // NSA_BLOCK) % LANES == 0
    f32, bf16 = jnp.float32, jnp.bfloat16
    bpp = PAGE_SIZE // NSA_BLOCK
    cache_t = jnp.transpose(cache, (0, 1, 3, 4, 5, 2)).reshape(cache.shape[0], cache.shape[1], 4, KVW, PAGE_SIZE)
    win_t = jnp.transpose(win_buf, (0, 2, 3, 4, 1)).reshape(B, 2, KVW, Wb)
    r_io = np.arange(PAGE_SIZE)
    onehot = jnp.asarray((r_io[None, :] // NSA_BLOCK == np.arange(SUBLANES)[:, None]), f32)
    pool_full = onehot[None] * jnp.tile(cmp_pool, (1, bpp))[:, None, :]
    pool_hi = pool_full.astype(bf16)
    pool_lo = (pool_full - pool_hi.astype(f32)).astype(bf16)
    poolm = jnp.concatenate([pool_hi, pool_lo], axis=1)
    cconst = jnp.tile(jnp.sum(cmp_pool[:, :, None] * cmp_pe, axis=1), (1, NSA_KV_HEADS))

    def page_spec(k):
        return pl.BlockSpec((1, 1, 4, KVW, PAGE_SIZE), lambda b, s, pt: (pt[b, s * PPS + k], layer, 0, 0, 0))

    grid_spec = pltpu.PrefetchScalarGridSpec(
        num_scalar_prefetch=1,
        grid=(B, npages // PPS),
        in_specs=[pl.BlockSpec(memory_space=pltpu.SMEM)] + [page_spec(k) for k in range(PPS)] + [
            pl.BlockSpec((1, Tn, NSA_HEADS * HEAD_DIM), lambda b, s, pt: (b, 0, 0)),
            pl.BlockSpec((1, Tn, 6 * KVW), lambda b, s, pt: (b, 0, 0)),
            pl.BlockSpec((1, Tn, 3 * NSA_HEADS), lambda b, s, pt: (b, 0, 0)),
            pl.BlockSpec((1, 2, KVW, Wb), lambda b, s, pt: (b, 0, 0, 0)),
            pl.BlockSpec((2, 2 * SUBLANES, PAGE_SIZE), lambda b, s, pt: (0, 0, 0)),
            pl.BlockSpec((2, KVW), lambda b, s, pt: (0, 0)),
        ],
        out_specs=pl.BlockSpec((1, Tn, NSA_HEADS * HEAD_DIM), lambda b, s, pt: (b, 0, 0)),
        scratch_shapes=[
            pltpu.VMEM((past // NSA_BLOCK, 2 * KVW), f32),
            pltpu.VMEM((KVW, past), bf16),
            pltpu.VMEM((KVW, past), bf16),
        ],
    )
    return pl.pallas_call(
        _nsa_sample_kernel,
        grid_spec=grid_spec,
        out_shape=jax.ShapeDtypeStruct((B, Tn, NSA_HEADS * HEAD_DIM), f32),
        compiler_params=pltpu.CompilerParams(dimension_semantics=("arbitrary", "arbitrary"),
                                             vmem_limit_bytes=VMEM_LIMIT),
        name="nsa_sample",
    )(page_table, rel_bias, *([cache_t] * PPS), q, kv, gate_logits, win_t, poolm, cconst)


def nsa_attention(q, kv_new, gate_logits, past_rows, win_buf, pos0, cmp_pool, cmp_pe, rel_bias):
    B, T = q.shape[:2]
    if past_rows is None:
        o = _nsa_prompt(q.reshape(B, T, -1), 0, kv_new.reshape(B, T, -1), 0, gate_logits.reshape(B, T, -1),
                        cmp_pool, cmp_pe, rel_bias)
        return o, kv_new[:, T - min(NSA_WINDOW, T):, 4:]
    if isinstance(past_rows, tuple):
        cache, layer, page_table = past_rows
        o = _nsa_sample(q.reshape(B, T, -1), kv_new.reshape(B, T, -1), gate_logits.reshape(B, T, -1),
                        cache, layer, page_table, win_buf, cmp_pool, cmp_pe, rel_bias)
        real = jnp.concatenate([win_buf, kv_new[:, :, 4:]], axis=1)
        return o, real[:, real.shape[1] - min(NSA_WINDOW, real.shape[1]):]
    dt = q.dtype
    qpos = pos0 + jnp.arange(T, dtype=jnp.int32)
    qg = (q * HEAD_DIM ** -0.5).reshape(B, T, NSA_KV_HEADS, NSA_GROUP, HEAD_DIM)
    rb = rel_bias.astype(jnp.float32).reshape(NUM_BUCKETS, NSA_KV_HEADS, NSA_GROUP)
    L = T if past_rows is None else past_rows.shape[1] + T
    nb = -(-L // NSA_BLOCK)
    new_rows = jnp.pad(kv_new[:, :, :4], ((0, 0), (0, nb * NSA_BLOCK - L), (0, 0), (0, 0), (0, 0)))
    full = new_rows if past_rows is None else jnp.concatenate([past_rows.astype(dt), new_rows], axis=1)
    blocks = full.reshape(B, nb, NSA_BLOCK, 4, NSA_KV_HEADS, HEAD_DIM)
    pe = jnp.transpose(cmp_pe, (1, 0, 2))[:, :, None, :].astype(dt)
    cmp = jnp.einsum('bnjckd,cj->bnckd', blocks[:, :, :, :2] + pe, cmp_pool.astype(dt))
    k_c, v_c = cmp[:, :, 0], cmp[:, :, 1]
    blk = jnp.arange(nb, dtype=jnp.int32)
    d_c = qpos[:, None] - (blk * NSA_BLOCK + NSA_BLOCK - 1)[None, :]
    s_c = jnp.einsum('btkgd,bnkd->bkgtn', qg, k_c).astype(jnp.float32) + jnp.transpose(rb[t5_bucket(d_c)], (2, 3, 0, 1))
    p_c = masked_softmax(s_c, d_c >= 0)
    o_c = jnp.einsum('bkgtn,bnkd->btkgd', p_c.astype(dt), v_c)
    cur = (qpos // NSA_BLOCK)[:, None]
    forced = (blk[None] == 0) | (blk[None] == cur) | (blk[None] == cur - 1)
    score = jnp.where(blk[None] <= cur, jnp.where(forced, 2.0, p_c.sum(axis=2)), -1.0)
    top_s, idx = lax.top_k(score, min(NSA_TOPK, nb))
    ok = top_s > -0.5
    ks = jnp.transpose(blocks[:, :, :, 2], (0, 3, 1, 2, 4))
    vs = jnp.transpose(blocks[:, :, :, 3], (0, 3, 1, 2, 4))
    take = jax.vmap(jax.vmap(lambda a, i: a[i]))
    kv_ix = jnp.arange(NSA_KV_HEADS)[None, :, None, None, None]
    rbk = jnp.transpose(rb, (1, 0, 2))

    def sel_block(args):
        qb, ib, okb, pb = args
        Bq, Q = qb.shape[:2]
        kg, vg = take(ks, ib), take(vs, ib)
        kpos = ib[..., None] * NSA_BLOCK + jnp.arange(NSA_BLOCK, dtype=jnp.int32)
        dist = pb[None, None, :, None, None] - kpos
        mask = (okb[..., None] & (dist >= 0)).reshape(Bq, NSA_KV_HEADS, 1, Q, -1)
        bias = jnp.moveaxis(rbk[kv_ix, t5_bucket(dist)], -1, 2)
        s = jnp.einsum('bqkgd,bkqsjd->bkgqsj', qb, kg).astype(jnp.float32) + bias
        p = masked_softmax(s.reshape(Bq, NSA_KV_HEADS, NSA_GROUP, Q, -1), mask)
        return jnp.einsum('bkgqn,bkqnd->bqkgd', p.astype(dt), vg.reshape(Bq, NSA_KV_HEADS, Q, -1, HEAD_DIM))

    o_s = over_query_blocks(sel_block, (qg, idx, ok, qpos), (1, 2, 2, 0), T)
    kw = kv_new[:, :, 4:]
    if win_buf is None:
        real = kw
        k_all = jnp.pad(kw, ((0, 0), (NSA_WINDOW, 0), (0, 0), (0, 0), (0, 0)))
        span = NSA_WINDOW
    else:
        real = jnp.concatenate([win_buf.astype(dt), kw], axis=1)
        k_all = real
        span = win_buf.shape[1]
    k0 = pos0 - span

    def win_block(args):
        qb, pb = args
        Q = qb.shape[1]
        start = pb[0] - pos0
        kb = lax.dynamic_slice_in_dim(k_all, start, span + Q, axis=1)
        kpos = k0 + start + jnp.arange(span + Q, dtype=jnp.int32)
        dist = pb[:, None] - kpos[None, :]
        mask = (dist >= 0) & (dist < NSA_WINDOW) & (kpos >= 0)[None, :]
        s = jnp.einsum('bqkgd,bnkd->bkgqn', qb, kb[:, :, 0]).astype(jnp.float32) + jnp.transpose(rb[t5_bucket(dist)], (2, 3, 0, 1))
        p = masked_softmax(s, mask)
        return jnp.einsum('bkgqn,bnkd->bqkgd', p.astype(dt), kb[:, :, 1])

    o_w = over_query_blocks(win_block, (qg, qpos), (1, 0), T)
    new_win = real[:, real.shape[1] - min(NSA_WINDOW, real.shape[1]):]
    g = jax.nn.sigmoid(gate_logits.astype(jnp.float32)).astype(dt).reshape(B, T, 3, NSA_KV_HEADS, NSA_GROUP, 1)
    o = g[:, :, 0] * o_c + g[:, :, 1] * o_s + g[:, :, 2] * o_w
    return o.reshape(B, T, NSA_HEADS * HEAD_DIM), new_win


def conformer_conv(u, buf, dw, dw_b, ln_g, ln_b):
    a, gte = jnp.split(u, 2, axis=-1)
    y, new_buf = causal_dwconv(a * jax.nn.sigmoid(gte), buf, dw)
    y = layer_norm(y + dw_b.astype(y.dtype), ln_g, ln_b)
    return jax.nn.silu(y), new_buf


def retention(q, k, v, gate, S0, pos0, gn):
    B, T, _ = q.shape
    dt = q.dtype
    H, d = RET_HEADS, HEAD_DIM
    pos = pos0 + jnp.arange(T, dtype=jnp.int32)
    f = lambda a: a.astype(jnp.float32).reshape(B, T, H, d)
    q = rotary(f(q), pos)
    k = rotary(f(k), pos) * d ** -0.5
    v = f(v)
    C = math.gcd(T, RET_CHUNK)
    n = T // C
    lg = jnp.log1p(-jnp.exp2(-5.0 - jnp.arange(H, dtype=jnp.float32)))
    ch = lambda a: a.reshape(B, n, C, H, d).transpose(0, 3, 1, 2, 4)
    qc, kc, vc = ch(q), ch(k), ch(v)
    i = jnp.arange(C, dtype=jnp.float32)
    diff = i[:, None] - i[None, :]
    Dm = jnp.where(diff >= 0, jnp.exp(jnp.maximum(diff, 0.0)[None] * lg[:, None, None]), 0.0)
    att = jnp.einsum('bhncd,bhnsd->bhncs', qc, kc) * Dm[None, :, None]
    o = jnp.einsum('bhncs,bhnse->bhnce', att, vc)
    xi = jnp.exp((i[None, :] + 1.0) * lg[:, None])
    zeta = jnp.exp((C - 1.0 - i[None, :]) * lg[:, None])
    kv = jnp.einsum('bhncd,bhnce->nbhde', kc * zeta[None, :, None, :, None], vc)
    decay = jnp.exp(C * lg)[None, :, None, None]

    def step(S, kvn):
        return S * decay + kvn, S

    S_fin, S_prev = lax.scan(step, S0.astype(jnp.float32), kv)
    o = o + jnp.einsum('bhncd,nbhde->bhnce', qc * xi[None, :, None, :, None], S_prev)
    o = o.transpose(0, 2, 3, 1, 4).reshape(B, T, H, d)
    mu = jnp.mean(o, axis=-1, keepdims=True)
    var = jnp.mean(jnp.square(o - mu), axis=-1, keepdims=True)
    o = ((o - mu) * lax.rsqrt(var + EPS)).reshape(B, T, H * d) * gn.astype(jnp.float32)
    return (o * jax.nn.silu(gate.astype(jnp.float32))).astype(dt), S_fin


def gated_deltanet(qkv, z, a, b, conv_buf, S0, conv_w, A_log, dt_bias, norm_g):
    B, T, _ = qkv.shape
    dt = qkv.dtype
    H, d = GDN_HEADS, HEAD_DIM
    y, new_buf = causal_dwconv(qkv, conv_buf, conv_w)
    y = jax.nn.silu(y.astype(jnp.float32))
    q, k, v = [t.reshape(B, T, H, d) for t in jnp.split(y, 3, axis=-1)]
    l2 = lambda t: t * lax.rsqrt(jnp.sum(t * t, axis=-1, keepdims=True) + EPS)
    q = l2(q) * d ** -0.5
    k = l2(k)
    g = -jnp.exp(A_log.astype(jnp.float32)) * jax.nn.softplus(a.astype(jnp.float32) + dt_bias.astype(jnp.float32))
    beta = jax.nn.sigmoid(b.astype(jnp.float32))
    C = math.gcd(T, GDN_CHUNK)
    n = T // C
    ch = lambda t: t.reshape(B, n, C, H, d).transpose(0, 3, 1, 2, 4)
    chs = lambda t: t.reshape(B, n, C, H).transpose(0, 3, 1, 2)
    qc, kc, vc = ch(q), ch(k), ch(v)
    gc = jnp.cumsum(chs(g), axis=-1)
    bc = chs(beta)
    ii = jnp.arange(C)
    tri = ii[:, None] >= ii[None, :]
    strict = ii[:, None] > ii[None, :]
    Lm = jnp.exp(jnp.where(tri, gc[..., :, None] - gc[..., None, :], -jnp.inf))
    kb = kc * bc[..., None]
    M = jnp.where(strict, jnp.einsum('bhncd,bhnsd->bhncs', kb, kc) * Lm, 0.0)
    A = M + jnp.eye(C, dtype=jnp.float32)
    rhs = jnp.concatenate([vc * bc[..., None], kb * jnp.exp(gc)[..., None]], axis=-1)
    sol = lax.linalg.triangular_solve(A, rhs, left_side=True, lower=True, unit_diagonal=True)
    u, w = sol[..., :d], sol[..., d:]
    att = jnp.einsum('bhncd,bhnsd->bhncs', qc, kc) * Lm
    qd = qc * jnp.exp(gc)[..., None]
    kd = kc * jnp.exp(gc[..., -1:] - gc)[..., None]
    glast = jnp.exp(gc[..., -1])
    xs = tuple(jnp.moveaxis(t, 2, 0) for t in (u, w, att, qd, kd, glast))

    def step(S, xs_n):
        u_, w_, at_, qd_, kd_, gl_ = xs_n
        vn = u_ - jnp.einsum('bhcd,bhde->bhce', w_, S)
        o_ = jnp.einsum('bhcd,bhde->bhce', qd_, S) + jnp.einsum('bhcs,bhse->bhce', at_, vn)
        S = S * gl_[..., None, None] + jnp.einsum('bhcd,bhce->bhde', kd_, vn)
        return S, o_

    S_fin, o = lax.scan(step, S0.astype(jnp.float32), xs)
    o = o.transpose(1, 0, 3, 2, 4).reshape(B, T, H, d)
    o = o * lax.rsqrt(jnp.mean(o * o, axis=-1, keepdims=True) + EPS) * norm_g.astype(jnp.float32)
    o = o * jax.nn.silu(z.astype(jnp.float32).reshape(B, T, H, d))
    return o.reshape(B, T, H * d).astype(dt), new_buf, S_fin


def conv_ffn(h, buf, w_up, dw, w_down):
    Bh, Th, _ = h.shape
    gpre, val = jnp.split(_pmm(h.reshape(Bh * Th, D_MODEL), w_up).reshape(Bh, Th, 2 * D_FF), 2, axis=-1)
    gconv, new_buf = causal_dwconv(gpre, buf, dw)
    return _pmm((jax.nn.gelu(gconv) * val).reshape(Bh * Th, D_FF), w_down).reshape(Bh, Th, D_MODEL), new_buf


PROJ_KV, PROJ_Q, PROJ_UCV, PROJ_RET, PROJ_GQKV, PROJ_GZ, PROJ_MISC = 0, 768, 1024, 1536, 2560, 3328, 3584
PROJ_COLS = PROJ_MISC + LANES
PROJ_TM = 512
MERGE_TM = 256
FFN_TM = 256


def _prep_w_in(w_in):
    o = [int(v) for v in np.cumsum((0,) + IN_SPLITS)]
    small = o[5] + 4 * GDN_W
    pieces = [w_in[:, o[1]:o[2]], w_in[:, o[0]:o[1]], w_in[:, o[3]:o[4]], w_in[:, o[4]:o[5]], w_in[:, o[5]:small],
              w_in[:, o[2]:o[3]], w_in[:, small:o[6]]]
    used = sum(pc.shape[1] for pc in pieces)
    pieces.append(jnp.zeros((D_MODEL, PROJ_COLS - used), w_in.dtype))
    return jnp.concatenate(pieces, axis=1), w_in[:, o[6]:]


def _modulated_norm(x, g, sc, sh):
    y = x * lax.rsqrt(jnp.mean(x * x, axis=-1, keepdims=True) + EPS)
    return (y * g) * (1.0 + sc) + sh


def _resident(shape):
    return pl.BlockSpec(shape, lambda *_: (0,) * len(shape), pipeline_mode=pl.Buffered(1))


def _mod_spec(mod, rows_per_group, tm):
    if mod.shape[1] == 1:
        return pl.BlockSpec((1, 1, D_MODEL), lambda i, *_: (i // (rows_per_group // tm), 0, 0))
    return pl.BlockSpec((1, tm, D_MODEL), lambda i, *_: (0, i, 0))


def _proj_kernel(x_ref, g_ref, sc_ref, sh_ref, w_ref, o_ref, wb_ref):
    @pl.when(pl.program_id(0) == 0)
    def _():
        wb_ref[...] = w_ref[...].astype(jnp.bfloat16)

    h = _modulated_norm(x_ref[...], g_ref[...], sc_ref[0], sh_ref[0]).astype(jnp.bfloat16)
    o_ref[...] = jnp.dot(h, wb_ref[...], preferred_element_type=jnp.float32)


def _proj(x2, g, sc, sh, w, rows_per_group, tm):
    M = x2.shape[0]
    N = w.shape[1]
    return pl.pallas_call(
        _proj_kernel,
        grid=(M // tm,),
        in_specs=[pl.BlockSpec((tm, D_MODEL), lambda i: (i, 0)), _resident((1, D_MODEL)),
                  _mod_spec(sc, rows_per_group, tm), _mod_spec(sh, rows_per_group, tm), _resident((D_MODEL, N))],
        out_specs=pl.BlockSpec((tm, N), lambda i: (i, 0)),
        out_shape=jax.ShapeDtypeStruct((M, N), jnp.float32),
        scratch_shapes=[pltpu.VMEM((D_MODEL, N), jnp.bfloat16)],
        compiler_params=pltpu.CompilerParams(dimension_semantics=("arbitrary",), vmem_limit_bytes=VMEM_LIMIT),
        name="in_proj",
    )(x2, g, sc, sh, w)


def _merge_kernel(x_ref, g0_ref, sc_ref, sh_ref, gt_ref, g1_ref, b0_ref, b1_ref, b2_ref, b3_ref,
                  wmg_ref, wbr_ref, wout_ref, o_ref, wmgb_ref):
    f32, bf16 = jnp.float32, jnp.bfloat16

    @pl.when(pl.program_id(0) == 0)
    def _():
        wmgb_ref[...] = wmg_ref[...].astype(bf16)

    x = x_ref[...]
    h = _modulated_norm(x, g0_ref[...], sc_ref[0], sh_ref[0]).astype(bf16)
    acc = jnp.zeros(x.shape, f32)
    for n, b_ref in enumerate((b0_ref, b1_ref, b2_ref, b3_ref)):
        gate = jax.nn.sigmoid(jnp.dot(h, wmgb_ref[:, n * D_MODEL:(n + 1) * D_MODEL], preferred_element_type=f32))
        acc = acc + gate * jnp.dot(b_ref[...].astype(bf16), wbr_ref[n], preferred_element_type=f32)
    mixed = jnp.dot(acc.astype(bf16), wout_ref[...], preferred_element_type=f32)
    y = mixed * lax.rsqrt(jnp.mean(mixed * mixed, axis=-1, keepdims=True) + EPS) * g1_ref[...]
    o_ref[...] = x + gt_ref[0] * y


def _merge(x2, g0, sc, sh, gt, g1, branches, wmg, wbr, wout, rows_per_group, tm):
    M = x2.shape[0]
    row = lambda w: pl.BlockSpec((tm, w), lambda i: (i, 0))
    mspec = _mod_spec(sc, rows_per_group, tm)
    return pl.pallas_call(
        _merge_kernel,
        grid=(M // tm,),
        in_specs=[row(D_MODEL), _resident((1, D_MODEL)), mspec, mspec, mspec, _resident((1, D_MODEL))]
                 + [row(BRANCH_W)] * N_BRANCH
                 + [_resident(wmg.shape), _resident(wbr.shape), _resident(wout.shape)],
        out_specs=row(D_MODEL),
        out_shape=jax.ShapeDtypeStruct((M, D_MODEL), jnp.float32),
        scratch_shapes=[pltpu.VMEM(wmg.shape, jnp.bfloat16)],
        compiler_params=pltpu.CompilerParams(dimension_semantics=("arbitrary",), vmem_limit_bytes=VMEM_LIMIT),
        name="merge",
    )(x2, g0, sc, sh, gt, g1, *branches, wmg, wbr, wout)


def _ffn_kernel(x_ref, g2_ref, sc_ref, sh_ref, gt_ref, g3_ref, buf_ref, wup_ref, dw_ref, wdn_ref,
                o_ref, st_ref, gp_ref):
    f32, bf16 = jnp.float32, jnp.bfloat16
    tm = x_ref.shape[1]
    HALO = SUBLANES

    @pl.when(pl.program_id(1) == 0)
    def _():
        gp_ref[0:HALO, :] = buf_ref[0]

    x = x_ref[0]
    h = _modulated_norm(x, g2_ref[...], sc_ref[0], sh_ref[0]).astype(bf16)
    gp_ref[HALO:HALO + tm, :] = jnp.dot(h, wup_ref[:, 0:D_FF], preferred_element_type=f32)
    val = jnp.dot(h, wup_ref[:, D_FF:2 * D_FF], preferred_element_type=f32)
    gconv = (dw_ref[2:3, :] * gp_ref[HALO:HALO + tm, :] + dw_ref[1:2, :] * gp_ref[HALO - 1:HALO - 1 + tm, :]
             + dw_ref[0:1, :] * gp_ref[HALO - 2:HALO - 2 + tm, :])
    a = (jax.nn.gelu(gconv) * val).astype(bf16)
    f = jnp.dot(a, wdn_ref[...], preferred_element_type=f32)
    y = f * lax.rsqrt(jnp.mean(f * f, axis=-1, keepdims=True) + EPS) * g3_ref[...]
    o_ref[0] = x + gt_ref[0] * y
    tail = gp_ref[tm:tm + HALO, :]
    gp_ref[0:HALO, :] = tail
    st_ref[0] = tail


def _ffn(x3, g2, sc, sh, gt, g3, buf, wup, dw, wdn, tm):
    B, T, _ = x3.shape
    buf8 = jnp.pad(buf, ((0, 0), (SUBLANES - (FFN_CONV - 1), 0), (0, 0)))
    mspec = pl.BlockSpec((1, 1, D_MODEL), lambda b, i: (b, 0, 0))
    y, st = pl.pallas_call(
        _ffn_kernel,
        grid=(B, T // tm),
        in_specs=[pl.BlockSpec((1, tm, D_MODEL), lambda b, i: (b, i, 0)), _resident((1, D_MODEL)), mspec, mspec, mspec,
                  _resident((1, D_MODEL)), pl.BlockSpec((1, SUBLANES, D_FF), lambda b, i: (b, 0, 0)),
                  _resident(wup.shape), _resident(dw.shape), _resident(wdn.shape)],
        out_specs=[pl.BlockSpec((1, tm, D_MODEL), lambda b, i: (b, i, 0)),
                   pl.BlockSpec((1, SUBLANES, D_FF), lambda b, i: (b, 0, 0))],
        out_shape=[jax.ShapeDtypeStruct((B, T, D_MODEL), jnp.float32),
                   jax.ShapeDtypeStruct((B, SUBLANES, D_FF), jnp.float32)],
        scratch_shapes=[pltpu.VMEM((tm + SUBLANES, D_FF), jnp.float32)],
        compiler_params=pltpu.CompilerParams(dimension_semantics=("arbitrary", "arbitrary"),
                                             vmem_limit_bytes=VMEM_LIMIT),
        name="conv_ffn",
    )(x3, g2, sc, sh, gt, g3, buf8, wup, dw, wdn)
    return y, st[:, SUBLANES - (FFN_CONV - 1):]


def _ffn_rows_kernel(x_ref, g2_ref, sc_ref, sh_ref, gt_ref, g3_ref, p1_ref, p2_ref, wup_ref, dw_ref, wdn_ref,
                     o_ref, gpre_ref, gp_ref, *, seg):
    f32, bf16 = jnp.float32, jnp.bfloat16
    M = x_ref.shape[0]
    HALO = SUBLANES
    x = x_ref[...]
    h = _modulated_norm(x, g2_ref[...], sc_ref[0], sh_ref[0]).astype(bf16)
    gpre = jnp.dot(h, wup_ref[:, 0:D_FF], preferred_element_type=f32)
    val = jnp.dot(h, wup_ref[:, D_FF:2 * D_FF], preferred_element_type=f32)
    gp_ref[0:HALO, :] = jnp.zeros((HALO, D_FF), f32)
    gp_ref[HALO:HALO + M, :] = gpre
    t = lax.broadcasted_iota(jnp.int32, (M, 1), 0) % seg
    prev1 = jnp.where(t >= 1, gp_ref[HALO - 1:HALO - 1 + M, :], p1_ref[...])
    prev2 = jnp.where(t >= 2, gp_ref[HALO - 2:HALO - 2 + M, :], p2_ref[...])
    gconv = dw_ref[2:3, :] * gpre + dw_ref[1:2, :] * prev1 + dw_ref[0:1, :] * prev2
    a = (jax.nn.gelu(gconv) * val).astype(bf16)
    f = jnp.dot(a, wdn_ref[...], preferred_element_type=f32)
    y = f * lax.rsqrt(jnp.mean(f * f, axis=-1, keepdims=True) + EPS) * g3_ref[...]
    o_ref[...] = x + gt_ref[0] * y
    gpre_ref[...] = gpre


def _ffn_rows(x3, g2, sc, sh, gt, g3, buf, wup, dw, wdn):
    B, T, _ = x3.shape
    M = B * T
    assert FFN_CONV == 3 and T >= FFN_CONV - 1
    zeros = jnp.zeros((B, T, D_FF), jnp.float32)
    p1 = zeros.at[:, 0].set(buf[:, 1]).reshape(M, D_FF)
    p2 = zeros.at[:, 0].set(buf[:, 0]).at[:, 1].set(buf[:, 1]).reshape(M, D_FF)
    full = lambda shape: pl.BlockSpec(shape, lambda i: (0,) * len(shape))
    y, gpre = pl.pallas_call(
        functools.partial(_ffn_rows_kernel, seg=T),
        grid=(1,),
        in_specs=[full((M, D_MODEL)), full((1, D_MODEL)), full((1, M, D_MODEL)), full((1, M, D_MODEL)),
                  full((1, M, D_MODEL)), full((1, D_MODEL)), full((M, D_FF)), full((M, D_FF)),
                  _resident(wup.shape), _resident(dw.shape), _resident(wdn.shape)],
        out_specs=[full((M, D_MODEL)), full((M, D_FF))],
        out_shape=[jax.ShapeDtypeStruct((M, D_MODEL), jnp.float32), jax.ShapeDtypeStruct((M, D_FF), jnp.float32)],
        scratch_shapes=[pltpu.VMEM((M + SUBLANES, D_FF), jnp.float32)],
        compiler_params=pltpu.CompilerParams(dimension_semantics=("arbitrary",), vmem_limit_bytes=VMEM_LIMIT),
        name="conv_ffn_rows",
    )(x3.reshape(M, D_MODEL), g2, sc, sh, gt, g3, p1, p2, wup, dw, wdn)
    return y.reshape(B, T, D_MODEL), gpre.reshape(B, T, D_FF)[:, T - (FFN_CONV - 1):]


CONF_HALO = 32
CONF_TM = 512


def _conformer_kernel(u_ref, buf_ref, dw_ref, dwb_ref, lng_ref, lnb_ref, o_ref, st_ref, xp_ref):
    tm = u_ref.shape[1]
    first = CONF_HALO - (CONV_WIDTH - 1)

    @pl.when(pl.program_id(1) == 0)
    def _():
        xp_ref[0:CONF_HALO, :] = buf_ref[0]

    u = u_ref[0]
    xp_ref[CONF_HALO:CONF_HALO + tm, :] = u[:, :CONV_CH] * jax.nn.sigmoid(u[:, CONV_CH:])
    acc = jnp.zeros((tm, CONV_CH), jnp.float32)
    for k in range(CONV_WIDTH):
        acc = acc + dw_ref[k:k + 1, :] * xp_ref[first + k:first + k + tm, :]
    y = acc + dwb_ref[...]
    mu = jnp.mean(y, axis=-1, keepdims=True)
    var = jnp.mean(jnp.square(y - mu), axis=-1, keepdims=True)
    yn = (y - mu) * lax.rsqrt(var + EPS) * lng_ref[...] + lnb_ref[...]
    o_ref[0] = yn * jax.nn.sigmoid(yn)
    tail = xp_ref[tm:tm + CONF_HALO, :]
    xp_ref[0:CONF_HALO, :] = tail
    st_ref[0] = tail


def _conformer(P, col_blk, buf, dw, dw_b, ln_g, ln_b, tm):
    B, T, _ = P.shape
    bufp = jnp.pad(buf, ((0, 0), (CONF_HALO - (CONV_WIDTH - 1), 0), (0, 0)))
    vec = lambda: _resident((1, CONV_CH))
    o, st = pl.pallas_call(
        _conformer_kernel,
        grid=(B, T // tm),
        in_specs=[pl.BlockSpec((1, tm, 2 * CONV_CH), lambda b, i: (b, i, col_blk)),
                  pl.BlockSpec((1, CONF_HALO, CONV_CH), lambda b, i: (b, 0, 0)),
                  _resident((CONV_WIDTH, CONV_CH)), vec(), vec(), vec()],
        out_specs=[pl.BlockSpec((1, tm, CONV_CH), lambda b, i: (b, i, 0)),
                   pl.BlockSpec((1, CONF_HALO, CONV_CH), lambda b, i: (b, 0, 0))],
        out_shape=[jax.ShapeDtypeStruct((B, T, CONV_CH), jnp.float32),
                   jax.ShapeDtypeStruct((B, CONF_HALO, CONV_CH), jnp.float32)],
        scratch_shapes=[pltpu.VMEM((tm + CONF_HALO, CONV_CH), jnp.float32)],
        compiler_params=pltpu.CompilerParams(dimension_semantics=("arbitrary", "arbitrary"),
                                             vmem_limit_bytes=VMEM_LIMIT),
        name="conformer",
    )(P, bufp, dw, dw_b[None], ln_g[None], ln_b[None])
    return o, st[:, CONF_HALO - (CONV_WIDTH - 1):]


RET_LOG_GAMMA = tuple(math.log1p(-2.0 ** (-5 - h)) for h in range(RET_HEADS))
RET_TM = 128


def _retention_kernel(q_ref, k_ref, v_ref, g_ref, cos_ref, sin_ref, s0_ref, gn_ref, o_ref, sfin_ref,
                      s_ref, dec_ref, *, n_valid):
    f32, bf16 = jnp.float32, jnp.bfloat16
    C = q_ref.shape[1]
    W = RET_W
    half = HEAD_DIM // 2

    @pl.when(pl.program_id(1) == 0)
    def _():
        s_ref[...] = s0_ref[0]

    @pl.when((pl.program_id(0) == 0) & (pl.program_id(1) == 0))
    def _():
        ii = lax.broadcasted_iota(jnp.int32, (C, C), 0)
        jj = lax.broadcasted_iota(jnp.int32, (C, C), 1)
        d = (ii - jj).astype(f32)
        for h in range(RET_HEADS):
            dec_ref[h] = jnp.where((ii >= jj) & (jj < n_valid), jnp.exp(jnp.maximum(d, 0.0) * RET_LOG_GAMMA[h]), 0.0)

    lane = lax.broadcasted_iota(jnp.int32, (C, W), 1)
    low = lane % HEAD_DIM < half
    cos = cos_ref[...]
    sin = sin_ref[...]

    def rot(x):
        other = jnp.where(low, pltpu.roll(x, W - half, axis=1), pltpu.roll(x, half, axis=1))
        return x * cos + other * sin

    q = rot(q_ref[0])
    k = rot(k_ref[0]) * HEAD_DIM ** -0.5
    v = v_ref[0]
    row = lax.broadcasted_iota(jnp.int32, (C, HEAD_DIM), 0)
    rowf = row.astype(f32)
    valid = row < n_valid
    outs = []
    for h in range(RET_HEADS):
        cs = slice(h * HEAD_DIM, (h + 1) * HEAD_DIM)
        lg = RET_LOG_GAMMA[h]
        qh, kh, vh = q[:, cs], k[:, cs], v[:, cs].astype(bf16)
        s_old = s_ref[h]
        att = _dot_nt(qh.astype(bf16), kh.astype(bf16)) * dec_ref[h]
        o = (jnp.dot(att.astype(bf16), vh, preferred_element_type=f32)
             + jnp.dot((qh * jnp.exp((rowf + 1.0) * lg)).astype(bf16), s_old.astype(bf16), preferred_element_type=f32))
        kz = jnp.where(valid, kh * jnp.exp((n_valid - 1.0 - rowf) * lg), 0.0)
        s_ref[h] = s_old * math.exp(n_valid * lg) + jnp.dot(kz.T.astype(bf16), vh, preferred_element_type=f32)
        mu = jnp.mean(o, axis=-1, keepdims=True)
        var = jnp.mean(jnp.square(o - mu), axis=-1, keepdims=True)
        outs.append((o - mu) * lax.rsqrt(var + EPS))
    g = g_ref[0]
    o_ref[0] = jnp.concatenate(outs, axis=1) * gn_ref[...] * (g * jax.nn.sigmoid(g))

    @pl.when(pl.program_id(1) == pl.num_programs(1) - 1)
    def _():
        sfin_ref[0] = s_ref[...]


def _retention(P, col_blk0, S0, pos, gn, chunk, n_valid):
    B, T, _ = P.shape
    half = HEAD_DIM // 2
    inv = ROPE_BASE ** (-jnp.arange(half, dtype=jnp.float32) / half)
    ang = pos.astype(jnp.float32)[:, None] * inv[None, :]
    cos, sin = jnp.cos(ang), jnp.sin(ang)
    cosf = jnp.tile(jnp.concatenate([cos, cos], axis=1), (1, RET_HEADS))
    sinf = jnp.tile(jnp.concatenate([-sin, sin], axis=1), (1, RET_HEADS))
    col = lambda j: pl.BlockSpec((1, chunk, RET_W), lambda b, i: (b, i, col_blk0 + j))
    tab = pl.BlockSpec((chunk, RET_W), lambda b, i: (i, 0))
    state = pl.BlockSpec((1, RET_HEADS, HEAD_DIM, HEAD_DIM), lambda b, i: (b, 0, 0, 0))
    return pl.pallas_call(
        functools.partial(_retention_kernel, n_valid=n_valid),
        grid=(B, T // chunk),
        in_specs=[col(0), col(1), col(2), col(3), tab, tab, state, _resident((1, RET_W))],
        out_specs=[pl.BlockSpec((1, chunk, RET_W), lambda b, i: (b, i, 0)), state],
        out_shape=[jax.ShapeDtypeStruct((B, T, RET_W), jnp.float32),
                   jax.ShapeDtypeStruct((B, RET_HEADS, HEAD_DIM, HEAD_DIM), jnp.float32)],
        scratch_shapes=[pltpu.VMEM((RET_HEADS, HEAD_DIM, HEAD_DIM), jnp.float32),
                        pltpu.VMEM((RET_HEADS, chunk, chunk), jnp.float32)],
        compiler_params=pltpu.CompilerParams(dimension_semantics=("arbitrary", "arbitrary"),
                                             vmem_limit_bytes=VMEM_LIMIT),
        name="retention",
    )(P, P, P, P, cosf, sinf, S0, gn[None])


GDN_RT = 2 * GDN_CHUNK
GDN_HALO = SUBLANES
GDN_A_LANE = 3 * NSA_HEADS
GDN_B_LANE = GDN_A_LANE + GDN_HEADS


def _mm1(a, b):
    return jnp.dot(a.astype(jnp.bfloat16), b.astype(jnp.bfloat16), preferred_element_type=jnp.float32)


def _mm3(a, b):
    f32, bf16 = jnp.float32, jnp.bfloat16
    ah, bh = a.astype(bf16), b.astype(bf16)
    al, bl = (a - ah.astype(f32)).astype(bf16), (b - bh.astype(f32)).astype(bf16)
    d = lambda x, y: jnp.dot(x, y, preferred_element_type=f32)
    return d(ah, bh) + (d(ah, bl) + d(al, bh))


def _segment_cumsum(x, axis, seg):
    idx = lax.broadcasted_iota(jnp.int32, x.shape, axis) % seg
    s = 1
    while s < seg:
        x = x + jnp.where(idx >= s, pltpu.roll(x, s, axis=axis), 0.0)
        s *= 2
    return x


def _gdn_kernel(q_ref, k_ref, v_ref, z_ref, ab_ref, abt_ref, buf_ref, s0_ref, cw_ref, ng_ref, alane_ref, dlane_ref,
                acol_ref, dcol_ref, o_ref, sfin_ref, s_ref, xp_ref, *, n_valid):
    f32 = jnp.float32
    RT = q_ref.shape[1]
    C = GDN_CHUNK
    W = GDN_W
    first = GDN_HALO - (GDN_CONV - 1)

    @pl.when(pl.program_id(1) == 0)
    def _():
        s_ref[...] = s0_ref[0]
        xp_ref[0:GDN_HALO, :] = buf_ref[0]

    xp_ref[GDN_HALO:GDN_HALO + RT, 0:W] = q_ref[0]
    xp_ref[GDN_HALO:GDN_HALO + RT, W:2 * W] = k_ref[0]
    xp_ref[GDN_HALO:GDN_HALO + RT, 2 * W:3 * W] = v_ref[0]
    y = jnp.zeros((RT, 3 * W), f32)
    for t in range(GDN_CONV):
        y = y + cw_ref[t:t + 1, :] * xp_ref[first + t:first + t + RT, :]
    y = y * jax.nn.sigmoid(y)
    xp_ref[0:GDN_HALO, :] = xp_ref[RT:RT + GDN_HALO, :]

    ab = ab_ref[0]
    g_lanes = -jnp.exp(alane_ref[...]) * jax.nn.softplus(ab + dlane_ref[...])
    beta_lanes = jax.nn.sigmoid(ab)
    g_rows = -jnp.exp(acol_ref[...]) * jax.nn.softplus(abt_ref[0] + dcol_ref[...])
    if n_valid < C:
        g_lanes = jnp.where(lax.broadcasted_iota(jnp.int32, g_lanes.shape, 0) % C < n_valid, g_lanes, 0.0)
        beta_lanes = jnp.where(lax.broadcasted_iota(jnp.int32, g_lanes.shape, 0) % C < n_valid, beta_lanes, 0.0)
        g_rows = jnp.where(lax.broadcasted_iota(jnp.int32, g_rows.shape, 1) % C < n_valid, g_rows, 0.0)
    gc_lanes = _segment_cumsum(g_lanes, 0, C)
    gc_rows = _segment_cumsum(g_rows, 1, C)

    ii = lax.broadcasted_iota(jnp.int32, (C, C), 0)
    jj = lax.broadcasted_iota(jnp.int32, (C, C), 1)
    tri = ii >= jj
    strict = ii > jj
    eye = (ii == jj).astype(f32)
    z = z_ref[0]
    bf16 = jnp.bfloat16
    nch = RT // C
    pairs = [(c, h) for c in range(nch) for h in range(GDN_HEADS)]
    qs, ks, ms, atts, rhss, gcs = {}, {}, {}, {}, {}, {}
    for c, h in pairs:
        rs = slice(c * C, (c + 1) * C)
        qh = y[rs, h * HEAD_DIM:(h + 1) * HEAD_DIM]
        kh = y[rs, W + h * HEAD_DIM:W + (h + 1) * HEAD_DIM]
        vh = y[rs, 2 * W + h * HEAD_DIM:2 * W + (h + 1) * HEAD_DIM]
        qh = qh * lax.rsqrt(jnp.sum(qh * qh, axis=-1, keepdims=True) + EPS) * HEAD_DIM ** -0.5
        kh = kh * lax.rsqrt(jnp.sum(kh * kh, axis=-1, keepdims=True) + EPS)
        gc_col = gc_lanes[rs, GDN_A_LANE + h:GDN_A_LANE + h + 1]
        beta = beta_lanes[rs, GDN_B_LANE + h:GDN_B_LANE + h + 1]
        gc_row = gc_rows[h:h + 1, c * C:(c + 1) * C]
        e_col = jnp.exp(gc_col)
        lm = jnp.exp(jnp.where(tri, gc_col - gc_row, NEG_INF))
        kb = kh * beta
        ms[c, h] = jnp.where(strict, _dot_nt(kb.astype(bf16), kh.astype(bf16)) * lm, 0.0)
        atts[c, h] = _dot_nt(qh.astype(bf16), kh.astype(bf16)) * lm
        rhss[c, h] = jnp.concatenate([vh * beta, kb * e_col], axis=1)
        qs[c, h], ks[c, h], gcs[c, h] = qh * e_col, kh, gc_col
    pw = {p: -ms[p] for p in pairs}
    xs = {p: eye + pw[p] for p in pairs}
    for _ in range(int(math.log2(C)) - 1):
        pw = {p: _mm1(pw[p], pw[p]) for p in pairs}
        xs = {p: xs[p] + _mm1(xs[p], pw[p]) for p in pairs}
    res = {p: eye - (xs[p] + _mm3(ms[p], xs[p])) for p in pairs}
    xs = {p: xs[p] + _mm1(xs[p], res[p]) for p in pairs}
    sols = {p: _mm3(xs[p], rhss[p]) for p in pairs}
    state = [s_ref[h] for h in range(GDN_HEADS)]
    outs = {}
    for c in range(nch):
        heads = range(GDN_HEADS)
        g_last = [gcs[c, h][C - 1:C, :] for h in heads]
        vn = [sols[c, h][:, :HEAD_DIM] - _mm1(sols[c, h][:, HEAD_DIM:], state[h]) for h in heads]
        o_in = [_mm1(qs[c, h], state[h]) for h in heads]
        kd_t = [(ks[c, h] * jnp.exp(g_last[h] - gcs[c, h])).T for h in heads]
        for h in heads:
            outs[c, h] = o_in[h] + _mm1(atts[c, h], vn[h])
        state = [state[h] * jnp.exp(g_last[h]) + _mm1(kd_t[h], vn[h]) for h in heads]
    for h in range(GDN_HEADS):
        s_ref[h] = state[h]
    row_outs = []
    for c in range(nch):
        head_outs = []
        for h in range(GDN_HEADS):
            o = outs[c, h]
            o = o * lax.rsqrt(jnp.mean(o * o, axis=-1, keepdims=True) + EPS) * ng_ref[...]
            zh = z[c * C:(c + 1) * C, h * HEAD_DIM:(h + 1) * HEAD_DIM]
            head_outs.append(o * (zh * jax.nn.sigmoid(zh)))
        row_outs.append(jnp.concatenate(head_outs, axis=1))
    o_ref[0] = jnp.concatenate(row_outs, axis=0)

    @pl.when(pl.program_id(1) == pl.num_programs(1) - 1)
    def _():
        sfin_ref[0] = s_ref[...]


def _gdn(P, blk_q, blk_misc, ga, conv_buf, S0, conv_w, A_log, dt_bias, norm_g, rt, n_valid):
    B, T, _ = P.shape
    f32 = jnp.float32
    abt = jnp.pad(jnp.transpose(ga, (0, 2, 1)), ((0, 0), (0, SUBLANES - GDN_HEADS), (0, 0)))
    buf8 = jnp.pad(conv_buf, ((0, 0), (GDN_HALO - (GDN_CONV - 1), 0), (0, 0)))
    lane_vec = lambda v: jnp.zeros((1, LANES), f32).at[0, GDN_A_LANE:GDN_A_LANE + GDN_HEADS].set(v)
    col_vec = lambda v: jnp.zeros((SUBLANES, 1), f32).at[0:GDN_HEADS, 0].set(v)
    col = lambda j: pl.BlockSpec((1, rt, GDN_W), lambda b, i: (b, i, blk_q + j))
    state = pl.BlockSpec((1, GDN_HEADS, HEAD_DIM, HEAD_DIM), lambda b, i: (b, 0, 0, 0))
    return pl.pallas_call(
        functools.partial(_gdn_kernel, n_valid=n_valid),
        grid=(B, T // rt),
        in_specs=[col(0), col(1), col(2), col(3),
                  pl.BlockSpec((1, rt, LANES), lambda b, i: (b, i, blk_misc)),
                  pl.BlockSpec((1, SUBLANES, rt), lambda b, i: (b, 0, i)),
                  pl.BlockSpec((1, GDN_HALO, 3 * GDN_W), lambda b, i: (b, 0, 0)), state,
                  _resident((GDN_CONV, 3 * GDN_W)), _resident((1, HEAD_DIM)),
                  _resident((1, LANES)), _resident((1, LANES)), _resident((SUBLANES, 1)), _resident((SUBLANES, 1))],
        out_specs=[pl.BlockSpec((1, rt, GDN_W), lambda b, i: (b, i, 0)), state],
        out_shape=[jax.ShapeDtypeStruct((B, T, GDN_W), f32),
                   jax.ShapeDtypeStruct((B, GDN_HEADS, HEAD_DIM, HEAD_DIM), f32)],
        scratch_shapes=[pltpu.VMEM((GDN_HEADS, HEAD_DIM, HEAD_DIM), f32),
                        pltpu.VMEM((rt + GDN_HALO, 3 * GDN_W), f32)],
        compiler_params=pltpu.CompilerParams(dimension_semantics=("arbitrary", "arbitrary"),
                                             vmem_limit_bytes=VMEM_LIMIT),
        name="gated_deltanet",
    )(P, P, P, P, P, abt, buf8, S0, conv_w, norm_g[None], lane_vec(A_log), lane_vec(dt_bias),
      col_vec(A_log), col_vec(dt_bias))


def _prep_layer(p):
    w_main, w_mg = _prep_w_in(p["w_in"])
    q = dict(p)
    q.update(w_main=w_main, w_mg=w_mg, w_branch_b=p["w_branch"].astype(jnp.bfloat16),
             w_out_b=p["w_out"].astype(jnp.bfloat16), ffn_up_b=p["ffn_up"].astype(jnp.bfloat16),
             ffn_down_b=p["ffn_down"].astype(jnp.bfloat16))
    return q


def trunk_layer(x, c, pos0, nsa_past, win_buf, conv_buf, ret_s, gdn_buf, gdn_s, ffn_buf, p, rel_bias):
    B, T, _ = x.shape
    M = B * T
    mod = _pmm(jax.nn.silu(c), p["w_ada"]) + p["b_ada"]
    per_row = T < MERGE_TM
    if per_row:
        sh1, sc1, gt1, sh2, sc2, gt2 = [jnp.repeat(m, T, axis=0)[None] for m in jnp.split(mod, 6, axis=-1)]
    else:
        sh1, sc1, gt1, sh2, sc2, gt2 = [m[:, None, :] for m in jnp.split(mod, 6, axis=-1)]
    norms = p["norms"][:, None, :]
    x2 = x.reshape(M, D_MODEL)
    P = _proj(x2, norms[0], sc1, sh1, p["w_main"], T, M if per_row else PROJ_TM).reshape(B, T, PROJ_COLS)
    nkv = P[:, :, PROJ_KV:PROJ_KV + 6 * KVW]
    ngt = P[:, :, PROJ_MISC:PROJ_MISC + 3 * NSA_HEADS]
    ga = P[:, :, PROJ_MISC + 3 * NSA_HEADS:PROJ_MISC + 3 * NSA_HEADS + GDN_HEADS]
    gb = P[:, :, PROJ_MISC + 3 * NSA_HEADS + GDN_HEADS:PROJ_MISC + 3 * NSA_HEADS + 2 * GDN_HEADS]
    kw = nkv[:, :, 4 * KVW:].reshape(B, T, 2, NSA_KV_HEADS, HEAD_DIM)
    if nsa_past is None:
        o_nsa = _nsa_prompt(P, PROJ_Q // (NSA_HEADS * HEAD_DIM), P, PROJ_KV // (6 * KVW), ngt,
                            p["cmp_pool"], p["cmp_pe"], rel_bias)
        new_win = kw[:, T - min(NSA_WINDOW, T):]
    else:
        cache, layer, page_table = nsa_past
        o_nsa = _nsa_sample(P[:, :, PROJ_Q:PROJ_Q + NSA_HEADS * HEAD_DIM], nkv, ngt, cache, layer, page_table,
                            win_buf, p["cmp_pool"], p["cmp_pe"], rel_bias)
        real = jnp.concatenate([win_buf, kw], axis=1)
        new_win = real[:, real.shape[1] - min(NSA_WINDOW, real.shape[1]):]
    o_conv, new_conv = _conformer(P, PROJ_UCV // (2 * CONV_CH), conv_buf, p["conv_dw"], p["conv_dw_b"],
                                  p["conv_ln_g"], p["conv_ln_b"], T if per_row else CONF_TM)
    if per_row:
        Pr = jnp.pad(P[:, :, PROJ_RET:PROJ_RET + 4 * RET_W], ((0, 0), (0, RET_CHUNK - T), (0, 0)))
        o_ret, new_ret = _retention(Pr, 0, ret_s, pos0 + jnp.arange(RET_CHUNK, dtype=jnp.int32), p["ret_gn"],
                                    RET_CHUNK, T)
        o_ret = o_ret[:, :T]
    else:
        o_ret, new_ret = _retention(P, PROJ_RET // RET_W, ret_s, pos0 + jnp.arange(T, dtype=jnp.int32), p["ret_gn"],
                                    RET_TM, RET_TM)
    gdn_w = (p["gdn_conv_w"], p["gdn_A_log"], p["gdn_dt_bias"], p["gdn_norm"])
    if per_row:
        pad_rows = ((0, 0), (0, GDN_CHUNK - T), (0, 0))
        Pg = jnp.pad(P[:, :, PROJ_GQKV:PROJ_COLS], pad_rows)
        o_gdn, new_gdn = _gdn(Pg, 0, (PROJ_MISC - PROJ_GQKV) // LANES, jnp.pad(ga, pad_rows), gdn_buf, gdn_s, *gdn_w,
                              GDN_CHUNK, T)
        o_gdn = o_gdn[:, :T]
    else:
        o_gdn, new_gdn = _gdn(P, PROJ_GQKV // GDN_W, PROJ_MISC // LANES, ga, gdn_buf, gdn_s, *gdn_w,
                              GDN_RT, GDN_CHUNK)
    new_gdn_buf = jnp.concatenate([gdn_buf, P[:, :, PROJ_GQKV:PROJ_GQKV + 3 * GDN_W]], axis=1)[:, T:]
    branches = [o.reshape(M, BRANCH_W) for o in (o_nsa, o_conv, o_ret, o_gdn)]
    x1 = _merge(x2, norms[0], sc1, sh1, gt1, norms[1], branches, p["w_mg"], p["w_branch_b"], p["w_out_b"],
                T, M if per_row else MERGE_TM).reshape(B, T, D_MODEL)
    if per_row:
        x_out, new_ffn = _ffn_rows(x1, norms[2], sc2, sh2, gt2, norms[3], ffn_buf, p["ffn_up_b"], p["ffn_dw"],
                                   p["ffn_down_b"])
    else:
        x_out, new_ffn = _ffn(x1, norms[2], sc2, sh2, gt2, norms[3], ffn_buf, p["ffn_up_b"], p["ffn_dw"],
                              p["ffn_down_b"], FFN_TM)
    kv_rows = nkv[:, :, :4 * KVW].reshape(B, T, 4, NSA_KV_HEADS, HEAD_DIM)
    return x_out, (kv_rows, new_win, new_conv, new_ret, new_gdn_buf, new_gdn, new_ffn)


def kernel(x_prompt, x_sample, cache_nsa_kv, cache_nsa_win, state_conv, state_ret, state_gdn_conv, state_gdn,
           state_ffn_conv, page_table, c_prompt, c_sample, w_ada, b_ada, norms, w_in, cmp_pool, cmp_pe, rel_bias,
           conv_dw, conv_dw_b, conv_ln_g, conv_ln_b, ret_gn, gdn_conv_w, gdn_A_log, gdn_dt_bias, gdn_norm,
           w_branch, w_out, ffn_up, ffn_dw, ffn_down):
    B = x_prompt.shape[0]
    Bd = x_sample.shape[0]
    past = page_table.shape[1] * PAGE_SIZE
    layer_w = {"w_ada": w_ada, "b_ada": b_ada, "norms": norms, "w_in": w_in, "cmp_pool": cmp_pool,
               "cmp_pe": cmp_pe, "conv_dw": conv_dw, "conv_dw_b": conv_dw_b, "conv_ln_g": conv_ln_g,
               "conv_ln_b": conv_ln_b, "ret_gn": ret_gn, "gdn_conv_w": gdn_conv_w, "gdn_A_log": gdn_A_log,
               "gdn_dt_bias": gdn_dt_bias, "gdn_norm": gdn_norm, "w_branch": w_branch, "w_out": w_out,
               "ffn_up": ffn_up, "ffn_dw": ffn_dw, "ffn_down": ffn_down}
    yp, ys = x_prompt, x_sample
    st_p, st_s = [], []
    for l in range(DEPTH):
        p = _prep_layer({name: w[l] for name, w in layer_w.items()})
        yp, sp = trunk_layer(
            yp, c_prompt, 0, None, None,
            jnp.zeros((B, CONV_WIDTH - 1, CONV_CH), x_prompt.dtype),
            jnp.zeros((B, RET_HEADS, HEAD_DIM, HEAD_DIM), jnp.float32),
            jnp.zeros((B, GDN_CONV - 1, 3 * GDN_W), x_prompt.dtype),
            jnp.zeros((B, GDN_HEADS, HEAD_DIM, HEAD_DIM), jnp.float32),
            jnp.zeros((B, FFN_CONV - 1, D_FF), x_prompt.dtype),
            p, rel_bias)
        ys, ss = trunk_layer(
            ys, c_sample, past, (cache_nsa_kv, l, page_table), cache_nsa_win[l], state_conv[l], state_ret[l],
            state_gdn_conv[l], state_gdn[l], state_ffn_conv[l], p, rel_bias)
        st_p.append(sp)
        st_s.append(ss)

    def stack(outs, i, axis):
        return jnp.stack([o[i] for o in outs], axis=axis)

    kv_p, kv_s = stack(st_p, 0, 1), stack(st_s, 0, 1)
    win_p, win_s = stack(st_p, 1, 0), stack(st_s, 1, 0)
    conv_p, conv_s = stack(st_p, 2, 0), stack(st_s, 2, 0)
    ret_p, ret_s = stack(st_p, 3, 0), stack(st_s, 3, 0)
    gdnc_p, gdnc_s = stack(st_p, 4, 0), stack(st_s, 4, 0)
    gdn_p, gdn_s = stack(st_p, 5, 0), stack(st_s, 5, 0)
    ffn_p, ffn_s = stack(st_p, 6, 0), stack(st_s, 6, 0)
    return (yp, ys, kv_p, kv_s, win_p, win_s, conv_p, conv_s, ret_p, ret_s, gdnc_p, gdnc_s, gdn_p, gdn_s, ffn_p, ffn_s)
```

```python
import functools
import math

import jax
import jax.numpy as jnp
import numpy as np
from jax import lax
from jax.experimental import pallas as pl
from jax.experimental.pallas import tpu as pltpu

D_MODEL = 1024
DEPTH = 2
PAGE_SIZE = 128

HEAD_DIM = 64
NSA_HEADS = 4
NSA_KV_HEADS = 2
NSA_GROUP = NSA_HEADS // NSA_KV_HEADS
NSA_BLOCK = 64
NSA_TOPK = 16
NSA_WINDOW = 512
NUM_BUCKETS = 32
MAX_DISTANCE = 128
CONV_CH = D_MODEL // 4
CONV_WIDTH = 31
RET_HEADS = 4
RET_W = RET_HEADS * HEAD_DIM
RET_CHUNK = 64
ROPE_BASE = 10000.0
GDN_HEADS = 4
GDN_W = GDN_HEADS * HEAD_DIM
GDN_CONV = 4
GDN_CHUNK = 64
D_FF = 2816
FFN_CONV = 3
N_BRANCH = 4
BRANCH_W = NSA_HEADS * HEAD_DIM
EPS = 1e-6
NEG_INF = -1e30
IN_SPLITS = (NSA_HEADS * HEAD_DIM, 6 * NSA_KV_HEADS * HEAD_DIM, 3 * NSA_HEADS, 2 * CONV_CH, 4 * RET_W,
             4 * GDN_W + 2 * GDN_HEADS, N_BRANCH * D_MODEL)
IN_COLS = sum(IN_SPLITS)

LANES = 128
SUBLANES = 8
VMEM_LIMIT = 56 * 1024 * 1024


def _round_up(a, m):
    return -(-a // m) * m


ADA_TN = 2048


def _adaln_kernel(c_ref, w_ref, b_ref, o_ref):
    c = c_ref[...]
    act = (c * jax.nn.sigmoid(c)).astype(jnp.bfloat16)
    o_ref[0] = jnp.dot(act, w_ref[0].astype(jnp.bfloat16), preferred_element_type=jnp.float32) + b_ref[0]


def _adaln(c, w_ada, b_ada):
    R = c.shape[0]
    L, _, N = w_ada.shape
    Rp = _round_up(R, SUBLANES)
    cp = jnp.pad(c, ((0, Rp - R), (0, 0)))
    out = pl.pallas_call(
        _adaln_kernel,
        grid=(L, N // ADA_TN),
        in_specs=[pl.BlockSpec((Rp, D_MODEL), lambda l, j: (0, 0)),
                  pl.BlockSpec((1, D_MODEL, ADA_TN), lambda l, j: (l, 0, j)),
                  pl.BlockSpec((1, 1, ADA_TN), lambda l, j: (l, 0, j))],
        out_specs=pl.BlockSpec((1, Rp, ADA_TN), lambda l, j: (l, 0, j)),
        out_shape=jax.ShapeDtypeStruct((L, Rp, N), jnp.float32),
        compiler_params=pltpu.CompilerParams(dimension_semantics=("arbitrary", "arbitrary"),
                                             vmem_limit_bytes=VMEM_LIMIT),
        name="adaln",
    )(cp, w_ada, b_ada[:, None, :])
    return out[:, :R]


NSA_TQ = 128
BLOCKS_PER_TILE = NSA_TQ // NSA_BLOCK


def _t5_thresholds():
    n = np.arange(0, 2 * MAX_DISTANCE)
    exact = NUM_BUCKETS // 2
    large = exact + (np.log(np.maximum(n, 1).astype(np.float32) / np.float32(exact))
                     / np.float32(math.log(MAX_DISTANCE / exact)) * (NUM_BUCKETS - exact)).astype(np.int32)
    bucket = np.where(n < exact, n, np.minimum(large, NUM_BUCKETS - 1))
    return tuple(int(np.argmax(bucket >= k)) for k in range(1, NUM_BUCKETS))


_T5_THR = _t5_thresholds()


def _bias_from_dist(dist, rb_ref, h):
    v = jnp.full(dist.shape, rb_ref[NUM_BUCKETS - 1, h], jnp.float32)
    for k in range(NUM_BUCKETS - 2, -1, -1):
        v = jnp.where(dist < _T5_THR[k], rb_ref[k, h], v)
    return v


def _dot_nt(a, b):
    return lax.dot_general(a, b, (((1,), (1,)), ((), ())), preferred_element_type=jnp.float32)


def _flash_tile(carry, k_t, vt_t, qs, bias, keep):
    m, l, acc = carry
    s = _dot_nt(k_t, qs) + bias
    if keep is not None:
        s = jnp.where(keep, s, NEG_INF)
    m_new = jnp.maximum(m, jnp.max(s, axis=0, keepdims=True))
    p = jnp.exp(s - m_new)
    alpha = jnp.exp(m - m_new)
    l = alpha * l + jnp.sum(p, axis=0, keepdims=True)
    acc = alpha * acc + jnp.dot(vt_t, p.astype(jnp.bfloat16), preferred_element_type=jnp.float32)
    return m_new, l, acc


FAR_GROUP = 8
NEAR_ROWS = 2 * NSA_TQ
WIN_ROWS = NSA_WINDOW + NSA_TQ


def _nsa_prompt_kernel(rb_ref, q_ref, kv_ref, gt_ref, poolt_ref, pe_ref, o_ref,
                       kc_ref, vc_ref, ksel_ref, vselt_ref, kwin_ref, vwint_ref, near_ref, wtbl_ref, score_ref,
                       sel_ref):
    b = pl.program_id(0)
    qi = pl.program_id(1)
    T = kv_ref.shape[1]
    nb = T // NSA_BLOCK
    topk = min(NSA_TOPK, nb)
    TQ = NSA_TQ
    CH = 512
    f32, bf16 = jnp.float32, jnp.bfloat16
    kvs = range(NSA_KV_HEADS)

    @pl.when((b == 0) & (qi == 0))
    def _tables():
        for kvg in kvs:
            ksel_ref[kvg, 0:TQ, :] = jnp.zeros((TQ, HEAD_DIM), bf16)
            vselt_ref[kvg, :, 0:TQ] = jnp.zeros((HEAD_DIM, TQ), bf16)
            kwin_ref[kvg, 0:NSA_WINDOW, :] = jnp.zeros((NSA_WINDOW, HEAD_DIM), bf16)
            vwint_ref[kvg, :, 0:NSA_WINDOW] = jnp.zeros((HEAD_DIM, NSA_WINDOW), bf16)
        d_near = (lax.broadcasted_iota(jnp.int32, (NEAR_ROWS, TQ), 1) + TQ
                  - lax.broadcasted_iota(jnp.int32, (NEAR_ROWS, TQ), 0))
        d_win = (lax.broadcasted_iota(jnp.int32, (WIN_ROWS, TQ), 1) + NSA_WINDOW
                 - lax.broadcasted_iota(jnp.int32, (WIN_ROWS, TQ), 0))
        for h in range(NSA_HEADS):
            kvg, g = divmod(h, NSA_GROUP)
            lanes = slice(g * TQ, (g + 1) * TQ)
            near_ref[kvg, :, lanes] = jnp.where(d_near >= 0, _bias_from_dist(d_near, rb_ref, h), NEG_INF)
            wtbl_ref[kvg, :, lanes] = jnp.where((d_win >= 0) & (d_win < NSA_WINDOW),
                                                _bias_from_dist(d_win, rb_ref, h), NEG_INF)

    @pl.when(qi == 0)
    def _prologue():
        def chunk(i, carry):
            r = pl.multiple_of(i * CH, CH)
            rs = pl.multiple_of(i * CH + TQ, TQ)
            rw = pl.multiple_of(i * CH + NSA_WINDOW, TQ)
            rb8 = pl.multiple_of(i * (CH // NSA_BLOCK), CH // NSA_BLOCK)
            for kvg in kvs:
                def col(c):
                    lo = c * NSA_KV_HEADS * HEAD_DIM + kvg * HEAD_DIM
                    return kv_ref[0, pl.ds(r, CH), lo:lo + HEAD_DIM]
                for c, dst in ((0, kc_ref), (1, vc_ref)):
                    x = col(c).reshape(CH // NSA_BLOCK, NSA_BLOCK, HEAD_DIM) + pe_ref[c][None]
                    dst[kvg, pl.ds(rb8, CH // NSA_BLOCK), :] = jnp.sum(x * poolt_ref[:, c:c + 1][None], axis=1)
                ksel_ref[kvg, pl.ds(rs, CH), :] = col(2).astype(bf16)
                vselt_ref[kvg, :, pl.ds(rs, CH)] = col(3).T.astype(bf16)
                kwin_ref[kvg, pl.ds(rw, CH), :] = col(4).astype(bf16)
                vwint_ref[kvg, :, pl.ds(rw, CH)] = col(5).T.astype(bf16)
            return carry
        lax.fori_loop(0, T // CH, chunk, 0)

    q = q_ref[0]
    gates = jax.nn.sigmoid(gt_ref[0])
    n_io = lax.broadcasted_iota(jnp.int32, (nb, TQ), 0)
    t_io = lax.broadcasted_iota(jnp.int32, (nb, TQ), 1)
    lane2 = lax.broadcasted_iota(jnp.int32, (1, 2 * TQ), 1)
    dist_c = qi * TQ + t_io - (n_io * NSA_BLOCK + NSA_BLOCK - 1)
    vis_c = dist_c >= 0
    vis_c2 = jnp.concatenate([vis_c, vis_c], axis=1)
    cur = (qi * TQ + t_io) // NSA_BLOCK
    forced = (n_io == 0) | (n_io == cur) | (n_io == cur - 1)
    q0 = pl.multiple_of(qi * TQ, TQ)

    qs, far_row, oc, score = [], [], [], []
    for kvg in kvs:
        base = kvg * NSA_GROUP * HEAD_DIM
        qk = jnp.concatenate([q[:, base + g * HEAD_DIM: base + (g + 1) * HEAD_DIM] for g in range(NSA_GROUP)], axis=0)
        qs.append((qk * HEAD_DIM ** -0.5).astype(bf16))
        far_row.append(jnp.where(lane2 < TQ, rb_ref[NUM_BUCKETS - 1, kvg * NSA_GROUP],
                                 rb_ref[NUM_BUCKETS - 1, kvg * NSA_GROUP + 1]))
        sc = _dot_nt(kc_ref[kvg].astype(bf16), qs[kvg])
        bias_c = jnp.concatenate([_bias_from_dist(dist_c, rb_ref, kvg * NSA_GROUP + g) for g in range(NSA_GROUP)],
                                 axis=1)
        sc = jnp.where(vis_c2, sc + bias_c, NEG_INF)
        e = jnp.exp(sc - jnp.max(sc, axis=0, keepdims=True))
        p_c = e / jnp.sum(e, axis=0, keepdims=True) * vis_c2.astype(f32)
        oc.append(jnp.dot(vc_ref[kvg].T.astype(bf16), p_c.astype(bf16), preferred_element_type=f32))
        score.append(jnp.where(n_io <= cur, jnp.where(forced, 2.0, p_c[:, :TQ] + p_c[:, TQ:]), -1.0))
        score_ref[kvg] = score[kvg]

    def rank_body(mi, ranks):
        out = []
        for kvg in kvs:
            row = score_ref[kvg, pl.ds(mi, 1), :]
            beats = (row > score[kvg]) | ((row == score[kvg]) & (mi < n_io))
            out.append(ranks[kvg] + beats.astype(jnp.int32))
        return tuple(out)

    ranks = lax.fori_loop(0, BLOCKS_PER_TILE * (qi + 1), rank_body,
                          tuple(jnp.zeros((nb, TQ), jnp.int32) for _ in kvs))
    for kvg in kvs:
        sel_ref[kvg] = ((ranks[kvg] < topk) & (n_io <= cur)).astype(f32)

    def keep_rows(kvg, blk0, nblk, limit):
        rows = []
        for u in range(nblk):
            blk = blk0 + u
            ok = (blk >= 0) & (blk < limit)
            row = sel_ref[kvg, pl.ds(jnp.clip(blk, 0, nb - 1), 1), :]
            rows.append(jnp.broadcast_to(jnp.where(ok, row, 0.0), (NSA_BLOCK, TQ)))
        mm = jnp.concatenate(rows, axis=0)
        return jnp.concatenate([mm, mm], axis=1) > 0.5

    carries = []
    for kvg in kvs:
        k_n = ksel_ref[kvg, pl.ds(q0, NEAR_ROWS), :]
        vt_n = vselt_ref[kvg, :, pl.ds(q0, NEAR_ROWS)]
        keep = keep_rows(kvg, BLOCKS_PER_TILE * (qi - 1), 2 * BLOCKS_PER_TILE, nb)
        s = jnp.where(keep, _dot_nt(k_n, qs[kvg]) + near_ref[kvg], NEG_INF)
        m = jnp.max(s, axis=0, keepdims=True)
        p = jnp.exp(s - m)
        carries.append((m, jnp.sum(p, axis=0, keepdims=True),
                        jnp.dot(vt_n, p.astype(bf16), preferred_element_type=f32)))

    n_far = jnp.maximum(qi - 1, 0)
    rows_far = FAR_GROUP * TQ

    def far_body(i, cs):
        out = []
        for kvg in kvs:
            r = pl.multiple_of(TQ + i * rows_far, TQ)
            keep = keep_rows(kvg, i * FAR_GROUP * BLOCKS_PER_TILE, FAR_GROUP * BLOCKS_PER_TILE,
                             n_far * BLOCKS_PER_TILE)
            out.append(_flash_tile(cs[kvg], ksel_ref[kvg, pl.ds(r, rows_far), :],
                                   vselt_ref[kvg, :, pl.ds(r, rows_far)], qs[kvg], far_row[kvg], keep))
        return tuple(out)

    carries = lax.fori_loop(0, (n_far + FAR_GROUP - 1) // FAR_GROUP, far_body, tuple(carries))

    w_io = lax.broadcasted_iota(jnp.int32, (WIN_ROWS, 2 * TQ), 0)
    outs = []
    for kvg in kvs:
        m_s, l_s, acc_s = carries[kvg]
        o_s = acc_s / l_s
        k_w = kwin_ref[kvg, pl.ds(q0, WIN_ROWS), :]
        vt_w = vwint_ref[kvg, :, pl.ds(q0, WIN_ROWS)]
        s = jnp.where(w_io >= NSA_WINDOW - q0, _dot_nt(k_w, qs[kvg]) + wtbl_ref[kvg], NEG_INF)
        e = jnp.exp(s - jnp.max(s, axis=0, keepdims=True))
        o_w = (jnp.dot(vt_w, e.astype(bf16), preferred_element_type=f32) / jnp.sum(e, axis=0, keepdims=True))
        for g in range(NSA_GROUP):
            h = kvg * NSA_GROUP + g
            lanes = slice(g * TQ, (g + 1) * TQ)
            o = (gates[h:h + 1] * oc[kvg][:, lanes] + gates[NSA_HEADS + h:NSA_HEADS + h + 1] * o_s[:, lanes]
                 + gates[2 * NSA_HEADS + h:2 * NSA_HEADS + h + 1] * o_w[:, lanes])
            outs.append(o.T)
    o_ref[0] = jnp.concatenate(outs, axis=1)


def _nsa_prompt(q, q_blk, kv, kv_blk, gate_logits, cmp_pool, cmp_pe, rel_bias):
    B, T, _ = q.shape
    nb = T // NSA_BLOCK
    assert T % 512 == 0 and nb % SUBLANES == 0 and (T // NSA_TQ) % FAR_GROUP == 0
    TQ = NSA_TQ
    gt = jnp.transpose(gate_logits, (0, 2, 1))
    f32, bf16 = jnp.float32, jnp.bfloat16
    return pl.pallas_call(
        _nsa_prompt_kernel,
        grid=(B, T // TQ),
        in_specs=[
            pl.BlockSpec(memory_space=pltpu.SMEM),
            pl.BlockSpec((1, TQ, NSA_HEADS * HEAD_DIM), lambda b, i: (b, i, q_blk)),
            pl.BlockSpec((1, T, 6 * NSA_KV_HEADS * HEAD_DIM), lambda b, i: (b, 0, kv_blk)),
            pl.BlockSpec((1, 3 * NSA_HEADS, TQ), lambda b, i: (b, 0, i)),
            pl.BlockSpec((NSA_BLOCK, 2), lambda b, i: (0, 0)),
            pl.BlockSpec((2, NSA_BLOCK, HEAD_DIM), lambda b, i: (0, 0, 0)),
        ],
        out_specs=pl.BlockSpec((1, TQ, NSA_HEADS * HEAD_DIM), lambda b, i: (b, i, 0)),
        out_shape=jax.ShapeDtypeStruct((B, T, NSA_HEADS * HEAD_DIM), f32),
        scratch_shapes=[
            pltpu.VMEM((NSA_KV_HEADS, nb, HEAD_DIM), f32),
            pltpu.VMEM((NSA_KV_HEADS, nb, HEAD_DIM), f32),
            pltpu.VMEM((NSA_KV_HEADS, T + TQ, HEAD_DIM), bf16),
            pltpu.VMEM((NSA_KV_HEADS, HEAD_DIM, T + TQ), bf16),
            pltpu.VMEM((NSA_KV_HEADS, T + NSA_WINDOW, HEAD_DIM), bf16),
            pltpu.VMEM((NSA_KV_HEADS, HEAD_DIM, T + NSA_WINDOW), bf16),
            pltpu.VMEM((NSA_KV_HEADS, NEAR_ROWS, NSA_GROUP * TQ), f32),
            pltpu.VMEM((NSA_KV_HEADS, WIN_ROWS, NSA_GROUP * TQ), f32),
            pltpu.VMEM((NSA_KV_HEADS, nb, TQ), f32),
            pltpu.VMEM((NSA_KV_HEADS, nb, TQ), f32),
        ],
        compiler_params=pltpu.CompilerParams(dimension_semantics=("arbitrary", "arbitrary"),
                                             vmem_limit_bytes=VMEM_LIMIT),
        name="nsa_prompt",
    )(rel_bias, q, kv, gt, jnp.transpose(cmp_pool), cmp_pe)


PAGES_PER_STEP = 32
SEL_TILE = 2048
KVW = NSA_KV_HEADS * HEAD_DIM


def _nsa_sample_kernel(pt_ref, rb_ref, *refs):
    PPS = PAGES_PER_STEP
    pages = refs[:PPS]
    q_ref, kvn_ref, gt_ref, win_ref, poolm_ref, cconst_ref, o_ref, cmp_ref, kselt_ref, vselt_ref = refs[PPS:]
    s_id = pl.program_id(1)
    f32, bf16 = jnp.float32, jnp.bfloat16
    Tn = q_ref.shape[1]
    past = kselt_ref.shape[1]
    nbp = past // NSA_BLOCK
    Wb = win_ref.shape[3]
    bpp = PAGE_SIZE // NSA_BLOCK
    R = NSA_HEADS * Tn

    rows = []
    for k in range(PPS):
        parts = []
        for c in range(2):
            a = _dot_nt(poolm_ref[c], pages[k][0, 0, c].astype(bf16))
            parts.append(a[0:bpp] + a[SUBLANES:SUBLANES + bpp] + cconst_ref[c:c + 1])
        rows.append(jnp.concatenate(parts, axis=1))
        r = pl.multiple_of((s_id * PPS + k) * PAGE_SIZE, PAGE_SIZE)
        kselt_ref[:, pl.ds(r, PAGE_SIZE)] = pages[k][0, 0, 2].astype(bf16)
        vselt_ref[:, pl.ds(r, PAGE_SIZE)] = pages[k][0, 0, 3].astype(bf16)
    cmp_ref[pl.ds(pl.multiple_of(s_id * PPS * bpp, PPS * bpp), PPS * bpp), :] = jnp.concatenate(rows, axis=0)

    @pl.when(s_id == pl.num_programs(1) - 1)
    def _attend():
        def per_head(fn):
            return jnp.concatenate([fn(h) for h in range(NSA_HEADS)], axis=0)

        q = q_ref[0] * HEAD_DIM ** -0.5
        zero = jnp.zeros((Tn, HEAD_DIM), f32)

        def q_rows(h):
            qh = q[:, h * HEAD_DIM:(h + 1) * HEAD_DIM]
            return jnp.concatenate([qh, zero] if h < NSA_GROUP else [zero, qh], axis=1)

        q2 = per_head(q_rows).astype(bf16)
        kvn = kvn_ref[0]
        pad = jnp.zeros((LANES - Tn, KVW), f32)

        def new_rows(c):
            return jnp.concatenate([kvn[:, c * KVW:(c + 1) * KVW], pad], axis=0).astype(bf16)

        tn_io = lax.broadcasted_iota(jnp.int32, (Tn, LANES), 0)
        jn_io = lax.broadcasted_iota(jnp.int32, (Tn, LANES), 1)
        d_new = tn_io - jn_io
        keep_new = per_head(lambda h: (d_new >= 0) & (jn_io < Tn))
        bias_new = per_head(lambda h: _bias_from_dist(d_new, rb_ref, h))
        far = per_head(lambda h: jnp.full((Tn, 1), rb_ref[NUM_BUCKETS - 1, h], f32))

        kc = cmp_ref[:, 0:KVW].astype(bf16)
        vc = cmp_ref[:, KVW:2 * KVW].astype(bf16)
        n_io = lax.broadcasted_iota(jnp.int32, (Tn, nbp), 1)
        t_io = lax.broadcasted_iota(jnp.int32, (Tn, nbp), 0)
        dist_c = past + t_io - (n_io * NSA_BLOCK + NSA_BLOCK - 1)
        sc = _dot_nt(q2, kc) + per_head(lambda h: _bias_from_dist(dist_c, rb_ref, h))
        e = jnp.exp(sc - jnp.max(sc, axis=1, keepdims=True))
        p_c = e / jnp.sum(e, axis=1, keepdims=True)
        oc = jnp.dot(p_c.astype(bf16), vc, preferred_element_type=f32)

        topk = min(NSA_TOPK, nbp + 1)
        m_io = lax.broadcasted_iota(jnp.int32, (nbp, nbp), 0)
        c_io = lax.broadcasted_iota(jnp.int32, (nbp, nbp), 1)
        lower = m_io < c_io
        forced = (n_io == 0) | (n_io == nbp - 1)
        sels = []
        for kvg in range(NSA_KV_HEADS):
            r0 = kvg * NSA_GROUP * Tn
            score = jnp.where(forced, 2.0, p_c[r0:r0 + Tn] + p_c[r0 + Tn:r0 + 2 * Tn])
            score_t = jnp.concatenate([score, jnp.zeros((LANES - Tn, nbp), f32)], axis=0).T
            ranks = []
            for t in range(Tn):
                colb = jnp.broadcast_to(score_t[:, t:t + 1], (nbp, nbp))
                rowb = jnp.broadcast_to(score[t:t + 1, :], (nbp, nbp))
                beats = (colb > rowb) | ((colb == rowb) & lower)
                ranks.append(jnp.sum(beats.astype(f32), axis=0, keepdims=True))
            sel = (jnp.concatenate(ranks, axis=0) < topk - 1).astype(f32)
            sels += [sel] * NSA_GROUP
        sel_rows = jnp.concatenate(sels, axis=0).astype(bf16)

        bpt = SEL_TILE // NSA_BLOCK
        expand = (lax.broadcasted_iota(jnp.int32, (bpt, SEL_TILE), 0)
                  == lax.broadcasted_iota(jnp.int32, (bpt, SEL_TILE), 1) // NSA_BLOCK).astype(bf16)
        d_last = LANES + tn_io - jn_io
        near = per_head(lambda h: _bias_from_dist(d_last, rb_ref, h))
        m = jnp.full((R, 1), NEG_INF, f32)
        l = jnp.zeros((R, 1), f32)
        acc = jnp.zeros((R, KVW), f32)

        def flash(carry, s, v_t, v_feature_major):
            m, l, acc = carry
            m_new = jnp.maximum(m, jnp.max(s, axis=1, keepdims=True))
            p = jnp.exp(s - m_new)
            alpha = jnp.exp(m - m_new)
            pb = p.astype(bf16)
            pv = _dot_nt(pb, v_t) if v_feature_major else jnp.dot(pb, v_t, preferred_element_type=f32)
            return m_new, alpha * l + jnp.sum(p, axis=1, keepdims=True), alpha * acc + pv

        carry = (m, l, acc)
        ntile = past // SEL_TILE
        for j in range(ntile):
            k_t = kselt_ref[:, j * SEL_TILE:(j + 1) * SEL_TILE]
            v_t = vselt_ref[:, j * SEL_TILE:(j + 1) * SEL_TILE]
            keep = jnp.dot(sel_rows[:, j * bpt:(j + 1) * bpt], expand, preferred_element_type=f32) > 0.5
            if j == ntile - 1:
                bias = jnp.concatenate([jnp.broadcast_to(far, (R, SEL_TILE - LANES)), near], axis=1)
            else:
                bias = far
            s_t = jnp.dot(q2, k_t, preferred_element_type=f32)
            carry = flash(carry, jnp.where(keep, s_t + bias, NEG_INF), v_t, True)
        s_new = jnp.where(keep_new, _dot_nt(q2, new_rows(2)) + bias_new, NEG_INF)
        m, l, acc = flash(carry, s_new, new_rows(3), False)
        o_s = acc / l

        tw_io = lax.broadcasted_iota(jnp.int32, (Tn, Wb), 0)
        cw_io = lax.broadcasted_iota(jnp.int32, (Tn, Wb), 1)
        d_w = Wb + tw_io - cw_io
        s_w = jnp.where(per_head(lambda h: d_w < NSA_WINDOW),
                        jnp.dot(q2, win_ref[0, 0].astype(bf16), preferred_element_type=f32)
                        + per_head(lambda h: _bias_from_dist(d_w, rb_ref, h)), NEG_INF)
        s_wn = jnp.where(keep_new, _dot_nt(q2, new_rows(4)) + bias_new, NEG_INF)
        s_all = jnp.concatenate([s_w, s_wn], axis=1)
        e = jnp.exp(s_all - jnp.max(s_all, axis=1, keepdims=True))
        p_w = (e / jnp.sum(e, axis=1, keepdims=True)).astype(bf16)
        o_w = _dot_nt(p_w[:, :Wb], win_ref[0, 1].astype(bf16)) + jnp.dot(p_w[:, Wb:], new_rows(5),
                                                                          preferred_element_type=f32)

        gates = jax.nn.sigmoid(gt_ref[0])
        outs = []
        for h in range(NSA_HEADS):
            rs = slice(h * Tn, (h + 1) * Tn)
            cs = slice((h // NSA_GROUP) * HEAD_DIM, (h // NSA_GROUP + 1) * HEAD_DIM)
            outs.append(gates[:, h:h + 1] * oc[rs, cs] + gates[:, NSA_HEADS + h:NSA_HEADS + h + 1] * o_s[rs, cs]
                        + gates[:, 2 * NSA_HEADS + h:2 * NSA_HEADS + h + 1] * o_w[rs, cs])
        o_ref[0] = jnp.concatenate(outs, axis=1)


def _nsa_sample(q, kv, gate_logits, cache, layer, page_table, win_buf, cmp_pool, cmp_pe, rel_bias):
    B, Tn, _ = q.shape
    npages = page_table.shape[1]
    past = npages * PAGE_SIZE
    Wb = win_buf.shape[1]
    PPS = PAGES_PER_STEP
    assert npages % PPS == 0 and past % SEL_TILE == 0 and Tn == SUBLANES and (past // NSA_BLOCK) % LANES == 0
    f32, bf16 = jnp.float32, jnp.bfloat16
    bpp = PAGE_SIZE // NSA_BLOCK
    cache_t = jnp.transpose(cache, (0, 1, 3, 4, 5, 2)).reshape(cache.shape[0], cache.shape[1], 4, KVW, PAGE_SIZE)
    win_t = jnp.transpose(win_buf, (0, 2, 3, 4, 1)).reshape(B, 2, KVW, Wb)
    r_io = np.arange(PAGE_SIZE)
    onehot = jnp.asarray((r_io[None, :] // NSA_BLOCK == np.arange(SUBLANES)[:, None]), f32)
    pool_full = onehot[None] * jnp.tile(cmp_pool, (1, bpp))[:, None, :]
    pool_hi = pool_full.astype(bf16)
    pool_lo = (pool_full - pool_hi.astype(f32)).astype(bf16)
    poolm = jnp.concatenate([pool_hi, pool_lo], axis=1)
    cconst = jnp.tile(jnp.sum(cmp_pool[:, :, None] * cmp_pe, axis=1), (1, NSA_KV_HEADS))

    def page_spec(k):
        return pl.BlockSpec((1, 1, 4, KVW, PAGE_SIZE), lambda b, s, pt: (pt[b, s * PPS + k], layer, 0, 0, 0))

    grid_spec = pltpu.PrefetchScalarGridSpec(
        num_scalar_prefetch=1,
        grid=(B, npages // PPS),
        in_specs=[pl.BlockSpec(memory_space=pltpu.SMEM)] + [page_spec(k) for k in range(PPS)] + [
            pl.BlockSpec((1, Tn, NSA_HEADS * HEAD_DIM), lambda b, s, pt: (b, 0, 0)),
            pl.BlockSpec((1, Tn, 6 * KVW), lambda b, s, pt: (b, 0, 0)),
            pl.BlockSpec((1, Tn, 3 * NSA_HEADS), lambda b, s, pt: (b, 0, 0)),
            pl.BlockSpec((1, 2, KVW, Wb), lambda b, s, pt: (b, 0, 0, 0)),
            pl.BlockSpec((2, 2 * SUBLANES, PAGE_SIZE), lambda b, s, pt: (0, 0, 0)),
            pl.BlockSpec((2, KVW), lambda b, s, pt: (0, 0)),
        ],
        out_specs=pl.BlockSpec((1, Tn, NSA_HEADS * HEAD_DIM), lambda b, s, pt: (b, 0, 0)),
        scratch_shapes=[
            pltpu.VMEM((past // NSA_BLOCK, 2 * KVW), f32),
            pltpu.VMEM((KVW, past), bf16),
            pltpu.VMEM((KVW, past), bf16),
        ],
    )
    return pl.pallas_call(
        _nsa_sample_kernel,
        grid_spec=grid_spec,
        out_shape=jax.ShapeDtypeStruct((B, Tn, NSA_HEADS * HEAD_DIM), f32),
        compiler_params=pltpu.CompilerParams(dimension_semantics=("arbitrary", "arbitrary"),
                                             vmem_limit_bytes=VMEM_LIMIT),
        name="nsa_sample",
    )(page_table, rel_bias, *([cache_t] * PPS), q, kv, gate_logits, win_t, poolm, cconst)


PROJ_KV, PROJ_Q, PROJ_UCV, PROJ_RET, PROJ_GQKV, PROJ_GZ, PROJ_MISC = 0, 768, 1024, 1536, 2560, 3328, 3584
PROJ_COLS = PROJ_MISC + LANES
PROJ_TM = 512
MERGE_TM = 256
FFN_TM = 256


def _prep_w_in(w_in):
    o = [int(v) for v in np.cumsum((0,) + IN_SPLITS)]
    small = o[5] + 4 * GDN_W
    pieces = [w_in[:, o[1]:o[2]], w_in[:, o[0]:o[1]], w_in[:, o[3]:o[4]], w_in[:, o[4]:o[5]], w_in[:, o[5]:small],
              w_in[:, o[2]:o[3]], w_in[:, small:o[6]]]
    used = sum(pc.shape[1] for pc in pieces)
    pieces.append(jnp.zeros((D_MODEL, PROJ_COLS - used), w_in.dtype))
    return jnp.concatenate(pieces, axis=1), w_in[:, o[6]:]


def _modulated_norm(x, g, sc, sh):
    y = x * lax.rsqrt(jnp.mean(x * x, axis=-1, keepdims=True) + EPS)
    return (y * g) * (1.0 + sc) + sh


def _resident(shape):
    return pl.BlockSpec(shape, lambda *_: (0,) * len(shape), pipeline_mode=pl.Buffered(1))


def _mod_spec(mod, rows_per_group, tm):
    if mod.shape[1] == 1:
        return pl.BlockSpec((1, 1, D_MODEL), lambda i, *_: (i // (rows_per_group // tm), 0, 0))
    return pl.BlockSpec((1, tm, D_MODEL), lambda i, *_: (0, i, 0))


def _proj_kernel(x_ref, g_ref, sc_ref, sh_ref, w_ref, o_ref, wb_ref):
    @pl.when(pl.program_id(0) == 0)
    def _():
        wb_ref[...] = w_ref[...].astype(jnp.bfloat16)

    h = _modulated_norm(x_ref[...], g_ref[...], sc_ref[0], sh_ref[0]).astype(jnp.bfloat16)
    o_ref[...] = jnp.dot(h, wb_ref[...], preferred_element_type=jnp.float32)


def _proj(x2, g, sc, sh, w, rows_per_group, tm):
    M = x2.shape[0]
    N = w.shape[1]
    return pl.pallas_call(
        _proj_kernel,
        grid=(M // tm,),
        in_specs=[pl.BlockSpec((tm, D_MODEL), lambda i: (i, 0)), _resident((1, D_MODEL)),
                  _mod_spec(sc, rows_per_group, tm), _mod_spec(sh, rows_per_group, tm), _resident((D_MODEL, N))],
        out_specs=pl.BlockSpec((tm, N), lambda i: (i, 0)),
        out_shape=jax.ShapeDtypeStruct((M, N), jnp.float32),
        scratch_shapes=[pltpu.VMEM((D_MODEL, N), jnp.bfloat16)],
        compiler_params=pltpu.CompilerParams(dimension_semantics=("arbitrary",), vmem_limit_bytes=VMEM_LIMIT),
        name="in_proj",
    )(x2, g, sc, sh, w)


def _merge_kernel(x_ref, g0_ref, sc_ref, sh_ref, gt_ref, g1_ref, b0_ref, b1_ref, b2_ref, b3_ref,
                  wmg_ref, wbr_ref, wout_ref, o_ref, wmgb_ref):
    f32, bf16 = jnp.float32, jnp.bfloat16

    @pl.when(pl.program_id(0) == 0)
    def _():
        wmgb_ref[...] = wmg_ref[...].astype(bf16)

    x = x_ref[...]
    h = _modulated_norm(x, g0_ref[...], sc_ref[0], sh_ref[0]).astype(bf16)
    acc = jnp.zeros(x.shape, f32)
    for n, b_ref in enumerate((b0_ref, b1_ref, b2_ref, b3_ref)):
        gate = jax.nn.sigmoid(jnp.dot(h, wmgb_ref[:, n * D_MODEL:(n + 1) * D_MODEL], preferred_element_type=f32))
        acc = acc + gate * jnp.dot(b_ref[...].astype(bf16), wbr_ref[n], preferred_element_type=f32)
    mixed = jnp.dot(acc.astype(bf16), wout_ref[...], preferred_element_type=f32)
    y = mixed * lax.rsqrt(jnp.mean(mixed * mixed, axis=-1, keepdims=True) + EPS) * g1_ref[...]
    o_ref[...] = x + gt_ref[0] * y


def _merge(x2, g0, sc, sh, gt, g1, branches, wmg, wbr, wout, rows_per_group, tm):
    M = x2.shape[0]
    row = lambda w: pl.BlockSpec((tm, w), lambda i: (i, 0))
    mspec = _mod_spec(sc, rows_per_group, tm)
    return pl.pallas_call(
        _merge_kernel,
        grid=(M // tm,),
        in_specs=[row(D_MODEL), _resident((1, D_MODEL)), mspec, mspec, mspec, _resident((1, D_MODEL))]
                 + [row(BRANCH_W)] * N_BRANCH
                 + [_resident(wmg.shape), _resident(wbr.shape), _resident(wout.shape)],
        out_specs=row(D_MODEL),
        out_shape=jax.ShapeDtypeStruct((M, D_MODEL), jnp.float32),
        scratch_shapes=[pltpu.VMEM(wmg.shape, jnp.bfloat16)],
        compiler_params=pltpu.CompilerParams(dimension_semantics=("arbitrary",), vmem_limit_bytes=VMEM_LIMIT),
        name="merge",
    )(x2, g0, sc, sh, gt, g1, *branches, wmg, wbr, wout)


def _ffn_kernel(x_ref, g2_ref, sc_ref, sh_ref, gt_ref, g3_ref, buf_ref, wup_ref, dw_ref, wdn_ref,
                o_ref, st_ref, gp_ref):
    f32, bf16 = jnp.float32, jnp.bfloat16
    tm = x_ref.shape[1]
    HALO = SUBLANES

    @pl.when(pl.program_id(1) == 0)
    def _():
        gp_ref[0:HALO, :] = buf_ref[0]

    x = x_ref[0]
    h = _modulated_norm(x, g2_ref[...], sc_ref[0], sh_ref[0]).astype(bf16)
    gp_ref[HALO:HALO + tm, :] = jnp.dot(h, wup_ref[:, 0:D_FF], preferred_element_type=f32)
    val = jnp.dot(h, wup_ref[:, D_FF:2 * D_FF], preferred_element_type=f32)
    gconv = (dw_ref[2:3, :] * gp_ref[HALO:HALO + tm, :] + dw_ref[1:2, :] * gp_ref[HALO - 1:HALO - 1 + tm, :]
             + dw_ref[0:1, :] * gp_ref[HALO - 2:HALO - 2 + tm, :])
    a = (jax.nn.gelu(gconv) * val).astype(bf16)
    f = jnp.dot(a, wdn_ref[...], preferred_element_type=f32)
    y = f * lax.rsqrt(jnp.mean(f * f, axis=-1, keepdims=True) + EPS) * g3_ref[...]
    o_ref[0] = x + gt_ref[0] * y
    tail = gp_ref[tm:tm + HALO, :]
    gp_ref[0:HALO, :] = tail
    st_ref[0] = tail


def _ffn(x3, g2, sc, sh, gt, g3, buf, wup, dw, wdn, tm):
    B, T, _ = x3.shape
    buf8 = jnp.pad(buf, ((0, 0), (SUBLANES - (FFN_CONV - 1), 0), (0, 0)))
    mspec = pl.BlockSpec((1, 1, D_MODEL), lambda b, i: (b, 0, 0))
    y, st = pl.pallas_call(
        _ffn_kernel,
        grid=(B, T // tm),
        in_specs=[pl.BlockSpec((1, tm, D_MODEL), lambda b, i: (b, i, 0)), _resident((1, D_MODEL)), mspec, mspec, mspec,
                  _resident((1, D_MODEL)), pl.BlockSpec((1, SUBLANES, D_FF), lambda b, i: (b, 0, 0)),
                  _resident(wup.shape), _resident(dw.shape), _resident(wdn.shape)],
        out_specs=[pl.BlockSpec((1, tm, D_MODEL), lambda b, i: (b, i, 0)),
                   pl.BlockSpec((1, SUBLANES, D_FF), lambda b, i: (b, 0, 0))],
        out_shape=[jax.ShapeDtypeStruct((B, T, D_MODEL), jnp.float32),
                   jax.ShapeDtypeStruct((B, SUBLANES, D_FF), jnp.float32)],
        scratch_shapes=[pltpu.VMEM((tm + SUBLANES, D_FF), jnp.float32)],
        compiler_params=pltpu.CompilerParams(dimension_semantics=("arbitrary", "arbitrary"),
                                             vmem_limit_bytes=VMEM_LIMIT),
        name="conv_ffn",
    )(x3, g2, sc, sh, gt, g3, buf8, wup, dw, wdn)
    return y, st[:, SUBLANES - (FFN_CONV - 1):]


def _ffn_rows_kernel(x_ref, g2_ref, sc_ref, sh_ref, gt_ref, g3_ref, p1_ref, p2_ref, wup_ref, dw_ref, wdn_ref,
                     o_ref, gpre_ref, gp_ref, *, seg):
    f32, bf16 = jnp.float32, jnp.bfloat16
    M = x_ref.shape[0]
    HALO = SUBLANES
    x = x_ref[...]
    h = _modulated_norm(x, g2_ref[...], sc_ref[0], sh_ref[0]).astype(bf16)
    gpre = jnp.dot(h, wup_ref[:, 0:D_FF], preferred_element_type=f32)
    val = jnp.dot(h, wup_ref[:, D_FF:2 * D_FF], preferred_element_type=f32)
    gp_ref[0:HALO, :] = jnp.zeros((HALO, D_FF), f32)
    gp_ref[HALO:HALO + M, :] = gpre
    t = lax.broadcasted_iota(jnp.int32, (M, 1), 0) % seg
    prev1 = jnp.where(t >= 1, gp_ref[HALO - 1:HALO - 1 + M, :], p1_ref[...])
    prev2 = jnp.where(t >= 2, gp_ref[HALO - 2:HALO - 2 + M, :], p2_ref[...])
    gconv = dw_ref[2:3, :] * gpre + dw_ref[1:2, :] * prev1 + dw_ref[0:1, :] * prev2
    a = (jax.nn.gelu(gconv) * val).astype(bf16)
    f = jnp.dot(a, wdn_ref[...], preferred_element_type=f32)
    y = f * lax.rsqrt(jnp.mean(f * f, axis=-1, keepdims=True) + EPS) * g3_ref[...]
    o_ref[...] = x + gt_ref[0] * y
    gpre_ref[...] = gpre


def _ffn_rows(x3, g2, sc, sh, gt, g3, buf, wup, dw, wdn):
    B, T, _ = x3.shape
    M = B * T
    assert FFN_CONV == 3 and T >= FFN_CONV - 1
    zeros = jnp.zeros((B, T, D_FF), jnp.float32)
    p1 = zeros.at[:, 0].set(buf[:, 1]).reshape(M, D_FF)
    p2 = zeros.at[:, 0].set(buf[:, 0]).at[:, 1].set(buf[:, 1]).reshape(M, D_FF)
    full = lambda shape: pl.BlockSpec(shape, lambda i: (0,) * len(shape))
    y, gpre = pl.pallas_call(
        functools.partial(_ffn_rows_kernel, seg=T),
        grid=(1,),
        in_specs=[full((M, D_MODEL)), full((1, D_MODEL)), full((1, M, D_MODEL)), full((1, M, D_MODEL)),
                  full((1, M, D_MODEL)), full((1, D_MODEL)), full((M, D_FF)), full((M, D_FF)),
                  _resident(wup.shape), _resident(dw.shape), _resident(wdn.shape)],
        out_specs=[full((M, D_MODEL)), full((M, D_FF))],
        out_shape=[jax.ShapeDtypeStruct((M, D_MODEL), jnp.float32), jax.ShapeDtypeStruct((M, D_FF), jnp.float32)],
        scratch_shapes=[pltpu.VMEM((M + SUBLANES, D_FF), jnp.float32)],
        compiler_params=pltpu.CompilerParams(dimension_semantics=("arbitrary",), vmem_limit_bytes=VMEM_LIMIT),
        name="conv_ffn_rows",
    )(x3.reshape(M, D_MODEL), g2, sc, sh, gt, g3, p1, p2, wup, dw, wdn)
    return y.reshape(B, T, D_MODEL), gpre.reshape(B, T, D_FF)[:, T - (FFN_CONV - 1):]


CONF_HALO = 32
CONF_TM = 1024


def _conformer_kernel(u_ref, buf_ref, dw_ref, dwb_ref, lng_ref, lnb_ref, o_ref, st_ref, xp_ref):
    tm = u_ref.shape[1]
    first = CONF_HALO - (CONV_WIDTH - 1)

    @pl.when(pl.program_id(1) == 0)
    def _():
        xp_ref[0:CONF_HALO, :] = buf_ref[0]

    u = u_ref[0]
    xp_ref[CONF_HALO:CONF_HALO + tm, :] = u[:, :CONV_CH] * jax.nn.sigmoid(u[:, CONV_CH:])
    acc = jnp.zeros((tm, CONV_CH), jnp.float32)
    for k in range(CONV_WIDTH):
        acc = acc + dw_ref[k:k + 1, :] * xp_ref[first + k:first + k + tm, :]
    y = acc + dwb_ref[...]
    mu = jnp.mean(y, axis=-1, keepdims=True)
    var = jnp.mean(jnp.square(y - mu), axis=-1, keepdims=True)
    yn = (y - mu) * lax.rsqrt(var + EPS) * lng_ref[...] + lnb_ref[...]
    o_ref[0] = yn * jax.nn.sigmoid(yn)
    tail = xp_ref[tm:tm + CONF_HALO, :]
    xp_ref[0:CONF_HALO, :] = tail
    st_ref[0] = tail


def _conformer(P, col_blk, buf, dw, dw_b, ln_g, ln_b, tm):
    B, T, _ = P.shape
    bufp = jnp.pad(buf, ((0, 0), (CONF_HALO - (CONV_WIDTH - 1), 0), (0, 0)))
    vec = lambda: _resident((1, CONV_CH))
    o, st = pl.pallas_call(
        _conformer_kernel,
        grid=(B, T // tm),
        in_specs=[pl.BlockSpec((1, tm, 2 * CONV_CH), lambda b, i: (b, i, col_blk)),
                  pl.BlockSpec((1, CONF_HALO, CONV_CH), lambda b, i: (b, 0, 0)),
                  _resident((CONV_WIDTH, CONV_CH)), vec(), vec(), vec()],
        out_specs=[pl.BlockSpec((1, tm, CONV_CH), lambda b, i: (b, i, 0)),
                   pl.BlockSpec((1, CONF_HALO, CONV_CH), lambda b, i: (b, 0, 0))],
        out_shape=[jax.ShapeDtypeStruct((B, T, CONV_CH), jnp.float32),
                   jax.ShapeDtypeStruct((B, CONF_HALO, CONV_CH), jnp.float32)],
        scratch_shapes=[pltpu.VMEM((tm + CONF_HALO, CONV_CH), jnp.float32)],
        compiler_params=pltpu.CompilerParams(dimension_semantics=("arbitrary", "arbitrary"),
                                             vmem_limit_bytes=VMEM_LIMIT),
        name="conformer",
    )(P, bufp, dw, dw_b[None], ln_g[None], ln_b[None])
    return o, st[:, CONF_HALO - (CONV_WIDTH - 1):]


RET_LOG_GAMMA = tuple(math.log1p(-2.0 ** (-5 - h)) for h in range(RET_HEADS))
RET_TM = 256


def _retention_kernel(q_ref, k_ref, v_ref, g_ref, cos_ref, sin_ref, s0_ref, gn_ref, o_ref, sfin_ref,
                      s_ref, dec_ref, *, n_valid):
    f32, bf16 = jnp.float32, jnp.bfloat16
    C = q_ref.shape[1]
    W = RET_W
    half = HEAD_DIM // 2

    @pl.when(pl.program_id(1) == 0)
    def _():
        s_ref[...] = s0_ref[0]

    @pl.when((pl.program_id(0) == 0) & (pl.program_id(1) == 0))
    def _():
        ii = lax.broadcasted_iota(jnp.int32, (C, C), 0)
        jj = lax.broadcasted_iota(jnp.int32, (C, C), 1)
        d = (ii - jj).astype(f32)
        for h in range(RET_HEADS):
            dec_ref[h] = jnp.where((ii >= jj) & (jj < n_valid), jnp.exp(jnp.maximum(d, 0.0) * RET_LOG_GAMMA[h]), 0.0)

    lane = lax.broadcasted_iota(jnp.int32, (C, W), 1)
    low = lane % HEAD_DIM < half
    cos = cos_ref[...]
    sin = sin_ref[...]

    def rot(x):
        other = jnp.where(low, pltpu.roll(x, W - half, axis=1), pltpu.roll(x, half, axis=1))
        return x * cos + other * sin

    q = rot(q_ref[0])
    k = rot(k_ref[0]) * HEAD_DIM ** -0.5
    v = v_ref[0]
    row = lax.broadcasted_iota(jnp.int32, (C, HEAD_DIM), 0)
    rowf = row.astype(f32)
    valid = row < n_valid
    outs = []
    for h in range(RET_HEADS):
        cs = slice(h * HEAD_DIM, (h + 1) * HEAD_DIM)
        lg = RET_LOG_GAMMA[h]
        qh, kh, vh = q[:, cs], k[:, cs], v[:, cs].astype(bf16)
        s_old = s_ref[h]
        att = _dot_nt(qh.astype(bf16), kh.astype(bf16)) * dec_ref[h]
        o = (jnp.dot(att.astype(bf16), vh, preferred_element_type=f32)
             + jnp.dot((qh * jnp.exp((rowf + 1.0) * lg)).astype(bf16), s_old.astype(bf16), preferred_element_type=f32))
        kz = jnp.where(valid, kh * jnp.exp((n_valid - 1.0 - rowf) * lg), 0.0)
        s_ref[h] = s_old * math.exp(n_valid * lg) + jnp.dot(kz.T.astype(bf16), vh, preferred_element_type=f32)
        mu = jnp.mean(o, axis=-1, keepdims=True)
        var = jnp.mean(jnp.square(o - mu), axis=-1, keepdims=True)
        outs.append((o - mu) * lax.rsqrt(var + EPS))
    g = g_ref[0]
    o_ref[0] = jnp.concatenate(outs, axis=1) * gn_ref[...] * (g * jax.nn.sigmoid(g))

    @pl.when(pl.program_id(1) == pl.num_programs(1) - 1)
    def _():
        sfin_ref[0] = s_ref[...]


def _retention(P, col_blk0, S0, pos, gn, chunk, n_valid):
    B, T, _ = P.shape
    half = HEAD_DIM // 2
    inv = ROPE_BASE ** (-jnp.arange(half, dtype=jnp.float32) / half)
    ang = pos.astype(jnp.float32)[:, None] * inv[None, :]
    cos, sin = jnp.cos(ang), jnp.sin(ang)
    cosf = jnp.tile(jnp.concatenate([cos, cos], axis=1), (1, RET_HEADS))
    sinf = jnp.tile(jnp.concatenate([-sin, sin], axis=1), (1, RET_HEADS))
    col = lambda j: pl.BlockSpec((1, chunk, RET_W), lambda b, i: (b, i, col_blk0 + j))
    tab = pl.BlockSpec((chunk, RET_W), lambda b, i: (i, 0))
    state = pl.BlockSpec((1, RET_HEADS, HEAD_DIM, HEAD_DIM), lambda b, i: (b, 0, 0, 0))
    return pl.pallas_call(
        functools.partial(_retention_kernel, n_valid=n_valid),
        grid=(B, T // chunk),
        in_specs=[col(0), col(1), col(2), col(3), tab, tab, state, _resident((1, RET_W))],
        out_specs=[pl.BlockSpec((1, chunk, RET_W), lambda b, i: (b, i, 0)), state],
        out_shape=[jax.ShapeDtypeStruct((B, T, RET_W), jnp.float32),
                   jax.ShapeDtypeStruct((B, RET_HEADS, HEAD_DIM, HEAD_DIM), jnp.float32)],
        scratch_shapes=[pltpu.VMEM((RET_HEADS, HEAD_DIM, HEAD_DIM), jnp.float32),
                        pltpu.VMEM((RET_HEADS, chunk, chunk), jnp.float32)],
        compiler_params=pltpu.CompilerParams(dimension_semantics=("arbitrary", "arbitrary"),
                                             vmem_limit_bytes=VMEM_LIMIT),
        name="retention",
    )(P, P, P, P, cosf, sinf, S0, gn[None])


GDN_RT = 2 * GDN_CHUNK
GDN_HALO = SUBLANES
GDN_A_LANE = 3 * NSA_HEADS
GDN_B_LANE = GDN_A_LANE + GDN_HEADS


def _mm1(a, b):
    return jnp.dot(a.astype(jnp.bfloat16), b.astype(jnp.bfloat16), preferred_element_type=jnp.float32)


def _mm3(a, b):
    f32, bf16 = jnp.float32, jnp.bfloat16
    ah, bh = a.astype(bf16), b.astype(bf16)
    al, bl = (a - ah.astype(f32)).astype(bf16), (b - bh.astype(f32)).astype(bf16)
    d = lambda x, y: jnp.dot(x, y, preferred_element_type=f32)
    return d(ah, bh) + (d(ah, bl) + d(al, bh))


def _segment_cumsum(x, axis, seg):
    idx = lax.broadcasted_iota(jnp.int32, x.shape, axis) % seg
    s = 1
    while s < seg:
        x = x + jnp.where(idx >= s, pltpu.roll(x, s, axis=axis), 0.0)
        s *= 2
    return x


def _gdn_kernel(q_ref, k_ref, v_ref, z_ref, ab_ref, abt_ref, buf_ref, s0_ref, cw_ref, ng_ref, alane_ref, dlane_ref,
                acol_ref, dcol_ref, o_ref, sfin_ref, s_ref, xp_ref, *, n_valid):
    f32 = jnp.float32
    RT = q_ref.shape[1]
    C = GDN_CHUNK
    W = GDN_W
    first = GDN_HALO - (GDN_CONV - 1)

    @pl.when(pl.program_id(1) == 0)
    def _():
        s_ref[...] = s0_ref[0]
        xp_ref[0:GDN_HALO, :] = buf_ref[0]

    xp_ref[GDN_HALO:GDN_HALO + RT, 0:W] = q_ref[0]
    xp_ref[GDN_HALO:GDN_HALO + RT, W:2 * W] = k_ref[0]
    xp_ref[GDN_HALO:GDN_HALO + RT, 2 * W:3 * W] = v_ref[0]
    y = jnp.zeros((RT, 3 * W), f32)
    for t in range(GDN_CONV):
        y = y + cw_ref[t:t + 1, :] * xp_ref[first + t:first + t + RT, :]
    y = y * jax.nn.sigmoid(y)
    xp_ref[0:GDN_HALO, :] = xp_ref[RT:RT + GDN_HALO, :]

    ab = ab_ref[0]
    g_lanes = -jnp.exp(alane_ref[...]) * jax.nn.softplus(ab + dlane_ref[...])
    beta_lanes = jax.nn.sigmoid(ab)
    g_rows = -jnp.exp(acol_ref[...]) * jax.nn.softplus(abt_ref[0] + dcol_ref[...])
    if n_valid < C:
        g_lanes = jnp.where(lax.broadcasted_iota(jnp.int32, g_lanes.shape, 0) % C < n_valid, g_lanes, 0.0)
        beta_lanes = jnp.where(lax.broadcasted_iota(jnp.int32, g_lanes.shape, 0) % C < n_valid, beta_lanes, 0.0)
        g_rows = jnp.where(lax.broadcasted_iota(jnp.int32, g_rows.shape, 1) % C < n_valid, g_rows, 0.0)
    gc_lanes = _segment_cumsum(g_lanes, 0, C)
    gc_rows = _segment_cumsum(g_rows, 1, C)

    ii = lax.broadcasted_iota(jnp.int32, (C, C), 0)
    jj = lax.broadcasted_iota(jnp.int32, (C, C), 1)
    tri = ii >= jj
    strict = ii > jj
    eye = (ii == jj).astype(f32)
    z = z_ref[0]
    bf16 = jnp.bfloat16
    nch = RT // C
    pairs = [(c, h) for c in range(nch) for h in range(GDN_HEADS)]
    qs, ks, ms, atts, rhss, gcs = {}, {}, {}, {}, {}, {}
    for c, h in pairs:
        rs = slice(c * C, (c + 1) * C)
        qh = y[rs, h * HEAD_DIM:(h + 1) * HEAD_DIM]
        kh = y[rs, W + h * HEAD_DIM:W + (h + 1) * HEAD_DIM]
        vh = y[rs, 2 * W + h * HEAD_DIM:2 * W + (h + 1) * HEAD_DIM]
        qh = qh * lax.rsqrt(jnp.sum(qh * qh, axis=-1, keepdims=True) + EPS) * HEAD_DIM ** -0.5
        kh = kh * lax.rsqrt(jnp.sum(kh * kh, axis=-1, keepdims=True) + EPS)
        gc_col = gc_lanes[rs, GDN_A_LANE + h:GDN_A_LANE + h + 1]
        beta = beta_lanes[rs, GDN_B_LANE + h:GDN_B_LANE + h + 1]
        gc_row = gc_rows[h:h + 1, c * C:(c + 1) * C]
        e_col = jnp.exp(gc_col)
        lm = jnp.exp(jnp.where(tri, gc_col - gc_row, NEG_INF))
        kb = kh * beta
        ms[c, h] = jnp.where(strict, _dot_nt(kb.astype(bf16), kh.astype(bf16)) * lm, 0.0)
        atts[c, h] = _dot_nt(qh.astype(bf16), kh.astype(bf16)) * lm
        rhss[c, h] = jnp.concatenate([vh * beta, kb * e_col], axis=1)
        qs[c, h], ks[c, h], gcs[c, h] = qh * e_col, kh, gc_col
    pw = {p: -ms[p] for p in pairs}
    xs = {p: eye + pw[p] for p in pairs}
    for _ in range(int(math.log2(C)) - 1):
        pw = {p: _mm1(pw[p], pw[p]) for p in pairs}
        xs = {p: xs[p] + _mm1(xs[p], pw[p]) for p in pairs}
    res = {p: eye - (xs[p] + _mm3(ms[p], xs[p])) for p in pairs}
    xs = {p: xs[p] + _mm1(xs[p], res[p]) for p in pairs}
    sols = {p: _mm3(xs[p], rhss[p]) for p in pairs}
    state = [s_ref[h] for h in range(GDN_HEADS)]
    outs = {}
    for c in range(nch):
        heads = range(GDN_HEADS)
        g_last = [gcs[c, h][C - 1:C, :] for h in heads]
        vn = [sols[c, h][:, :HEAD_DIM] - _mm1(sols[c, h][:, HEAD_DIM:], state[h]) for h in heads]
        o_in = [_mm1(qs[c, h], state[h]) for h in heads]
        kd_t = [(ks[c, h] * jnp.exp(g_last[h] - gcs[c, h])).T for h in heads]
        for h in heads:
            outs[c, h] = o_in[h] + _mm1(atts[c, h], vn[h])
        state = [state[h] * jnp.exp(g_last[h]) + _mm1(kd_t[h], vn[h]) for h in heads]
    for h in range(GDN_HEADS):
        s_ref[h] = state[h]
    row_outs = []
    for c in range(nch):
        head_outs = []
        for h in range(GDN_HEADS):
            o = outs[c, h]
            o = o * lax.rsqrt(jnp.mean(o * o, axis=-1, keepdims=True) + EPS) * ng_ref[...]
            zh = z[c * C:(c + 1) * C, h * HEAD_DIM:(h + 1) * HEAD_DIM]
            head_outs.append(o * (zh * jax.nn.sigmoid(zh)))
        row_outs.append(jnp.concatenate(head_outs, axis=1))
    o_ref[0] = jnp.concatenate(row_outs, axis=0)

    @pl.when(pl.program_id(1) == pl.num_programs(1) - 1)
    def _():
        sfin_ref[0] = s_ref[...]


def _gdn(P, blk_q, blk_misc, ga, conv_buf, S0, conv_w, A_log, dt_bias, norm_g, rt, n_valid):
    B, T, _ = P.shape
    f32 = jnp.float32
    abt = jnp.pad(jnp.transpose(ga, (0, 2, 1)), ((0, 0), (0, SUBLANES - GDN_HEADS), (0, 0)))
    buf8 = jnp.pad(conv_buf, ((0, 0), (GDN_HALO - (GDN_CONV - 1), 0), (0, 0)))
    lane_vec = lambda v: jnp.zeros((1, LANES), f32).at[0, GDN_A_LANE:GDN_A_LANE + GDN_HEADS].set(v)
    col_vec = lambda v: jnp.zeros((SUBLANES, 1), f32).at[0:GDN_HEADS, 0].set(v)
    col = lambda j: pl.BlockSpec((1, rt, GDN_W), lambda b, i: (b, i, blk_q + j))
    state = pl.BlockSpec((1, GDN_HEADS, HEAD_DIM, HEAD_DIM), lambda b, i: (b, 0, 0, 0))
    return pl.pallas_call(
        functools.partial(_gdn_kernel, n_valid=n_valid),
        grid=(B, T // rt),
        in_specs=[col(0), col(1), col(2), col(3),
                  pl.BlockSpec((1, rt, LANES), lambda b, i: (b, i, blk_misc)),
                  pl.BlockSpec((1, SUBLANES, rt), lambda b, i: (b, 0, i)),
                  pl.BlockSpec((1, GDN_HALO, 3 * GDN_W), lambda b, i: (b, 0, 0)), state,
                  _resident((GDN_CONV, 3 * GDN_W)), _resident((1, HEAD_DIM)),
                  _resident((1, LANES)), _resident((1, LANES)), _resident((SUBLANES, 1)), _resident((SUBLANES, 1))],
        out_specs=[pl.BlockSpec((1, rt, GDN_W), lambda b, i: (b, i, 0)), state],
        out_shape=[jax.ShapeDtypeStruct((B, T, GDN_W), f32),
                   jax.ShapeDtypeStruct((B, GDN_HEADS, HEAD_DIM, HEAD_DIM), f32)],
        scratch_shapes=[pltpu.VMEM((GDN_HEADS, HEAD_DIM, HEAD_DIM), f32),
                        pltpu.VMEM((rt + GDN_HALO, 3 * GDN_W), f32)],
        compiler_params=pltpu.CompilerParams(dimension_semantics=("arbitrary", "arbitrary"),
                                             vmem_limit_bytes=VMEM_LIMIT),
        name="gated_deltanet",
    )(P, P, P, P, P, abt, buf8, S0, conv_w, norm_g[None], lane_vec(A_log), lane_vec(dt_bias),
      col_vec(A_log), col_vec(dt_bias))


def _prep_layer(p):
    w_main, w_mg = _prep_w_in(p["w_in"])
    q = dict(p)
    q.update(w_main=w_main, w_mg=w_mg, w_branch_b=p["w_branch"].astype(jnp.bfloat16),
             w_out_b=p["w_out"].astype(jnp.bfloat16), ffn_up_b=p["ffn_up"].astype(jnp.bfloat16),
             ffn_down_b=p["ffn_down"].astype(jnp.bfloat16))
    return q


def trunk_layer(x, mod, pos0, nsa_past, win_buf, conv_buf, ret_s, gdn_buf, gdn_s, ffn_buf, p, rel_bias):
    B, T, _ = x.shape
    M = B * T
    per_row = T < MERGE_TM
    if per_row:
        sh1, sc1, gt1, sh2, sc2, gt2 = [jnp.repeat(m, T, axis=0)[None] for m in jnp.split(mod, 6, axis=-1)]
    else:
        sh1, sc1, gt1, sh2, sc2, gt2 = [m[:, None, :] for m in jnp.split(mod, 6, axis=-1)]
    norms = p["norms"][:, None, :]
    x2 = x.reshape(M, D_MODEL)
    P = _proj(x2, norms[0], sc1, sh1, p["w_main"], T, M if per_row else PROJ_TM).reshape(B, T, PROJ_COLS)
    nkv = P[:, :, PROJ_KV:PROJ_KV + 6 * KVW]
    ngt = P[:, :, PROJ_MISC:PROJ_MISC + 3 * NSA_HEADS]
    ga = P[:, :, PROJ_MISC + 3 * NSA_HEADS:PROJ_MISC + 3 * NSA_HEADS + GDN_HEADS]
    gb = P[:, :, PROJ_MISC + 3 * NSA_HEADS + GDN_HEADS:PROJ_MISC + 3 * NSA_HEADS + 2 * GDN_HEADS]
    keep = min(NSA_WINDOW, T)
    kw = P[:, T - keep:, PROJ_KV + 4 * KVW:PROJ_KV + 6 * KVW].reshape(B, keep, 2, NSA_KV_HEADS, HEAD_DIM)
    if nsa_past is None:
        o_nsa = _nsa_prompt(P, PROJ_Q // (NSA_HEADS * HEAD_DIM), P, PROJ_KV // (6 * KVW), ngt,
                            p["cmp_pool"], p["cmp_pe"], rel_bias)
        new_win = kw
    else:
        cache, layer, page_table = nsa_past
        o_nsa = _nsa_sample(P[:, :, PROJ_Q:PROJ_Q + NSA_HEADS * HEAD_DIM], nkv, ngt, cache, layer, page_table,
                            win_buf, p["cmp_pool"], p["cmp_pe"], rel_bias)
        real = jnp.concatenate([win_buf, kw], axis=1)
        new_win = real[:, real.shape[1] - min(NSA_WINDOW, real.shape[1]):]
    o_conv, new_conv = _conformer(P, PROJ_UCV // (2 * CONV_CH), conv_buf, p["conv_dw"], p["conv_dw_b"],
                                  p["conv_ln_g"], p["conv_ln_b"], T if per_row else CONF_TM)
    if per_row:
        Pr = jnp.pad(P[:, :, PROJ_RET:PROJ_RET + 4 * RET_W], ((0, 0), (0, RET_CHUNK - T), (0, 0)))
        o_ret, new_ret = _retention(Pr, 0, ret_s, pos0 + jnp.arange(RET_CHUNK, dtype=jnp.int32), p["ret_gn"],
                                    RET_CHUNK, T)
        o_ret = o_ret[:, :T]
    else:
        o_ret, new_ret = _retention(P, PROJ_RET // RET_W, ret_s, pos0 + jnp.arange(T, dtype=jnp.int32), p["ret_gn"],
                                    RET_TM, RET_TM)
    gdn_w = (p["gdn_conv_w"], p["gdn_A_log"], p["gdn_dt_bias"], p["gdn_norm"])
    if per_row:
        pad_rows = ((0, 0), (0, GDN_CHUNK - T), (0, 0))
        Pg = jnp.pad(P[:, :, PROJ_GQKV:PROJ_COLS], pad_rows)
        o_gdn, new_gdn = _gdn(Pg, 0, (PROJ_MISC - PROJ_GQKV) // LANES, jnp.pad(ga, pad_rows), gdn_buf, gdn_s, *gdn_w,
                              GDN_CHUNK, T)
        o_gdn = o_gdn[:, :T]
    else:
        o_gdn, new_gdn = _gdn(P, PROJ_GQKV // GDN_W, PROJ_MISC // LANES, ga, gdn_buf, gdn_s, *gdn_w,
                              GDN_RT, GDN_CHUNK)
    new_gdn_buf = jnp.concatenate([gdn_buf, P[:, :, PROJ_GQKV:PROJ_GQKV + 3 * GDN_W]], axis=1)[:, T:]
    branches = [o.reshape(M, BRANCH_W) for o in (o_nsa, o_conv, o_ret, o_gdn)]
    x1 = _merge(x2, norms[0], sc1, sh1, gt1, norms[1], branches, p["w_mg"], p["w_branch_b"], p["w_out_b"],
                T, M if per_row else MERGE_TM).reshape(B, T, D_MODEL)
    if per_row:
        x_out, new_ffn = _ffn_rows(x1, norms[2], sc2, sh2, gt2, norms[3], ffn_buf, p["ffn_up_b"], p["ffn_dw"],
                                   p["ffn_down_b"])
    else:
        x_out, new_ffn = _ffn(x1, norms[2], sc2, sh2, gt2, norms[3], ffn_buf, p["ffn_up_b"], p["ffn_dw"],
                              p["ffn_down_b"], FFN_TM)
    kv_rows = nkv[:, :, :4 * KVW].reshape(B, T, 4, NSA_KV_HEADS, HEAD_DIM)
    return x_out, (kv_rows, new_win, new_conv, new_ret, new_gdn_buf, new_gdn, new_ffn)


def kernel(x_prompt, x_sample, cache_nsa_kv, cache_nsa_win, state_conv, state_ret, state_gdn_conv, state_gdn,
           state_ffn_conv, page_table, c_prompt, c_sample, w_ada, b_ada, norms, w_in, cmp_pool, cmp_pe, rel_bias,
           conv_dw, conv_dw_b, conv_ln_g, conv_ln_b, ret_gn, gdn_conv_w, gdn_A_log, gdn_dt_bias, gdn_norm,
           w_branch, w_out, ffn_up, ffn_dw, ffn_down):
    B = x_prompt.shape[0]
    Bd = x_sample.shape[0]
    past = page_table.shape[1] * PAGE_SIZE
    mods = _adaln(jnp.concatenate([c_prompt, c_sample], axis=0), w_ada, b_ada)
    layer_w = {"norms": norms, "w_in": w_in, "cmp_pool": cmp_pool,
               "cmp_pe": cmp_pe, "conv_dw": conv_dw, "conv_dw_b": conv_dw_b, "conv_ln_g": conv_ln_g,
               "conv_ln_b": conv_ln_b, "ret_gn": ret_gn, "gdn_conv_w": gdn_conv_w, "gdn_A_log": gdn_A_log,
               "gdn_dt_bias": gdn_dt_bias, "gdn_norm": gdn_norm, "w_branch": w_branch, "w_out": w_out,
               "ffn_up": ffn_up, "ffn_dw": ffn_dw, "ffn_down": ffn_down}
    yp, ys = x_prompt, x_sample
    st_p, st_s = [], []
    for l in range(DEPTH):
        p = _prep_layer({name: w[l] for name, w in layer_w.items()})
        yp, sp = trunk_layer(
            yp, mods[l, :B], 0, None, None,
            jnp.zeros((B, CONV_WIDTH - 1, CONV_CH), x_prompt.dtype),
            jnp.zeros((B, RET_HEADS, HEAD_DIM, HEAD_DIM), jnp.float32),
            jnp.zeros((B, GDN_CONV - 1, 3 * GDN_W), x_prompt.dtype),
            jnp.zeros((B, GDN_HEADS, HEAD_DIM, HEAD_DIM), jnp.float32),
            jnp.zeros((B, FFN_CONV - 1, D_FF), x_prompt.dtype),
            p, rel_bias)
        ys, ss = trunk_layer(
            ys, mods[l, B:], past, (cache_nsa_kv, l, page_table), cache_nsa_win[l], state_conv[l], state_ret[l],
            state_gdn_conv[l], state_gdn[l], state_ffn_conv[l], p, rel_bias)
        st_p.append(sp)
        st_s.append(ss)

    def stack(outs, i, axis):
        return jnp.stack([o[i] for o in outs], axis=axis)

    kv_p, kv_s = stack(st_p, 0, 1), stack(st_s, 0, 1)
    win_p, win_s = stack(st_p, 1, 0), stack(st_s, 1, 0)
    conv_p, conv_s = stack(st_p, 2, 0), stack(st_s, 2, 0)
    ret_p, ret_s = stack(st_p, 3, 0), stack(st_s, 3, 0)
    gdnc_p, gdnc_s = stack(st_p, 4, 0), stack(st_s, 4, 0)
    gdn_p, gdn_s = stack(st_p, 5, 0), stack(st_s, 5, 0)
    ffn_p, ffn_s = stack(st_p, 6, 0), stack(st_s, 6, 0)
    return (yp, ys, kv_p, kv_s, win_p, win_s, conv_p, conv_s, ret_p, ret_s, gdnc_p, gdnc_s, gdn_p, gdn_s, ffn_p, ffn_s)
```

```python
import functools
import math

import jax
import jax.numpy as jnp
import numpy as np
from jax import lax
from jax.experimental import pallas as pl
from jax.experimental.pallas import tpu as pltpu

D_MODEL = 1024
DEPTH = 2
PAGE_SIZE = 128

HEAD_DIM = 64
NSA_HEADS = 4
NSA_KV_HEADS = 2
NSA_GROUP = NSA_HEADS // NSA_KV_HEADS
NSA_BLOCK = 64
NSA_TOPK = 16
NSA_WINDOW = 512
NUM_BUCKETS = 32
MAX_DISTANCE = 128
CONV_CH = D_MODEL // 4
CONV_WIDTH = 31
RET_HEADS = 4
RET_W = RET_HEADS * HEAD_DIM
RET_CHUNK = 64
ROPE_BASE = 10000.0
GDN_HEADS = 4
GDN_W = GDN_HEADS * HEAD_DIM
GDN_CONV = 4
GDN_CHUNK = 64
D_FF = 2816
FFN_CONV = 3
N_BRANCH = 4
BRANCH_W = NSA_HEADS * HEAD_DIM
EPS = 1e-6
NEG_INF = -1e30
IN_SPLITS = (NSA_HEADS * HEAD_DIM, 6 * NSA_KV_HEADS * HEAD_DIM, 3 * NSA_HEADS, 2 * CONV_CH, 4 * RET_W,
             4 * GDN_W + 2 * GDN_HEADS, N_BRANCH * D_MODEL)
IN_COLS = sum(IN_SPLITS)

LANES = 128
SUBLANES = 8
VMEM_LIMIT = 56 * 1024 * 1024


def _round_up(a, m):
    return -(-a // m) * m


ADA_TN = 2048


def _adaln_kernel(c_ref, w_ref, b_ref, o_ref):
    c = c_ref[...]
    act = (c * jax.nn.sigmoid(c)).astype(jnp.bfloat16)
    o_ref[0] = jnp.dot(act, w_ref[0].astype(jnp.bfloat16), preferred_element_type=jnp.float32) + b_ref[0]


def _adaln(c, w_ada, b_ada):
    R = c.shape[0]
    L, _, N = w_ada.shape
    Rp = _round_up(R, SUBLANES)
    cp = jnp.pad(c, ((0, Rp - R), (0, 0)))
    out = pl.pallas_call(
        _adaln_kernel,
        grid=(L, N // ADA_TN),
        in_specs=[pl.BlockSpec((Rp, D_MODEL), lambda l, j: (0, 0)),
                  pl.BlockSpec((1, D_MODEL, ADA_TN), lambda l, j: (l, 0, j)),
                  pl.BlockSpec((1, 1, ADA_TN), lambda l, j: (l, 0, j))],
        out_specs=pl.BlockSpec((1, Rp, ADA_TN), lambda l, j: (l, 0, j)),
        out_shape=jax.ShapeDtypeStruct((L, Rp, N), jnp.float32),
        compiler_params=pltpu.CompilerParams(dimension_semantics=("arbitrary", "arbitrary"),
                                             vmem_limit_bytes=VMEM_LIMIT),
        name="adaln",
    )(cp, w_ada, b_ada[:, None, :])
    return out[:, :R]


NSA_TQ = 128
BLOCKS_PER_TILE = NSA_TQ // NSA_BLOCK


def _t5_thresholds():
    n = np.arange(0, 2 * MAX_DISTANCE)
    exact = NUM_BUCKETS // 2
    large = exact + (np.log(np.maximum(n, 1).astype(np.float32) / np.float32(exact))
                     / np.float32(math.log(MAX_DISTANCE / exact)) * (NUM_BUCKETS - exact)).astype(np.int32)
    bucket = np.where(n < exact, n, np.minimum(large, NUM_BUCKETS - 1))
    return tuple(int(np.argmax(bucket >= k)) for k in range(1, NUM_BUCKETS))


_T5_THR = _t5_thresholds()


def _bias_from_dist(dist, rb_ref, h):
    v = jnp.full(dist.shape, rb_ref[NUM_BUCKETS - 1, h], jnp.float32)
    for k in range(NUM_BUCKETS - 2, -1, -1):
        v = jnp.where(dist < _T5_THR[k], rb_ref[k, h], v)
    return v


def _dot_nt(a, b):
    return lax.dot_general(a, b, (((1,), (1,)), ((), ())), preferred_element_type=jnp.float32)


def _flash_tile(carry, k_t, vt_t, qs, mask_add):
    m, l, acc = carry
    s = _dot_nt(k_t, qs) + mask_add
    m_new = jnp.maximum(m, jnp.max(s, axis=0, keepdims=True))
    p = jnp.exp(s - m_new)
    alpha = jnp.exp(m - m_new)
    l = alpha * l + jnp.sum(p, axis=0, keepdims=True)
    acc = alpha * acc + jnp.dot(vt_t, p.astype(jnp.bfloat16), preferred_element_type=jnp.float32)
    return m_new, l, acc


FAR_GROUP = 8
NEAR_ROWS = 2 * NSA_TQ
WIN_ROWS = NSA_WINDOW + NSA_TQ


def _nsa_prompt_kernel(rb_ref, q_ref, kv_ref, gt_ref, poolt_ref, pe_ref, o_ref,
                       kc_ref, vc_ref, ksel_ref, vselt_ref, kwin_ref, vwint_ref, near_ref, wtbl_ref, score_ref,
                       sel_ref):
    b = pl.program_id(0)
    qi = pl.program_id(1)
    T = kv_ref.shape[1]
    nb = T // NSA_BLOCK
    topk = min(NSA_TOPK, nb)
    TQ = NSA_TQ
    CH = 512
    f32, bf16 = jnp.float32, jnp.bfloat16
    kvs = range(NSA_KV_HEADS)

    @pl.when((b == 0) & (qi == 0))
    def _tables():
        for kvg in kvs:
            ksel_ref[kvg, 0:TQ, :] = jnp.zeros((TQ, HEAD_DIM), bf16)
            vselt_ref[kvg, :, 0:TQ] = jnp.zeros((HEAD_DIM, TQ), bf16)
            kwin_ref[kvg, 0:NSA_WINDOW, :] = jnp.zeros((NSA_WINDOW, HEAD_DIM), bf16)
            vwint_ref[kvg, :, 0:NSA_WINDOW] = jnp.zeros((HEAD_DIM, NSA_WINDOW), bf16)
        d_near = (lax.broadcasted_iota(jnp.int32, (NEAR_ROWS, TQ), 1) + TQ
                  - lax.broadcasted_iota(jnp.int32, (NEAR_ROWS, TQ), 0))
        d_win = (lax.broadcasted_iota(jnp.int32, (WIN_ROWS, TQ), 1) + NSA_WINDOW
                 - lax.broadcasted_iota(jnp.int32, (WIN_ROWS, TQ), 0))
        for h in range(NSA_HEADS):
            kvg, g = divmod(h, NSA_GROUP)
            lanes = slice(g * TQ, (g + 1) * TQ)
            near_ref[kvg, :, lanes] = jnp.where(
                d_near >= 0, _bias_from_dist(d_near, rb_ref, h) - rb_ref[NUM_BUCKETS - 1, h], NEG_INF)
            wtbl_ref[kvg, :, lanes] = jnp.where((d_win >= 0) & (d_win < NSA_WINDOW),
                                                _bias_from_dist(d_win, rb_ref, h), NEG_INF)

    @pl.when(qi == 0)
    def _prologue():
        def chunk(i, carry):
            r = pl.multiple_of(i * CH, CH)
            rs = pl.multiple_of(i * CH + TQ, TQ)
            rw = pl.multiple_of(i * CH + NSA_WINDOW, TQ)
            rb8 = pl.multiple_of(i * (CH // NSA_BLOCK), CH // NSA_BLOCK)
            for kvg in kvs:
                def col(c):
                    lo = c * NSA_KV_HEADS * HEAD_DIM + kvg * HEAD_DIM
                    return kv_ref[0, pl.ds(r, CH), lo:lo + HEAD_DIM]
                for c, dst in ((0, kc_ref), (1, vc_ref)):
                    x = col(c).reshape(CH // NSA_BLOCK, NSA_BLOCK, HEAD_DIM) + pe_ref[c][None]
                    dst[kvg, pl.ds(rb8, CH // NSA_BLOCK), :] = jnp.sum(x * poolt_ref[:, c:c + 1][None], axis=1)
                ksel_ref[kvg, pl.ds(rs, CH), :] = col(2).astype(bf16)
                vselt_ref[kvg, :, pl.ds(rs, CH)] = col(3).T.astype(bf16)
                kwin_ref[kvg, pl.ds(rw, CH), :] = col(4).astype(bf16)
                vwint_ref[kvg, :, pl.ds(rw, CH)] = col(5).T.astype(bf16)
            return carry
        lax.fori_loop(0, T // CH, chunk, 0)

    q = q_ref[0]
    gates = jax.nn.sigmoid(gt_ref[0])
    n_io = lax.broadcasted_iota(jnp.int32, (nb, TQ), 0)
    t_io = lax.broadcasted_iota(jnp.int32, (nb, TQ), 1)
    dist_c = qi * TQ + t_io - (n_io * NSA_BLOCK + NSA_BLOCK - 1)
    vis_c = dist_c >= 0
    vis_c2 = jnp.concatenate([vis_c, vis_c], axis=1)
    cur = (qi * TQ + t_io) // NSA_BLOCK
    forced = (n_io == 0) | (n_io == cur) | (n_io == cur - 1)
    q0 = pl.multiple_of(qi * TQ, TQ)

    qs, oc, score = [], [], []
    for kvg in kvs:
        base = kvg * NSA_GROUP * HEAD_DIM
        qk = jnp.concatenate([q[:, base + g * HEAD_DIM: base + (g + 1) * HEAD_DIM] for g in range(NSA_GROUP)], axis=0)
        qs.append((qk * HEAD_DIM ** -0.5).astype(bf16))
        sc = _dot_nt(kc_ref[kvg].astype(bf16), qs[kvg])
        bias_c = jnp.concatenate([_bias_from_dist(dist_c, rb_ref, kvg * NSA_GROUP + g) for g in range(NSA_GROUP)],
                                 axis=1)
        sc = jnp.where(vis_c2, sc + bias_c, NEG_INF)
        e = jnp.exp(sc - jnp.max(sc, axis=0, keepdims=True))
        p_c = e / jnp.sum(e, axis=0, keepdims=True) * vis_c2.astype(f32)
        oc.append(jnp.dot(vc_ref[kvg].T.astype(bf16), p_c.astype(bf16), preferred_element_type=f32))
        score.append(jnp.where(n_io <= cur, jnp.where(forced, 2.0, p_c[:, :TQ] + p_c[:, TQ:]), -1.0))
        score_ref[kvg] = score[kvg]

    def rank_body(mi, ranks):
        out = []
        for kvg in kvs:
            row = score_ref[kvg, pl.ds(mi, 1), :]
            beats = (row > score[kvg]) | ((row == score[kvg]) & (mi < n_io))
            out.append(ranks[kvg] + beats.astype(jnp.int32))
        return tuple(out)

    ranks = lax.fori_loop(0, BLOCKS_PER_TILE * (qi + 1), rank_body,
                          tuple(jnp.zeros((nb, TQ), jnp.int32) for _ in kvs))
    for kvg in kvs:
        sel_ref[kvg] = jnp.where((ranks[kvg] < topk) & (n_io <= cur), 0.0, NEG_INF)

    def mask_rows(kvg, blk0, nblk, limit):
        rows = []
        for u in range(nblk):
            blk = blk0 + u
            ok = (blk >= 0) & (blk < limit)
            row = sel_ref[kvg, pl.ds(jnp.clip(blk, 0, nb - 1), 1), :]
            rows.append(jnp.broadcast_to(jnp.where(ok, row, NEG_INF), (NSA_BLOCK, TQ)))
        mm = jnp.concatenate(rows, axis=0)
        return jnp.concatenate([mm, mm], axis=1)

    carries = []
    for kvg in kvs:
        k_n = ksel_ref[kvg, pl.ds(q0, NEAR_ROWS), :]
        vt_n = vselt_ref[kvg, :, pl.ds(q0, NEAR_ROWS)]
        s = (_dot_nt(k_n, qs[kvg]) + near_ref[kvg]
             + mask_rows(kvg, BLOCKS_PER_TILE * (qi - 1), 2 * BLOCKS_PER_TILE, nb))
        m = jnp.max(s, axis=0, keepdims=True)
        p = jnp.exp(s - m)
        carries.append((m, jnp.sum(p, axis=0, keepdims=True),
                        jnp.dot(vt_n, p.astype(bf16), preferred_element_type=f32)))

    n_far = jnp.maximum(qi - 1, 0)
    rows_far = FAR_GROUP * TQ

    def far_body(i, cs):
        out = []
        for kvg in kvs:
            r = pl.multiple_of(TQ + i * rows_far, TQ)
            mask = mask_rows(kvg, i * FAR_GROUP * BLOCKS_PER_TILE, FAR_GROUP * BLOCKS_PER_TILE,
                             n_far * BLOCKS_PER_TILE)
            out.append(_flash_tile(cs[kvg], ksel_ref[kvg, pl.ds(r, rows_far), :],
                                   vselt_ref[kvg, :, pl.ds(r, rows_far)], qs[kvg], mask))
        return tuple(out)

    carries = lax.fori_loop(0, (n_far + FAR_GROUP - 1) // FAR_GROUP, far_body, tuple(carries))

    w_io = lax.broadcasted_iota(jnp.int32, (WIN_ROWS, 2 * TQ), 0)
    outs = []
    for kvg in kvs:
        m_s, l_s, acc_s = carries[kvg]
        o_s = acc_s / l_s
        k_w = kwin_ref[kvg, pl.ds(q0, WIN_ROWS), :]
        vt_w = vwint_ref[kvg, :, pl.ds(q0, WIN_ROWS)]
        s = jnp.where(w_io >= NSA_WINDOW - q0, _dot_nt(k_w, qs[kvg]) + wtbl_ref[kvg], NEG_INF)
        e = jnp.exp(s - jnp.max(s, axis=0, keepdims=True))
        o_w = (jnp.dot(vt_w, e.astype(bf16), preferred_element_type=f32) / jnp.sum(e, axis=0, keepdims=True))
        for g in range(NSA_GROUP):
            h = kvg * NSA_GROUP + g
            lanes = slice(g * TQ, (g + 1) * TQ)
            o = (gates[h:h + 1] * oc[kvg][:, lanes] + gates[NSA_HEADS + h:NSA_HEADS + h + 1] * o_s[:, lanes]
                 + gates[2 * NSA_HEADS + h:2 * NSA_HEADS + h + 1] * o_w[:, lanes])
            outs.append(o.T)
    o_ref[0] = jnp.concatenate(outs, axis=1)


def _nsa_prompt(q, q_blk, kv, kv_blk, gate_logits, cmp_pool, cmp_pe, rel_bias):
    B, T, _ = q.shape
    nb = T // NSA_BLOCK
    assert T % 512 == 0 and nb % SUBLANES == 0 and (T // NSA_TQ) % FAR_GROUP == 0
    TQ = NSA_TQ
    gt = jnp.transpose(gate_logits, (0, 2, 1))
    f32, bf16 = jnp.float32, jnp.bfloat16
    return pl.pallas_call(
        _nsa_prompt_kernel,
        grid=(B, T // TQ),
        in_specs=[
            pl.BlockSpec(memory_space=pltpu.SMEM),
            pl.BlockSpec((1, TQ, NSA_HEADS * HEAD_DIM), lambda b, i: (b, i, q_blk)),
            pl.BlockSpec((1, T, 6 * NSA_KV_HEADS * HEAD_DIM), lambda b, i: (b, 0, kv_blk)),
            pl.BlockSpec((1, 3 * NSA_HEADS, TQ), lambda b, i: (b, 0, i)),
            pl.BlockSpec((NSA_BLOCK, 2), lambda b, i: (0, 0)),
            pl.BlockSpec((2, NSA_BLOCK, HEAD_DIM), lambda b, i: (0, 0, 0)),
        ],
        out_specs=pl.BlockSpec((1, TQ, NSA_HEADS * HEAD_DIM), lambda b, i: (b, i, 0)),
        out_shape=jax.ShapeDtypeStruct((B, T, NSA_HEADS * HEAD_DIM), f32),
        scratch_shapes=[
            pltpu.VMEM((NSA_KV_HEADS, nb, HEAD_DIM), f32),
            pltpu.VMEM((NSA_KV_HEADS, nb, HEAD_DIM), f32),
            pltpu.VMEM((NSA_KV_HEADS, T + TQ, HEAD_DIM), bf16),
            pltpu.VMEM((NSA_KV_HEADS, HEAD_DIM, T + TQ), bf16),
            pltpu.VMEM((NSA_KV_HEADS, T + NSA_WINDOW, HEAD_DIM), bf16),
            pltpu.VMEM((NSA_KV_HEADS, HEAD_DIM, T + NSA_WINDOW), bf16),
            pltpu.VMEM((NSA_KV_HEADS, NEAR_ROWS, NSA_GROUP * TQ), f32),
            pltpu.VMEM((NSA_KV_HEADS, WIN_ROWS, NSA_GROUP * TQ), f32),
            pltpu.VMEM((NSA_KV_HEADS, nb, TQ), f32),
            pltpu.VMEM((NSA_KV_HEADS, nb, TQ), f32),
        ],
        compiler_params=pltpu.CompilerParams(dimension_semantics=("arbitrary", "arbitrary"),
                                             vmem_limit_bytes=VMEM_LIMIT),
        name="nsa_prompt",
    )(rel_bias, q, kv, gt, jnp.transpose(cmp_pool), cmp_pe)


PAGES_PER_STEP = 32
SEL_TILE = 2048
KVW = NSA_KV_HEADS * HEAD_DIM


def _nsa_sample_kernel(pt_ref, rb_ref, *refs):
    PPS = PAGES_PER_STEP
    pages = refs[:PPS]
    q_ref, kvn_ref, gt_ref, win_ref, poolm_ref, cconst_ref, o_ref, cmp_ref, kselt_ref, vselt_ref = refs[PPS:]
    s_id = pl.program_id(1)
    f32, bf16 = jnp.float32, jnp.bfloat16
    Tn = q_ref.shape[1]
    past = kselt_ref.shape[1]
    nbp = past // NSA_BLOCK
    Wb = win_ref.shape[3]
    bpp = PAGE_SIZE // NSA_BLOCK
    R = NSA_HEADS * Tn

    rows = []
    for k in range(PPS):
        parts = []
        for c in range(2):
            a = _dot_nt(poolm_ref[c], pages[k][0, 0, c].astype(bf16))
            parts.append(a[0:bpp] + a[SUBLANES:SUBLANES + bpp] + cconst_ref[c:c + 1])
        rows.append(jnp.concatenate(parts, axis=1))
        r = pl.multiple_of((s_id * PPS + k) * PAGE_SIZE, PAGE_SIZE)
        kselt_ref[:, pl.ds(r, PAGE_SIZE)] = pages[k][0, 0, 2].astype(bf16)
        vselt_ref[:, pl.ds(r, PAGE_SIZE)] = pages[k][0, 0, 3].astype(bf16)
    cmp_ref[pl.ds(pl.multiple_of(s_id * PPS * bpp, PPS * bpp), PPS * bpp), :] = jnp.concatenate(rows, axis=0)

    @pl.when(s_id == pl.num_programs(1) - 1)
    def _attend():
        def per_head(fn):
            return jnp.concatenate([fn(h) for h in range(NSA_HEADS)], axis=0)

        q = q_ref[0] * HEAD_DIM ** -0.5
        zero = jnp.zeros((Tn, HEAD_DIM), f32)

        def q_rows(h):
            qh = q[:, h * HEAD_DIM:(h + 1) * HEAD_DIM]
            return jnp.concatenate([qh, zero] if h < NSA_GROUP else [zero, qh], axis=1)

        q2 = per_head(q_rows).astype(bf16)
        kvn = kvn_ref[0]
        pad = jnp.zeros((LANES - Tn, KVW), f32)

        def new_rows(c):
            return jnp.concatenate([kvn[:, c * KVW:(c + 1) * KVW], pad], axis=0).astype(bf16)

        tn_io = lax.broadcasted_iota(jnp.int32, (Tn, LANES), 0)
        jn_io = lax.broadcasted_iota(jnp.int32, (Tn, LANES), 1)
        d_new = tn_io - jn_io
        keep_new = per_head(lambda h: (d_new >= 0) & (jn_io < Tn))
        bias_new = per_head(lambda h: _bias_from_dist(d_new, rb_ref, h))
        far = per_head(lambda h: jnp.full((Tn, 1), rb_ref[NUM_BUCKETS - 1, h], f32))

        kc = cmp_ref[:, 0:KVW].astype(bf16)
        vc = cmp_ref[:, KVW:2 * KVW].astype(bf16)
        n_io = lax.broadcasted_iota(jnp.int32, (Tn, nbp), 1)
        t_io = lax.broadcasted_iota(jnp.int32, (Tn, nbp), 0)
        dist_c = past + t_io - (n_io * NSA_BLOCK + NSA_BLOCK - 1)
        sc = _dot_nt(q2, kc) + per_head(lambda h: _bias_from_dist(dist_c, rb_ref, h))
        e = jnp.exp(sc - jnp.max(sc, axis=1, keepdims=True))
        p_c = e / jnp.sum(e, axis=1, keepdims=True)
        oc = jnp.dot(p_c.astype(bf16), vc, preferred_element_type=f32)

        topk = min(NSA_TOPK, nbp + 1)
        m_io = lax.broadcasted_iota(jnp.int32, (nbp, nbp), 0)
        c_io = lax.broadcasted_iota(jnp.int32, (nbp, nbp), 1)
        lower = m_io < c_io
        forced = (n_io == 0) | (n_io == nbp - 1)
        sels = []
        for kvg in range(NSA_KV_HEADS):
            r0 = kvg * NSA_GROUP * Tn
            score = jnp.where(forced, 2.0, p_c[r0:r0 + Tn] + p_c[r0 + Tn:r0 + 2 * Tn])
            score_t = jnp.concatenate([score, jnp.zeros((LANES - Tn, nbp), f32)], axis=0).T
            ranks = []
            for t in range(Tn):
                colb = jnp.broadcast_to(score_t[:, t:t + 1], (nbp, nbp))
                rowb = jnp.broadcast_to(score[t:t + 1, :], (nbp, nbp))
                beats = (colb > rowb) | ((colb == rowb) & lower)
                ranks.append(jnp.sum(beats.astype(f32), axis=0, keepdims=True))
            sel = (jnp.concatenate(ranks, axis=0) < topk - 1).astype(f32)
            sels += [sel] * NSA_GROUP
        sel_rows = jnp.concatenate(sels, axis=0).astype(bf16)

        bpt = SEL_TILE // NSA_BLOCK
        expand = (lax.broadcasted_iota(jnp.int32, (bpt, SEL_TILE), 0)
                  == lax.broadcasted_iota(jnp.int32, (bpt, SEL_TILE), 1) // NSA_BLOCK).astype(bf16)
        d_last = LANES + tn_io - jn_io
        near = per_head(lambda h: _bias_from_dist(d_last, rb_ref, h))
        m = jnp.full((R, 1), NEG_INF, f32)
        l = jnp.zeros((R, 1), f32)
        acc = jnp.zeros((R, KVW), f32)

        def flash(carry, s, v_t, v_feature_major):
            m, l, acc = carry
            m_new = jnp.maximum(m, jnp.max(s, axis=1, keepdims=True))
            p = jnp.exp(s - m_new)
            alpha = jnp.exp(m - m_new)
            pb = p.astype(bf16)
            pv = _dot_nt(pb, v_t) if v_feature_major else jnp.dot(pb, v_t, preferred_element_type=f32)
            return m_new, alpha * l + jnp.sum(p, axis=1, keepdims=True), alpha * acc + pv

        carry = (m, l, acc)
        ntile = past // SEL_TILE
        for j in range(ntile):
            k_t = kselt_ref[:, j * SEL_TILE:(j + 1) * SEL_TILE]
            v_t = vselt_ref[:, j * SEL_TILE:(j + 1) * SEL_TILE]
            keep = jnp.dot(sel_rows[:, j * bpt:(j + 1) * bpt], expand, preferred_element_type=f32) > 0.5
            if j == ntile - 1:
                bias = jnp.concatenate([jnp.broadcast_to(far, (R, SEL_TILE - LANES)), near], axis=1)
            else:
                bias = far
            s_t = jnp.dot(q2, k_t, preferred_element_type=f32)
            carry = flash(carry, jnp.where(keep, s_t + bias, NEG_INF), v_t, True)
        s_new = jnp.where(keep_new, _dot_nt(q2, new_rows(2)) + bias_new, NEG_INF)
        m, l, acc = flash(carry, s_new, new_rows(3), False)
        o_s = acc / l

        tw_io = lax.broadcasted_iota(jnp.int32, (Tn, Wb), 0)
        cw_io = lax.broadcasted_iota(jnp.int32, (Tn, Wb), 1)
        d_w = Wb + tw_io - cw_io
        s_w = jnp.where(per_head(lambda h: d_w < NSA_WINDOW),
                        jnp.dot(q2, win_ref[0, 0].astype(bf16), preferred_element_type=f32)
                        + per_head(lambda h: _bias_from_dist(d_w, rb_ref, h)), NEG_INF)
        s_wn = jnp.where(keep_new, _dot_nt(q2, new_rows(4)) + bias_new, NEG_INF)
        s_all = jnp.concatenate([s_w, s_wn], axis=1)
        e = jnp.exp(s_all - jnp.max(s_all, axis=1, keepdims=True))
        p_w = (e / jnp.sum(e, axis=1, keepdims=True)).astype(bf16)
        o_w = _dot_nt(p_w[:, :Wb], win_ref[0, 1].astype(bf16)) + jnp.dot(p_w[:, Wb:], new_rows(5),
                                                                          preferred_element_type=f32)

        gates = jax.nn.sigmoid(gt_ref[0])
        outs = []
        for h in range(NSA_HEADS):
            rs = slice(h * Tn, (h + 1) * Tn)
            cs = slice((h // NSA_GROUP) * HEAD_DIM, (h // NSA_GROUP + 1) * HEAD_DIM)
            outs.append(gates[:, h:h + 1] * oc[rs, cs] + gates[:, NSA_HEADS + h:NSA_HEADS + h + 1] * o_s[rs, cs]
                        + gates[:, 2 * NSA_HEADS + h:2 * NSA_HEADS + h + 1] * o_w[rs, cs])
        o_ref[0] = jnp.concatenate(outs, axis=1)


def _nsa_sample(q, kv, gate_logits, cache, layer, page_table, win_buf, cmp_pool, cmp_pe, rel_bias):
    B, Tn, _ = q.shape
    npages = page_table.shape[1]
    past = npages * PAGE_SIZE
    Wb = win_buf.shape[1]
    PPS = PAGES_PER_STEP
    assert npages % PPS == 0 and past % SEL_TILE == 0 and Tn == SUBLANES and (past // NSA_BLOCK) % LANES == 0
    f32, bf16 = jnp.float32, jnp.bfloat16
    bpp = PAGE_SIZE // NSA_BLOCK
    cache_t = jnp.transpose(cache, (0, 1, 3, 4, 5, 2)).reshape(cache.shape[0], cache.shape[1], 4, KVW, PAGE_SIZE)
    win_t = jnp.transpose(win_buf, (0, 2, 3, 4, 1)).reshape(B, 2, KVW, Wb)
    r_io = np.arange(PAGE_SIZE)
    onehot = jnp.asarray((r_io[None, :] // NSA_BLOCK == np.arange(SUBLANES)[:, None]), f32)
    pool_full = onehot[None] * jnp.tile(cmp_pool, (1, bpp))[:, None, :]
    pool_hi = pool_full.astype(bf16)
    pool_lo = (pool_full - pool_hi.astype(f32)).astype(bf16)
    poolm = jnp.concatenate([pool_hi, pool_lo], axis=1)
    cconst = jnp.tile(jnp.sum(cmp_pool[:, :, None] * cmp_pe, axis=1), (1, NSA_KV_HEADS))

    def page_spec(k):
        return pl.BlockSpec((1, 1, 4, KVW, PAGE_SIZE), lambda b, s, pt: (pt[b, s * PPS + k], layer, 0, 0, 0))

    grid_spec = pltpu.PrefetchScalarGridSpec(
        num_scalar_prefetch=1,
        grid=(B, npages // PPS),
        in_specs=[pl.BlockSpec(memory_space=pltpu.SMEM)] + [page_spec(k) for k in range(PPS)] + [
            pl.BlockSpec((1, Tn, NSA_HEADS * HEAD_DIM), lambda b, s, pt: (b, 0, 0)),
            pl.BlockSpec((1, Tn, 6 * KVW), lambda b, s, pt: (b, 0, 0)),
            pl.BlockSpec((1, Tn, 3 * NSA_HEADS), lambda b, s, pt: (b, 0, 0)),
            pl.BlockSpec((1, 2, KVW, Wb), lambda b, s, pt: (b, 0, 0, 0)),
            pl.BlockSpec((2, 2 * SUBLANES, PAGE_SIZE), lambda b, s, pt: (0, 0, 0)),
            pl.BlockSpec((2, KVW), lambda b, s, pt: (0, 0)),
        ],
        out_specs=pl.BlockSpec((1, Tn, NSA_HEADS * HEAD_DIM), lambda b, s, pt: (b, 0, 0)),
        scratch_shapes=[
            pltpu.VMEM((past // NSA_BLOCK, 2 * KVW), f32),
            pltpu.VMEM((KVW, past), bf16),
            pltpu.VMEM((KVW, past), bf16),
        ],
    )
    return pl.pallas_call(
        _nsa_sample_kernel,
        grid_spec=grid_spec,
        out_shape=jax.ShapeDtypeStruct((B, Tn, NSA_HEADS * HEAD_DIM), f32),
        compiler_params=pltpu.CompilerParams(dimension_semantics=("arbitrary", "arbitrary"),
                                             vmem_limit_bytes=VMEM_LIMIT),
        name="nsa_sample",
    )(page_table, rel_bias, *([cache_t] * PPS), q, kv, gate_logits, win_t, poolm, cconst)


PROJ_KV, PROJ_Q, PROJ_UCV, PROJ_RET, PROJ_GQKV, PROJ_GZ, PROJ_MISC = 0, 768, 1024, 1536, 2560, 3328, 3584
PROJ_COLS = PROJ_MISC + LANES
PROJ_TM = 512
MERGE_TM = 256
FFN_TM = 256


def _prep_w_in(w_in):
    o = [int(v) for v in np.cumsum((0,) + IN_SPLITS)]
    small = o[5] + 4 * GDN_W
    pieces = [w_in[:, o[1]:o[2]], w_in[:, o[0]:o[1]], w_in[:, o[3]:o[4]], w_in[:, o[4]:o[5]], w_in[:, o[5]:small],
              w_in[:, o[2]:o[3]], w_in[:, small:o[6]]]
    used = sum(pc.shape[1] for pc in pieces)
    pieces.append(jnp.zeros((D_MODEL, PROJ_COLS - used), w_in.dtype))
    return jnp.concatenate(pieces, axis=1), w_in[:, o[6]:]


def _modulated_norm(x, g, sc, sh):
    y = x * lax.rsqrt(jnp.mean(x * x, axis=-1, keepdims=True) + EPS)
    return (y * g) * (1.0 + sc) + sh


def _resident(shape):
    return pl.BlockSpec(shape, lambda *_: (0,) * len(shape), pipeline_mode=pl.Buffered(1))


def _mod_spec(mod, rows_per_group, tm):
    if mod.shape[1] == 1:
        return pl.BlockSpec((1, 1, D_MODEL), lambda i, *_: (i // (rows_per_group // tm), 0, 0))
    return pl.BlockSpec((1, tm, D_MODEL), lambda i, *_: (0, i, 0))


def _proj_kernel(x_ref, g_ref, sc_ref, sh_ref, w_ref, o_ref, wb_ref):
    @pl.when(pl.program_id(0) == 0)
    def _():
        wb_ref[...] = w_ref[...].astype(jnp.bfloat16)

    h = _modulated_norm(x_ref[...], g_ref[...], sc_ref[0], sh_ref[0]).astype(jnp.bfloat16)
    o_ref[...] = jnp.dot(h, wb_ref[...], preferred_element_type=jnp.float32)


def _proj(x2, g, sc, sh, w, rows_per_group, tm):
    M = x2.shape[0]
    N = w.shape[1]
    return pl.pallas_call(
        _proj_kernel,
        grid=(M // tm,),
        in_specs=[pl.BlockSpec((tm, D_MODEL), lambda i: (i, 0)), _resident((1, D_MODEL)),
                  _mod_spec(sc, rows_per_group, tm), _mod_spec(sh, rows_per_group, tm), _resident((D_MODEL, N))],
        out_specs=pl.BlockSpec((tm, N), lambda i: (i, 0)),
        out_shape=jax.ShapeDtypeStruct((M, N), jnp.float32),
        scratch_shapes=[pltpu.VMEM((D_MODEL, N), jnp.bfloat16)],
        compiler_params=pltpu.CompilerParams(dimension_semantics=("arbitrary",), vmem_limit_bytes=VMEM_LIMIT),
        name="in_proj",
    )(x2, g, sc, sh, w)


def _merge_kernel(x_ref, g0_ref, sc_ref, sh_ref, gt_ref, g1_ref, b0_ref, b1_ref, b2_ref, b3_ref,
                  wmg_ref, wbr_ref, wout_ref, o_ref, wmgb_ref):
    f32, bf16 = jnp.float32, jnp.bfloat16

    @pl.when(pl.program_id(0) == 0)
    def _():
        wmgb_ref[...] = wmg_ref[...].astype(bf16)

    x = x_ref[...]
    h = _modulated_norm(x, g0_ref[...], sc_ref[0], sh_ref[0]).astype(bf16)
    acc = jnp.zeros(x.shape, f32)
    for n, b_ref in enumerate((b0_ref, b1_ref, b2_ref, b3_ref)):
        gate = jax.nn.sigmoid(jnp.dot(h, wmgb_ref[:, n * D_MODEL:(n + 1) * D_MODEL], preferred_element_type=f32))
        acc = acc + gate * jnp.dot(b_ref[...].astype(bf16), wbr_ref[n], preferred_element_type=f32)
    mixed = jnp.dot(acc.astype(bf16), wout_ref[...], preferred_element_type=f32)
    y = mixed * lax.rsqrt(jnp.mean(mixed * mixed, axis=-1, keepdims=True) + EPS) * g1_ref[...]
    o_ref[...] = x + gt_ref[0] * y


def _merge(x2, g0, sc, sh, gt, g1, branches, wmg, wbr, wout, rows_per_group, tm):
    M = x2.shape[0]
    row = lambda w: pl.BlockSpec((tm, w), lambda i: (i, 0))
    mspec = _mod_spec(sc, rows_per_group, tm)
    return pl.pallas_call(
        _merge_kernel,
        grid=(M // tm,),
        in_specs=[row(D_MODEL), _resident((1, D_MODEL)), mspec, mspec, mspec, _resident((1, D_MODEL))]
                 + [row(BRANCH_W)] * N_BRANCH
                 + [_resident(wmg.shape), _resident(wbr.shape), _resident(wout.shape)],
        out_specs=row(D_MODEL),
        out_shape=jax.ShapeDtypeStruct((M, D_MODEL), jnp.float32),
        scratch_shapes=[pltpu.VMEM(wmg.shape, jnp.bfloat16)],
        compiler_params=pltpu.CompilerParams(dimension_semantics=("arbitrary",), vmem_limit_bytes=VMEM_LIMIT),
        name="merge",
    )(x2, g0, sc, sh, gt, g1, *branches, wmg, wbr, wout)


def _ffn_kernel(x_ref, g2_ref, sc_ref, sh_ref, gt_ref, g3_ref, buf_ref, wup_ref, dw_ref, wdn_ref,
                o_ref, st_ref, gp_ref):
    f32, bf16 = jnp.float32, jnp.bfloat16
    tm = x_ref.shape[1]
    HALO = SUBLANES

    @pl.when(pl.program_id(1) == 0)
    def _():
        gp_ref[0:HALO, :] = buf_ref[0]

    x = x_ref[0]
    h = _modulated_norm(x, g2_ref[...], sc_ref[0], sh_ref[0]).astype(bf16)
    gp_ref[HALO:HALO + tm, :] = jnp.dot(h, wup_ref[:, 0:D_FF], preferred_element_type=f32)
    val = jnp.dot(h, wup_ref[:, D_FF:2 * D_FF], preferred_element_type=f32)
    gconv = (dw_ref[2:3, :] * gp_ref[HALO:HALO + tm, :] + dw_ref[1:2, :] * gp_ref[HALO - 1:HALO - 1 + tm, :]
             + dw_ref[0:1, :] * gp_ref[HALO - 2:HALO - 2 + tm, :])
    a = (jax.nn.gelu(gconv) * val).astype(bf16)
    f = jnp.dot(a, wdn_ref[...], preferred_element_type=f32)
    y = f * lax.rsqrt(jnp.mean(f * f, axis=-1, keepdims=True) + EPS) * g3_ref[...]
    o_ref[0] = x + gt_ref[0] * y
    tail = gp_ref[tm:tm + HALO, :]
    gp_ref[0:HALO, :] = tail
    st_ref[0] = tail


def _ffn(x3, g2, sc, sh, gt, g3, buf, wup, dw, wdn, tm):
    B, T, _ = x3.shape
    buf8 = jnp.pad(buf, ((0, 0), (SUBLANES - (FFN_CONV - 1), 0), (0, 0)))
    mspec = pl.BlockSpec((1, 1, D_MODEL), lambda b, i: (b, 0, 0))
    y, st = pl.pallas_call(
        _ffn_kernel,
        grid=(B, T // tm),
        in_specs=[pl.BlockSpec((1, tm, D_MODEL), lambda b, i: (b, i, 0)), _resident((1, D_MODEL)), mspec, mspec, mspec,
                  _resident((1, D_MODEL)), pl.BlockSpec((1, SUBLANES, D_FF), lambda b, i: (b, 0, 0)),
                  _resident(wup.shape), _resident(dw.shape), _resident(wdn.shape)],
        out_specs=[pl.BlockSpec((1, tm, D_MODEL), lambda b, i: (b, i, 0)),
                   pl.BlockSpec((1, SUBLANES, D_FF), lambda b, i: (b, 0, 0))],
        out_shape=[jax.ShapeDtypeStruct((B, T, D_MODEL), jnp.float32),
                   jax.ShapeDtypeStruct((B, SUBLANES, D_FF), jnp.float32)],
        scratch_shapes=[pltpu.VMEM((tm + SUBLANES, D_FF), jnp.float32)],
        compiler_params=pltpu.CompilerParams(dimension_semantics=("arbitrary", "arbitrary"),
                                             vmem_limit_bytes=VMEM_LIMIT),
        name="conv_ffn",
    )(x3, g2, sc, sh, gt, g3, buf8, wup, dw, wdn)
    return y, st[:, SUBLANES - (FFN_CONV - 1):]


def _ffn_rows_kernel(x_ref, g2_ref, sc_ref, sh_ref, gt_ref, g3_ref, p1_ref, p2_ref, wup_ref, dw_ref, wdn_ref,
                     o_ref, gpre_ref, gp_ref, *, seg):
    f32, bf16 = jnp.float32, jnp.bfloat16
    M = x_ref.shape[0]
    HALO = SUBLANES
    x = x_ref[...]
    h = _modulated_norm(x, g2_ref[...], sc_ref[0], sh_ref[0]).astype(bf16)
    gpre = jnp.dot(h, wup_ref[:, 0:D_FF], preferred_element_type=f32)
    val = jnp.dot(h, wup_ref[:, D_FF:2 * D_FF], preferred_element_type=f32)
    gp_ref[0:HALO, :] = jnp.zeros((HALO, D_FF), f32)
    gp_ref[HALO:HALO + M, :] = gpre
    t = lax.broadcasted_iota(jnp.int32, (M, 1), 0) % seg
    prev1 = jnp.where(t >= 1, gp_ref[HALO - 1:HALO - 1 + M, :], p1_ref[...])
    prev2 = jnp.where(t >= 2, gp_ref[HALO - 2:HALO - 2 + M, :], p2_ref[...])
    gconv = dw_ref[2:3, :] * gpre + dw_ref[1:2, :] * prev1 + dw_ref[0:1, :] * prev2
    a = (jax.nn.gelu(gconv) * val).astype(bf16)
    f = jnp.dot(a, wdn_ref[...], preferred_element_type=f32)
    y = f * lax.rsqrt(jnp.mean(f * f, axis=-1, keepdims=True) + EPS) * g3_ref[...]
    o_ref[...] = x + gt_ref[0] * y
    gpre_ref[...] = gpre


def _ffn_rows(x3, g2, sc, sh, gt, g3, buf, wup, dw, wdn):
    B, T, _ = x3.shape
    M = B * T
    assert FFN_CONV == 3 and T >= FFN_CONV - 1
    zeros = jnp.zeros((B, T, D_FF), jnp.float32)
    p1 = zeros.at[:, 0].set(buf[:, 1]).reshape(M, D_FF)
    p2 = zeros.at[:, 0].set(buf[:, 0]).at[:, 1].set(buf[:, 1]).reshape(M, D_FF)
    full = lambda shape: pl.BlockSpec(shape, lambda i: (0,) * len(shape))
    y, gpre = pl.pallas_call(
        functools.partial(_ffn_rows_kernel, seg=T),
        grid=(1,),
        in_specs=[full((M, D_MODEL)), full((1, D_MODEL)), full((1, M, D_MODEL)), full((1, M, D_MODEL)),
                  full((1, M, D_MODEL)), full((1, D_MODEL)), full((M, D_FF)), full((M, D_FF)),
                  _resident(wup.shape), _resident(dw.shape), _resident(wdn.shape)],
        out_specs=[full((M, D_MODEL)), full((M, D_FF))],
        out_shape=[jax.ShapeDtypeStruct((M, D_MODEL), jnp.float32), jax.ShapeDtypeStruct((M, D_FF), jnp.float32)],
        scratch_shapes=[pltpu.VMEM((M + SUBLANES, D_FF), jnp.float32)],
        compiler_params=pltpu.CompilerParams(dimension_semantics=("arbitrary",), vmem_limit_bytes=VMEM_LIMIT),
        name="conv_ffn_rows",
    )(x3.reshape(M, D_MODEL), g2, sc, sh, gt, g3, p1, p2, wup, dw, wdn)
    return y.reshape(B, T, D_MODEL), gpre.reshape(B, T, D_FF)[:, T - (FFN_CONV - 1):]


CONF_HALO = 32
CONF_TM = 1024


def _conformer_kernel(u_ref, buf_ref, dw_ref, dwb_ref, lng_ref, lnb_ref, o_ref, st_ref, xp_ref):
    tm = u_ref.shape[1]
    first = CONF_HALO - (CONV_WIDTH - 1)

    @pl.when(pl.program_id(1) == 0)
    def _():
        xp_ref[0:CONF_HALO, :] = buf_ref[0]

    u = u_ref[0]
    xp_ref[CONF_HALO:CONF_HALO + tm, :] = u[:, :CONV_CH] * jax.nn.sigmoid(u[:, CONV_CH:])
    acc = jnp.zeros((tm, CONV_CH), jnp.float32)
    for k in range(CONV_WIDTH):
        acc = acc + dw_ref[k:k + 1, :] * xp_ref[first + k:first + k + tm, :]
    y = acc + dwb_ref[...]
    mu = jnp.mean(y, axis=-1, keepdims=True)
    var = jnp.mean(jnp.square(y - mu), axis=-1, keepdims=True)
    yn = (y - mu) * lax.rsqrt(var + EPS) * lng_ref[...] + lnb_ref[...]
    o_ref[0] = yn * jax.nn.sigmoid(yn)
    tail = xp_ref[tm:tm + CONF_HALO, :]
    xp_ref[0:CONF_HALO, :] = tail
    st_ref[0] = tail


def _conformer(P, col_blk, buf, dw, dw_b, ln_g, ln_b, tm):
    B, T, _ = P.shape
    bufp = jnp.pad(buf, ((0, 0), (CONF_HALO - (CONV_WIDTH - 1), 0), (0, 0)))
    vec = lambda: _resident((1, CONV_CH))
    o, st = pl.pallas_call(
        _conformer_kernel,
        grid=(B, T // tm),
        in_specs=[pl.BlockSpec((1, tm, 2 * CONV_CH), lambda b, i: (b, i, col_blk)),
                  pl.BlockSpec((1, CONF_HALO, CONV_CH), lambda b, i: (b, 0, 0)),
                  _resident((CONV_WIDTH, CONV_CH)), vec(), vec(), vec()],
        out_specs=[pl.BlockSpec((1, tm, CONV_CH), lambda b, i: (b, i, 0)),
                   pl.BlockSpec((1, CONF_HALO, CONV_CH), lambda b, i: (b, 0, 0))],
        out_shape=[jax.ShapeDtypeStruct((B, T, CONV_CH), jnp.float32),
                   jax.ShapeDtypeStruct((B, CONF_HALO, CONV_CH), jnp.float32)],
        scratch_shapes=[pltpu.VMEM((tm + CONF_HALO, CONV_CH), jnp.float32)],
        compiler_params=pltpu.CompilerParams(dimension_semantics=("arbitrary", "arbitrary"),
                                             vmem_limit_bytes=VMEM_LIMIT),
        name="conformer",
    )(P, bufp, dw, dw_b[None], ln_g[None], ln_b[None])
    return o, st[:, CONF_HALO - (CONV_WIDTH - 1):]


RET_LOG_GAMMA = tuple(math.log1p(-2.0 ** (-5 - h)) for h in range(RET_HEADS))
RET_TM = 256


def _retention_kernel(q_ref, k_ref, v_ref, g_ref, cos_ref, sin_ref, s0_ref, gn_ref, o_ref, sfin_ref,
                      s_ref, dec_ref, *, n_valid):
    f32, bf16 = jnp.float32, jnp.bfloat16
    C = q_ref.shape[1]
    W = RET_W
    half = HEAD_DIM // 2

    @pl.when(pl.program_id(1) == 0)
    def _():
        s_ref[...] = s0_ref[0]

    @pl.when((pl.program_id(0) == 0) & (pl.program_id(1) == 0))
    def _():
        ii = lax.broadcasted_iota(jnp.int32, (C, C), 0)
        jj = lax.broadcasted_iota(jnp.int32, (C, C), 1)
        d = (ii - jj).astype(f32)
        for h in range(RET_HEADS):
            dec_ref[h] = jnp.where((ii >= jj) & (jj < n_valid), jnp.exp(jnp.maximum(d, 0.0) * RET_LOG_GAMMA[h]), 0.0)

    lane = lax.broadcasted_iota(jnp.int32, (C, W), 1)
    low = lane % HEAD_DIM < half
    cos = cos_ref[...]
    sin = sin_ref[...]

    def rot(x):
        other = jnp.where(low, pltpu.roll(x, W - half, axis=1), pltpu.roll(x, half, axis=1))
        return x * cos + other * sin

    q = rot(q_ref[0])
    k = rot(k_ref[0]) * HEAD_DIM ** -0.5
    v = v_ref[0]
    row = lax.broadcasted_iota(jnp.int32, (C, HEAD_DIM), 0)
    rowf = row.astype(f32)
    valid = row < n_valid
    outs = []
    for h in range(RET_HEADS):
        cs = slice(h * HEAD_DIM, (h + 1) * HEAD_DIM)
        lg = RET_LOG_GAMMA[h]
        qh, kh, vh = q[:, cs], k[:, cs], v[:, cs].astype(bf16)
        s_old = s_ref[h]
        att = _dot_nt(qh.astype(bf16), kh.astype(bf16)) * dec_ref[h]
        o = (jnp.dot(att.astype(bf16), vh, preferred_element_type=f32)
             + jnp.dot((qh * jnp.exp((rowf + 1.0) * lg)).astype(bf16), s_old.astype(bf16), preferred_element_type=f32))
        kz = jnp.where(valid, kh * jnp.exp((n_valid - 1.0 - rowf) * lg), 0.0)
        s_ref[h] = s_old * math.exp(n_valid * lg) + jnp.dot(kz.T.astype(bf16), vh, preferred_element_type=f32)
        mu = jnp.mean(o, axis=-1, keepdims=True)
        var = jnp.mean(jnp.square(o - mu), axis=-1, keepdims=True)
        outs.append((o - mu) * lax.rsqrt(var + EPS))
    g = g_ref[0]
    o_ref[0] = jnp.concatenate(outs, axis=1) * gn_ref[...] * (g * jax.nn.sigmoid(g))

    @pl.when(pl.program_id(1) == pl.num_programs(1) - 1)
    def _():
        sfin_ref[0] = s_ref[...]


def _retention(P, col_blk0, S0, pos, gn, chunk, n_valid):
    B, T, _ = P.shape
    half = HEAD_DIM // 2
    inv = ROPE_BASE ** (-jnp.arange(half, dtype=jnp.float32) / half)
    ang = pos.astype(jnp.float32)[:, None] * inv[None, :]
    cos, sin = jnp.cos(ang), jnp.sin(ang)
    cosf = jnp.tile(jnp.concatenate([cos, cos], axis=1), (1, RET_HEADS))
    sinf = jnp.tile(jnp.concatenate([-sin, sin], axis=1), (1, RET_HEADS))
    col = lambda j: pl.BlockSpec((1, chunk, RET_W), lambda b, i: (b, i, col_blk0 + j))
    tab = pl.BlockSpec((chunk, RET_W), lambda b, i: (i, 0))
    state = pl.BlockSpec((1, RET_HEADS, HEAD_DIM, HEAD_DIM), lambda b, i: (b, 0, 0, 0))
    return pl.pallas_call(
        functools.partial(_retention_kernel, n_valid=n_valid),
        grid=(B, T // chunk),
        in_specs=[col(0), col(1), col(2), col(3), tab, tab, state, _resident((1, RET_W))],
        out_specs=[pl.BlockSpec((1, chunk, RET_W), lambda b, i: (b, i, 0)), state],
        out_shape=[jax.ShapeDtypeStruct((B, T, RET_W), jnp.float32),
                   jax.ShapeDtypeStruct((B, RET_HEADS, HEAD_DIM, HEAD_DIM), jnp.float32)],
        scratch_shapes=[pltpu.VMEM((RET_HEADS, HEAD_DIM, HEAD_DIM), jnp.float32),
                        pltpu.VMEM((RET_HEADS, chunk, chunk), jnp.float32)],
        compiler_params=pltpu.CompilerParams(dimension_semantics=("arbitrary", "arbitrary"),
                                             vmem_limit_bytes=VMEM_LIMIT),
        name="retention",
    )(P, P, P, P, cosf, sinf, S0, gn[None])


GDN_RT = 4 * GDN_CHUNK
GDN_HALO = SUBLANES
GDN_A_LANE = 3 * NSA_HEADS
GDN_B_LANE = GDN_A_LANE + GDN_HEADS


def _mm1(a, b):
    return jnp.dot(a.astype(jnp.bfloat16), b.astype(jnp.bfloat16), preferred_element_type=jnp.float32)


def _mm3(a, b):
    f32, bf16 = jnp.float32, jnp.bfloat16
    ah, bh = a.astype(bf16), b.astype(bf16)
    al, bl = (a - ah.astype(f32)).astype(bf16), (b - bh.astype(f32)).astype(bf16)
    d = lambda x, y: jnp.dot(x, y, preferred_element_type=f32)
    return d(ah, bh) + (d(ah, bl) + d(al, bh))


def _segment_cumsum(x, axis, seg):
    idx = lax.broadcasted_iota(jnp.int32, x.shape, axis) % seg
    s = 1
    while s < seg:
        x = x + jnp.where(idx >= s, pltpu.roll(x, s, axis=axis), 0.0)
        s *= 2
    return x


def _gdn_kernel(q_ref, k_ref, v_ref, z_ref, ab_ref, abt_ref, buf_ref, s0_ref, cw_ref, ng_ref, alane_ref, dlane_ref,
                acol_ref, dcol_ref, o_ref, sfin_ref, s_ref, xp_ref, *, n_valid):
    f32 = jnp.float32
    RT = q_ref.shape[1]
    C = GDN_CHUNK
    W = GDN_W
    first = GDN_HALO - (GDN_CONV - 1)

    @pl.when(pl.program_id(1) == 0)
    def _():
        s_ref[...] = s0_ref[0]
        xp_ref[0:GDN_HALO, :] = buf_ref[0]

    xp_ref[GDN_HALO:GDN_HALO + RT, 0:W] = q_ref[0]
    xp_ref[GDN_HALO:GDN_HALO + RT, W:2 * W] = k_ref[0]
    xp_ref[GDN_HALO:GDN_HALO + RT, 2 * W:3 * W] = v_ref[0]
    y = jnp.zeros((RT, 3 * W), f32)
    for t in range(GDN_CONV):
        y = y + cw_ref[t:t + 1, :] * xp_ref[first + t:first + t + RT, :]
    y = y * jax.nn.sigmoid(y)
    xp_ref[0:GDN_HALO, :] = xp_ref[RT:RT + GDN_HALO, :]

    ab = ab_ref[0]
    g_lanes = -jnp.exp(alane_ref[...]) * jax.nn.softplus(ab + dlane_ref[...])
    beta_lanes = jax.nn.sigmoid(ab)
    g_rows = -jnp.exp(acol_ref[...]) * jax.nn.softplus(abt_ref[0] + dcol_ref[...])
    if n_valid < C:
        g_lanes = jnp.where(lax.broadcasted_iota(jnp.int32, g_lanes.shape, 0) % C < n_valid, g_lanes, 0.0)
        beta_lanes = jnp.where(lax.broadcasted_iota(jnp.int32, g_lanes.shape, 0) % C < n_valid, beta_lanes, 0.0)
        g_rows = jnp.where(lax.broadcasted_iota(jnp.int32, g_rows.shape, 1) % C < n_valid, g_rows, 0.0)
    gc_lanes = _segment_cumsum(g_lanes, 0, C)
    gc_rows = _segment_cumsum(g_rows, 1, C)

    ii = lax.broadcasted_iota(jnp.int32, (C, C), 0)
    jj = lax.broadcasted_iota(jnp.int32, (C, C), 1)
    tri = ii >= jj
    strict = ii > jj
    eye = (ii == jj).astype(f32)
    z = z_ref[0]
    bf16 = jnp.bfloat16
    nch = RT // C
    pairs = [(c, h) for c in range(nch) for h in range(GDN_HEADS)]
    qs, ks, ms, atts, rhss, gcs = {}, {}, {}, {}, {}, {}
    for c, h in pairs:
        rs = slice(c * C, (c + 1) * C)
        qh = y[rs, h * HEAD_DIM:(h + 1) * HEAD_DIM]
        kh = y[rs, W + h * HEAD_DIM:W + (h + 1) * HEAD_DIM]
        vh = y[rs, 2 * W + h * HEAD_DIM:2 * W + (h + 1) * HEAD_DIM]
        qh = qh * lax.rsqrt(jnp.sum(qh * qh, axis=-1, keepdims=True) + EPS) * HEAD_DIM ** -0.5
        kh = kh * lax.rsqrt(jnp.sum(kh * kh, axis=-1, keepdims=True) + EPS)
        gc_col = gc_lanes[rs, GDN_A_LANE + h:GDN_A_LANE + h + 1]
        beta = beta_lanes[rs, GDN_B_LANE + h:GDN_B_LANE + h + 1]
        gc_row = gc_rows[h:h + 1, c * C:(c + 1) * C]
        e_col = jnp.exp(gc_col)
        lm = jnp.exp(jnp.where(tri, gc_col - gc_row, NEG_INF))
        kb = kh * beta
        ms[c, h] = jnp.where(strict, _dot_nt(kb.astype(bf16), kh.astype(bf16)) * lm, 0.0)
        atts[c, h] = _dot_nt(qh.astype(bf16), kh.astype(bf16)) * lm
        rhss[c, h] = jnp.concatenate([vh * beta, kb * e_col], axis=1)
        qs[c, h], ks[c, h], gcs[c, h] = qh * e_col, kh, gc_col
    pw = {p: -ms[p] for p in pairs}
    xs = {p: eye + pw[p] for p in pairs}
    for _ in range(int(math.log2(C)) - 1):
        pw = {p: _mm1(pw[p], pw[p]) for p in pairs}
        xs = {p: xs[p] + _mm1(xs[p], pw[p]) for p in pairs}
    res = {p: eye - (xs[p] + _mm3(ms[p], xs[p])) for p in pairs}
    xs = {p: xs[p] + _mm1(xs[p], res[p]) for p in pairs}
    sols = {p: _mm3(xs[p], rhss[p]) for p in pairs}
    state = [s_ref[h] for h in range(GDN_HEADS)]
    outs = {}
    for c in range(nch):
        heads = range(GDN_HEADS)
        g_last = [gcs[c, h][C - 1:C, :] for h in heads]
        vn = [sols[c, h][:, :HEAD_DIM] - _mm1(sols[c, h][:, HEAD_DIM:], state[h]) for h in heads]
        o_in = [_mm1(qs[c, h], state[h]) for h in heads]
        kd_t = [(ks[c, h] * jnp.exp(g_last[h] - gcs[c, h])).T for h in heads]
        for h in heads:
            outs[c, h] = o_in[h] + _mm1(atts[c, h], vn[h])
        state = [state[h] * jnp.exp(g_last[h]) + _mm1(kd_t[h], vn[h]) for h in heads]
    for h in range(GDN_HEADS):
        s_ref[h] = state[h]
    row_outs = []
    for c in range(nch):
        head_outs = []
        for h in range(GDN_HEADS):
            o = outs[c, h]
            o = o * lax.rsqrt(jnp.mean(o * o, axis=-1, keepdims=True) + EPS) * ng_ref[...]
            zh = z[c * C:(c + 1) * C, h * HEAD_DIM:(h + 1) * HEAD_DIM]
            head_outs.append(o * (zh * jax.nn.sigmoid(zh)))
        row_outs.append(jnp.concatenate(head_outs, axis=1))
    o_ref[0] = jnp.concatenate(row_outs, axis=0)

    @pl.when(pl.program_id(1) == pl.num_programs(1) - 1)
    def _():
        sfin_ref[0] = s_ref[...]


def _gdn(P, blk_q, blk_misc, ga, conv_buf, S0, conv_w, A_log, dt_bias, norm_g, rt, n_valid):
    B, T, _ = P.shape
    f32 = jnp.float32
    abt = jnp.pad(jnp.transpose(ga, (0, 2, 1)), ((0, 0), (0, SUBLANES - GDN_HEADS), (0, 0)))
    buf8 = jnp.pad(conv_buf, ((0, 0), (GDN_HALO - (GDN_CONV - 1), 0), (0, 0)))
    lane_vec = lambda v: jnp.zeros((1, LANES), f32).at[0, GDN_A_LANE:GDN_A_LANE + GDN_HEADS].set(v)
    col_vec = lambda v: jnp.zeros((SUBLANES, 1), f32).at[0:GDN_HEADS, 0].set(v)
    col = lambda j: pl.BlockSpec((1, rt, GDN_W), lambda b, i: (b, i, blk_q + j))
    state = pl.BlockSpec((1, GDN_HEADS, HEAD_DIM, HEAD_DIM), lambda b, i: (b, 0, 0, 0))
    return pl.pallas_call(
        functools.partial(_gdn_kernel, n_valid=n_valid),
        grid=(B, T // rt),
        in_specs=[col(0), col(1), col(2), col(3),
                  pl.BlockSpec((1, rt, LANES), lambda b, i: (b, i, blk_misc)),
                  pl.BlockSpec((1, SUBLANES, rt), lambda b, i: (b, 0, i)),
                  pl.BlockSpec((1, GDN_HALO, 3 * GDN_W), lambda b, i: (b, 0, 0)), state,
                  _resident((GDN_CONV, 3 * GDN_W)), _resident((1, HEAD_DIM)),
                  _resident((1, LANES)), _resident((1, LANES)), _resident((SUBLANES, 1)), _resident((SUBLANES, 1))],
        out_specs=[pl.BlockSpec((1, rt, GDN_W), lambda b, i: (b, i, 0)), state],
        out_shape=[jax.ShapeDtypeStruct((B, T, GDN_W), f32),
                   jax.ShapeDtypeStruct((B, GDN_HEADS, HEAD_DIM, HEAD_DIM), f32)],
        scratch_shapes=[pltpu.VMEM((GDN_HEADS, HEAD_DIM, HEAD_DIM), f32),
                        pltpu.VMEM((rt + GDN_HALO, 3 * GDN_W), f32)],
        compiler_params=pltpu.CompilerParams(dimension_semantics=("arbitrary", "arbitrary"),
                                             vmem_limit_bytes=VMEM_LIMIT),
        name="gated_deltanet",
    )(P, P, P, P, P, abt, buf8, S0, conv_w, norm_g[None], lane_vec(A_log), lane_vec(dt_bias),
      col_vec(A_log), col_vec(dt_bias))


def _prep_layer(p):
    w_main, w_mg = _prep_w_in(p["w_in"])
    q = dict(p)
    q.update(w_main=w_main, w_mg=w_mg, w_branch_b=p["w_branch"].astype(jnp.bfloat16),
             w_out_b=p["w_out"].astype(jnp.bfloat16), ffn_up_b=p["ffn_up"].astype(jnp.bfloat16),
             ffn_down_b=p["ffn_down"].astype(jnp.bfloat16))
    return q


def trunk_layer(x, mod, pos0, nsa_past, win_buf, conv_buf, ret_s, gdn_buf, gdn_s, ffn_buf, p, rel_bias):
    B, T, _ = x.shape
    M = B * T
    per_row = T < MERGE_TM
    if per_row:
        sh1, sc1, gt1, sh2, sc2, gt2 = [jnp.repeat(m, T, axis=0)[None] for m in jnp.split(mod, 6, axis=-1)]
    else:
        sh1, sc1, gt1, sh2, sc2, gt2 = [m[:, None, :] for m in jnp.split(mod, 6, axis=-1)]
    norms = p["norms"][:, None, :]
    x2 = x.reshape(M, D_MODEL)
    P = _proj(x2, norms[0], sc1, sh1, p["w_main"], T, M if per_row else PROJ_TM).reshape(B, T, PROJ_COLS)
    nkv = P[:, :, PROJ_KV:PROJ_KV + 6 * KVW]
    ngt = P[:, :, PROJ_MISC:PROJ_MISC + 3 * NSA_HEADS]
    ga = P[:, :, PROJ_MISC + 3 * NSA_HEADS:PROJ_MISC + 3 * NSA_HEADS + GDN_HEADS]
    gb = P[:, :, PROJ_MISC + 3 * NSA_HEADS + GDN_HEADS:PROJ_MISC + 3 * NSA_HEADS + 2 * GDN_HEADS]
    keep = min(NSA_WINDOW, T)
    kw = P[:, T - keep:, PROJ_KV + 4 * KVW:PROJ_KV + 6 * KVW].reshape(B, keep, 2, NSA_KV_HEADS, HEAD_DIM)
    if nsa_past is None:
        o_nsa = _nsa_prompt(P, PROJ_Q // (NSA_HEADS * HEAD_DIM), P, PROJ_KV // (6 * KVW), ngt,
                            p["cmp_pool"], p["cmp_pe"], rel_bias)
        new_win = kw
    else:
        cache, layer, page_table = nsa_past
        o_nsa = _nsa_sample(P[:, :, PROJ_Q:PROJ_Q + NSA_HEADS * HEAD_DIM], nkv, ngt, cache, layer, page_table,
                            win_buf, p["cmp_pool"], p["cmp_pe"], rel_bias)
        real = jnp.concatenate([win_buf, kw], axis=1)
        new_win = real[:, real.shape[1] - min(NSA_WINDOW, real.shape[1]):]
    o_conv, new_conv = _conformer(P, PROJ_UCV // (2 * CONV_CH), conv_buf, p["conv_dw"], p["conv_dw_b"],
                                  p["conv_ln_g"], p["conv_ln_b"], T if per_row else CONF_TM)
    if per_row:
        Pr = jnp.pad(P[:, :, PROJ_RET:PROJ_RET + 4 * RET_W], ((0, 0), (0, RET_CHUNK - T), (0, 0)))
        o_ret, new_ret = _retention(Pr, 0, ret_s, pos0 + jnp.arange(RET_CHUNK, dtype=jnp.int32), p["ret_gn"],
                                    RET_CHUNK, T)
        o_ret = o_ret[:, :T]
    else:
        o_ret, new_ret = _retention(P, PROJ_RET // RET_W, ret_s, pos0 + jnp.arange(T, dtype=jnp.int32), p["ret_gn"],
                                    RET_TM, RET_TM)
    gdn_w = (p["gdn_conv_w"], p["gdn_A_log"], p["gdn_dt_bias"], p["gdn_norm"])
    if per_row:
        pad_rows = ((0, 0), (0, GDN_CHUNK - T), (0, 0))
        Pg = jnp.pad(P[:, :, PROJ_GQKV:PROJ_COLS], pad_rows)
        o_gdn, new_gdn = _gdn(Pg, 0, (PROJ_MISC - PROJ_GQKV) // LANES, jnp.pad(ga, pad_rows), gdn_buf, gdn_s, *gdn_w,
                              GDN_CHUNK, T)
        o_gdn = o_gdn[:, :T]
    else:
        o_gdn, new_gdn = _gdn(P, PROJ_GQKV // GDN_W, PROJ_MISC // LANES, ga, gdn_buf, gdn_s, *gdn_w,
                              GDN_RT, GDN_CHUNK)
    new_gdn_buf = jnp.concatenate([gdn_buf, P[:, :, PROJ_GQKV:PROJ_GQKV + 3 * GDN_W]], axis=1)[:, T:]
    branches = [o.reshape(M, BRANCH_W) for o in (o_nsa, o_conv, o_ret, o_gdn)]
    x1 = _merge(x2, norms[0], sc1, sh1, gt1, norms[1], branches, p["w_mg"], p["w_branch_b"], p["w_out_b"],
                T, M if per_row else MERGE_TM).reshape(B, T, D_MODEL)
    if per_row:
        x_out, new_ffn = _ffn_rows(x1, norms[2], sc2, sh2, gt2, norms[3], ffn_buf, p["ffn_up_b"], p["ffn_dw"],
                                   p["ffn_down_b"])
    else:
        x_out, new_ffn = _ffn(x1, norms[2], sc2, sh2, gt2, norms[3], ffn_buf, p["ffn_up_b"], p["ffn_dw"],
                              p["ffn_down_b"], FFN_TM)
    kv_rows = nkv[:, :, :4 * KVW].reshape(B, T, 4, NSA_KV_HEADS, HEAD_DIM)
    return x_out, (kv_rows, new_win, new_conv, new_ret, new_gdn_buf, new_gdn, new_ffn)


def kernel(x_prompt, x_sample, cache_nsa_kv, cache_nsa_win, state_conv, state_ret, state_gdn_conv, state_gdn,
           state_ffn_conv, page_table, c_prompt, c_sample, w_ada, b_ada, norms, w_in, cmp_pool, cmp_pe, rel_bias,
           conv_dw, conv_dw_b, conv_ln_g, conv_ln_b, ret_gn, gdn_conv_w, gdn_A_log, gdn_dt_bias, gdn_norm,
           w_branch, w_out, ffn_up, ffn_dw, ffn_down):
    B = x_prompt.shape[0]
    Bd = x_sample.shape[0]
    past = page_table.shape[1] * PAGE_SIZE
    mods = _adaln(jnp.concatenate([c_prompt, c_sample], axis=0), w_ada, b_ada)
    layer_w = {"norms": norms, "w_in": w_in, "cmp_pool": cmp_pool,
               "cmp_pe": cmp_pe, "conv_dw": conv_dw, "conv_dw_b": conv_dw_b, "conv_ln_g": conv_ln_g,
               "conv_ln_b": conv_ln_b, "ret_gn": ret_gn, "gdn_conv_w": gdn_conv_w, "gdn_A_log": gdn_A_log,
               "gdn_dt_bias": gdn_dt_bias, "gdn_norm": gdn_norm, "w_branch": w_branch, "w_out": w_out,
               "ffn_up": ffn_up, "ffn_dw": ffn_dw, "ffn_down": ffn_down}
    yp, ys = x_prompt, x_sample
    st_p, st_s = [], []
    for l in range(DEPTH):
        p = _prep_layer({name: w[l] for name, w in layer_w.items()})
        yp, sp = trunk_layer(
            yp, mods[l, :B], 0, None, None,
            jnp.zeros((B, CONV_WIDTH - 1, CONV_CH), x_prompt.dtype),
            jnp.zeros((B, RET_HEADS, HEAD_DIM, HEAD_DIM), jnp.float32),
            jnp.zeros((B, GDN_CONV - 1, 3 * GDN_W), x_prompt.dtype),
            jnp.zeros((B, GDN_HEADS, HEAD_DIM, HEAD_DIM), jnp.float32),
            jnp.zeros((B, FFN_CONV - 1, D_FF), x_prompt.dtype),
            p, rel_bias)
        ys, ss = trunk_layer(
            ys, mods[l, B:], past, (cache_nsa_kv, l, page_table), cache_nsa_win[l], state_conv[l], state_ret[l],
            state_gdn_conv[l], state_gdn[l], state_ffn_conv[l], p, rel_bias)
        st_p.append(sp)
        st_s.append(ss)

    def stack(outs, i, axis):
        return jnp.stack([o[i] for o in outs], axis=axis)

    kv_p, kv_s = stack(st_p, 0, 1), stack(st_s, 0, 1)
    win_p, win_s = stack(st_p, 1, 0), stack(st_s, 1, 0)
    conv_p, conv_s = stack(st_p, 2, 0), stack(st_s, 2, 0)
    ret_p, ret_s = stack(st_p, 3, 0), stack(st_s, 3, 0)
    gdnc_p, gdnc_s = stack(st_p, 4, 0), stack(st_s, 4, 0)
    gdn_p, gdn_s = stack(st_p, 5, 0), stack(st_s, 5, 0)
    ffn_p, ffn_s = stack(st_p, 6, 0), stack(st_s, 6, 0)
    return (yp, ys, kv_p, kv_s, win_p, win_s, conv_p, conv_s, ret_p, ret_s, gdnc_p, gdnc_s, gdn_p, gdn_s, ffn_p, ffn_s)
```

```python
import functools
import math

import jax
import jax.numpy as jnp
import numpy as np
from jax import lax
from jax.experimental import pallas as pl
from jax.experimental.pallas import tpu as pltpu

D_MODEL = 1024
DEPTH = 2
PAGE_SIZE = 128

HEAD_DIM = 64
NSA_HEADS = 4
NSA_KV_HEADS = 2
NSA_GROUP = NSA_HEADS // NSA_KV_HEADS
NSA_BLOCK = 64
NSA_TOPK = 16
NSA_WINDOW = 512
NUM_BUCKETS = 32
MAX_DISTANCE = 128
CONV_CH = D_MODEL // 4
CONV_WIDTH = 31
RET_HEADS = 4
RET_W = RET_HEADS * HEAD_DIM
RET_CHUNK = 64
ROPE_BASE = 10000.0
GDN_HEADS = 4
GDN_W = GDN_HEADS * HEAD_DIM
GDN_CONV = 4
GDN_CHUNK = 64
D_FF = 2816
FFN_CONV = 3
N_BRANCH = 4
BRANCH_W = NSA_HEADS * HEAD_DIM
EPS = 1e-6
NEG_INF = -1e30
IN_SPLITS = (NSA_HEADS * HEAD_DIM, 6 * NSA_KV_HEADS * HEAD_DIM, 3 * NSA_HEADS, 2 * CONV_CH, 4 * RET_W,
             4 * GDN_W + 2 * GDN_HEADS, N_BRANCH * D_MODEL)
IN_COLS = sum(IN_SPLITS)

LANES = 128
SUBLANES = 8
VMEM_LIMIT = 56 * 1024 * 1024


def _round_up(a, m):
    return -(-a // m) * m


ADA_TN = 2048


def _adaln_kernel(c_ref, w_ref, b_ref, o_ref):
    c = c_ref[...]
    act = (c * jax.nn.sigmoid(c)).astype(jnp.bfloat16)
    o_ref[0] = jnp.dot(act, w_ref[0].astype(jnp.bfloat16), preferred_element_type=jnp.float32) + b_ref[0]


def _adaln(c, w_ada, b_ada):
    R = c.shape[0]
    L, _, N = w_ada.shape
    Rp = _round_up(R, SUBLANES)
    cp = jnp.pad(c, ((0, Rp - R), (0, 0)))
    out = pl.pallas_call(
        _adaln_kernel,
        grid=(L, N // ADA_TN),
        in_specs=[pl.BlockSpec((Rp, D_MODEL), lambda l, j: (0, 0)),
                  pl.BlockSpec((1, D_MODEL, ADA_TN), lambda l, j: (l, 0, j)),
                  pl.BlockSpec((1, 1, ADA_TN), lambda l, j: (l, 0, j))],
        out_specs=pl.BlockSpec((1, Rp, ADA_TN), lambda l, j: (l, 0, j)),
        out_shape=jax.ShapeDtypeStruct((L, Rp, N), jnp.float32),
        compiler_params=pltpu.CompilerParams(dimension_semantics=("arbitrary", "arbitrary"),
                                             vmem_limit_bytes=VMEM_LIMIT),
        name="adaln",
    )(cp, w_ada, b_ada[:, None, :])
    return out[:, :R]


NSA_TQ = 128
BLOCKS_PER_TILE = NSA_TQ // NSA_BLOCK


def _t5_thresholds():
    n = np.arange(0, 2 * MAX_DISTANCE)
    exact = NUM_BUCKETS // 2
    large = exact + (np.log(np.maximum(n, 1).astype(np.float32) / np.float32(exact))
                     / np.float32(math.log(MAX_DISTANCE / exact)) * (NUM_BUCKETS - exact)).astype(np.int32)
    bucket = np.where(n < exact, n, np.minimum(large, NUM_BUCKETS - 1))
    return tuple(int(np.argmax(bucket >= k)) for k in range(1, NUM_BUCKETS))


_T5_THR = _t5_thresholds()


def _bias_from_dist(dist, rb_ref, h):
    v = jnp.full(dist.shape, rb_ref[NUM_BUCKETS - 1, h], jnp.float32)
    for k in range(NUM_BUCKETS - 2, -1, -1):
        v = jnp.where(dist < _T5_THR[k], rb_ref[k, h], v)
    return v


def _dot_nt(a, b):
    return lax.dot_general(a, b, (((1,), (1,)), ((), ())), preferred_element_type=jnp.float32)


def _flash_tile(carry, k_t, vt_t, qs, mask_add):
    m, l, acc = carry
    s = _dot_nt(k_t, qs) + mask_add
    m_new = jnp.maximum(m, jnp.max(s, axis=0, keepdims=True))
    p = jnp.exp(s - m_new)
    alpha = jnp.exp(m - m_new)
    l = alpha * l + jnp.sum(p, axis=0, keepdims=True)
    acc = alpha * acc + jnp.dot(vt_t, p.astype(jnp.bfloat16), preferred_element_type=jnp.float32)
    return m_new, l, acc


FAR_GROUP = 8
NEAR_ROWS = 2 * NSA_TQ
WIN_ROWS = NSA_WINDOW + NSA_TQ


def _nsa_prompt_kernel(rb_ref, q_ref, kv_ref, gt_ref, poolt_ref, pe_ref, o_ref,
                       kc_ref, vc_ref, ksel_ref, vselt_ref, kwin_ref, vwint_ref, near_ref, wtbl_ref, score_ref,
                       sel_ref, s_ref, p_ref):
    b = pl.program_id(0)
    qi = pl.program_id(1)
    T = kv_ref.shape[1]
    nb = T // NSA_BLOCK
    topk = min(NSA_TOPK, nb)
    TQ = NSA_TQ
    CH = 512
    f32, bf16 = jnp.float32, jnp.bfloat16
    kvs = range(NSA_KV_HEADS)

    @pl.when((b == 0) & (qi == 0))
    def _tables():
        for kvg in kvs:
            ksel_ref[kvg, 0:TQ, :] = jnp.zeros((TQ, HEAD_DIM), bf16)
            vselt_ref[kvg, :, 0:TQ] = jnp.zeros((HEAD_DIM, TQ), bf16)
            kwin_ref[kvg, 0:NSA_WINDOW, :] = jnp.zeros((NSA_WINDOW, HEAD_DIM), bf16)
            vwint_ref[kvg, :, 0:NSA_WINDOW] = jnp.zeros((HEAD_DIM, NSA_WINDOW), bf16)
        d_near = (lax.broadcasted_iota(jnp.int32, (NEAR_ROWS, TQ), 1) + TQ
                  - lax.broadcasted_iota(jnp.int32, (NEAR_ROWS, TQ), 0))
        d_win = (lax.broadcasted_iota(jnp.int32, (WIN_ROWS, TQ), 1) + NSA_WINDOW
                 - lax.broadcasted_iota(jnp.int32, (WIN_ROWS, TQ), 0))
        for h in range(NSA_HEADS):
            kvg, g = divmod(h, NSA_GROUP)
            lanes = slice(g * TQ, (g + 1) * TQ)
            near_ref[kvg, :, lanes] = jnp.where(
                d_near >= 0, _bias_from_dist(d_near, rb_ref, h) - rb_ref[NUM_BUCKETS - 1, h], NEG_INF)
            wtbl_ref[kvg, :, lanes] = jnp.where((d_win >= 0) & (d_win < NSA_WINDOW),
                                                _bias_from_dist(d_win, rb_ref, h), NEG_INF)

    @pl.when(qi == 0)
    def _prologue():
        def chunk(i, carry):
            r = pl.multiple_of(i * CH, CH)
            rs = pl.multiple_of(i * CH + TQ, TQ)
            rw = pl.multiple_of(i * CH + NSA_WINDOW, TQ)
            rb8 = pl.multiple_of(i * (CH // NSA_BLOCK), CH // NSA_BLOCK)
            for kvg in kvs:
                def col(c):
                    lo = c * NSA_KV_HEADS * HEAD_DIM + kvg * HEAD_DIM
                    return kv_ref[0, pl.ds(r, CH), lo:lo + HEAD_DIM]
                for c, dst in ((0, kc_ref), (1, vc_ref)):
                    x = col(c).reshape(CH // NSA_BLOCK, NSA_BLOCK, HEAD_DIM) + pe_ref[c][None]
                    dst[kvg, pl.ds(rb8, CH // NSA_BLOCK), :] = jnp.sum(x * poolt_ref[:, c:c + 1][None], axis=1)
                ksel_ref[kvg, pl.ds(rs, CH), :] = col(2).astype(bf16)
                vselt_ref[kvg, :, pl.ds(rs, CH)] = col(3).T.astype(bf16)
                kwin_ref[kvg, pl.ds(rw, CH), :] = col(4).astype(bf16)
                vwint_ref[kvg, :, pl.ds(rw, CH)] = col(5).T.astype(bf16)
            return carry
        lax.fori_loop(0, T // CH, chunk, 0)

    q = q_ref[0]
    gates = jax.nn.sigmoid(gt_ref[0])
    n_io = lax.broadcasted_iota(jnp.int32, (nb, TQ), 0)
    t_io = lax.broadcasted_iota(jnp.int32, (nb, TQ), 1)
    dist_c = qi * TQ + t_io - (n_io * NSA_BLOCK + NSA_BLOCK - 1)
    vis_c = dist_c >= 0
    vis_c2 = jnp.concatenate([vis_c, vis_c], axis=1)
    cur = (qi * TQ + t_io) // NSA_BLOCK
    forced = (n_io == 0) | (n_io == cur) | (n_io == cur - 1)
    q0 = pl.multiple_of(qi * TQ, TQ)

    qs, oc, score = [], [], []
    for kvg in kvs:
        base = kvg * NSA_GROUP * HEAD_DIM
        qk = jnp.concatenate([q[:, base + g * HEAD_DIM: base + (g + 1) * HEAD_DIM] for g in range(NSA_GROUP)], axis=0)
        qs.append((qk * HEAD_DIM ** -0.5).astype(bf16))
        sc = _dot_nt(kc_ref[kvg].astype(bf16), qs[kvg])
        bias_c = jnp.concatenate([_bias_from_dist(dist_c, rb_ref, kvg * NSA_GROUP + g) for g in range(NSA_GROUP)],
                                 axis=1)
        sc = jnp.where(vis_c2, sc + bias_c, NEG_INF)
        e = jnp.exp(sc - jnp.max(sc, axis=0, keepdims=True))
        p_c = e / jnp.sum(e, axis=0, keepdims=True) * vis_c2.astype(f32)
        oc.append(jnp.dot(vc_ref[kvg].T.astype(bf16), p_c.astype(bf16), preferred_element_type=f32))
        score.append(jnp.where(n_io <= cur, jnp.where(forced, 2.0, p_c[:, :TQ] + p_c[:, TQ:]), -1.0))
        score_ref[kvg] = score[kvg]

    def rank_body(mi, ranks):
        out = []
        for kvg in kvs:
            row = score_ref[kvg, pl.ds(mi, 1), :]
            beats = (row > score[kvg]) | ((row == score[kvg]) & (mi < n_io))
            out.append(ranks[kvg] + beats.astype(jnp.int32))
        return tuple(out)

    ranks = lax.fori_loop(0, BLOCKS_PER_TILE * (qi + 1), rank_body,
                          tuple(jnp.zeros((nb, TQ), jnp.int32) for _ in kvs))
    for kvg in kvs:
        sel_ref[kvg] = jnp.where((ranks[kvg] < topk) & (n_io <= cur), 0.0, NEG_INF)

    def mask_rows(kvg, blk0, nblk, limit):
        rows = []
        for u in range(nblk):
            blk = blk0 + u
            ok = (blk >= 0) & (blk < limit)
            row = sel_ref[kvg, pl.ds(jnp.clip(blk, 0, nb - 1), 1), :]
            rows.append(jnp.broadcast_to(jnp.where(ok, row, NEG_INF), (NSA_BLOCK, TQ)))
        mm = jnp.concatenate(rows, axis=0)
        return jnp.concatenate([mm, mm], axis=1)

    carries = []
    for kvg in kvs:
        k_n = ksel_ref[kvg, pl.ds(q0, NEAR_ROWS), :]
        vt_n = vselt_ref[kvg, :, pl.ds(q0, NEAR_ROWS)]
        s = (_dot_nt(k_n, qs[kvg]) + near_ref[kvg]
             + mask_rows(kvg, BLOCKS_PER_TILE * (qi - 1), 2 * BLOCKS_PER_TILE, nb))
        m = jnp.max(s, axis=0, keepdims=True)
        p = jnp.exp(s - m)
        carries.append((m, jnp.sum(p, axis=0, keepdims=True),
                        jnp.dot(vt_n, p.astype(bf16), preferred_element_type=f32)))

    n_far = jnp.maximum(qi - 1, 0)
    rows_far = FAR_GROUP * TQ

    def far_body(i, cs):
        out = []
        for kvg in kvs:
            m_old, l_old, acc_old = cs[kvg]
            r = pl.multiple_of(TQ + i * rows_far, TQ)
            m_new = m_old
            for u in range(FAR_GROUP):
                s_u = (_dot_nt(ksel_ref[kvg, pl.ds(r + u * TQ, TQ), :], qs[kvg])
                       + mask_rows(kvg, (i * FAR_GROUP + u) * BLOCKS_PER_TILE, BLOCKS_PER_TILE,
                                   n_far * BLOCKS_PER_TILE))
                s_ref[kvg, u * TQ:(u + 1) * TQ, :] = s_u
                m_new = jnp.maximum(m_new, jnp.max(s_u, axis=0, keepdims=True))
            l_new = jnp.exp(m_old - m_new) * l_old
            for u in range(FAR_GROUP):
                p_u = jnp.exp(s_ref[kvg, u * TQ:(u + 1) * TQ, :] - m_new)
                l_new = l_new + jnp.sum(p_u, axis=0, keepdims=True)
                p_ref[kvg, u * TQ:(u + 1) * TQ, :] = p_u.astype(bf16)
            acc = jnp.exp(m_old - m_new) * acc_old + jnp.dot(vselt_ref[kvg, :, pl.ds(r, rows_far)], p_ref[kvg],
                                                             preferred_element_type=f32)
            out.append((m_new, l_new, acc))
        return tuple(out)

    carries = lax.fori_loop(0, (n_far + FAR_GROUP - 1) // FAR_GROUP, far_body, tuple(carries))

    w_io = lax.broadcasted_iota(jnp.int32, (WIN_ROWS, 2 * TQ), 0)
    outs = []
    for kvg in kvs:
        m_s, l_s, acc_s = carries[kvg]
        o_s = acc_s / l_s
        k_w = kwin_ref[kvg, pl.ds(q0, WIN_ROWS), :]
        vt_w = vwint_ref[kvg, :, pl.ds(q0, WIN_ROWS)]
        s = jnp.where(w_io >= NSA_WINDOW - q0, _dot_nt(k_w, qs[kvg]) + wtbl_ref[kvg], NEG_INF)
        e = jnp.exp(s - jnp.max(s, axis=0, keepdims=True))
        o_w = (jnp.dot(vt_w, e.astype(bf16), preferred_element_type=f32) / jnp.sum(e, axis=0, keepdims=True))
        for g in range(NSA_GROUP):
            h = kvg * NSA_GROUP + g
            lanes = slice(g * TQ, (g + 1) * TQ)
            o = (gates[h:h + 1] * oc[kvg][:, lanes] + gates[NSA_HEADS + h:NSA_HEADS + h + 1] * o_s[:, lanes]
                 + gates[2 * NSA_HEADS + h:2 * NSA_HEADS + h + 1] * o_w[:, lanes])
            outs.append(o.T)
    o_ref[0] = jnp.concatenate(outs, axis=1)


def _nsa_prompt(q, q_blk, kv, kv_blk, gate_logits, cmp_pool, cmp_pe, rel_bias):
    B, T, _ = q.shape
    nb = T // NSA_BLOCK
    assert T % 512 == 0 and nb % SUBLANES == 0 and (T // NSA_TQ) % FAR_GROUP == 0
    TQ = NSA_TQ
    gt = jnp.transpose(gate_logits, (0, 2, 1))
    f32, bf16 = jnp.float32, jnp.bfloat16
    return pl.pallas_call(
        _nsa_prompt_kernel,
        grid=(B, T // TQ),
        in_specs=[
            pl.BlockSpec(memory_space=pltpu.SMEM),
            pl.BlockSpec((1, TQ, NSA_HEADS * HEAD_DIM), lambda b, i: (b, i, q_blk)),
            pl.BlockSpec((1, T, 6 * NSA_KV_HEADS * HEAD_DIM), lambda b, i: (b, 0, kv_blk)),
            pl.BlockSpec((1, 3 * NSA_HEADS, TQ), lambda b, i: (b, 0, i)),
            pl.BlockSpec((NSA_BLOCK, 2), lambda b, i: (0, 0)),
            pl.BlockSpec((2, NSA_BLOCK, HEAD_DIM), lambda b, i: (0, 0, 0)),
        ],
        out_specs=pl.BlockSpec((1, TQ, NSA_HEADS * HEAD_DIM), lambda b, i: (b, i, 0)),
        out_shape=jax.ShapeDtypeStruct((B, T, NSA_HEADS * HEAD_DIM), f32),
        scratch_shapes=[
            pltpu.VMEM((NSA_KV_HEADS, nb, HEAD_DIM), f32),
            pltpu.VMEM((NSA_KV_HEADS, nb, HEAD_DIM), f32),
            pltpu.VMEM((NSA_KV_HEADS, T + TQ, HEAD_DIM), bf16),
            pltpu.VMEM((NSA_KV_HEADS, HEAD_DIM, T + TQ), bf16),
            pltpu.VMEM((NSA_KV_HEADS, T + NSA_WINDOW, HEAD_DIM), bf16),
            pltpu.VMEM((NSA_KV_HEADS, HEAD_DIM, T + NSA_WINDOW), bf16),
            pltpu.VMEM((NSA_KV_HEADS, NEAR_ROWS, NSA_GROUP * TQ), f32),
            pltpu.VMEM((NSA_KV_HEADS, WIN_ROWS, NSA_GROUP * TQ), f32),
            pltpu.VMEM((NSA_KV_HEADS, nb, TQ), f32),
            pltpu.VMEM((NSA_KV_HEADS, nb, TQ), f32),
            pltpu.VMEM((NSA_KV_HEADS, FAR_GROUP * TQ, NSA_GROUP * TQ), f32),
            pltpu.VMEM((NSA_KV_HEADS, FAR_GROUP * TQ, NSA_GROUP * TQ), bf16),
        ],
        compiler_params=pltpu.CompilerParams(dimension_semantics=("arbitrary", "arbitrary"),
                                             vmem_limit_bytes=VMEM_LIMIT),
        name="nsa_prompt",
    )(rel_bias, q, kv, gt, jnp.transpose(cmp_pool), cmp_pe)


PAGES_PER_STEP = 32
SEL_TILE = 2048
KVW = NSA_KV_HEADS * HEAD_DIM


def _nsa_sample_kernel(pt_ref, rb_ref, *refs):
    PPS = PAGES_PER_STEP
    pages = refs[:PPS]
    q_ref, kvn_ref, gt_ref, win_ref, poolm_ref, cconst_ref, o_ref, cmp_ref, kselt_ref, vselt_ref = refs[PPS:]
    s_id = pl.program_id(1)
    f32, bf16 = jnp.float32, jnp.bfloat16
    Tn = q_ref.shape[1]
    past = kselt_ref.shape[1]
    nbp = past // NSA_BLOCK
    Wb = win_ref.shape[3]
    bpp = PAGE_SIZE // NSA_BLOCK
    R = NSA_HEADS * Tn

    rows = []
    for k in range(PPS):
        parts = []
        for c in range(2):
            a = _dot_nt(poolm_ref[c], pages[k][0, 0, c].astype(bf16))
            parts.append(a[0:bpp] + a[SUBLANES:SUBLANES + bpp] + cconst_ref[c:c + 1])
        rows.append(jnp.concatenate(parts, axis=1))
        r = pl.multiple_of((s_id * PPS + k) * PAGE_SIZE, PAGE_SIZE)
        kselt_ref[:, pl.ds(r, PAGE_SIZE)] = pages[k][0, 0, 2].astype(bf16)
        vselt_ref[:, pl.ds(r, PAGE_SIZE)] = pages[k][0, 0, 3].astype(bf16)
    cmp_ref[pl.ds(pl.multiple_of(s_id * PPS * bpp, PPS * bpp), PPS * bpp), :] = jnp.concatenate(rows, axis=0)

    @pl.when(s_id == pl.num_programs(1) - 1)
    def _attend():
        def per_head(fn):
            return jnp.concatenate([fn(h) for h in range(NSA_HEADS)], axis=0)

        q = q_ref[0] * HEAD_DIM ** -0.5
        zero = jnp.zeros((Tn, HEAD_DIM), f32)

        def q_rows(h):
            qh = q[:, h * HEAD_DIM:(h + 1) * HEAD_DIM]
            return jnp.concatenate([qh, zero] if h < NSA_GROUP else [zero, qh], axis=1)

        q2 = per_head(q_rows).astype(bf16)
        kvn = kvn_ref[0]
        pad = jnp.zeros((LANES - Tn, KVW), f32)

        def new_rows(c):
            return jnp.concatenate([kvn[:, c * KVW:(c + 1) * KVW], pad], axis=0).astype(bf16)

        tn_io = lax.broadcasted_iota(jnp.int32, (Tn, LANES), 0)
        jn_io = lax.broadcasted_iota(jnp.int32, (Tn, LANES), 1)
        d_new = tn_io - jn_io
        keep_new = per_head(lambda h: (d_new >= 0) & (jn_io < Tn))
        bias_new = per_head(lambda h: _bias_from_dist(d_new, rb_ref, h))
        far = per_head(lambda h: jnp.full((Tn, 1), rb_ref[NUM_BUCKETS - 1, h], f32))

        kc = cmp_ref[:, 0:KVW].astype(bf16)
        vc = cmp_ref[:, KVW:2 * KVW].astype(bf16)
        n_io = lax.broadcasted_iota(jnp.int32, (Tn, nbp), 1)
        t_io = lax.broadcasted_iota(jnp.int32, (Tn, nbp), 0)
        dist_c = past + t_io - (n_io * NSA_BLOCK + NSA_BLOCK - 1)
        sc = _dot_nt(q2, kc) + per_head(lambda h: _bias_from_dist(dist_c, rb_ref, h))
        e = jnp.exp(sc - jnp.max(sc, axis=1, keepdims=True))
        p_c = e / jnp.sum(e, axis=1, keepdims=True)
        oc = jnp.dot(p_c.astype(bf16), vc, preferred_element_type=f32)

        topk = min(NSA_TOPK, nbp + 1)
        m_io = lax.broadcasted_iota(jnp.int32, (nbp, nbp), 0)
        c_io = lax.broadcasted_iota(jnp.int32, (nbp, nbp), 1)
        lower = m_io < c_io
        forced = (n_io == 0) | (n_io == nbp - 1)
        sels = []
        for kvg in range(NSA_KV_HEADS):
            r0 = kvg * NSA_GROUP * Tn
            score = jnp.where(forced, 2.0, p_c[r0:r0 + Tn] + p_c[r0 + Tn:r0 + 2 * Tn])
            score_t = jnp.concatenate([score, jnp.zeros((LANES - Tn, nbp), f32)], axis=0).T
            ranks = []
            for t in range(Tn):
                colb = jnp.broadcast_to(score_t[:, t:t + 1], (nbp, nbp))
                rowb = jnp.broadcast_to(score[t:t + 1, :], (nbp, nbp))
                beats = (colb > rowb) | ((colb == rowb) & lower)
                ranks.append(jnp.sum(beats.astype(f32), axis=0, keepdims=True))
            sel = (jnp.concatenate(ranks, axis=0) < topk - 1).astype(f32)
            sels += [sel] * NSA_GROUP
        sel_rows = jnp.concatenate(sels, axis=0).astype(bf16)

        bpt = SEL_TILE // NSA_BLOCK
        expand = (lax.broadcasted_iota(jnp.int32, (bpt, SEL_TILE), 0)
                  == lax.broadcasted_iota(jnp.int32, (bpt, SEL_TILE), 1) // NSA_BLOCK).astype(bf16)
        d_last = LANES + tn_io - jn_io
        near = per_head(lambda h: _bias_from_dist(d_last, rb_ref, h))
        m = jnp.full((R, 1), NEG_INF, f32)
        l = jnp.zeros((R, 1), f32)
        acc = jnp.zeros((R, KVW), f32)

        def flash(carry, s, v_t, v_feature_major):
            m, l, acc = carry
            m_new = jnp.maximum(m, jnp.max(s, axis=1, keepdims=True))
            p = jnp.exp(s - m_new)
            alpha = jnp.exp(m - m_new)
            pb = p.astype(bf16)
            pv = _dot_nt(pb, v_t) if v_feature_major else jnp.dot(pb, v_t, preferred_element_type=f32)
            return m_new, alpha * l + jnp.sum(p, axis=1, keepdims=True), alpha * acc + pv

        carry = (m, l, acc)
        ntile = past // SEL_TILE
        for j in range(ntile):
            k_t = kselt_ref[:, j * SEL_TILE:(j + 1) * SEL_TILE]
            v_t = vselt_ref[:, j * SEL_TILE:(j + 1) * SEL_TILE]
            keep = jnp.dot(sel_rows[:, j * bpt:(j + 1) * bpt], expand, preferred_element_type=f32) > 0.5
            if j == ntile - 1:
                bias = jnp.concatenate([jnp.broadcast_to(far, (R, SEL_TILE - LANES)), near], axis=1)
            else:
                bias = far
            s_t = jnp.dot(q2, k_t, preferred_element_type=f32)
            carry = flash(carry, jnp.where(keep, s_t + bias, NEG_INF), v_t, True)
        s_new = jnp.where(keep_new, _dot_nt(q2, new_rows(2)) + bias_new, NEG_INF)
        m, l, acc = flash(carry, s_new, new_rows(3), False)
        o_s = acc / l

        tw_io = lax.broadcasted_iota(jnp.int32, (Tn, Wb), 0)
        cw_io = lax.broadcasted_iota(jnp.int32, (Tn, Wb), 1)
        d_w = Wb + tw_io - cw_io
        s_w = jnp.where(per_head(lambda h: d_w < NSA_WINDOW),
                        jnp.dot(q2, win_ref[0, 0].astype(bf16), preferred_element_type=f32)
                        + per_head(lambda h: _bias_from_dist(d_w, rb_ref, h)), NEG_INF)
        s_wn = jnp.where(keep_new, _dot_nt(q2, new_rows(4)) + bias_new, NEG_INF)
        s_all = jnp.concatenate([s_w, s_wn], axis=1)
        e = jnp.exp(s_all - jnp.max(s_all, axis=1, keepdims=True))
        p_w = (e / jnp.sum(e, axis=1, keepdims=True)).astype(bf16)
        o_w = _dot_nt(p_w[:, :Wb], win_ref[0, 1].astype(bf16)) + jnp.dot(p_w[:, Wb:], new_rows(5),
                                                                          preferred_element_type=f32)

        gates = jax.nn.sigmoid(gt_ref[0])
        outs = []
        for h in range(NSA_HEADS):
            rs = slice(h * Tn, (h + 1) * Tn)
            cs = slice((h // NSA_GROUP) * HEAD_DIM, (h // NSA_GROUP + 1) * HEAD_DIM)
            outs.append(gates[:, h:h + 1] * oc[rs, cs] + gates[:, NSA_HEADS + h:NSA_HEADS + h + 1] * o_s[rs, cs]
                        + gates[:, 2 * NSA_HEADS + h:2 * NSA_HEADS + h + 1] * o_w[rs, cs])
        o_ref[0] = jnp.concatenate(outs, axis=1)


def _nsa_sample(q, kv, gate_logits, cache, layer, page_table, win_buf, cmp_pool, cmp_pe, rel_bias):
    B, Tn, _ = q.shape
    npages = page_table.shape[1]
    past = npages * PAGE_SIZE
    Wb = win_buf.shape[1]
    PPS = PAGES_PER_STEP
    assert npages % PPS == 0 and past % SEL_TILE == 0 and Tn == SUBLANES and (past // NSA_BLOCK) % LANES == 0
    f32, bf16 = jnp.float32, jnp.bfloat16
    bpp = PAGE_SIZE // NSA_BLOCK
    cache_t = jnp.transpose(cache, (0, 1, 3, 4, 5, 2)).reshape(cache.shape[0], cache.shape[1], 4, KVW, PAGE_SIZE)
    win_t = jnp.transpose(win_buf, (0, 2, 3, 4, 1)).reshape(B, 2, KVW, Wb)
    r_io = np.arange(PAGE_SIZE)
    onehot = jnp.asarray((r_io[None, :] // NSA_BLOCK == np.arange(SUBLANES)[:, None]), f32)
    pool_full = onehot[None] * jnp.tile(cmp_pool, (1, bpp))[:, None, :]
    pool_hi = pool_full.astype(bf16)
    pool_lo = (pool_full - pool_hi.astype(f32)).astype(bf16)
    poolm = jnp.concatenate([pool_hi, pool_lo], axis=1)
    cconst = jnp.tile(jnp.sum(cmp_pool[:, :, None] * cmp_pe, axis=1), (1, NSA_KV_HEADS))

    def page_spec(k):
        return pl.BlockSpec((1, 1, 4, KVW, PAGE_SIZE), lambda b, s, pt: (pt[b, s * PPS + k], layer, 0, 0, 0))

    grid_spec = pltpu.PrefetchScalarGridSpec(
        num_scalar_prefetch=1,
        grid=(B, npages // PPS),
        in_specs=[pl.BlockSpec(memory_space=pltpu.SMEM)] + [page_spec(k) for k in range(PPS)] + [
            pl.BlockSpec((1, Tn, NSA_HEADS * HEAD_DIM), lambda b, s, pt: (b, 0, 0)),
            pl.BlockSpec((1, Tn, 6 * KVW), lambda b, s, pt: (b, 0, 0)),
            pl.BlockSpec((1, Tn, 3 * NSA_HEADS), lambda b, s, pt: (b, 0, 0)),
            pl.BlockSpec((1, 2, KVW, Wb), lambda b, s, pt: (b, 0, 0, 0)),
            pl.BlockSpec((2, 2 * SUBLANES, PAGE_SIZE), lambda b, s, pt: (0, 0, 0)),
            pl.BlockSpec((2, KVW), lambda b, s, pt: (0, 0)),
        ],
        out_specs=pl.BlockSpec((1, Tn, NSA_HEADS * HEAD_DIM), lambda b, s, pt: (b, 0, 0)),
        scratch_shapes=[
            pltpu.VMEM((past // NSA_BLOCK, 2 * KVW), f32),
            pltpu.VMEM((KVW, past), bf16),
            pltpu.VMEM((KVW, past), bf16),
        ],
    )
    return pl.pallas_call(
        _nsa_sample_kernel,
        grid_spec=grid_spec,
        out_shape=jax.ShapeDtypeStruct((B, Tn, NSA_HEADS * HEAD_DIM), f32),
        compiler_params=pltpu.CompilerParams(dimension_semantics=("arbitrary", "arbitrary"),
                                             vmem_limit_bytes=VMEM_LIMIT),
        name="nsa_sample",
    )(page_table, rel_bias, *([cache_t] * PPS), q, kv, gate_logits, win_t, poolm, cconst)


PROJ_KV, PROJ_Q, PROJ_UCV, PROJ_RET, PROJ_GQKV, PROJ_GZ, PROJ_MISC = 0, 768, 1024, 1536, 2560, 3328, 3584
PROJ_COLS = PROJ_MISC + LANES
PROJ_TM = 512
MERGE_TM = 256
FFN_TM = 256


def _prep_w_in(w_in):
    o = [int(v) for v in np.cumsum((0,) + IN_SPLITS)]
    small = o[5] + 4 * GDN_W
    pieces = [w_in[:, o[1]:o[2]], w_in[:, o[0]:o[1]], w_in[:, o[3]:o[4]], w_in[:, o[4]:o[5]], w_in[:, o[5]:small],
              w_in[:, o[2]:o[3]], w_in[:, small:o[6]]]
    used = sum(pc.shape[1] for pc in pieces)
    pieces.append(jnp.zeros((D_MODEL, PROJ_COLS - used), w_in.dtype))
    return jnp.concatenate(pieces, axis=1), w_in[:, o[6]:]


def _modulated_norm(x, g, sc, sh):
    y = x * lax.rsqrt(jnp.mean(x * x, axis=-1, keepdims=True) + EPS)
    return (y * g) * (1.0 + sc) + sh


def _resident(shape):
    return pl.BlockSpec(shape, lambda *_: (0,) * len(shape), pipeline_mode=pl.Buffered(1))


def _mod_spec(mod, rows_per_group, tm):
    if mod.shape[1] == 1:
        return pl.BlockSpec((1, 1, D_MODEL), lambda i, *_: (i // (rows_per_group // tm), 0, 0))
    return pl.BlockSpec((1, tm, D_MODEL), lambda i, *_: (0, i, 0))


def _proj_kernel(x_ref, g_ref, sc_ref, sh_ref, w_ref, o_ref, wb_ref):
    @pl.when(pl.program_id(0) == 0)
    def _():
        wb_ref[...] = w_ref[...].astype(jnp.bfloat16)

    h = _modulated_norm(x_ref[...], g_ref[...], sc_ref[0], sh_ref[0]).astype(jnp.bfloat16)
    o_ref[...] = jnp.dot(h, wb_ref[...], preferred_element_type=jnp.float32)


def _proj(x2, g, sc, sh, w, rows_per_group, tm):
    M = x2.shape[0]
    N = w.shape[1]
    return pl.pallas_call(
        _proj_kernel,
        grid=(M // tm,),
        in_specs=[pl.BlockSpec((tm, D_MODEL), lambda i: (i, 0)), _resident((1, D_MODEL)),
                  _mod_spec(sc, rows_per_group, tm), _mod_spec(sh, rows_per_group, tm), _resident((D_MODEL, N))],
        out_specs=pl.BlockSpec((tm, N), lambda i: (i, 0)),
        out_shape=jax.ShapeDtypeStruct((M, N), jnp.float32),
        scratch_shapes=[pltpu.VMEM((D_MODEL, N), jnp.bfloat16)],
        compiler_params=pltpu.CompilerParams(dimension_semantics=("arbitrary",), vmem_limit_bytes=VMEM_LIMIT),
        name="in_proj",
    )(x2, g, sc, sh, w)


def _merge_kernel(x_ref, g0_ref, sc_ref, sh_ref, gt_ref, g1_ref, b0_ref, b1_ref, b2_ref, b3_ref,
                  wmg_ref, wbr_ref, wout_ref, o_ref, wmgb_ref):
    f32, bf16 = jnp.float32, jnp.bfloat16

    @pl.when(pl.program_id(0) == 0)
    def _():
        wmgb_ref[...] = wmg_ref[...].astype(bf16)

    x = x_ref[...]
    h = _modulated_norm(x, g0_ref[...], sc_ref[0], sh_ref[0]).astype(bf16)
    acc = jnp.zeros(x.shape, f32)
    for n, b_ref in enumerate((b0_ref, b1_ref, b2_ref, b3_ref)):
        gate = jax.nn.sigmoid(jnp.dot(h, wmgb_ref[:, n * D_MODEL:(n + 1) * D_MODEL], preferred_element_type=f32))
        acc = acc + gate * jnp.dot(b_ref[...].astype(bf16), wbr_ref[n], preferred_element_type=f32)
    mixed = jnp.dot(acc.astype(bf16), wout_ref[...], preferred_element_type=f32)
    y = mixed * lax.rsqrt(jnp.mean(mixed * mixed, axis=-1, keepdims=True) + EPS) * g1_ref[...]
    o_ref[...] = x + gt_ref[0] * y


def _merge(x2, g0, sc, sh, gt, g1, branches, wmg, wbr, wout, rows_per_group, tm):
    M = x2.shape[0]
    row = lambda w: pl.BlockSpec((tm, w), lambda i: (i, 0))
    mspec = _mod_spec(sc, rows_per_group, tm)
    return pl.pallas_call(
        _merge_kernel,
        grid=(M // tm,),
        in_specs=[row(D_MODEL), _resident((1, D_MODEL)), mspec, mspec, mspec, _resident((1, D_MODEL))]
                 + [row(BRANCH_W)] * N_BRANCH
                 + [_resident(wmg.shape), _resident(wbr.shape), _resident(wout.shape)],
        out_specs=row(D_MODEL),
        out_shape=jax.ShapeDtypeStruct((M, D_MODEL), jnp.float32),
        scratch_shapes=[pltpu.VMEM(wmg.shape, jnp.bfloat16)],
        compiler_params=pltpu.CompilerParams(dimension_semantics=("arbitrary",), vmem_limit_bytes=VMEM_LIMIT),
        name="merge",
    )(x2, g0, sc, sh, gt, g1, *branches, wmg, wbr, wout)


def _ffn_kernel(x_ref, g2_ref, sc_ref, sh_ref, gt_ref, g3_ref, buf_ref, wup_ref, dw_ref, wdn_ref,
                o_ref, st_ref, gp_ref):
    f32, bf16 = jnp.float32, jnp.bfloat16
    tm = x_ref.shape[1]
    HALO = SUBLANES

    @pl.when(pl.program_id(1) == 0)
    def _():
        gp_ref[0:HALO, :] = buf_ref[0]

    x = x_ref[0]
    h = _modulated_norm(x, g2_ref[...], sc_ref[0], sh_ref[0]).astype(bf16)
    gp_ref[HALO:HALO + tm, :] = jnp.dot(h, wup_ref[:, 0:D_FF], preferred_element_type=f32)
    val = jnp.dot(h, wup_ref[:, D_FF:2 * D_FF], preferred_element_type=f32)
    gconv = (dw_ref[2:3, :] * gp_ref[HALO:HALO + tm, :] + dw_ref[1:2, :] * gp_ref[HALO - 1:HALO - 1 + tm, :]
             + dw_ref[0:1, :] * gp_ref[HALO - 2:HALO - 2 + tm, :])
    a = (jax.nn.gelu(gconv) * val).astype(bf16)
    f = jnp.dot(a, wdn_ref[...], preferred_element_type=f32)
    y = f * lax.rsqrt(jnp.mean(f * f, axis=-1, keepdims=True) + EPS) * g3_ref[...]
    o_ref[0] = x + gt_ref[0] * y
    tail = gp_ref[tm:tm + HALO, :]
    gp_ref[0:HALO, :] = tail
    st_ref[0] = tail


def _ffn(x3, g2, sc, sh, gt, g3, buf, wup, dw, wdn, tm):
    B, T, _ = x3.shape
    buf8 = jnp.pad(buf, ((0, 0), (SUBLANES - (FFN_CONV - 1), 0), (0, 0)))
    mspec = pl.BlockSpec((1, 1, D_MODEL), lambda b, i: (b, 0, 0))
    y, st = pl.pallas_call(
        _ffn_kernel,
        grid=(B, T // tm),
        in_specs=[pl.BlockSpec((1, tm, D_MODEL), lambda b, i: (b, i, 0)), _resident((1, D_MODEL)), mspec, mspec, mspec,
                  _resident((1, D_MODEL)), pl.BlockSpec((1, SUBLANES, D_FF), lambda b, i: (b, 0, 0)),
                  _resident(wup.shape), _resident(dw.shape), _resident(wdn.shape)],
        out_specs=[pl.BlockSpec((1, tm, D_MODEL), lambda b, i: (b, i, 0)),
                   pl.BlockSpec((1, SUBLANES, D_FF), lambda b, i: (b, 0, 0))],
        out_shape=[jax.ShapeDtypeStruct((B, T, D_MODEL), jnp.float32),
                   jax.ShapeDtypeStruct((B, SUBLANES, D_FF), jnp.float32)],
        scratch_shapes=[pltpu.VMEM((tm + SUBLANES, D_FF), jnp.float32)],
        compiler_params=pltpu.CompilerParams(dimension_semantics=("arbitrary", "arbitrary"),
                                             vmem_limit_bytes=VMEM_LIMIT),
        name="conv_ffn",
    )(x3, g2, sc, sh, gt, g3, buf8, wup, dw, wdn)
    return y, st[:, SUBLANES - (FFN_CONV - 1):]


def _ffn_rows_kernel(x_ref, g2_ref, sc_ref, sh_ref, gt_ref, g3_ref, p1_ref, p2_ref, wup_ref, dw_ref, wdn_ref,
                     o_ref, gpre_ref, gp_ref, *, seg):
    f32, bf16 = jnp.float32, jnp.bfloat16
    M = x_ref.shape[0]
    HALO = SUBLANES
    x = x_ref[...]
    h = _modulated_norm(x, g2_ref[...], sc_ref[0], sh_ref[0]).astype(bf16)
    gpre = jnp.dot(h, wup_ref[:, 0:D_FF], preferred_element_type=f32)
    val = jnp.dot(h, wup_ref[:, D_FF:2 * D_FF], preferred_element_type=f32)
    gp_ref[0:HALO, :] = jnp.zeros((HALO, D_FF), f32)
    gp_ref[HALO:HALO + M, :] = gpre
    t = lax.broadcasted_iota(jnp.int32, (M, 1), 0) % seg
    prev1 = jnp.where(t >= 1, gp_ref[HALO - 1:HALO - 1 + M, :], p1_ref[...])
    prev2 = jnp.where(t >= 2, gp_ref[HALO - 2:HALO - 2 + M, :], p2_ref[...])
    gconv = dw_ref[2:3, :] * gpre + dw_ref[1:2, :] * prev1 + dw_ref[0:1, :] * prev2
    a = (jax.nn.gelu(gconv) * val).astype(bf16)
    f = jnp.dot(a, wdn_ref[...], preferred_element_type=f32)
    y = f * lax.rsqrt(jnp.mean(f * f, axis=-1, keepdims=True) + EPS) * g3_ref[...]
    o_ref[...] = x + gt_ref[0] * y
    gpre_ref[...] = gpre


def _ffn_rows(x3, g2, sc, sh, gt, g3, buf, wup, dw, wdn):
    B, T, _ = x3.shape
    M = B * T
    assert FFN_CONV == 3 and T >= FFN_CONV - 1
    zeros = jnp.zeros((B, T, D_FF), jnp.float32)
    p1 = zeros.at[:, 0].set(buf[:, 1]).reshape(M, D_FF)
    p2 = zeros.at[:, 0].set(buf[:, 0]).at[:, 1].set(buf[:, 1]).reshape(M, D_FF)
    full = lambda shape: pl.BlockSpec(shape, lambda i: (0,) * len(shape))
    y, gpre = pl.pallas_call(
        functools.partial(_ffn_rows_kernel, seg=T),
        grid=(1,),
        in_specs=[full((M, D_MODEL)), full((1, D_MODEL)), full((1, M, D_MODEL)), full((1, M, D_MODEL)),
                  full((1, M, D_MODEL)), full((1, D_MODEL)), full((M, D_FF)), full((M, D_FF)),
                  _resident(wup.shape), _resident(dw.shape), _resident(wdn.shape)],
        out_specs=[full((M, D_MODEL)), full((M, D_FF))],
        out_shape=[jax.ShapeDtypeStruct((M, D_MODEL), jnp.float32), jax.ShapeDtypeStruct((M, D_FF), jnp.float32)],
        scratch_shapes=[pltpu.VMEM((M + SUBLANES, D_FF), jnp.float32)],
        compiler_params=pltpu.CompilerParams(dimension_semantics=("arbitrary",), vmem_limit_bytes=VMEM_LIMIT),
        name="conv_ffn_rows",
    )(x3.reshape(M, D_MODEL), g2, sc, sh, gt, g3, p1, p2, wup, dw, wdn)
    return y.reshape(B, T, D_MODEL), gpre.reshape(B, T, D_FF)[:, T - (FFN_CONV - 1):]


CONF_HALO = 32
CONF_TM = 1024


def _conformer_kernel(u_ref, buf_ref, dw_ref, dwb_ref, lng_ref, lnb_ref, o_ref, st_ref, xp_ref):
    tm = u_ref.shape[1]
    first = CONF_HALO - (CONV_WIDTH - 1)

    @pl.when(pl.program_id(1) == 0)
    def _():
        xp_ref[0:CONF_HALO, :] = buf_ref[0]

    u = u_ref[0]
    xp_ref[CONF_HALO:CONF_HALO + tm, :] = u[:, :CONV_CH] * jax.nn.sigmoid(u[:, CONV_CH:])
    acc = jnp.zeros((tm, CONV_CH), jnp.float32)
    for k in range(CONV_WIDTH):
        acc = acc + dw_ref[k:k + 1, :] * xp_ref[first + k:first + k + tm, :]
    y = acc + dwb_ref[...]
    mu = jnp.mean(y, axis=-1, keepdims=True)
    var = jnp.mean(jnp.square(y - mu), axis=-1, keepdims=True)
    yn = (y - mu) * lax.rsqrt(var + EPS) * lng_ref[...] + lnb_ref[...]
    o_ref[0] = yn * jax.nn.sigmoid(yn)
    tail = xp_ref[tm:tm + CONF_HALO, :]
    xp_ref[0:CONF_HALO, :] = tail
    st_ref[0] = tail


def _conformer(P, col_blk, buf, dw, dw_b, ln_g, ln_b, tm):
    B, T, _ = P.shape
    bufp = jnp.pad(buf, ((0, 0), (CONF_HALO - (CONV_WIDTH - 1), 0), (0, 0)))
    vec = lambda: _resident((1, CONV_CH))
    o, st = pl.pallas_call(
        _conformer_kernel,
        grid=(B, T // tm),
        in_specs=[pl.BlockSpec((1, tm, 2 * CONV_CH), lambda b, i: (b, i, col_blk)),
                  pl.BlockSpec((1, CONF_HALO, CONV_CH), lambda b, i: (b, 0, 0)),
                  _resident((CONV_WIDTH, CONV_CH)), vec(), vec(), vec()],
        out_specs=[pl.BlockSpec((1, tm, CONV_CH), lambda b, i: (b, i, 0)),
                   pl.BlockSpec((1, CONF_HALO, CONV_CH), lambda b, i: (b, 0, 0))],
        out_shape=[jax.ShapeDtypeStruct((B, T, CONV_CH), jnp.float32),
                   jax.ShapeDtypeStruct((B, CONF_HALO, CONV_CH), jnp.float32)],
        scratch_shapes=[pltpu.VMEM((tm + CONF_HALO, CONV_CH), jnp.float32)],
        compiler_params=pltpu.CompilerParams(dimension_semantics=("arbitrary", "arbitrary"),
                                             vmem_limit_bytes=VMEM_LIMIT),
        name="conformer",
    )(P, bufp, dw, dw_b[None], ln_g[None], ln_b[None])
    return o, st[:, CONF_HALO - (CONV_WIDTH - 1):]


RET_LOG_GAMMA = tuple(math.log1p(-2.0 ** (-5 - h)) for h in range(RET_HEADS))
RET_TM = 256


def _retention_kernel(q_ref, k_ref, v_ref, g_ref, cos_ref, sin_ref, s0_ref, gn_ref, o_ref, sfin_ref,
                      s_ref, dec_ref, *, n_valid):
    f32, bf16 = jnp.float32, jnp.bfloat16
    C = q_ref.shape[1]
    W = RET_W
    half = HEAD_DIM // 2

    @pl.when(pl.program_id(1) == 0)
    def _():
        s_ref[...] = s0_ref[0]

    @pl.when((pl.program_id(0) == 0) & (pl.program_id(1) == 0))
    def _():
        ii = lax.broadcasted_iota(jnp.int32, (C, C), 0)
        jj = lax.broadcasted_iota(jnp.int32, (C, C), 1)
        d = (ii - jj).astype(f32)
        for h in range(RET_HEADS):
            dec_ref[h] = jnp.where((ii >= jj) & (jj < n_valid), jnp.exp(jnp.maximum(d, 0.0) * RET_LOG_GAMMA[h]), 0.0)

    lane = lax.broadcasted_iota(jnp.int32, (C, W), 1)
    low = lane % HEAD_DIM < half
    cos = cos_ref[...]
    sin = sin_ref[...]

    def rot(x):
        other = jnp.where(low, pltpu.roll(x, W - half, axis=1), pltpu.roll(x, half, axis=1))
        return x * cos + other * sin

    q = rot(q_ref[0])
    k = rot(k_ref[0]) * HEAD_DIM ** -0.5
    v = v_ref[0]
    row = lax.broadcasted_iota(jnp.int32, (C, HEAD_DIM), 0)
    rowf = row.astype(f32)
    valid = row < n_valid
    outs = []
    for h in range(RET_HEADS):
        cs = slice(h * HEAD_DIM, (h + 1) * HEAD_DIM)
        lg = RET_LOG_GAMMA[h]
        qh, kh, vh = q[:, cs], k[:, cs], v[:, cs].astype(bf16)
        s_old = s_ref[h]
        att = _dot_nt(qh.astype(bf16), kh.astype(bf16)) * dec_ref[h]
        o = (jnp.dot(att.astype(bf16), vh, preferred_element_type=f32)
             + jnp.dot((qh * jnp.exp((rowf + 1.0) * lg)).astype(bf16), s_old.astype(bf16), preferred_element_type=f32))
        kz = jnp.where(valid, kh * jnp.exp((n_valid - 1.0 - rowf) * lg), 0.0)
        s_ref[h] = s_old * math.exp(n_valid * lg) + jnp.dot(kz.T.astype(bf16), vh, preferred_element_type=f32)
        mu = jnp.mean(o, axis=-1, keepdims=True)
        var = jnp.mean(jnp.square(o - mu), axis=-1, keepdims=True)
        outs.append((o - mu) * lax.rsqrt(var + EPS))
    g = g_ref[0]
    o_ref[0] = jnp.concatenate(outs, axis=1) * gn_ref[...] * (g * jax.nn.sigmoid(g))

    @pl.when(pl.program_id(1) == pl.num_programs(1) - 1)
    def _():
        sfin_ref[0] = s_ref[...]


def _retention(P, col_blk0, S0, pos, gn, chunk, n_valid):
    B, T, _ = P.shape
    half = HEAD_DIM // 2
    inv = ROPE_BASE ** (-jnp.arange(half, dtype=jnp.float32) / half)
    ang = pos.astype(jnp.float32)[:, None] * inv[None, :]
    cos, sin = jnp.cos(ang), jnp.sin(ang)
    cosf = jnp.tile(jnp.concatenate([cos, cos], axis=1), (1, RET_HEADS))
    sinf = jnp.tile(jnp.concatenate([-sin, sin], axis=1), (1, RET_HEADS))
    col = lambda j: pl.BlockSpec((1, chunk, RET_W), lambda b, i: (b, i, col_blk0 + j))
    tab = pl.BlockSpec((chunk, RET_W), lambda b, i: (i, 0))
    state = pl.BlockSpec((1, RET_HEADS, HEAD_DIM, HEAD_DIM), lambda b, i: (b, 0, 0, 0))
    return pl.pallas_call(
        functools.partial(_retention_kernel, n_valid=n_valid),
        grid=(B, T // chunk),
        in_specs=[col(0), col(1), col(2), col(3), tab, tab, state, _resident((1, RET_W))],
        out_specs=[pl.BlockSpec((1, chunk, RET_W), lambda b, i: (b, i, 0)), state],
        out_shape=[jax.ShapeDtypeStruct((B, T, RET_W), jnp.float32),
                   jax.ShapeDtypeStruct((B, RET_HEADS, HEAD_DIM, HEAD_DIM), jnp.float32)],
        scratch_shapes=[pltpu.VMEM((RET_HEADS, HEAD_DIM, HEAD_DIM), jnp.float32),
                        pltpu.VMEM((RET_HEADS, chunk, chunk), jnp.float32)],
        compiler_params=pltpu.CompilerParams(dimension_semantics=("arbitrary", "arbitrary"),
                                             vmem_limit_bytes=VMEM_LIMIT),
        name="retention",
    )(P, P, P, P, cosf, sinf, S0, gn[None])


GDN_RT = 4 * GDN_CHUNK
GDN_HALO = SUBLANES
GDN_A_LANE = 3 * NSA_HEADS
GDN_B_LANE = GDN_A_LANE + GDN_HEADS


def _mm1(a, b):
    return jnp.dot(a.astype(jnp.bfloat16), b.astype(jnp.bfloat16), preferred_element_type=jnp.float32)


def _mm3(a, b):
    f32, bf16 = jnp.float32, jnp.bfloat16
    ah, bh = a.astype(bf16), b.astype(bf16)
    al, bl = (a - ah.astype(f32)).astype(bf16), (b - bh.astype(f32)).astype(bf16)
    d = lambda x, y: jnp.dot(x, y, preferred_element_type=f32)
    return d(ah, bh) + (d(ah, bl) + d(al, bh))


def _segment_cumsum(x, axis, seg):
    idx = lax.broadcasted_iota(jnp.int32, x.shape, axis) % seg
    s = 1
    while s < seg:
        x = x + jnp.where(idx >= s, pltpu.roll(x, s, axis=axis), 0.0)
        s *= 2
    return x


def _gdn_kernel(q_ref, k_ref, v_ref, z_ref, ab_ref, abt_ref, buf_ref, s0_ref, cw_ref, ng_ref, alane_ref, dlane_ref,
                acol_ref, dcol_ref, o_ref, sfin_ref, s_ref, xp_ref, *, n_valid):
    f32 = jnp.float32
    RT = q_ref.shape[1]
    C = GDN_CHUNK
    W = GDN_W
    first = GDN_HALO - (GDN_CONV - 1)

    @pl.when(pl.program_id(1) == 0)
    def _():
        s_ref[...] = s0_ref[0]
        xp_ref[0:GDN_HALO, :] = buf_ref[0]

    xp_ref[GDN_HALO:GDN_HALO + RT, 0:W] = q_ref[0]
    xp_ref[GDN_HALO:GDN_HALO + RT, W:2 * W] = k_ref[0]
    xp_ref[GDN_HALO:GDN_HALO + RT, 2 * W:3 * W] = v_ref[0]
    y = jnp.zeros((RT, 3 * W), f32)
    for t in range(GDN_CONV):
        y = y + cw_ref[t:t + 1, :] * xp_ref[first + t:first + t + RT, :]
    y = y * jax.nn.sigmoid(y)
    xp_ref[0:GDN_HALO, :] = xp_ref[RT:RT + GDN_HALO, :]

    ab = ab_ref[0]
    g_lanes = -jnp.exp(alane_ref[...]) * jax.nn.softplus(ab + dlane_ref[...])
    beta_lanes = jax.nn.sigmoid(ab)
    g_rows = -jnp.exp(acol_ref[...]) * jax.nn.softplus(abt_ref[0] + dcol_ref[...])
    if n_valid < C:
        g_lanes = jnp.where(lax.broadcasted_iota(jnp.int32, g_lanes.shape, 0) % C < n_valid, g_lanes, 0.0)
        beta_lanes = jnp.where(lax.broadcasted_iota(jnp.int32, g_lanes.shape, 0) % C < n_valid, beta_lanes, 0.0)
        g_rows = jnp.where(lax.broadcasted_iota(jnp.int32, g_rows.shape, 1) % C < n_valid, g_rows, 0.0)
    gc_lanes = _segment_cumsum(g_lanes, 0, C)
    gc_rows = _segment_cumsum(g_rows, 1, C)

    ii = lax.broadcasted_iota(jnp.int32, (C, C), 0)
    jj = lax.broadcasted_iota(jnp.int32, (C, C), 1)
    tri = ii >= jj
    strict = ii > jj
    eye = (ii == jj).astype(f32)
    z = z_ref[0]
    bf16 = jnp.bfloat16
    nch = RT // C
    pairs = [(c, h) for c in range(nch) for h in range(GDN_HEADS)]
    qs, ks, ms, atts, rhss, gcs = {}, {}, {}, {}, {}, {}
    for c, h in pairs:
        rs = slice(c * C, (c + 1) * C)
        qh = y[rs, h * HEAD_DIM:(h + 1) * HEAD_DIM]
        kh = y[rs, W + h * HEAD_DIM:W + (h + 1) * HEAD_DIM]
        vh = y[rs, 2 * W + h * HEAD_DIM:2 * W + (h + 1) * HEAD_DIM]
        qh = qh * lax.rsqrt(jnp.sum(qh * qh, axis=-1, keepdims=True) + EPS) * HEAD_DIM ** -0.5
        kh = kh * lax.rsqrt(jnp.sum(kh * kh, axis=-1, keepdims=True) + EPS)
        gc_col = gc_lanes[rs, GDN_A_LANE + h:GDN_A_LANE + h + 1]
        beta = beta_lanes[rs, GDN_B_LANE + h:GDN_B_LANE + h + 1]
        gc_row = gc_rows[h:h + 1, c * C:(c + 1) * C]
        e_col = jnp.exp(gc_col)
        lm = jnp.exp(jnp.where(tri, gc_col - gc_row, NEG_INF))
        kb = kh * beta
        ms[c, h] = jnp.where(strict, _dot_nt(kb.astype(bf16), kh.astype(bf16)) * lm, 0.0)
        atts[c, h] = _dot_nt(qh.astype(bf16), kh.astype(bf16)) * lm
        rhss[c, h] = jnp.concatenate([vh * beta, kb * e_col], axis=1)
        qs[c, h], ks[c, h], gcs[c, h] = qh * e_col, kh, gc_col
    pw = {p: -ms[p] for p in pairs}
    xs = {p: eye + pw[p] for p in pairs}
    for _ in range(int(math.log2(C)) - 1):
        pw = {p: _mm1(pw[p], pw[p]) for p in pairs}
        xs = {p: xs[p] + _mm1(xs[p], pw[p]) for p in pairs}
    res = {p: eye - (xs[p] + _mm3(ms[p], xs[p])) for p in pairs}
    xs = {p: xs[p] + _mm1(xs[p], res[p]) for p in pairs}
    sols = {p: _mm3(xs[p], rhss[p]) for p in pairs}
    state = [s_ref[h] for h in range(GDN_HEADS)]
    outs = {}
    for c in range(nch):
        heads = range(GDN_HEADS)
        g_last = [gcs[c, h][C - 1:C, :] for h in heads]
        vn = [sols[c, h][:, :HEAD_DIM] - _mm1(sols[c, h][:, HEAD_DIM:], state[h]) for h in heads]
        o_in = [_mm1(qs[c, h], state[h]) for h in heads]
        kd_t = [(ks[c, h] * jnp.exp(g_last[h] - gcs[c, h])).T for h in heads]
        for h in heads:
            outs[c, h] = o_in[h] + _mm1(atts[c, h], vn[h])
        state = [state[h] * jnp.exp(g_last[h]) + _mm1(kd_t[h], vn[h]) for h in heads]
    for h in range(GDN_HEADS):
        s_ref[h] = state[h]
    row_outs = []
    for c in range(nch):
        head_outs = []
        for h in range(GDN_HEADS):
            o = outs[c, h]
            o = o * lax.rsqrt(jnp.mean(o * o, axis=-1, keepdims=True) + EPS) * ng_ref[...]
            zh = z[c * C:(c + 1) * C, h * HEAD_DIM:(h + 1) * HEAD_DIM]
            head_outs.append(o * (zh * jax.nn.sigmoid(zh)))
        row_outs.append(jnp.concatenate(head_outs, axis=1))
    o_ref[0] = jnp.concatenate(row_outs, axis=0)

    @pl.when(pl.program_id(1) == pl.num_programs(1) - 1)
    def _():
        sfin_ref[0] = s_ref[...]


def _gdn(P, blk_q, blk_misc, ga, conv_buf, S0, conv_w, A_log, dt_bias, norm_g, rt, n_valid):
    B, T, _ = P.shape
    f32 = jnp.float32
    abt = jnp.pad(jnp.transpose(ga, (0, 2, 1)), ((0, 0), (0, SUBLANES - GDN_HEADS), (0, 0)))
    buf8 = jnp.pad(conv_buf, ((0, 0), (GDN_HALO - (GDN_CONV - 1), 0), (0, 0)))
    lane_vec = lambda v: jnp.zeros((1, LANES), f32).at[0, GDN_A_LANE:GDN_A_LANE + GDN_HEADS].set(v)
    col_vec = lambda v: jnp.zeros((SUBLANES, 1), f32).at[0:GDN_HEADS, 0].set(v)
    col = lambda j: pl.BlockSpec((1, rt, GDN_W), lambda b, i: (b, i, blk_q + j))
    state = pl.BlockSpec((1, GDN_HEADS, HEAD_DIM, HEAD_DIM), lambda b, i: (b, 0, 0, 0))
    return pl.pallas_call(
        functools.partial(_gdn_kernel, n_valid=n_valid),
        grid=(B, T // rt),
        in_specs=[col(0), col(1), col(2), col(3),
                  pl.BlockSpec((1, rt, LANES), lambda b, i: (b, i, blk_misc)),
                  pl.BlockSpec((1, SUBLANES, rt), lambda b, i: (b, 0, i)),
                  pl.BlockSpec((1, GDN_HALO, 3 * GDN_W), lambda b, i: (b, 0, 0)), state,
                  _resident((GDN_CONV, 3 * GDN_W)), _resident((1, HEAD_DIM)),
                  _resident((1, LANES)), _resident((1, LANES)), _resident((SUBLANES, 1)), _resident((SUBLANES, 1))],
        out_specs=[pl.BlockSpec((1, rt, GDN_W), lambda b, i: (b, i, 0)), state],
        out_shape=[jax.ShapeDtypeStruct((B, T, GDN_W), f32),
                   jax.ShapeDtypeStruct((B, GDN_HEADS, HEAD_DIM, HEAD_DIM), f32)],
        scratch_shapes=[pltpu.VMEM((GDN_HEADS, HEAD_DIM, HEAD_DIM), f32),
                        pltpu.VMEM((rt + GDN_HALO, 3 * GDN_W), f32)],
        compiler_params=pltpu.CompilerParams(dimension_semantics=("arbitrary", "arbitrary"),
                                             vmem_limit_bytes=VMEM_LIMIT),
        name="gated_deltanet",
    )(P, P, P, P, P, abt, buf8, S0, conv_w, norm_g[None], lane_vec(A_log), lane_vec(dt_bias),
      col_vec(A_log), col_vec(dt_bias))


def _prep_layer(p):
    w_main, w_mg = _prep_w_in(p["w_in"])
    q = dict(p)
    q.update(w_main=w_main, w_mg=w_mg, w_branch_b=p["w_branch"].astype(jnp.bfloat16),
             w_out_b=p["w_out"].astype(jnp.bfloat16), ffn_up_b=p["ffn_up"].astype(jnp.bfloat16),
             ffn_down_b=p["ffn_down"].astype(jnp.bfloat16))
    return q


def trunk_layer(x, mod, pos0, nsa_past, win_buf, conv_buf, ret_s, gdn_buf, gdn_s, ffn_buf, p, rel_bias):
    B, T, _ = x.shape
    M = B * T
    per_row = T < MERGE_TM
    if per_row:
        sh1, sc1, gt1, sh2, sc2, gt2 = [jnp.repeat(m, T, axis=0)[None] for m in jnp.split(mod, 6, axis=-1)]
    else:
        sh1, sc1, gt1, sh2, sc2, gt2 = [m[:, None, :] for m in jnp.split(mod, 6, axis=-1)]
    norms = p["norms"][:, None, :]
    x2 = x.reshape(M, D_MODEL)
    P = _proj(x2, norms[0], sc1, sh1, p["w_main"], T, M if per_row else PROJ_TM).reshape(B, T, PROJ_COLS)
    nkv = P[:, :, PROJ_KV:PROJ_KV + 6 * KVW]
    ngt = P[:, :, PROJ_MISC:PROJ_MISC + 3 * NSA_HEADS]
    ga = P[:, :, PROJ_MISC + 3 * NSA_HEADS:PROJ_MISC + 3 * NSA_HEADS + GDN_HEADS]
    gb = P[:, :, PROJ_MISC + 3 * NSA_HEADS + GDN_HEADS:PROJ_MISC + 3 * NSA_HEADS + 2 * GDN_HEADS]
    keep = min(NSA_WINDOW, T)
    kw = P[:, T - keep:, PROJ_KV + 4 * KVW:PROJ_KV + 6 * KVW].reshape(B, keep, 2, NSA_KV_HEADS, HEAD_DIM)
    if nsa_past is None:
        o_nsa = _nsa_prompt(P, PROJ_Q // (NSA_HEADS * HEAD_DIM), P, PROJ_KV // (6 * KVW), ngt,
                            p["cmp_pool"], p["cmp_pe"], rel_bias)
        new_win = kw
    else:
        cache, layer, page_table = nsa_past
        o_nsa = _nsa_sample(P[:, :, PROJ_Q:PROJ_Q + NSA_HEADS * HEAD_DIM], nkv, ngt, cache, layer, page_table,
                            win_buf, p["cmp_pool"], p["cmp_pe"], rel_bias)
        real = jnp.concatenate([win_buf, kw], axis=1)
        new_win = real[:, real.shape[1] - min(NSA_WINDOW, real.shape[1]):]
    o_conv, new_conv = _conformer(P, PROJ_UCV // (2 * CONV_CH), conv_buf, p["conv_dw"], p["conv_dw_b"],
                                  p["conv_ln_g"], p["conv_ln_b"], T if per_row else CONF_TM)
    if per_row:
        Pr = jnp.pad(P[:, :, PROJ_RET:PROJ_RET + 4 * RET_W], ((0, 0), (0, RET_CHUNK - T), (0, 0)))
        o_ret, new_ret = _retention(Pr, 0, ret_s, pos0 + jnp.arange(RET_CHUNK, dtype=jnp.int32), p["ret_gn"],
                                    RET_CHUNK, T)
        o_ret = o_ret[:, :T]
    else:
        o_ret, new_ret = _retention(P, PROJ_RET // RET_W, ret_s, pos0 + jnp.arange(T, dtype=jnp.int32), p["ret_gn"],
                                    RET_TM, RET_TM)
    gdn_w = (p["gdn_conv_w"], p["gdn_A_log"], p["gdn_dt_bias"], p["gdn_norm"])
    if per_row:
        pad_rows = ((0, 0), (0, GDN_CHUNK - T), (0, 0))
        Pg = jnp.pad(P[:, :, PROJ_GQKV:PROJ_COLS], pad_rows)
        o_gdn, new_gdn = _gdn(Pg, 0, (PROJ_MISC - PROJ_GQKV) // LANES, jnp.pad(ga, pad_rows), gdn_buf, gdn_s, *gdn_w,
                              GDN_CHUNK, T)
        o_gdn = o_gdn[:, :T]
    else:
        o_gdn, new_gdn = _gdn(P, PROJ_GQKV // GDN_W, PROJ_MISC // LANES, ga, gdn_buf, gdn_s, *gdn_w,
                              GDN_RT, GDN_CHUNK)
    new_gdn_buf = jnp.concatenate([gdn_buf, P[:, :, PROJ_GQKV:PROJ_GQKV + 3 * GDN_W]], axis=1)[:, T:]
    branches = [o.reshape(M, BRANCH_W) for o in (o_nsa, o_conv, o_ret, o_gdn)]
    x1 = _merge(x2, norms[0], sc1, sh1, gt1, norms[1], branches, p["w_mg"], p["w_branch_b"], p["w_out_b"],
                T, M if per_row else MERGE_TM).reshape(B, T, D_MODEL)
    if per_row:
        x_out, new_ffn = _ffn_rows(x1, norms[2], sc2, sh2, gt2, norms[3], ffn_buf, p["ffn_up_b"], p["ffn_dw"],
                                   p["ffn_down_b"])
    else:
        x_out, new_ffn = _ffn(x1, norms[2], sc2, sh2, gt2, norms[3], ffn_buf, p["ffn_up_b"], p["ffn_dw"],
                              p["ffn_down_b"], FFN_TM)
    kv_rows = nkv[:, :, :4 * KVW].reshape(B, T, 4, NSA_KV_HEADS, HEAD_DIM)
    return x_out, (kv_rows, new_win, new_conv, new_ret, new_gdn_buf, new_gdn, new_ffn)


def kernel(x_prompt, x_sample, cache_nsa_kv, cache_nsa_win, state_conv, state_ret, state_gdn_conv, state_gdn,
           state_ffn_conv, page_table, c_prompt, c_sample, w_ada, b_ada, norms, w_in, cmp_pool, cmp_pe, rel_bias,
           conv_dw, conv_dw_b, conv_ln_g, conv_ln_b, ret_gn, gdn_conv_w, gdn_A_log, gdn_dt_bias, gdn_norm,
           w_branch, w_out, ffn_up, ffn_dw, ffn_down):
    B = x_prompt.shape[0]
    Bd = x_sample.shape[0]
    past = page_table.shape[1] * PAGE_SIZE
    mods = _adaln(jnp.concatenate([c_prompt, c_sample], axis=0), w_ada, b_ada)
    layer_w = {"norms": norms, "w_in": w_in, "cmp_pool": cmp_pool,
               "cmp_pe": cmp_pe, "conv_dw": conv_dw, "conv_dw_b": conv_dw_b, "conv_ln_g": conv_ln_g,
               "conv_ln_b": conv_ln_b, "ret_gn": ret_gn, "gdn_conv_w": gdn_conv_w, "gdn_A_log": gdn_A_log,
               "gdn_dt_bias": gdn_dt_bias, "gdn_norm": gdn_norm, "w_branch": w_branch, "w_out": w_out,
               "ffn_up": ffn_up, "ffn_dw": ffn_dw, "ffn_down": ffn_down}
    yp, ys = x_prompt, x_sample
    st_p, st_s = [], []
    for l in range(DEPTH):
        p = _prep_layer({name: w[l] for name, w in layer_w.items()})
        yp, sp = trunk_layer(
            yp, mods[l, :B], 0, None, None,
            jnp.zeros((B, CONV_WIDTH - 1, CONV_CH), x_prompt.dtype),
            jnp.zeros((B, RET_HEADS, HEAD_DIM, HEAD_DIM), jnp.float32),
            jnp.zeros((B, GDN_CONV - 1, 3 * GDN_W), x_prompt.dtype),
            jnp.zeros((B, GDN_HEADS, HEAD_DIM, HEAD_DIM), jnp.float32),
            jnp.zeros((B, FFN_CONV - 1, D_FF), x_prompt.dtype),
            p, rel_bias)
        ys, ss = trunk_layer(
            ys, mods[l, B:], past, (cache_nsa_kv, l, page_table), cache_nsa_win[l], state_conv[l], state_ret[l],
            state_gdn_conv[l], state_gdn[l], state_ffn_conv[l], p, rel_bias)
        st_p.append(sp)
        st_s.append(ss)

    def stack(outs, i, axis):
        return jnp.stack([o[i] for o in outs], axis=axis)

    kv_p, kv_s = stack(st_p, 0, 1), stack(st_s, 0, 1)
    win_p, win_s = stack(st_p, 1, 0), stack(st_s, 1, 0)
    conv_p, conv_s = stack(st_p, 2, 0), stack(st_s, 2, 0)
    ret_p, ret_s = stack(st_p, 3, 0), stack(st_s, 3, 0)
    gdnc_p, gdnc_s = stack(st_p, 4, 0), stack(st_s, 4, 0)
    gdn_p, gdn_s = stack(st_p, 5, 0), stack(st_s, 5, 0)
    ffn_p, ffn_s = stack(st_p, 6, 0), stack(st_s, 6, 0)
    return (yp, ys, kv_p, kv_s, win_p, win_s, conv_p, conv_s, ret_p, ret_s, gdnc_p, gdnc_s, gdn_p, gdn_s, ffn_p, ffn_s)
```

```python
import functools
import math

import jax
import jax.numpy as jnp
import numpy as np
from jax import lax
from jax.experimental import pallas as pl
from jax.experimental.pallas import tpu as pltpu

D_MODEL = 1024
DEPTH = 2
PAGE_SIZE = 128

HEAD_DIM = 64
NSA_HEADS = 4
NSA_KV_HEADS = 2
NSA_GROUP = NSA_HEADS // NSA_KV_HEADS
NSA_BLOCK = 64
NSA_TOPK = 16
NSA_WINDOW = 512
NUM_BUCKETS = 32
MAX_DISTANCE = 128
CONV_CH = D_MODEL // 4
CONV_WIDTH = 31
RET_HEADS = 4
RET_W = RET_HEADS * HEAD_DIM
RET_CHUNK = 64
ROPE_BASE = 10000.0
GDN_HEADS = 4
GDN_W = GDN_HEADS * HEAD_DIM
GDN_CONV = 4
GDN_CHUNK = 64
D_FF = 2816
FFN_CONV = 3
N_BRANCH = 4
BRANCH_W = NSA_HEADS * HEAD_DIM
EPS = 1e-6
NEG_INF = -1e30
IN_SPLITS = (NSA_HEADS * HEAD_DIM, 6 * NSA_KV_HEADS * HEAD_DIM, 3 * NSA_HEADS, 2 * CONV_CH, 4 * RET_W,
             4 * GDN_W + 2 * GDN_HEADS, N_BRANCH * D_MODEL)
IN_COLS = sum(IN_SPLITS)

LANES = 128
SUBLANES = 8
VMEM_LIMIT = 56 * 1024 * 1024


def _round_up(a, m):
    return -(-a // m) * m


ADA_TN = 2048


def _adaln_kernel(c_ref, w_ref, b_ref, o_ref):
    c = c_ref[...]
    act = (c * jax.nn.sigmoid(c)).astype(jnp.bfloat16)
    o_ref[0] = jnp.dot(act, w_ref[0].astype(jnp.bfloat16), preferred_element_type=jnp.float32) + b_ref[0]


def _adaln(c, w_ada, b_ada):
    R = c.shape[0]
    L, _, N = w_ada.shape
    Rp = _round_up(R, SUBLANES)
    cp = jnp.pad(c, ((0, Rp - R), (0, 0)))
    out = pl.pallas_call(
        _adaln_kernel,
        grid=(L, N // ADA_TN),
        in_specs=[pl.BlockSpec((Rp, D_MODEL), lambda l, j: (0, 0)),
                  pl.BlockSpec((1, D_MODEL, ADA_TN), lambda l, j: (l, 0, j)),
                  pl.BlockSpec((1, 1, ADA_TN), lambda l, j: (l, 0, j))],
        out_specs=pl.BlockSpec((1, Rp, ADA_TN), lambda l, j: (l, 0, j)),
        out_shape=jax.ShapeDtypeStruct((L, Rp, N), jnp.float32),
        compiler_params=pltpu.CompilerParams(dimension_semantics=("arbitrary", "arbitrary"),
                                             vmem_limit_bytes=VMEM_LIMIT),
        name="adaln",
    )(cp, w_ada, b_ada[:, None, :])
    return out[:, :R]


NSA_TQ = 128
BLOCKS_PER_TILE = NSA_TQ // NSA_BLOCK


def _t5_thresholds():
    n = np.arange(0, 2 * MAX_DISTANCE)
    exact = NUM_BUCKETS // 2
    large = exact + (np.log(np.maximum(n, 1).astype(np.float32) / np.float32(exact))
                     / np.float32(math.log(MAX_DISTANCE / exact)) * (NUM_BUCKETS - exact)).astype(np.int32)
    bucket = np.where(n < exact, n, np.minimum(large, NUM_BUCKETS - 1))
    return tuple(int(np.argmax(bucket >= k)) for k in range(1, NUM_BUCKETS))


_T5_THR = _t5_thresholds()


def _bias_from_dist(dist, rb_ref, h):
    v = jnp.full(dist.shape, rb_ref[NUM_BUCKETS - 1, h], jnp.float32)
    for k in range(NUM_BUCKETS - 2, -1, -1):
        v = jnp.where(dist < _T5_THR[k], rb_ref[k, h], v)
    return v


def _dot_nt(a, b):
    return lax.dot_general(a, b, (((1,), (1,)), ((), ())), preferred_element_type=jnp.float32)


def _flash_tile(carry, k_t, vt_t, qs, mask_add):
    m, l, acc = carry
    s = _dot_nt(k_t, qs) + mask_add
    m_new = jnp.maximum(m, jnp.max(s, axis=0, keepdims=True))
    p = jnp.exp(s - m_new)
    alpha = jnp.exp(m - m_new)
    l = alpha * l + jnp.sum(p, axis=0, keepdims=True)
    acc = alpha * acc + jnp.dot(vt_t, p.astype(jnp.bfloat16), preferred_element_type=jnp.float32)
    return m_new, l, acc


FAR_GROUP = 8
NEAR_ROWS = 2 * NSA_TQ
WIN_ROWS = NSA_WINDOW + NSA_TQ
assert WIN_ROWS <= FAR_GROUP * NSA_TQ


def _nsa_prompt_kernel(rb_ref, q_ref, kv_ref, gt_ref, poolt_ref, pe_ref, o_ref,
                       kc_ref, vc_ref, ksel_ref, vselt_ref, kwin_ref, vwint_ref, near_ref, wtbl_ref, score_ref,
                       sel_ref, s_ref, p_ref):
    b = pl.program_id(0)
    qi = pl.program_id(1)
    T = kv_ref.shape[1]
    nb = T // NSA_BLOCK
    topk = min(NSA_TOPK, nb)
    TQ = NSA_TQ
    CH = 512
    f32, bf16 = jnp.float32, jnp.bfloat16
    kvs = range(NSA_KV_HEADS)

    @pl.when((b == 0) & (qi == 0))
    def _tables():
        for kvg in kvs:
            ksel_ref[kvg, 0:TQ, :] = jnp.zeros((TQ, HEAD_DIM), bf16)
            vselt_ref[kvg, :, 0:TQ] = jnp.zeros((HEAD_DIM, TQ), bf16)
            kwin_ref[kvg, 0:NSA_WINDOW, :] = jnp.zeros((NSA_WINDOW, HEAD_DIM), bf16)
            vwint_ref[kvg, :, 0:NSA_WINDOW] = jnp.zeros((HEAD_DIM, NSA_WINDOW), bf16)
        d_near = (lax.broadcasted_iota(jnp.int32, (NEAR_ROWS, TQ), 1) + TQ
                  - lax.broadcasted_iota(jnp.int32, (NEAR_ROWS, TQ), 0))
        d_win = (lax.broadcasted_iota(jnp.int32, (WIN_ROWS, TQ), 1) + NSA_WINDOW
                 - lax.broadcasted_iota(jnp.int32, (WIN_ROWS, TQ), 0))
        for h in range(NSA_HEADS):
            kvg, g = divmod(h, NSA_GROUP)
            lanes = slice(g * TQ, (g + 1) * TQ)
            near_ref[kvg, :, lanes] = jnp.where(
                d_near >= 0, _bias_from_dist(d_near, rb_ref, h) - rb_ref[NUM_BUCKETS - 1, h], NEG_INF)
            wtbl_ref[kvg, :, lanes] = jnp.where((d_win >= 0) & (d_win < NSA_WINDOW),
                                                _bias_from_dist(d_win, rb_ref, h), NEG_INF)

    @pl.when(qi == 0)
    def _prologue():
        def chunk(i, carry):
            r = pl.multiple_of(i * CH, CH)
            rs = pl.multiple_of(i * CH + TQ, TQ)
            rw = pl.multiple_of(i * CH + NSA_WINDOW, TQ)
            rb8 = pl.multiple_of(i * (CH // NSA_BLOCK), CH // NSA_BLOCK)
            for kvg in kvs:
                def col(c):
                    lo = c * NSA_KV_HEADS * HEAD_DIM + kvg * HEAD_DIM
                    return kv_ref[0, pl.ds(r, CH), lo:lo + HEAD_DIM]
                for c, dst in ((0, kc_ref), (1, vc_ref)):
                    x = col(c).reshape(CH // NSA_BLOCK, NSA_BLOCK, HEAD_DIM) + pe_ref[c][None]
                    dst[kvg, pl.ds(rb8, CH // NSA_BLOCK), :] = jnp.sum(x * poolt_ref[:, c:c + 1][None], axis=1)
                ksel_ref[kvg, pl.ds(rs, CH), :] = col(2).astype(bf16)
                vselt_ref[kvg, :, pl.ds(rs, CH)] = col(3).T.astype(bf16)
                kwin_ref[kvg, pl.ds(rw, CH), :] = col(4).astype(bf16)
                vwint_ref[kvg, :, pl.ds(rw, CH)] = col(5).T.astype(bf16)
            return carry
        lax.fori_loop(0, T // CH, chunk, 0)

    q = q_ref[0]
    gates = jax.nn.sigmoid(gt_ref[0])
    n_io = lax.broadcasted_iota(jnp.int32, (nb, TQ), 0)
    t_io = lax.broadcasted_iota(jnp.int32, (nb, TQ), 1)
    dist_c = qi * TQ + t_io - (n_io * NSA_BLOCK + NSA_BLOCK - 1)
    vis_c = dist_c >= 0
    vis_c2 = jnp.concatenate([vis_c, vis_c], axis=1)
    cur = (qi * TQ + t_io) // NSA_BLOCK
    forced = (n_io == 0) | (n_io == cur) | (n_io == cur - 1)
    q0 = pl.multiple_of(qi * TQ, TQ)

    qs, oc, score = [], [], []
    for kvg in kvs:
        base = kvg * NSA_GROUP * HEAD_DIM
        qk = jnp.concatenate([q[:, base + g * HEAD_DIM: base + (g + 1) * HEAD_DIM] for g in range(NSA_GROUP)], axis=0)
        qs.append((qk * HEAD_DIM ** -0.5).astype(bf16))
        sc = _dot_nt(kc_ref[kvg].astype(bf16), qs[kvg])
        bias_c = jnp.concatenate([_bias_from_dist(dist_c, rb_ref, kvg * NSA_GROUP + g) for g in range(NSA_GROUP)],
                                 axis=1)
        sc = jnp.where(vis_c2, sc + bias_c, NEG_INF)
        e = jnp.exp(sc - jnp.max(sc, axis=0, keepdims=True))
        p_c = e / jnp.sum(e, axis=0, keepdims=True) * vis_c2.astype(f32)
        oc.append(jnp.dot(vc_ref[kvg].T.astype(bf16), p_c.astype(bf16), preferred_element_type=f32))
        score.append(jnp.where(n_io <= cur, jnp.where(forced, 2.0, p_c[:, :TQ] + p_c[:, TQ:]), -1.0))
        score_ref[kvg] = score[kvg]

    def rank_body(mi, ranks):
        out = []
        for kvg in kvs:
            row = score_ref[kvg, pl.ds(mi, 1), :]
            beats = (row > score[kvg]) | ((row == score[kvg]) & (mi < n_io))
            out.append(ranks[kvg] + beats.astype(jnp.int32))
        return tuple(out)

    ranks = lax.fori_loop(0, BLOCKS_PER_TILE * (qi + 1), rank_body,
                          tuple(jnp.zeros((nb, TQ), jnp.int32) for _ in kvs))
    for kvg in kvs:
        sel_ref[kvg] = jnp.where((ranks[kvg] < topk) & (n_io <= cur), 0.0, NEG_INF)

    def mask_rows(kvg, blk0, nblk, limit):
        rows = []
        for u in range(nblk):
            blk = blk0 + u
            ok = (blk >= 0) & (blk < limit)
            row = sel_ref[kvg, pl.ds(jnp.clip(blk, 0, nb - 1), 1), :]
            rows.append(jnp.broadcast_to(jnp.where(ok, row, NEG_INF), (NSA_BLOCK, TQ)))
        mm = jnp.concatenate(rows, axis=0)
        return jnp.concatenate([mm, mm], axis=1)

    carries = []
    for kvg in kvs:
        k_n = ksel_ref[kvg, pl.ds(q0, NEAR_ROWS), :]
        vt_n = vselt_ref[kvg, :, pl.ds(q0, NEAR_ROWS)]
        s = (_dot_nt(k_n, qs[kvg]) + near_ref[kvg]
             + mask_rows(kvg, BLOCKS_PER_TILE * (qi - 1), 2 * BLOCKS_PER_TILE, nb))
        m = jnp.max(s, axis=0, keepdims=True)
        p = jnp.exp(s - m)
        carries.append((m, jnp.sum(p, axis=0, keepdims=True),
                        jnp.dot(vt_n, p.astype(bf16), preferred_element_type=f32)))

    n_far = jnp.maximum(qi - 1, 0)
    rows_far = FAR_GROUP * TQ

    def far_body(i, cs):
        out = []
        for kvg in kvs:
            m_old, l_old, acc_old = cs[kvg]
            r = pl.multiple_of(TQ + i * rows_far, TQ)
            m_new = m_old
            for u in range(FAR_GROUP):
                s_u = (_dot_nt(ksel_ref[kvg, pl.ds(r + u * TQ, TQ), :], qs[kvg])
                       + mask_rows(kvg, (i * FAR_GROUP + u) * BLOCKS_PER_TILE, BLOCKS_PER_TILE,
                                   n_far * BLOCKS_PER_TILE))
                s_ref[kvg, u * TQ:(u + 1) * TQ, :] = s_u
                m_new = jnp.maximum(m_new, jnp.max(s_u, axis=0, keepdims=True))
            l_new = jnp.exp(m_old - m_new) * l_old
            for u in range(FAR_GROUP):
                p_u = jnp.exp(s_ref[kvg, u * TQ:(u + 1) * TQ, :] - m_new)
                l_new = l_new + jnp.sum(p_u, axis=0, keepdims=True)
                p_ref[kvg, u * TQ:(u + 1) * TQ, :] = p_u.astype(bf16)
            acc = jnp.exp(m_old - m_new) * acc_old + jnp.dot(vselt_ref[kvg, :, pl.ds(r, rows_far)], p_ref[kvg],
                                                             preferred_element_type=f32)
            out.append((m_new, l_new, acc))
        return tuple(out)

    carries = lax.fori_loop(0, (n_far + FAR_GROUP - 1) // FAR_GROUP, far_body, tuple(carries))

    w_io = lax.broadcasted_iota(jnp.int32, (TQ, 2 * TQ), 0)
    outs = []
    for kvg in kvs:
        m_s, l_s, acc_s = carries[kvg]
        o_s = acc_s / l_s
        m_w = jnp.full((1, 2 * TQ), NEG_INF, f32)
        for u in range(WIN_ROWS // TQ):
            s_u = (_dot_nt(kwin_ref[kvg, pl.ds(q0 + u * TQ, TQ), :], qs[kvg])
                   + wtbl_ref[kvg, u * TQ:(u + 1) * TQ, :])
            s_u = jnp.where(w_io >= NSA_WINDOW - u * TQ - q0, s_u, NEG_INF)
            s_ref[kvg, u * TQ:(u + 1) * TQ, :] = s_u
            m_w = jnp.maximum(m_w, jnp.max(s_u, axis=0, keepdims=True))
        l_w = jnp.zeros((1, 2 * TQ), f32)
        for u in range(WIN_ROWS // TQ):
            p_u = jnp.exp(s_ref[kvg, u * TQ:(u + 1) * TQ, :] - m_w)
            l_w = l_w + jnp.sum(p_u, axis=0, keepdims=True)
            p_ref[kvg, u * TQ:(u + 1) * TQ, :] = p_u.astype(bf16)
        o_w = jnp.dot(vwint_ref[kvg, :, pl.ds(q0, WIN_ROWS)], p_ref[kvg, 0:WIN_ROWS, :],
                      preferred_element_type=f32) / l_w
        for g in range(NSA_GROUP):
            h = kvg * NSA_GROUP + g
            lanes = slice(g * TQ, (g + 1) * TQ)
            o = (gates[h:h + 1] * oc[kvg][:, lanes] + gates[NSA_HEADS + h:NSA_HEADS + h + 1] * o_s[:, lanes]
                 + gates[2 * NSA_HEADS + h:2 * NSA_HEADS + h + 1] * o_w[:, lanes])
            outs.append(o.T)
    o_ref[0] = jnp.concatenate(outs, axis=1)


def _nsa_prompt(q, q_blk, kv, kv_blk, gate_logits, cmp_pool, cmp_pe, rel_bias):
    B, T, _ = q.shape
    nb = T // NSA_BLOCK
    assert T % 512 == 0 and nb % SUBLANES == 0 and (T // NSA_TQ) % FAR_GROUP == 0
    TQ = NSA_TQ
    gt = jnp.transpose(gate_logits, (0, 2, 1))
    f32, bf16 = jnp.float32, jnp.bfloat16
    return pl.pallas_call(
        _nsa_prompt_kernel,
        grid=(B, T // TQ),
        in_specs=[
            pl.BlockSpec(memory_space=pltpu.SMEM),
            pl.BlockSpec((1, TQ, NSA_HEADS * HEAD_DIM), lambda b, i: (b, i, q_blk)),
            pl.BlockSpec((1, T, 6 * NSA_KV_HEADS * HEAD_DIM), lambda b, i: (b, 0, kv_blk)),
            pl.BlockSpec((1, 3 * NSA_HEADS, TQ), lambda b, i: (b, 0, i)),
            pl.BlockSpec((NSA_BLOCK, 2), lambda b, i: (0, 0)),
            pl.BlockSpec((2, NSA_BLOCK, HEAD_DIM), lambda b, i: (0, 0, 0)),
        ],
        out_specs=pl.BlockSpec((1, TQ, NSA_HEADS * HEAD_DIM), lambda b, i: (b, i, 0)),
        out_shape=jax.ShapeDtypeStruct((B, T, NSA_HEADS * HEAD_DIM), f32),
        scratch_shapes=[
            pltpu.VMEM((NSA_KV_HEADS, nb, HEAD_DIM), f32),
            pltpu.VMEM((NSA_KV_HEADS, nb, HEAD_DIM), f32),
            pltpu.VMEM((NSA_KV_HEADS, T + TQ, HEAD_DIM), bf16),
            pltpu.VMEM((NSA_KV_HEADS, HEAD_DIM, T + TQ), bf16),
            pltpu.VMEM((NSA_KV_HEADS, T + NSA_WINDOW, HEAD_DIM), bf16),
            pltpu.VMEM((NSA_KV_HEADS, HEAD_DIM, T + NSA_WINDOW), bf16),
            pltpu.VMEM((NSA_KV_HEADS, NEAR_ROWS, NSA_GROUP * TQ), f32),
            pltpu.VMEM((NSA_KV_HEADS, WIN_ROWS, NSA_GROUP * TQ), f32),
            pltpu.VMEM((NSA_KV_HEADS, nb, TQ), f32),
            pltpu.VMEM((NSA_KV_HEADS, nb, TQ), f32),
            pltpu.VMEM((NSA_KV_HEADS, FAR_GROUP * TQ, NSA_GROUP * TQ), f32),
            pltpu.VMEM((NSA_KV_HEADS, FAR_GROUP * TQ, NSA_GROUP * TQ), bf16),
        ],
        compiler_params=pltpu.CompilerParams(dimension_semantics=("arbitrary", "arbitrary"),
                                             vmem_limit_bytes=VMEM_LIMIT),
        name="nsa_prompt",
    )(rel_bias, q, kv, gt, jnp.transpose(cmp_pool), cmp_pe)


PAGES_PER_STEP = 32
SEL_TILE = 2048
KVW = NSA_KV_HEADS * HEAD_DIM


def _nsa_sample_kernel(pt_ref, rb_ref, *refs):
    PPS = PAGES_PER_STEP
    pages = refs[:PPS]
    (q_ref, kvn_ref, gt_ref, win_ref, poolm_ref, cconst_ref, o_ref, cmp_ref, kselt_ref, vselt_ref,
     ssel_ref) = refs[PPS:]
    s_id = pl.program_id(1)
    f32, bf16 = jnp.float32, jnp.bfloat16
    Tn = q_ref.shape[1]
    past = kselt_ref.shape[1]
    nbp = past // NSA_BLOCK
    Wb = win_ref.shape[3]
    bpp = PAGE_SIZE // NSA_BLOCK
    R = NSA_HEADS * Tn

    rows = []
    for k in range(PPS):
        parts = []
        for c in range(2):
            a = _dot_nt(poolm_ref[c], pages[k][0, 0, c].astype(bf16))
            parts.append(a[0:bpp] + a[SUBLANES:SUBLANES + bpp] + cconst_ref[c:c + 1])
        rows.append(jnp.concatenate(parts, axis=1))
        r = pl.multiple_of((s_id * PPS + k) * PAGE_SIZE, PAGE_SIZE)
        kselt_ref[:, pl.ds(r, PAGE_SIZE)] = pages[k][0, 0, 2].astype(bf16)
        vselt_ref[:, pl.ds(r, PAGE_SIZE)] = pages[k][0, 0, 3].astype(bf16)
    cmp_ref[pl.ds(pl.multiple_of(s_id * PPS * bpp, PPS * bpp), PPS * bpp), :] = jnp.concatenate(rows, axis=0)

    @pl.when(s_id == pl.num_programs(1) - 1)
    def _attend():
        def per_head(fn):
            return jnp.concatenate([fn(h) for h in range(NSA_HEADS)], axis=0)

        q = q_ref[0] * HEAD_DIM ** -0.5
        zero = jnp.zeros((Tn, HEAD_DIM), f32)

        def q_rows(h):
            qh = q[:, h * HEAD_DIM:(h + 1) * HEAD_DIM]
            return jnp.concatenate([qh, zero] if h < NSA_GROUP else [zero, qh], axis=1)

        q2 = per_head(q_rows).astype(bf16)
        kvn = kvn_ref[0]
        pad = jnp.zeros((LANES - Tn, KVW), f32)

        def new_rows(c):
            return jnp.concatenate([kvn[:, c * KVW:(c + 1) * KVW], pad], axis=0).astype(bf16)

        tn_io = lax.broadcasted_iota(jnp.int32, (Tn, LANES), 0)
        jn_io = lax.broadcasted_iota(jnp.int32, (Tn, LANES), 1)
        d_new = tn_io - jn_io
        keep_new = per_head(lambda h: (d_new >= 0) & (jn_io < Tn))
        bias_new = per_head(lambda h: _bias_from_dist(d_new, rb_ref, h))
        far = per_head(lambda h: jnp.full((Tn, 1), rb_ref[NUM_BUCKETS - 1, h], f32))

        kc = cmp_ref[:, 0:KVW].astype(bf16)
        vc = cmp_ref[:, KVW:2 * KVW].astype(bf16)
        n_io = lax.broadcasted_iota(jnp.int32, (Tn, nbp), 1)
        t_io = lax.broadcasted_iota(jnp.int32, (Tn, nbp), 0)
        dist_c = past + t_io - (n_io * NSA_BLOCK + NSA_BLOCK - 1)
        sc = _dot_nt(q2, kc) + per_head(lambda h: _bias_from_dist(dist_c, rb_ref, h))
        e = jnp.exp(sc - jnp.max(sc, axis=1, keepdims=True))
        p_c = e / jnp.sum(e, axis=1, keepdims=True)
        oc = jnp.dot(p_c.astype(bf16), vc, preferred_element_type=f32)

        topk = min(NSA_TOPK, nbp + 1)
        m_io = lax.broadcasted_iota(jnp.int32, (nbp, nbp), 0)
        c_io = lax.broadcasted_iota(jnp.int32, (nbp, nbp), 1)
        lower = m_io < c_io
        forced = (n_io == 0) | (n_io == nbp - 1)
        sels = []
        for kvg in range(NSA_KV_HEADS):
            r0 = kvg * NSA_GROUP * Tn
            score = jnp.where(forced, 2.0, p_c[r0:r0 + Tn] + p_c[r0 + Tn:r0 + 2 * Tn])
            score_t = jnp.concatenate([score, jnp.zeros((LANES - Tn, nbp), f32)], axis=0).T
            ranks = []
            for t in range(Tn):
                colb = jnp.broadcast_to(score_t[:, t:t + 1], (nbp, nbp))
                rowb = jnp.broadcast_to(score[t:t + 1, :], (nbp, nbp))
                beats = (colb > rowb) | ((colb == rowb) & lower)
                ranks.append(jnp.sum(beats.astype(f32), axis=0, keepdims=True))
            sel = (jnp.concatenate(ranks, axis=0) < topk - 1).astype(f32)
            sels += [sel] * NSA_GROUP
        sel_rows = jnp.concatenate(sels, axis=0).astype(bf16)

        bpt = SEL_TILE // NSA_BLOCK
        expand = (lax.broadcasted_iota(jnp.int32, (bpt, SEL_TILE), 0)
                  == lax.broadcasted_iota(jnp.int32, (bpt, SEL_TILE), 1) // NSA_BLOCK).astype(bf16)
        d_last = LANES + tn_io - jn_io
        near = per_head(lambda h: _bias_from_dist(d_last, rb_ref, h))
        ntile = past // SEL_TILE
        s_new = jnp.where(keep_new, _dot_nt(q2, new_rows(2)) + bias_new, NEG_INF)
        m = jnp.max(s_new, axis=1, keepdims=True)
        for j in range(ntile):
            keep = jnp.dot(sel_rows[:, j * bpt:(j + 1) * bpt], expand, preferred_element_type=f32) > 0.5
            if j == ntile - 1:
                bias = jnp.concatenate([jnp.broadcast_to(far, (R, SEL_TILE - LANES)), near], axis=1)
            else:
                bias = far
            s_t = jnp.dot(q2, kselt_ref[:, j * SEL_TILE:(j + 1) * SEL_TILE], preferred_element_type=f32)
            s_t = jnp.where(keep, s_t + bias, NEG_INF)
            ssel_ref[:, j * SEL_TILE:(j + 1) * SEL_TILE] = s_t
            m = jnp.maximum(m, jnp.max(s_t, axis=1, keepdims=True))
        p_new = jnp.exp(s_new - m)
        l = jnp.sum(p_new, axis=1, keepdims=True)
        acc = jnp.dot(p_new.astype(bf16), new_rows(3), preferred_element_type=f32)
        for j in range(ntile):
            p = jnp.exp(ssel_ref[:, j * SEL_TILE:(j + 1) * SEL_TILE] - m)
            l = l + jnp.sum(p, axis=1, keepdims=True)
            acc = acc + _dot_nt(p.astype(bf16), vselt_ref[:, j * SEL_TILE:(j + 1) * SEL_TILE])
        o_s = acc / l

        tw_io = lax.broadcasted_iota(jnp.int32, (Tn, Wb), 0)
        cw_io = lax.broadcasted_iota(jnp.int32, (Tn, Wb), 1)
        d_w = Wb + tw_io - cw_io
        s_w = jnp.where(per_head(lambda h: d_w < NSA_WINDOW),
                        jnp.dot(q2, win_ref[0, 0].astype(bf16), preferred_element_type=f32)
                        + per_head(lambda h: _bias_from_dist(d_w, rb_ref, h)), NEG_INF)
        s_wn = jnp.where(keep_new, _dot_nt(q2, new_rows(4)) + bias_new, NEG_INF)
        s_all = jnp.concatenate([s_w, s_wn], axis=1)
        e = jnp.exp(s_all - jnp.max(s_all, axis=1, keepdims=True))
        p_w = (e / jnp.sum(e, axis=1, keepdims=True)).astype(bf16)
        o_w = _dot_nt(p_w[:, :Wb], win_ref[0, 1].astype(bf16)) + jnp.dot(p_w[:, Wb:], new_rows(5),
                                                                          preferred_element_type=f32)

        gates = jax.nn.sigmoid(gt_ref[0])
        outs = []
        for h in range(NSA_HEADS):
            rs = slice(h * Tn, (h + 1) * Tn)
            cs = slice((h // NSA_GROUP) * HEAD_DIM, (h // NSA_GROUP + 1) * HEAD_DIM)
            outs.append(gates[:, h:h + 1] * oc[rs, cs] + gates[:, NSA_HEADS + h:NSA_HEADS + h + 1] * o_s[rs, cs]
                        + gates[:, 2 * NSA_HEADS + h:2 * NSA_HEADS + h + 1] * o_w[rs, cs])
        o_ref[0] = jnp.concatenate(outs, axis=1)


def _nsa_sample(q, kv, gate_logits, cache, layer, page_table, win_buf, cmp_pool, cmp_pe, rel_bias):
    B, Tn, _ = q.shape
    npages = page_table.shape[1]
    past = npages * PAGE_SIZE
    Wb = win_buf.shape[1]
    PPS = PAGES_PER_STEP
    assert npages % PPS == 0 and past % SEL_TILE == 0 and Tn == SUBLANES and (past // NSA_BLOCK) % LANES == 0
    f32, bf16 = jnp.float32, jnp.bfloat16
    bpp = PAGE_SIZE // NSA_BLOCK
    cache_t = jnp.transpose(cache, (0, 1, 3, 4, 5, 2)).reshape(cache.shape[0], cache.shape[1], 4, KVW, PAGE_SIZE)
    win_t = jnp.transpose(win_buf, (0, 2, 3, 4, 1)).reshape(B, 2, KVW, Wb)
    r_io = np.arange(PAGE_SIZE)
    onehot = jnp.asarray((r_io[None, :] // NSA_BLOCK == np.arange(SUBLANES)[:, None]), f32)
    pool_full = onehot[None] * jnp.tile(cmp_pool, (1, bpp))[:, None, :]
    pool_hi = pool_full.astype(bf16)
    pool_lo = (pool_full - pool_hi.astype(f32)).astype(bf16)
    poolm = jnp.concatenate([pool_hi, pool_lo], axis=1)
    cconst = jnp.tile(jnp.sum(cmp_pool[:, :, None] * cmp_pe, axis=1), (1, NSA_KV_HEADS))

    def page_spec(k):
        return pl.BlockSpec((1, 1, 4, KVW, PAGE_SIZE), lambda b, s, pt: (pt[b, s * PPS + k], layer, 0, 0, 0))

    grid_spec = pltpu.PrefetchScalarGridSpec(
        num_scalar_prefetch=1,
        grid=(B, npages // PPS),
        in_specs=[pl.BlockSpec(memory_space=pltpu.SMEM)] + [page_spec(k) for k in range(PPS)] + [
            pl.BlockSpec((1, Tn, NSA_HEADS * HEAD_DIM), lambda b, s, pt: (b, 0, 0)),
            pl.BlockSpec((1, Tn, 6 * KVW), lambda b, s, pt: (b, 0, 0)),
            pl.BlockSpec((1, Tn, 3 * NSA_HEADS), lambda b, s, pt: (b, 0, 0)),
            pl.BlockSpec((1, 2, KVW, Wb), lambda b, s, pt: (b, 0, 0, 0)),
            pl.BlockSpec((2, 2 * SUBLANES, PAGE_SIZE), lambda b, s, pt: (0, 0, 0)),
            pl.BlockSpec((2, KVW), lambda b, s, pt: (0, 0)),
        ],
        out_specs=pl.BlockSpec((1, Tn, NSA_HEADS * HEAD_DIM), lambda b, s, pt: (b, 0, 0)),
        scratch_shapes=[
            pltpu.VMEM((past // NSA_BLOCK, 2 * KVW), f32),
            pltpu.VMEM((KVW, past), bf16),
            pltpu.VMEM((KVW, past), bf16),
            pltpu.VMEM((NSA_HEADS * Tn, past), f32),
        ],
    )
    return pl.pallas_call(
        _nsa_sample_kernel,
        grid_spec=grid_spec,
        out_shape=jax.ShapeDtypeStruct((B, Tn, NSA_HEADS * HEAD_DIM), f32),
        compiler_params=pltpu.CompilerParams(dimension_semantics=("arbitrary", "arbitrary"),
                                             vmem_limit_bytes=VMEM_LIMIT),
        name="nsa_sample",
    )(page_table, rel_bias, *([cache_t] * PPS), q, kv, gate_logits, win_t, poolm, cconst)


PROJ_KV, PROJ_Q, PROJ_UCV, PROJ_RET, PROJ_GQKV, PROJ_GZ, PROJ_MISC = 0, 768, 1024, 1536, 2560, 3328, 3584
PROJ_COLS = PROJ_MISC + LANES
PROJ_TM = 512
MERGE_TM = 256
FFN_TM = 256


def _prep_w_in(w_in):
    o = [int(v) for v in np.cumsum((0,) + IN_SPLITS)]
    small = o[5] + 4 * GDN_W
    pieces = [w_in[:, o[1]:o[2]], w_in[:, o[0]:o[1]], w_in[:, o[3]:o[4]], w_in[:, o[4]:o[5]], w_in[:, o[5]:small],
              w_in[:, o[2]:o[3]], w_in[:, small:o[6]]]
    used = sum(pc.shape[1] for pc in pieces)
    pieces.append(jnp.zeros((D_MODEL, PROJ_COLS - used), w_in.dtype))
    return jnp.concatenate(pieces, axis=1), w_in[:, o[6]:]


def _modulated_norm(x, g, sc, sh):
    y = x * lax.rsqrt(jnp.mean(x * x, axis=-1, keepdims=True) + EPS)
    return (y * g) * (1.0 + sc) + sh


def _resident(shape):
    return pl.BlockSpec(shape, lambda *_: (0,) * len(shape), pipeline_mode=pl.Buffered(1))


def _mod_spec(mod, rows_per_group, tm):
    if mod.shape[1] == 1:
        return pl.BlockSpec((1, 1, D_MODEL), lambda i, *_: (i // (rows_per_group // tm), 0, 0))
    return pl.BlockSpec((1, tm, D_MODEL), lambda i, *_: (0, i, 0))


def _proj_kernel(x_ref, g_ref, sc_ref, sh_ref, w_ref, o_ref, wb_ref):
    @pl.when(pl.program_id(0) == 0)
    def _():
        wb_ref[...] = w_ref[...].astype(jnp.bfloat16)

    h = _modulated_norm(x_ref[...], g_ref[...], sc_ref[0], sh_ref[0]).astype(jnp.bfloat16)
    o_ref[...] = jnp.dot(h, wb_ref[...], preferred_element_type=jnp.float32)


def _proj(x2, g, sc, sh, w, rows_per_group, tm):
    M = x2.shape[0]
    N = w.shape[1]
    return pl.pallas_call(
        _proj_kernel,
        grid=(M // tm,),
        in_specs=[pl.BlockSpec((tm, D_MODEL), lambda i: (i, 0)), _resident((1, D_MODEL)),
                  _mod_spec(sc, rows_per_group, tm), _mod_spec(sh, rows_per_group, tm), _resident((D_MODEL, N))],
        out_specs=pl.BlockSpec((tm, N), lambda i: (i, 0)),
        out_shape=jax.ShapeDtypeStruct((M, N), jnp.float32),
        scratch_shapes=[pltpu.VMEM((D_MODEL, N), jnp.bfloat16)],
        compiler_params=pltpu.CompilerParams(dimension_semantics=("arbitrary",), vmem_limit_bytes=VMEM_LIMIT),
        name="in_proj",
    )(x2, g, sc, sh, w)


def _merge_kernel(x_ref, g0_ref, sc_ref, sh_ref, gt_ref, g1_ref, b0_ref, b1_ref, b2_ref, b3_ref,
                  wmg_ref, wbr_ref, wout_ref, o_ref, wmgb_ref):
    f32, bf16 = jnp.float32, jnp.bfloat16

    @pl.when(pl.program_id(0) == 0)
    def _():
        wmgb_ref[...] = wmg_ref[...].astype(bf16)

    x = x_ref[...]
    h = _modulated_norm(x, g0_ref[...], sc_ref[0], sh_ref[0]).astype(bf16)
    acc = jnp.zeros(x.shape, f32)
    for n, b_ref in enumerate((b0_ref, b1_ref, b2_ref, b3_ref)):
        gate = jax.nn.sigmoid(jnp.dot(h, wmgb_ref[:, n * D_MODEL:(n + 1) * D_MODEL], preferred_element_type=f32))
        acc = acc + gate * jnp.dot(b_ref[...].astype(bf16), wbr_ref[n], preferred_element_type=f32)
    mixed = jnp.dot(acc.astype(bf16), wout_ref[...], preferred_element_type=f32)
    y = mixed * lax.rsqrt(jnp.mean(mixed * mixed, axis=-1, keepdims=True) + EPS) * g1_ref[...]
    o_ref[...] = x + gt_ref[0] * y


def _merge(x2, g0, sc, sh, gt, g1, branches, wmg, wbr, wout, rows_per_group, tm):
    M = x2.shape[0]
    row = lambda w: pl.BlockSpec((tm, w), lambda i: (i, 0))
    mspec = _mod_spec(sc, rows_per_group, tm)
    return pl.pallas_call(
        _merge_kernel,
        grid=(M // tm,),
        in_specs=[row(D_MODEL), _resident((1, D_MODEL)), mspec, mspec, mspec, _resident((1, D_MODEL))]
                 + [row(BRANCH_W)] * N_BRANCH
                 + [_resident(wmg.shape), _resident(wbr.shape), _resident(wout.shape)],
        out_specs=row(D_MODEL),
        out_shape=jax.ShapeDtypeStruct((M, D_MODEL), jnp.float32),
        scratch_shapes=[pltpu.VMEM(wmg.shape, jnp.bfloat16)],
        compiler_params=pltpu.CompilerParams(dimension_semantics=("arbitrary",), vmem_limit_bytes=VMEM_LIMIT),
        name="merge",
    )(x2, g0, sc, sh, gt, g1, *branches, wmg, wbr, wout)


def _ffn_kernel(x_ref, g2_ref, sc_ref, sh_ref, gt_ref, g3_ref, buf_ref, wup_ref, dw_ref, wdn_ref,
                o_ref, st_ref, gp_ref):
    f32, bf16 = jnp.float32, jnp.bfloat16
    tm = x_ref.shape[1]
    HALO = SUBLANES

    @pl.when(pl.program_id(1) == 0)
    def _():
        gp_ref[0:HALO, :] = buf_ref[0]

    x = x_ref[0]
    h = _modulated_norm(x, g2_ref[...], sc_ref[0], sh_ref[0]).astype(bf16)
    gp_ref[HALO:HALO + tm, :] = jnp.dot(h, wup_ref[:, 0:D_FF], preferred_element_type=f32)
    val = jnp.dot(h, wup_ref[:, D_FF:2 * D_FF], preferred_element_type=f32)
    gconv = (dw_ref[2:3, :] * gp_ref[HALO:HALO + tm, :] + dw_ref[1:2, :] * gp_ref[HALO - 1:HALO - 1 + tm, :]
             + dw_ref[0:1, :] * gp_ref[HALO - 2:HALO - 2 + tm, :])
    a = (jax.nn.gelu(gconv) * val).astype(bf16)
    f = jnp.dot(a, wdn_ref[...], preferred_element_type=f32)
    y = f * lax.rsqrt(jnp.mean(f * f, axis=-1, keepdims=True) + EPS) * g3_ref[...]
    o_ref[0] = x + gt_ref[0] * y
    tail = gp_ref[tm:tm + HALO, :]
    gp_ref[0:HALO, :] = tail
    st_ref[0] = tail


def _ffn(x3, g2, sc, sh, gt, g3, buf, wup, dw, wdn, tm):
    B, T, _ = x3.shape
    buf8 = jnp.pad(buf, ((0, 0), (SUBLANES - (FFN_CONV - 1), 0), (0, 0)))
    mspec = pl.BlockSpec((1, 1, D_MODEL), lambda b, i: (b, 0, 0))
    y, st = pl.pallas_call(
        _ffn_kernel,
        grid=(B, T // tm),
        in_specs=[pl.BlockSpec((1, tm, D_MODEL), lambda b, i: (b, i, 0)), _resident((1, D_MODEL)), mspec, mspec, mspec,
                  _resident((1, D_MODEL)), pl.BlockSpec((1, SUBLANES, D_FF), lambda b, i: (b, 0, 0)),
                  _resident(wup.shape), _resident(dw.shape), _resident(wdn.shape)],
        out_specs=[pl.BlockSpec((1, tm, D_MODEL), lambda b, i: (b, i, 0)),
                   pl.BlockSpec((1, SUBLANES, D_FF), lambda b, i: (b, 0, 0))],
        out_shape=[jax.ShapeDtypeStruct((B, T, D_MODEL), jnp.float32),
                   jax.ShapeDtypeStruct((B, SUBLANES, D_FF), jnp.float32)],
        scratch_shapes=[pltpu.VMEM((tm + SUBLANES, D_FF), jnp.float32)],
        compiler_params=pltpu.CompilerParams(dimension_semantics=("arbitrary", "arbitrary"),
                                             vmem_limit_bytes=VMEM_LIMIT),
        name="conv_ffn",
    )(x3, g2, sc, sh, gt, g3, buf8, wup, dw, wdn)
    return y, st[:, SUBLANES - (FFN_CONV - 1):]


def _ffn_rows_kernel(x_ref, g2_ref, sc_ref, sh_ref, gt_ref, g3_ref, p1_ref, p2_ref, wup_ref, dw_ref, wdn_ref,
                     o_ref, gpre_ref, gp_ref, *, seg):
    f32, bf16 = jnp.float32, jnp.bfloat16
    M = x_ref.shape[0]
    HALO = SUBLANES
    x = x_ref[...]
    h = _modulated_norm(x, g2_ref[...], sc_ref[0], sh_ref[0]).astype(bf16)
    gpre = jnp.dot(h, wup_ref[:, 0:D_FF], preferred_element_type=f32)
    val = jnp.dot(h, wup_ref[:, D_FF:2 * D_FF], preferred_element_type=f32)
    gp_ref[0:HALO, :] = jnp.zeros((HALO, D_FF), f32)
    gp_ref[HALO:HALO + M, :] = gpre
    t = lax.broadcasted_iota(jnp.int32, (M, 1), 0) % seg
    prev1 = jnp.where(t >= 1, gp_ref[HALO - 1:HALO - 1 + M, :], p1_ref[...])
    prev2 = jnp.where(t >= 2, gp_ref[HALO - 2:HALO - 2 + M, :], p2_ref[...])
    gconv = dw_ref[2:3, :] * gpre + dw_ref[1:2, :] * prev1 + dw_ref[0:1, :] * prev2
    a = (jax.nn.gelu(gconv) * val).astype(bf16)
    f = jnp.dot(a, wdn_ref[...], preferred_element_type=f32)
    y = f * lax.rsqrt(jnp.mean(f * f, axis=-1, keepdims=True) + EPS) * g3_ref[...]
    o_ref[...] = x + gt_ref[0] * y
    gpre_ref[...] = gpre


def _ffn_rows(x3, g2, sc, sh, gt, g3, buf, wup, dw, wdn):
    B, T, _ = x3.shape
    M = B * T
    assert FFN_CONV == 3 and T >= FFN_CONV - 1
    zeros = jnp.zeros((B, T, D_FF), jnp.float32)
    p1 = zeros.at[:, 0].set(buf[:, 1]).reshape(M, D_FF)
    p2 = zeros.at[:, 0].set(buf[:, 0]).at[:, 1].set(buf[:, 1]).reshape(M, D_FF)
    full = lambda shape: pl.BlockSpec(shape, lambda i: (0,) * len(shape))
    y, gpre = pl.pallas_call(
        functools.partial(_ffn_rows_kernel, seg=T),
        grid=(1,),
        in_specs=[full((M, D_MODEL)), full((1, D_MODEL)), full((1, M, D_MODEL)), full((1, M, D_MODEL)),
                  full((1, M, D_MODEL)), full((1, D_MODEL)), full((M, D_FF)), full((M, D_FF)),
                  _resident(wup.shape), _resident(dw.shape), _resident(wdn.shape)],
        out_specs=[full((M, D_MODEL)), full((M, D_FF))],
        out_shape=[jax.ShapeDtypeStruct((M, D_MODEL), jnp.float32), jax.ShapeDtypeStruct((M, D_FF), jnp.float32)],
        scratch_shapes=[pltpu.VMEM((M + SUBLANES, D_FF), jnp.float32)],
        compiler_params=pltpu.CompilerParams(dimension_semantics=("arbitrary",), vmem_limit_bytes=VMEM_LIMIT),
        name="conv_ffn_rows",
    )(x3.reshape(M, D_MODEL), g2, sc, sh, gt, g3, p1, p2, wup, dw, wdn)
    return y.reshape(B, T, D_MODEL), gpre.reshape(B, T, D_FF)[:, T - (FFN_CONV - 1):]


CONF_HALO = 32
CONF_TM = 1024


def _conformer_kernel(u_ref, buf_ref, dw_ref, dwb_ref, lng_ref, lnb_ref, o_ref, st_ref, xp_ref):
    tm = u_ref.shape[1]
    first = CONF_HALO - (CONV_WIDTH - 1)

    @pl.when(pl.program_id(1) == 0)
    def _():
        xp_ref[0:CONF_HALO, :] = buf_ref[0]

    u = u_ref[0]
    xp_ref[CONF_HALO:CONF_HALO + tm, :] = u[:, :CONV_CH] * jax.nn.sigmoid(u[:, CONV_CH:])
    acc = jnp.zeros((tm, CONV_CH), jnp.float32)
    for k in range(CONV_WIDTH):
        acc = acc + dw_ref[k:k + 1, :] * xp_ref[first + k:first + k + tm, :]
    y = acc + dwb_ref[...]
    mu = jnp.mean(y, axis=-1, keepdims=True)
    var = jnp.mean(jnp.square(y - mu), axis=-1, keepdims=True)
    yn = (y - mu) * lax.rsqrt(var + EPS) * lng_ref[...] + lnb_ref[...]
    o_ref[0] = yn * jax.nn.sigmoid(yn)
    tail = xp_ref[tm:tm + CONF_HALO, :]
    xp_ref[0:CONF_HALO, :] = tail
    st_ref[0] = tail


def _conformer(P, col_blk, buf, dw, dw_b, ln_g, ln_b, tm):
    B, T, _ = P.shape
    bufp = jnp.pad(buf, ((0, 0), (CONF_HALO - (CONV_WIDTH - 1), 0), (0, 0)))
    vec = lambda: _resident((1, CONV_CH))
    o, st = pl.pallas_call(
        _conformer_kernel,
        grid=(B, T // tm),
        in_specs=[pl.BlockSpec((1, tm, 2 * CONV_CH), lambda b, i: (b, i, col_blk)),
                  pl.BlockSpec((1, CONF_HALO, CONV_CH), lambda b, i: (b, 0, 0)),
                  _resident((CONV_WIDTH, CONV_CH)), vec(), vec(), vec()],
        out_specs=[pl.BlockSpec((1, tm, CONV_CH), lambda b, i: (b, i, 0)),
                   pl.BlockSpec((1, CONF_HALO, CONV_CH), lambda b, i: (b, 0, 0))],
        out_shape=[jax.ShapeDtypeStruct((B, T, CONV_CH), jnp.float32),
                   jax.ShapeDtypeStruct((B, CONF_HALO, CONV_CH), jnp.float32)],
        scratch_shapes=[pltpu.VMEM((tm + CONF_HALO, CONV_CH), jnp.float32)],
        compiler_params=pltpu.CompilerParams(dimension_semantics=("arbitrary", "arbitrary"),
                                             vmem_limit_bytes=VMEM_LIMIT),
        name="conformer",
    )(P, bufp, dw, dw_b[None], ln_g[None], ln_b[None])
    return o, st[:, CONF_HALO - (CONV_WIDTH - 1):]


RET_LOG_GAMMA = tuple(math.log1p(-2.0 ** (-5 - h)) for h in range(RET_HEADS))
RET_TM = 256


def _retention_kernel(q_ref, k_ref, v_ref, g_ref, cos_ref, sin_ref, s0_ref, gn_ref, o_ref, sfin_ref,
                      s_ref, dec_ref, *, n_valid):
    f32, bf16 = jnp.float32, jnp.bfloat16
    C = q_ref.shape[1]
    W = RET_W
    half = HEAD_DIM // 2

    @pl.when(pl.program_id(1) == 0)
    def _():
        s_ref[...] = s0_ref[0]

    @pl.when((pl.program_id(0) == 0) & (pl.program_id(1) == 0))
    def _():
        ii = lax.broadcasted_iota(jnp.int32, (C, C), 0)
        jj = lax.broadcasted_iota(jnp.int32, (C, C), 1)
        d = (ii - jj).astype(f32)
        for h in range(RET_HEADS):
            dec_ref[h] = jnp.where((ii >= jj) & (jj < n_valid), jnp.exp(jnp.maximum(d, 0.0) * RET_LOG_GAMMA[h]), 0.0)

    lane = lax.broadcasted_iota(jnp.int32, (C, W), 1)
    low = lane % HEAD_DIM < half
    cos = cos_ref[...]
    sin = sin_ref[...]

    def rot(x):
        other = jnp.where(low, pltpu.roll(x, W - half, axis=1), pltpu.roll(x, half, axis=1))
        return x * cos + other * sin

    q = rot(q_ref[0])
    k = rot(k_ref[0]) * HEAD_DIM ** -0.5
    v = v_ref[0]
    row = lax.broadcasted_iota(jnp.int32, (C, HEAD_DIM), 0)
    rowf = row.astype(f32)
    valid = row < n_valid
    outs = []
    for h in range(RET_HEADS):
        cs = slice(h * HEAD_DIM, (h + 1) * HEAD_DIM)
        lg = RET_LOG_GAMMA[h]
        qh, kh, vh = q[:, cs], k[:, cs], v[:, cs].astype(bf16)
        s_old = s_ref[h]
        att = _dot_nt(qh.astype(bf16), kh.astype(bf16)) * dec_ref[h]
        o = (jnp.dot(att.astype(bf16), vh, preferred_element_type=f32)
             + jnp.dot((qh * jnp.exp((rowf + 1.0) * lg)).astype(bf16), s_old.astype(bf16), preferred_element_type=f32))
        kz = jnp.where(valid, kh * jnp.exp((n_valid - 1.0 - rowf) * lg), 0.0)
        s_ref[h] = s_old * math.exp(n_valid * lg) + jnp.dot(kz.T.astype(bf16), vh, preferred_element_type=f32)
        mu = jnp.mean(o, axis=-1, keepdims=True)
        var = jnp.mean(jnp.square(o - mu), axis=-1, keepdims=True)
        outs.append((o - mu) * lax.rsqrt(var + EPS))
    g = g_ref[0]
    o_ref[0] = jnp.concatenate(outs, axis=1) * gn_ref[...] * (g * jax.nn.sigmoid(g))

    @pl.when(pl.program_id(1) == pl.num_programs(1) - 1)
    def _():
        sfin_ref[0] = s_ref[...]


def _retention(P, col_blk0, S0, pos, gn, chunk, n_valid):
    B, T, _ = P.shape
    half = HEAD_DIM // 2
    inv = ROPE_BASE ** (-jnp.arange(half, dtype=jnp.float32) / half)
    ang = pos.astype(jnp.float32)[:, None] * inv[None, :]
    cos, sin = jnp.cos(ang), jnp.sin(ang)
    cosf = jnp.tile(jnp.concatenate([cos, cos], axis=1), (1, RET_HEADS))
    sinf = jnp.tile(jnp.concatenate([-sin, sin], axis=1), (1, RET_HEADS))
    col = lambda j: pl.BlockSpec((1, chunk, RET_W), lambda b, i: (b, i, col_blk0 + j))
    tab = pl.BlockSpec((chunk, RET_W), lambda b, i: (i, 0))
    state = pl.BlockSpec((1, RET_HEADS, HEAD_DIM, HEAD_DIM), lambda b, i: (b, 0, 0, 0))
    return pl.pallas_call(
        functools.partial(_retention_kernel, n_valid=n_valid),
        grid=(B, T // chunk),
        in_specs=[col(0), col(1), col(2), col(3), tab, tab, state, _resident((1, RET_W))],
        out_specs=[pl.BlockSpec((1, chunk, RET_W), lambda b, i: (b, i, 0)), state],
        out_shape=[jax.ShapeDtypeStruct((B, T, RET_W), jnp.float32),
                   jax.ShapeDtypeStruct((B, RET_HEADS, HEAD_DIM, HEAD_DIM), jnp.float32)],
        scratch_shapes=[pltpu.VMEM((RET_HEADS, HEAD_DIM, HEAD_DIM), jnp.float32),
                        pltpu.VMEM((RET_HEADS, chunk, chunk), jnp.float32)],
        compiler_params=pltpu.CompilerParams(dimension_semantics=("arbitrary", "arbitrary"),
                                             vmem_limit_bytes=VMEM_LIMIT),
        name="retention",
    )(P, P, P, P, cosf, sinf, S0, gn[None])


GDN_RT = 4 * GDN_CHUNK
GDN_HALO = SUBLANES
GDN_A_LANE = 3 * NSA_HEADS
GDN_B_LANE = GDN_A_LANE + GDN_HEADS


def _mm1(a, b):
    return jnp.dot(a.astype(jnp.bfloat16), b.astype(jnp.bfloat16), preferred_element_type=jnp.float32)


def _mm3(a, b):
    f32, bf16 = jnp.float32, jnp.bfloat16
    ah, bh = a.astype(bf16), b.astype(bf16)
    al, bl = (a - ah.astype(f32)).astype(bf16), (b - bh.astype(f32)).astype(bf16)
    d = lambda x, y: jnp.dot(x, y, preferred_element_type=f32)
    return d(ah, bh) + (d(ah, bl) + d(al, bh))


def _segment_cumsum(x, axis, seg):
    idx = lax.broadcasted_iota(jnp.int32, x.shape, axis) % seg
    s = 1
    while s < seg:
        x = x + jnp.where(idx >= s, pltpu.roll(x, s, axis=axis), 0.0)
        s *= 2
    return x


def _gdn_kernel(q_ref, k_ref, v_ref, z_ref, ab_ref, abt_ref, buf_ref, s0_ref, cw_ref, ng_ref, alane_ref, dlane_ref,
                acol_ref, dcol_ref, o_ref, sfin_ref, s_ref, xp_ref, *, n_valid):
    f32 = jnp.float32
    RT = q_ref.shape[1]
    C = GDN_CHUNK
    W = GDN_W
    first = GDN_HALO - (GDN_CONV - 1)

    @pl.when(pl.program_id(1) == 0)
    def _():
        s_ref[...] = s0_ref[0]
        xp_ref[0:GDN_HALO, :] = buf_ref[0]

    xp_ref[GDN_HALO:GDN_HALO + RT, 0:W] = q_ref[0]
    xp_ref[GDN_HALO:GDN_HALO + RT, W:2 * W] = k_ref[0]
    xp_ref[GDN_HALO:GDN_HALO + RT, 2 * W:3 * W] = v_ref[0]
    y = jnp.zeros((RT, 3 * W), f32)
    for t in range(GDN_CONV):
        y = y + cw_ref[t:t + 1, :] * xp_ref[first + t:first + t + RT, :]
    y = y * jax.nn.sigmoid(y)
    xp_ref[0:GDN_HALO, :] = xp_ref[RT:RT + GDN_HALO, :]

    ab = ab_ref[0]
    g_lanes = -jnp.exp(alane_ref[...]) * jax.nn.softplus(ab + dlane_ref[...])
    beta_lanes = jax.nn.sigmoid(ab)
    g_rows = -jnp.exp(acol_ref[...]) * jax.nn.softplus(abt_ref[0] + dcol_ref[...])
    if n_valid < C:
        g_lanes = jnp.where(lax.broadcasted_iota(jnp.int32, g_lanes.shape, 0) % C < n_valid, g_lanes, 0.0)
        beta_lanes = jnp.where(lax.broadcasted_iota(jnp.int32, g_lanes.shape, 0) % C < n_valid, beta_lanes, 0.0)
        g_rows = jnp.where(lax.broadcasted_iota(jnp.int32, g_rows.shape, 1) % C < n_valid, g_rows, 0.0)
    gc_lanes = _segment_cumsum(g_lanes, 0, C)
    gc_rows = _segment_cumsum(g_rows, 1, C)

    ii = lax.broadcasted_iota(jnp.int32, (C, C), 0)
    jj = lax.broadcasted_iota(jnp.int32, (C, C), 1)
    tri = ii >= jj
    strict = ii > jj
    eye = (ii == jj).astype(f32)
    z = z_ref[0]
    bf16 = jnp.bfloat16
    nch = RT // C
    pairs = [(c, h) for c in range(nch) for h in range(GDN_HEADS)]
    qs, ks, ms, atts, rhss, gcs = {}, {}, {}, {}, {}, {}
    for c, h in pairs:
        rs = slice(c * C, (c + 1) * C)
        qh = y[rs, h * HEAD_DIM:(h + 1) * HEAD_DIM]
        kh = y[rs, W + h * HEAD_DIM:W + (h + 1) * HEAD_DIM]
        vh = y[rs, 2 * W + h * HEAD_DIM:2 * W + (h + 1) * HEAD_DIM]
        qh = qh * lax.rsqrt(jnp.sum(qh * qh, axis=-1, keepdims=True) + EPS) * HEAD_DIM ** -0.5
        kh = kh * lax.rsqrt(jnp.sum(kh * kh, axis=-1, keepdims=True) + EPS)
        gc_col = gc_lanes[rs, GDN_A_LANE + h:GDN_A_LANE + h + 1]
        beta = beta_lanes[rs, GDN_B_LANE + h:GDN_B_LANE + h + 1]
        gc_row = gc_rows[h:h + 1, c * C:(c + 1) * C]
        e_col = jnp.exp(gc_col)
        lm = jnp.exp(jnp.where(tri, gc_col - gc_row, NEG_INF))
        kb = kh * beta
        ms[c, h] = jnp.where(strict, _dot_nt(kb.astype(bf16), kh.astype(bf16)) * lm, 0.0)
        atts[c, h] = _dot_nt(qh.astype(bf16), kh.astype(bf16)) * lm
        rhss[c, h] = jnp.concatenate([vh * beta, kb * e_col], axis=1)
        qs[c, h], ks[c, h], gcs[c, h] = qh * e_col, kh, gc_col
    pw = {p: -ms[p] for p in pairs}
    xs = {p: eye + pw[p] for p in pairs}
    for _ in range(int(math.log2(C)) - 1):
        pw = {p: _mm1(pw[p], pw[p]) for p in pairs}
        xs = {p: xs[p] + _mm1(xs[p], pw[p]) for p in pairs}
    res = {p: eye - (xs[p] + _mm3(ms[p], xs[p])) for p in pairs}
    xs = {p: xs[p] + _mm1(xs[p], res[p]) for p in pairs}
    sols = {p: _mm3(xs[p], rhss[p]) for p in pairs}
    state = [s_ref[h] for h in range(GDN_HEADS)]
    outs = {}
    for c in range(nch):
        heads = range(GDN_HEADS)
        g_last = [gcs[c, h][C - 1:C, :] for h in heads]
        vn = [sols[c, h][:, :HEAD_DIM] - _mm1(sols[c, h][:, HEAD_DIM:], state[h]) for h in heads]
        o_in = [_mm1(qs[c, h], state[h]) for h in heads]
        kd_t = [(ks[c, h] * jnp.exp(g_last[h] - gcs[c, h])).T for h in heads]
        for h in heads:
            outs[c, h] = o_in[h] + _mm1(atts[c, h], vn[h])
        state = [state[h] * jnp.exp(g_last[h]) + _mm1(kd_t[h], vn[h]) for h in heads]
    for h in range(GDN_HEADS):
        s_ref[h] = state[h]
    row_outs = []
    for c in range(nch):
        head_outs = []
        for h in range(GDN_HEADS):
            o = outs[c, h]
            o = o * lax.rsqrt(jnp.mean(o * o, axis=-1, keepdims=True) + EPS) * ng_ref[...]
            zh = z[c * C:(c + 1) * C, h * HEAD_DIM:(h + 1) * HEAD_DIM]
            head_outs.append(o * (zh * jax.nn.sigmoid(zh)))
        row_outs.append(jnp.concatenate(head_outs, axis=1))
    o_ref[0] = jnp.concatenate(row_outs, axis=0)

    @pl.when(pl.program_id(1) == pl.num_programs(1) - 1)
    def _():
        sfin_ref[0] = s_ref[...]


def _gdn(P, blk_q, blk_misc, ga, conv_buf, S0, conv_w, A_log, dt_bias, norm_g, rt, n_valid):
    B, T, _ = P.shape
    f32 = jnp.float32
    abt = jnp.pad(jnp.transpose(ga, (0, 2, 1)), ((0, 0), (0, SUBLANES - GDN_HEADS), (0, 0)))
    buf8 = jnp.pad(conv_buf, ((0, 0), (GDN_HALO - (GDN_CONV - 1), 0), (0, 0)))
    lane_vec = lambda v: jnp.zeros((1, LANES), f32).at[0, GDN_A_LANE:GDN_A_LANE + GDN_HEADS].set(v)
    col_vec = lambda v: jnp.zeros((SUBLANES, 1), f32).at[0:GDN_HEADS, 0].set(v)
    col = lambda j: pl.BlockSpec((1, rt, GDN_W), lambda b, i: (b, i, blk_q + j))
    state = pl.BlockSpec((1, GDN_HEADS, HEAD_DIM, HEAD_DIM), lambda b, i: (b, 0, 0, 0))
    return pl.pallas_call(
        functools.partial(_gdn_kernel, n_valid=n_valid),
        grid=(B, T // rt),
        in_specs=[col(0), col(1), col(2), col(3),
                  pl.BlockSpec((1, rt, LANES), lambda b, i: (b, i, blk_misc)),
                  pl.BlockSpec((1, SUBLANES, rt), lambda b, i: (b, 0, i)),
                  pl.BlockSpec((1, GDN_HALO, 3 * GDN_W), lambda b, i: (b, 0, 0)), state,
                  _resident((GDN_CONV, 3 * GDN_W)), _resident((1, HEAD_DIM)),
                  _resident((1, LANES)), _resident((1, LANES)), _resident((SUBLANES, 1)), _resident((SUBLANES, 1))],
        out_specs=[pl.BlockSpec((1, rt, GDN_W), lambda b, i: (b, i, 0)), state],
        out_shape=[jax.ShapeDtypeStruct((B, T, GDN_W), f32),
                   jax.ShapeDtypeStruct((B, GDN_HEADS, HEAD_DIM, HEAD_DIM), f32)],
        scratch_shapes=[pltpu.VMEM((GDN_HEADS, HEAD_DIM, HEAD_DIM), f32),
                        pltpu.VMEM((rt + GDN_HALO, 3 * GDN_W), f32)],
        compiler_params=pltpu.CompilerParams(dimension_semantics=("arbitrary", "arbitrary"),
                                             vmem_limit_bytes=VMEM_LIMIT),
        name="gated_deltanet",
    )(P, P, P, P, P, abt, buf8, S0, conv_w, norm_g[None], lane_vec(A_log), lane_vec(dt_bias),
      col_vec(A_log), col_vec(dt_bias))


def _prep_layer(p):
    w_main, w_mg = _prep_w_in(p["w_in"])
    q = dict(p)
    q.update(w_main=w_main, w_mg=w_mg, w_branch_b=p["w_branch"].astype(jnp.bfloat16),
             w_out_b=p["w_out"].astype(jnp.bfloat16), ffn_up_b=p["ffn_up"].astype(jnp.bfloat16),
             ffn_down_b=p["ffn_down"].astype(jnp.bfloat16))
    return q


def trunk_layer(x, mod, pos0, nsa_past, win_buf, conv_buf, ret_s, gdn_buf, gdn_s, ffn_buf, p, rel_bias):
    B, T, _ = x.shape
    M = B * T
    per_row = T < MERGE_TM
    if per_row:
        sh1, sc1, gt1, sh2, sc2, gt2 = [jnp.repeat(m, T, axis=0)[None] for m in jnp.split(mod, 6, axis=-1)]
    else:
        sh1, sc1, gt1, sh2, sc2, gt2 = [m[:, None, :] for m in jnp.split(mod, 6, axis=-1)]
    norms = p["norms"][:, None, :]
    x2 = x.reshape(M, D_MODEL)
    P = _proj(x2, norms[0], sc1, sh1, p["w_main"], T, M if per_row else PROJ_TM).reshape(B, T, PROJ_COLS)
    nkv = P[:, :, PROJ_KV:PROJ_KV + 6 * KVW]
    ngt = P[:, :, PROJ_MISC:PROJ_MISC + 3 * NSA_HEADS]
    ga = P[:, :, PROJ_MISC + 3 * NSA_HEADS:PROJ_MISC + 3 * NSA_HEADS + GDN_HEADS]
    gb = P[:, :, PROJ_MISC + 3 * NSA_HEADS + GDN_HEADS:PROJ_MISC + 3 * NSA_HEADS + 2 * GDN_HEADS]
    keep = min(NSA_WINDOW, T)
    kw = P[:, T - keep:, PROJ_KV + 4 * KVW:PROJ_KV + 6 * KVW].reshape(B, keep, 2, NSA_KV_HEADS, HEAD_DIM)
    if nsa_past is None:
        o_nsa = _nsa_prompt(P, PROJ_Q // (NSA_HEADS * HEAD_DIM), P, PROJ_KV // (6 * KVW), ngt,
                            p["cmp_pool"], p["cmp_pe"], rel_bias)
        new_win = kw
    else:
        cache, layer, page_table = nsa_past
        o_nsa = _nsa_sample(P[:, :, PROJ_Q:PROJ_Q + NSA_HEADS * HEAD_DIM], nkv, ngt, cache, layer, page_table,
                            win_buf, p["cmp_pool"], p["cmp_pe"], rel_bias)
        real = jnp.concatenate([win_buf, kw], axis=1)
        new_win = real[:, real.shape[1] - min(NSA_WINDOW, real.shape[1]):]
    o_conv, new_conv = _conformer(P, PROJ_UCV // (2 * CONV_CH), conv_buf, p["conv_dw"], p["conv_dw_b"],
                                  p["conv_ln_g"], p["conv_ln_b"], T if per_row else CONF_TM)
    if per_row:
        Pr = jnp.pad(P[:, :, PROJ_RET:PROJ_RET + 4 * RET_W], ((0, 0), (0, RET_CHUNK - T), (0, 0)))
        o_ret, new_ret = _retention(Pr, 0, ret_s, pos0 + jnp.arange(RET_CHUNK, dtype=jnp.int32), p["ret_gn"],
                                    RET_CHUNK, T)
        o_ret = o_ret[:, :T]
    else:
        o_ret, new_ret = _retention(P, PROJ_RET // RET_W, ret_s, pos0 + jnp.arange(T, dtype=jnp.int32), p["ret_gn"],
                                    RET_TM, RET_TM)
    gdn_w = (p["gdn_conv_w"], p["gdn_A_log"], p["gdn_dt_bias"], p["gdn_norm"])
    if per_row:
        pad_rows = ((0, 0), (0, GDN_CHUNK - T), (0, 0))
        Pg = jnp.pad(P[:, :, PROJ_GQKV:PROJ_COLS], pad_rows)
        o_gdn, new_gdn = _gdn(Pg, 0, (PROJ_MISC - PROJ_GQKV) // LANES, jnp.pad(ga, pad_rows), gdn_buf, gdn_s, *gdn_w,
                              GDN_CHUNK, T)
        o_gdn = o_gdn[:, :T]
    else:
        o_gdn, new_gdn = _gdn(P, PROJ_GQKV // GDN_W, PROJ_MISC // LANES, ga, gdn_buf, gdn_s, *gdn_w,
                              GDN_RT, GDN_CHUNK)
    new_gdn_buf = jnp.concatenate([gdn_buf, P[:, :, PROJ_GQKV:PROJ_GQKV + 3 * GDN_W]], axis=1)[:, T:]
    branches = [o.reshape(M, BRANCH_W) for o in (o_nsa, o_conv, o_ret, o_gdn)]
    x1 = _merge(x2, norms[0], sc1, sh1, gt1, norms[1], branches, p["w_mg"], p["w_branch_b"], p["w_out_b"],
                T, M if per_row else MERGE_TM).reshape(B, T, D_MODEL)
    if per_row:
        x_out, new_ffn = _ffn_rows(x1, norms[2], sc2, sh2, gt2, norms[3], ffn_buf, p["ffn_up_b"], p["ffn_dw"],
                                   p["ffn_down_b"])
    else:
        x_out, new_ffn = _ffn(x1, norms[2], sc2, sh2, gt2, norms[3], ffn_buf, p["ffn_up_b"], p["ffn_dw"],
                              p["ffn_down_b"], FFN_TM)
    kv_rows = nkv[:, :, :4 * KVW].reshape(B, T, 4, NSA_KV_HEADS, HEAD_DIM)
    return x_out, (kv_rows, new_win, new_conv, new_ret, new_gdn_buf, new_gdn, new_ffn)


def kernel(x_prompt, x_sample, cache_nsa_kv, cache_nsa_win, state_conv, state_ret, state_gdn_conv, state_gdn,
           state_ffn_conv, page_table, c_prompt, c_sample, w_ada, b_ada, norms, w_in, cmp_pool, cmp_pe, rel_bias,
           conv_dw, conv_dw_b, conv_ln_g, conv_ln_b, ret_gn, gdn_conv_w, gdn_A_log, gdn_dt_bias, gdn_norm,
           w_branch, w_out, ffn_up, ffn_dw, ffn_down):
    B = x_prompt.shape[0]
    Bd = x_sample.shape[0]
    past = page_table.shape[1] * PAGE_SIZE
    mods = _adaln(jnp.concatenate([c_prompt, c_sample], axis=0), w_ada, b_ada)
    layer_w = {"norms": norms, "w_in": w_in, "cmp_pool": cmp_pool,
               "cmp_pe": cmp_pe, "conv_dw": conv_dw, "conv_dw_b": conv_dw_b, "conv_ln_g": conv_ln_g,
               "conv_ln_b": conv_ln_b, "ret_gn": ret_gn, "gdn_conv_w": gdn_conv_w, "gdn_A_log": gdn_A_log,
               "gdn_dt_bias": gdn_dt_bias, "gdn_norm": gdn_norm, "w_branch": w_branch, "w_out": w_out,
               "ffn_up": ffn_up, "ffn_dw": ffn_dw, "ffn_down": ffn_down}
    yp, ys = x_prompt, x_sample
    st_p, st_s = [], []
    for l in range(DEPTH):
        p = _prep_layer({name: w[l] for name, w in layer_w.items()})
        yp, sp = trunk_layer(
            yp, mods[l, :B], 0, None, None,
            jnp.zeros((B, CONV_WIDTH - 1, CONV_CH), x_prompt.dtype),
            jnp.zeros((B, RET_HEADS, HEAD_DIM, HEAD_DIM), jnp.float32),
            jnp.zeros((B, GDN_CONV - 1, 3 * GDN_W), x_prompt.dtype),
            jnp.zeros((B, GDN_HEADS, HEAD_DIM, HEAD_DIM), jnp.float32),
            jnp.zeros((B, FFN_CONV - 1, D_FF), x_prompt.dtype),
            p, rel_bias)
        ys, ss = trunk_layer(
            ys, mods[l, B:], past, (cache_nsa_kv, l, page_table), cache_nsa_win[l], state_conv[l], state_ret[l],
            state_gdn_conv[l], state_gdn[l], state_ffn_conv[l], p, rel_bias)
        st_p.append(sp)
        st_s.append(ss)

    def stack(outs, i, axis):
        return jnp.stack([o[i] for o in outs], axis=axis)

    kv_p, kv_s = stack(st_p, 0, 1), stack(st_s, 0, 1)
    win_p, win_s = stack(st_p, 1, 0), stack(st_s, 1, 0)
    conv_p, conv_s = stack(st_p, 2, 0), stack(st_s, 2, 0)
    ret_p, ret_s = stack(st_p, 3, 0), stack(st_s, 3, 0)
    gdnc_p, gdnc_s = stack(st_p, 4, 0), stack(st_s, 4, 0)
    gdn_p, gdn_s = stack(st_p, 5, 0), stack(st_s, 5, 0)
    ffn_p, ffn_s = stack(st_p, 6, 0), stack(st_s, 6, 0)
    return (yp, ys, kv_p, kv_s, win_p, win_s, conv_p, conv_s, ret_p, ret_s, gdnc_p, gdnc_s, gdn_p, gdn_s, ffn_p, ffn_s)
```

```python
import functools
import math

import jax
import jax.numpy as jnp
import numpy as np
from jax import lax
from jax.experimental import pallas as pl
from jax.experimental.pallas import tpu as pltpu

D_MODEL = 1024
DEPTH = 2
PAGE_SIZE = 128

HEAD_DIM = 64
NSA_HEADS = 4
NSA_KV_HEADS = 2
NSA_GROUP = NSA_HEADS // NSA_KV_HEADS
NSA_BLOCK = 64
NSA_TOPK = 16
NSA_WINDOW = 512
NUM_BUCKETS = 32
MAX_DISTANCE = 128
CONV_CH = D_MODEL // 4
CONV_WIDTH = 31
RET_HEADS = 4
RET_W = RET_HEADS * HEAD_DIM
RET_CHUNK = 64
ROPE_BASE = 10000.0
GDN_HEADS = 4
GDN_W = GDN_HEADS * HEAD_DIM
GDN_CONV = 4
GDN_CHUNK = 64
D_FF = 2816
FFN_CONV = 3
N_BRANCH = 4
BRANCH_W = NSA_HEADS * HEAD_DIM
EPS = 1e-6
NEG_INF = -1e30
IN_SPLITS = (NSA_HEADS * HEAD_DIM, 6 * NSA_KV_HEADS * HEAD_DIM, 3 * NSA_HEADS, 2 * CONV_CH, 4 * RET_W,
             4 * GDN_W + 2 * GDN_HEADS, N_BRANCH * D_MODEL)
IN_COLS = sum(IN_SPLITS)

LANES = 128
SUBLANES = 8
VMEM_LIMIT = 56 * 1024 * 1024


def _round_up(a, m):
    return -(-a // m) * m


ADA_TN = 2048


def _adaln_kernel(c_ref, w_ref, b_ref, o_ref):
    c = c_ref[...]
    act = (c * jax.nn.sigmoid(c)).astype(jnp.bfloat16)
    o_ref[0] = jnp.dot(act, w_ref[0].astype(jnp.bfloat16), preferred_element_type=jnp.float32) + b_ref[0]


def _adaln(c, w_ada, b_ada):
    R = c.shape[0]
    L, _, N = w_ada.shape
    Rp = _round_up(R, SUBLANES)
    cp = jnp.pad(c, ((0, Rp - R), (0, 0)))
    out = pl.pallas_call(
        _adaln_kernel,
        grid=(L, N // ADA_TN),
        in_specs=[pl.BlockSpec((Rp, D_MODEL), lambda l, j: (0, 0)),
                  pl.BlockSpec((1, D_MODEL, ADA_TN), lambda l, j: (l, 0, j)),
                  pl.BlockSpec((1, 1, ADA_TN), lambda l, j: (l, 0, j))],
        out_specs=pl.BlockSpec((1, Rp, ADA_TN), lambda l, j: (l, 0, j)),
        out_shape=jax.ShapeDtypeStruct((L, Rp, N), jnp.float32),
        compiler_params=pltpu.CompilerParams(dimension_semantics=("arbitrary", "arbitrary"),
                                             vmem_limit_bytes=VMEM_LIMIT),
        name="adaln",
    )(cp, w_ada, b_ada[:, None, :])
    return out[:, :R]


NSA_TQ = 128
BLOCKS_PER_TILE = NSA_TQ // NSA_BLOCK


def _t5_thresholds():
    n = np.arange(0, 2 * MAX_DISTANCE)
    exact = NUM_BUCKETS // 2
    large = exact + (np.log(np.maximum(n, 1).astype(np.float32) / np.float32(exact))
                     / np.float32(math.log(MAX_DISTANCE / exact)) * (NUM_BUCKETS - exact)).astype(np.int32)
    bucket = np.where(n < exact, n, np.minimum(large, NUM_BUCKETS - 1))
    return tuple(int(np.argmax(bucket >= k)) for k in range(1, NUM_BUCKETS))


_T5_THR = _t5_thresholds()


def _bias_from_dist(dist, rb_ref, h):
    v = jnp.full(dist.shape, rb_ref[NUM_BUCKETS - 1, h], jnp.float32)
    for k in range(NUM_BUCKETS - 2, -1, -1):
        v = jnp.where(dist < _T5_THR[k], rb_ref[k, h], v)
    return v


def _dot_nt(a, b):
    return lax.dot_general(a, b, (((1,), (1,)), ((), ())), preferred_element_type=jnp.float32)


def _flash_tile(carry, k_t, vt_t, qs, mask_add):
    m, l, acc = carry
    s = _dot_nt(k_t, qs) + mask_add
    m_new = jnp.maximum(m, jnp.max(s, axis=0, keepdims=True))
    p = jnp.exp(s - m_new)
    alpha = jnp.exp(m - m_new)
    l = alpha * l + jnp.sum(p, axis=0, keepdims=True)
    acc = alpha * acc + jnp.dot(vt_t, p.astype(jnp.bfloat16), preferred_element_type=jnp.float32)
    return m_new, l, acc


FAR_GROUP = 8
NEAR_ROWS = 2 * NSA_TQ
WIN_ROWS = NSA_WINDOW + NSA_TQ
assert WIN_ROWS <= FAR_GROUP * NSA_TQ


def _nsa_prompt_kernel(rb_ref, q_ref, kv_ref, gt_ref, poolt_ref, pe_ref, o_ref,
                       kc_ref, vc_ref, ksel_ref, vselt_ref, kwin_ref, vwint_ref, near_ref, wtbl_ref, score_ref,
                       sel_ref, s_ref, p_ref):
    b = pl.program_id(0)
    qi = pl.program_id(1)
    T = kv_ref.shape[1]
    nb = T // NSA_BLOCK
    topk = min(NSA_TOPK, nb)
    TQ = NSA_TQ
    CH = 512
    f32, bf16 = jnp.float32, jnp.bfloat16
    kvs = range(NSA_KV_HEADS)

    @pl.when((b == 0) & (qi == 0))
    def _tables():
        for kvg in kvs:
            ksel_ref[kvg, 0:TQ, :] = jnp.zeros((TQ, HEAD_DIM), bf16)
            vselt_ref[kvg, :, 0:TQ] = jnp.zeros((HEAD_DIM, TQ), bf16)
            kwin_ref[kvg, 0:NSA_WINDOW, :] = jnp.zeros((NSA_WINDOW, HEAD_DIM), bf16)
            vwint_ref[kvg, :, 0:NSA_WINDOW] = jnp.zeros((HEAD_DIM, NSA_WINDOW), bf16)
        d_near = (lax.broadcasted_iota(jnp.int32, (NEAR_ROWS, TQ), 1) + TQ
                  - lax.broadcasted_iota(jnp.int32, (NEAR_ROWS, TQ), 0))
        d_win = (lax.broadcasted_iota(jnp.int32, (WIN_ROWS, TQ), 1) + NSA_WINDOW
                 - lax.broadcasted_iota(jnp.int32, (WIN_ROWS, TQ), 0))
        for h in range(NSA_HEADS):
            kvg, g = divmod(h, NSA_GROUP)
            lanes = slice(g * TQ, (g + 1) * TQ)
            near_ref[kvg, :, lanes] = jnp.where(
                d_near >= 0, _bias_from_dist(d_near, rb_ref, h) - rb_ref[NUM_BUCKETS - 1, h], NEG_INF)
            wtbl_ref[kvg, :, lanes] = jnp.where((d_win >= 0) & (d_win < NSA_WINDOW),
                                                _bias_from_dist(d_win, rb_ref, h), NEG_INF)

    @pl.when(qi == 0)
    def _prologue():
        def chunk(i, carry):
            r = pl.multiple_of(i * CH, CH)
            rs = pl.multiple_of(i * CH + TQ, TQ)
            rw = pl.multiple_of(i * CH + NSA_WINDOW, TQ)
            rb8 = pl.multiple_of(i * (CH // NSA_BLOCK), CH // NSA_BLOCK)
            for kvg in kvs:
                def col(c):
                    lo = c * NSA_KV_HEADS * HEAD_DIM + kvg * HEAD_DIM
                    return kv_ref[0, pl.ds(r, CH), lo:lo + HEAD_DIM]
                for c, dst in ((0, kc_ref), (1, vc_ref)):
                    x = col(c).reshape(CH // NSA_BLOCK, NSA_BLOCK, HEAD_DIM) + pe_ref[c][None]
                    dst[kvg, pl.ds(rb8, CH // NSA_BLOCK), :] = jnp.sum(x * poolt_ref[:, c:c + 1][None], axis=1)
                ksel_ref[kvg, pl.ds(rs, CH), :] = col(2).astype(bf16)
                vselt_ref[kvg, :, pl.ds(rs, CH)] = col(3).T.astype(bf16)
                kwin_ref[kvg, pl.ds(rw, CH), :] = col(4).astype(bf16)
                vwint_ref[kvg, :, pl.ds(rw, CH)] = col(5).T.astype(bf16)
            return carry
        lax.fori_loop(0, T // CH, chunk, 0)

    q = q_ref[0]
    gates = jax.nn.sigmoid(gt_ref[0])
    n_io = lax.broadcasted_iota(jnp.int32, (nb, TQ), 0)
    t_io = lax.broadcasted_iota(jnp.int32, (nb, TQ), 1)
    dist_c = qi * TQ + t_io - (n_io * NSA_BLOCK + NSA_BLOCK - 1)
    vis_c = dist_c >= 0
    vis_c2 = jnp.concatenate([vis_c, vis_c], axis=1)
    cur = (qi * TQ + t_io) // NSA_BLOCK
    forced = (n_io == 0) | (n_io == cur) | (n_io == cur - 1)
    q0 = pl.multiple_of(qi * TQ, TQ)

    qs, oc, score = [], [], []
    for kvg in kvs:
        base = kvg * NSA_GROUP * HEAD_DIM
        qk = jnp.concatenate([q[:, base + g * HEAD_DIM: base + (g + 1) * HEAD_DIM] for g in range(NSA_GROUP)], axis=0)
        qs.append((qk * HEAD_DIM ** -0.5).astype(bf16))
        sc = _dot_nt(kc_ref[kvg].astype(bf16), qs[kvg])
        bias_c = jnp.concatenate([_bias_from_dist(dist_c, rb_ref, kvg * NSA_GROUP + g) for g in range(NSA_GROUP)],
                                 axis=1)
        sc = jnp.where(vis_c2, sc + bias_c, NEG_INF)
        e = jnp.exp(sc - jnp.max(sc, axis=0, keepdims=True))
        p_c = e / jnp.sum(e, axis=0, keepdims=True) * vis_c2.astype(f32)
        oc.append(jnp.dot(vc_ref[kvg].T.astype(bf16), p_c.astype(bf16), preferred_element_type=f32))
        score.append(jnp.where(n_io <= cur, jnp.where(forced, 2.0, p_c[:, :TQ] + p_c[:, TQ:]), -1.0))
        score_ref[kvg] = score[kvg]

    def rank_body(mi, ranks):
        out = []
        for kvg in kvs:
            row = score_ref[kvg, pl.ds(mi, 1), :]
            beats = (row > score[kvg]) | ((row == score[kvg]) & (mi < n_io))
            out.append(ranks[kvg] + beats.astype(jnp.int32))
        return tuple(out)

    ranks = lax.fori_loop(0, BLOCKS_PER_TILE * (qi + 1), rank_body,
                          tuple(jnp.zeros((nb, TQ), jnp.int32) for _ in kvs))
    for kvg in kvs:
        sel_ref[kvg] = jnp.where((ranks[kvg] < topk) & (n_io <= cur), 0.0, NEG_INF)

    def mask_rows(kvg, blk0, nblk, limit):
        rows = []
        for u in range(nblk):
            blk = blk0 + u
            ok = (blk >= 0) & (blk < limit)
            row = sel_ref[kvg, pl.ds(jnp.clip(blk, 0, nb - 1), 1), :]
            rows.append(jnp.broadcast_to(jnp.where(ok, row, NEG_INF), (NSA_BLOCK, TQ)))
        mm = jnp.concatenate(rows, axis=0)
        return jnp.concatenate([mm, mm], axis=1)

    carries = []
    for kvg in kvs:
        k_n = ksel_ref[kvg, pl.ds(q0, NEAR_ROWS), :]
        vt_n = vselt_ref[kvg, :, pl.ds(q0, NEAR_ROWS)]
        s = (_dot_nt(k_n, qs[kvg]) + near_ref[kvg]
             + mask_rows(kvg, BLOCKS_PER_TILE * (qi - 1), 2 * BLOCKS_PER_TILE, nb))
        m = jnp.max(s, axis=0, keepdims=True)
        p = jnp.exp(s - m)
        carries.append((m, jnp.sum(p, axis=0, keepdims=True),
                        jnp.dot(vt_n, p.astype(bf16), preferred_element_type=f32)))

    n_far = jnp.maximum(qi - 1, 0)
    rows_far = FAR_GROUP * TQ

    def far_body(i, cs):
        out = []
        for kvg in kvs:
            m_old, l_old, acc_old = cs[kvg]
            r = pl.multiple_of(TQ + i * rows_far, TQ)
            m_new = m_old
            for u in range(FAR_GROUP):
                s_u = (_dot_nt(ksel_ref[kvg, pl.ds(r + u * TQ, TQ), :], qs[kvg])
                       + mask_rows(kvg, (i * FAR_GROUP + u) * BLOCKS_PER_TILE, BLOCKS_PER_TILE,
                                   n_far * BLOCKS_PER_TILE))
                s_ref[kvg, u * TQ:(u + 1) * TQ, :] = s_u
                m_new = jnp.maximum(m_new, jnp.max(s_u, axis=0, keepdims=True))
            l_new = jnp.exp(m_old - m_new) * l_old
            for u in range(FAR_GROUP):
                p_u = jnp.exp(s_ref[kvg, u * TQ:(u + 1) * TQ, :] - m_new)
                l_new = l_new + jnp.sum(p_u, axis=0, keepdims=True)
                p_ref[kvg, u * TQ:(u + 1) * TQ, :] = p_u.astype(bf16)
            acc = jnp.exp(m_old - m_new) * acc_old + jnp.dot(vselt_ref[kvg, :, pl.ds(r, rows_far)], p_ref[kvg],
                                                             preferred_element_type=f32)
            out.append((m_new, l_new, acc))
        return tuple(out)

    carries = lax.fori_loop(0, (n_far + FAR_GROUP - 1) // FAR_GROUP, far_body, tuple(carries))

    w_io = lax.broadcasted_iota(jnp.int32, (TQ, 2 * TQ), 0)
    outs = []
    for kvg in kvs:
        m_s, l_s, acc_s = carries[kvg]
        o_s = acc_s / l_s
        m_w = jnp.full((1, 2 * TQ), NEG_INF, f32)
        for u in range(WIN_ROWS // TQ):
            s_u = (_dot_nt(kwin_ref[kvg, pl.ds(q0 + u * TQ, TQ), :], qs[kvg])
                   + wtbl_ref[kvg, u * TQ:(u + 1) * TQ, :])
            s_u = jnp.where(w_io >= NSA_WINDOW - u * TQ - q0, s_u, NEG_INF)
            s_ref[kvg, u * TQ:(u + 1) * TQ, :] = s_u
            m_w = jnp.maximum(m_w, jnp.max(s_u, axis=0, keepdims=True))
        l_w = jnp.zeros((1, 2 * TQ), f32)
        for u in range(WIN_ROWS // TQ):
            p_u = jnp.exp(s_ref[kvg, u * TQ:(u + 1) * TQ, :] - m_w)
            l_w = l_w + jnp.sum(p_u, axis=0, keepdims=True)
            p_ref[kvg, u * TQ:(u + 1) * TQ, :] = p_u.astype(bf16)
        o_w = jnp.dot(vwint_ref[kvg, :, pl.ds(q0, WIN_ROWS)], p_ref[kvg, 0:WIN_ROWS, :],
                      preferred_element_type=f32) / l_w
        for g in range(NSA_GROUP):
            h = kvg * NSA_GROUP + g
            lanes = slice(g * TQ, (g + 1) * TQ)
            o = (gates[h:h + 1] * oc[kvg][:, lanes] + gates[NSA_HEADS + h:NSA_HEADS + h + 1] * o_s[:, lanes]
                 + gates[2 * NSA_HEADS + h:2 * NSA_HEADS + h + 1] * o_w[:, lanes])
            outs.append(o.T)
    o_ref[0] = jnp.concatenate(outs, axis=1)


def _nsa_prompt(q, q_blk, kv, kv_blk, gate_logits, cmp_pool, cmp_pe, rel_bias):
    B, T, _ = q.shape
    nb = T // NSA_BLOCK
    assert T % 512 == 0 and nb % SUBLANES == 0 and (T // NSA_TQ) % FAR_GROUP == 0
    TQ = NSA_TQ
    gt = jnp.transpose(gate_logits, (0, 2, 1))
    f32, bf16 = jnp.float32, jnp.bfloat16
    return pl.pallas_call(
        _nsa_prompt_kernel,
        grid=(B, T // TQ),
        in_specs=[
            pl.BlockSpec(memory_space=pltpu.SMEM),
            pl.BlockSpec((1, TQ, NSA_HEADS * HEAD_DIM), lambda b, i: (b, i, q_blk)),
            pl.BlockSpec((1, T, 6 * NSA_KV_HEADS * HEAD_DIM), lambda b, i: (b, 0, kv_blk)),
            pl.BlockSpec((1, 3 * NSA_HEADS, TQ), lambda b, i: (b, 0, i)),
            pl.BlockSpec((NSA_BLOCK, 2), lambda b, i: (0, 0)),
            pl.BlockSpec((2, NSA_BLOCK, HEAD_DIM), lambda b, i: (0, 0, 0)),
        ],
        out_specs=pl.BlockSpec((1, TQ, NSA_HEADS * HEAD_DIM), lambda b, i: (b, i, 0)),
        out_shape=jax.ShapeDtypeStruct((B, T, NSA_HEADS * HEAD_DIM), f32),
        scratch_shapes=[
            pltpu.VMEM((NSA_KV_HEADS, nb, HEAD_DIM), f32),
            pltpu.VMEM((NSA_KV_HEADS, nb, HEAD_DIM), f32),
            pltpu.VMEM((NSA_KV_HEADS, T + TQ, HEAD_DIM), bf16),
            pltpu.VMEM((NSA_KV_HEADS, HEAD_DIM, T + TQ), bf16),
            pltpu.VMEM((NSA_KV_HEADS, T + NSA_WINDOW, HEAD_DIM), bf16),
            pltpu.VMEM((NSA_KV_HEADS, HEAD_DIM, T + NSA_WINDOW), bf16),
            pltpu.VMEM((NSA_KV_HEADS, NEAR_ROWS, NSA_GROUP * TQ), f32),
            pltpu.VMEM((NSA_KV_HEADS, WIN_ROWS, NSA_GROUP * TQ), f32),
            pltpu.VMEM((NSA_KV_HEADS, nb, TQ), f32),
            pltpu.VMEM((NSA_KV_HEADS, nb, TQ), f32),
            pltpu.VMEM((NSA_KV_HEADS, FAR_GROUP * TQ, NSA_GROUP * TQ), f32),
            pltpu.VMEM((NSA_KV_HEADS, FAR_GROUP * TQ, NSA_GROUP * TQ), bf16),
        ],
        compiler_params=pltpu.CompilerParams(dimension_semantics=("arbitrary", "arbitrary"),
                                             vmem_limit_bytes=VMEM_LIMIT),
        name="nsa_prompt",
    )(rel_bias, q, kv, gt, jnp.transpose(cmp_pool), cmp_pe)


PAGES_PER_STEP = 64
SEL_TILE = 2048
KVW = NSA_KV_HEADS * HEAD_DIM


def _nsa_sample_kernel(pt_ref, rb_ref, *refs):
    PPS = PAGES_PER_STEP
    pages = refs[:PPS]
    (q_ref, kvn_ref, gt_ref, win_ref, poolm_ref, cconst_ref, o_ref, cmp_ref, kselt_ref, vselt_ref,
     ssel_ref) = refs[PPS:]
    s_id = pl.program_id(1)
    f32, bf16 = jnp.float32, jnp.bfloat16
    Tn = q_ref.shape[1]
    past = kselt_ref.shape[1]
    nbp = past // NSA_BLOCK
    Wb = win_ref.shape[3]
    bpp = PAGE_SIZE // NSA_BLOCK
    R = NSA_HEADS * Tn

    rows = []
    for k in range(PPS):
        parts = []
        for c in range(2):
            a = _dot_nt(poolm_ref[c], pages[k][0, 0, c].astype(bf16))
            parts.append(a[0:bpp] + a[SUBLANES:SUBLANES + bpp] + cconst_ref[c:c + 1])
        rows.append(jnp.concatenate(parts, axis=1))
        r = pl.multiple_of((s_id * PPS + k) * PAGE_SIZE, PAGE_SIZE)
        kselt_ref[:, pl.ds(r, PAGE_SIZE)] = pages[k][0, 0, 2].astype(bf16)
        vselt_ref[:, pl.ds(r, PAGE_SIZE)] = pages[k][0, 0, 3].astype(bf16)
    cmp_ref[pl.ds(pl.multiple_of(s_id * PPS * bpp, PPS * bpp), PPS * bpp), :] = jnp.concatenate(rows, axis=0)

    @pl.when(s_id == pl.num_programs(1) - 1)
    def _attend():
        def per_head(fn):
            return jnp.concatenate([fn(h) for h in range(NSA_HEADS)], axis=0)

        q = q_ref[0] * HEAD_DIM ** -0.5
        zero = jnp.zeros((Tn, HEAD_DIM), f32)

        def q_rows(h):
            qh = q[:, h * HEAD_DIM:(h + 1) * HEAD_DIM]
            return jnp.concatenate([qh, zero] if h < NSA_GROUP else [zero, qh], axis=1)

        q2 = per_head(q_rows).astype(bf16)
        kvn = kvn_ref[0]
        pad = jnp.zeros((LANES - Tn, KVW), f32)

        def new_rows(c):
            return jnp.concatenate([kvn[:, c * KVW:(c + 1) * KVW], pad], axis=0).astype(bf16)

        tn_io = lax.broadcasted_iota(jnp.int32, (Tn, LANES), 0)
        jn_io = lax.broadcasted_iota(jnp.int32, (Tn, LANES), 1)
        d_new = tn_io - jn_io
        keep_new = per_head(lambda h: (d_new >= 0) & (jn_io < Tn))
        bias_new = per_head(lambda h: _bias_from_dist(d_new, rb_ref, h))
        far = per_head(lambda h: jnp.full((Tn, 1), rb_ref[NUM_BUCKETS - 1, h], f32))

        kc = cmp_ref[:, 0:KVW].astype(bf16)
        vc = cmp_ref[:, KVW:2 * KVW].astype(bf16)
        n_io = lax.broadcasted_iota(jnp.int32, (Tn, nbp), 1)
        t_io = lax.broadcasted_iota(jnp.int32, (Tn, nbp), 0)
        dist_c = past + t_io - (n_io * NSA_BLOCK + NSA_BLOCK - 1)
        sc = _dot_nt(q2, kc) + per_head(lambda h: _bias_from_dist(dist_c, rb_ref, h))
        e = jnp.exp(sc - jnp.max(sc, axis=1, keepdims=True))
        p_c = e / jnp.sum(e, axis=1, keepdims=True)
        oc = jnp.dot(p_c.astype(bf16), vc, preferred_element_type=f32)

        topk = min(NSA_TOPK, nbp + 1)
        m_io = lax.broadcasted_iota(jnp.int32, (nbp, nbp), 0)
        c_io = lax.broadcasted_iota(jnp.int32, (nbp, nbp), 1)
        lower = m_io < c_io
        forced = (n_io == 0) | (n_io == nbp - 1)
        sels = []
        for kvg in range(NSA_KV_HEADS):
            r0 = kvg * NSA_GROUP * Tn
            score = jnp.where(forced, 2.0, p_c[r0:r0 + Tn] + p_c[r0 + Tn:r0 + 2 * Tn])
            score_t = jnp.concatenate([score, jnp.zeros((LANES - Tn, nbp), f32)], axis=0).T
            ranks = []
            for t in range(Tn):
                colb = jnp.broadcast_to(score_t[:, t:t + 1], (nbp, nbp))
                rowb = jnp.broadcast_to(score[t:t + 1, :], (nbp, nbp))
                beats = (colb > rowb) | ((colb == rowb) & lower)
                ranks.append(jnp.sum(beats.astype(f32), axis=0, keepdims=True))
            sel = (jnp.concatenate(ranks, axis=0) < topk - 1).astype(f32)
            sels += [sel] * NSA_GROUP
        sel_rows = jnp.concatenate(sels, axis=0).astype(bf16)

        bpt = SEL_TILE // NSA_BLOCK
        expand = (lax.broadcasted_iota(jnp.int32, (bpt, SEL_TILE), 0)
                  == lax.broadcasted_iota(jnp.int32, (bpt, SEL_TILE), 1) // NSA_BLOCK).astype(bf16)
        d_last = LANES + tn_io - jn_io
        near = per_head(lambda h: _bias_from_dist(d_last, rb_ref, h))
        ntile = past // SEL_TILE
        s_new = jnp.where(keep_new, _dot_nt(q2, new_rows(2)) + bias_new, NEG_INF)
        m = jnp.max(s_new, axis=1, keepdims=True)
        for j in range(ntile):
            keep = jnp.dot(sel_rows[:, j * bpt:(j + 1) * bpt], expand, preferred_element_type=f32) > 0.5
            if j == ntile - 1:
                bias = jnp.concatenate([jnp.broadcast_to(far, (R, SEL_TILE - LANES)), near], axis=1)
            else:
                bias = far
            s_t = jnp.dot(q2, kselt_ref[:, j * SEL_TILE:(j + 1) * SEL_TILE], preferred_element_type=f32)
            s_t = jnp.where(keep, s_t + bias, NEG_INF)
            ssel_ref[:, j * SEL_TILE:(j + 1) * SEL_TILE] = s_t
            m = jnp.maximum(m, jnp.max(s_t, axis=1, keepdims=True))
        p_new = jnp.exp(s_new - m)
        l = jnp.sum(p_new, axis=1, keepdims=True)
        acc = jnp.dot(p_new.astype(bf16), new_rows(3), preferred_element_type=f32)
        for j in range(ntile):
            p = jnp.exp(ssel_ref[:, j * SEL_TILE:(j + 1) * SEL_TILE] - m)
            l = l + jnp.sum(p, axis=1, keepdims=True)
            acc = acc + _dot_nt(p.astype(bf16), vselt_ref[:, j * SEL_TILE:(j + 1) * SEL_TILE])
        o_s = acc / l

        tw_io = lax.broadcasted_iota(jnp.int32, (Tn, Wb), 0)
        cw_io = lax.broadcasted_iota(jnp.int32, (Tn, Wb), 1)
        d_w = Wb + tw_io - cw_io
        s_w = jnp.where(per_head(lambda h: d_w < NSA_WINDOW),
                        jnp.dot(q2, win_ref[0, 0].astype(bf16), preferred_element_type=f32)
                        + per_head(lambda h: _bias_from_dist(d_w, rb_ref, h)), NEG_INF)
        s_wn = jnp.where(keep_new, _dot_nt(q2, new_rows(4)) + bias_new, NEG_INF)
        s_all = jnp.concatenate([s_w, s_wn], axis=1)
        e = jnp.exp(s_all - jnp.max(s_all, axis=1, keepdims=True))
        p_w = (e / jnp.sum(e, axis=1, keepdims=True)).astype(bf16)
        o_w = _dot_nt(p_w[:, :Wb], win_ref[0, 1].astype(bf16)) + jnp.dot(p_w[:, Wb:], new_rows(5),
                                                                          preferred_element_type=f32)

        gates = jax.nn.sigmoid(gt_ref[0])
        outs = []
        for h in range(NSA_HEADS):
            rs = slice(h * Tn, (h + 1) * Tn)
            cs = slice((h // NSA_GROUP) * HEAD_DIM, (h // NSA_GROUP + 1) * HEAD_DIM)
            outs.append(gates[:, h:h + 1] * oc[rs, cs] + gates[:, NSA_HEADS + h:NSA_HEADS + h + 1] * o_s[rs, cs]
                        + gates[:, 2 * NSA_HEADS + h:2 * NSA_HEADS + h + 1] * o_w[rs, cs])
        o_ref[0] = jnp.concatenate(outs, axis=1)


def _nsa_sample(q, kv, gate_logits, cache, layer, page_table, win_buf, cmp_pool, cmp_pe, rel_bias):
    B, Tn, _ = q.shape
    npages = page_table.shape[1]
    past = npages * PAGE_SIZE
    Wb = win_buf.shape[1]
    PPS = PAGES_PER_STEP
    assert npages % PPS == 0 and past % SEL_TILE == 0 and Tn == SUBLANES and (past // NSA_BLOCK) % LANES == 0
    f32, bf16 = jnp.float32, jnp.bfloat16
    bpp = PAGE_SIZE // NSA_BLOCK
    cache_t = jnp.transpose(cache, (0, 1, 3, 4, 5, 2)).reshape(cache.shape[0], cache.shape[1], 4, KVW, PAGE_SIZE)
    win_t = jnp.transpose(win_buf, (0, 2, 3, 4, 1)).reshape(B, 2, KVW, Wb)
    r_io = np.arange(PAGE_SIZE)
    onehot = jnp.asarray((r_io[None, :] // NSA_BLOCK == np.arange(SUBLANES)[:, None]), f32)
    pool_full = onehot[None] * jnp.tile(cmp_pool, (1, bpp))[:, None, :]
    pool_hi = pool_full.astype(bf16)
    pool_lo = (pool_full - pool_hi.astype(f32)).astype(bf16)
    poolm = jnp.concatenate([pool_hi, pool_lo], axis=1)
    cconst = jnp.tile(jnp.sum(cmp_pool[:, :, None] * cmp_pe, axis=1), (1, NSA_KV_HEADS))

    def page_spec(k):
        return pl.BlockSpec((1, 1, 4, KVW, PAGE_SIZE), lambda b, s, pt: (pt[b, s * PPS + k], layer, 0, 0, 0))

    grid_spec = pltpu.PrefetchScalarGridSpec(
        num_scalar_prefetch=1,
        grid=(B, npages // PPS),
        in_specs=[pl.BlockSpec(memory_space=pltpu.SMEM)] + [page_spec(k) for k in range(PPS)] + [
            pl.BlockSpec((1, Tn, NSA_HEADS * HEAD_DIM), lambda b, s, pt: (b, 0, 0)),
            pl.BlockSpec((1, Tn, 6 * KVW), lambda b, s, pt: (b, 0, 0)),
            pl.BlockSpec((1, Tn, 3 * NSA_HEADS), lambda b, s, pt: (b, 0, 0)),
            pl.BlockSpec((1, 2, KVW, Wb), lambda b, s, pt: (b, 0, 0, 0)),
            pl.BlockSpec((2, 2 * SUBLANES, PAGE_SIZE), lambda b, s, pt: (0, 0, 0)),
            pl.BlockSpec((2, KVW), lambda b, s, pt: (0, 0)),
        ],
        out_specs=pl.BlockSpec((1, Tn, NSA_HEADS * HEAD_DIM), lambda b, s, pt: (b, 0, 0)),
        scratch_shapes=[
            pltpu.VMEM((past // NSA_BLOCK, 2 * KVW), f32),
            pltpu.VMEM((KVW, past), bf16),
            pltpu.VMEM((KVW, past), bf16),
            pltpu.VMEM((NSA_HEADS * Tn, past), f32),
        ],
    )
    return pl.pallas_call(
        _nsa_sample_kernel,
        grid_spec=grid_spec,
        out_shape=jax.ShapeDtypeStruct((B, Tn, NSA_HEADS * HEAD_DIM), f32),
        compiler_params=pltpu.CompilerParams(dimension_semantics=("arbitrary", "arbitrary"),
                                             vmem_limit_bytes=VMEM_LIMIT),
        name="nsa_sample",
    )(page_table, rel_bias, *([cache_t] * PPS), q, kv, gate_logits, win_t, poolm, cconst)


PROJ_KV, PROJ_Q, PROJ_UCV, PROJ_RET, PROJ_GQKV, PROJ_GZ, PROJ_MISC = 0, 768, 1024, 1536, 2560, 3328, 3584
PROJ_COLS = PROJ_MISC + LANES
PROJ_TM = 512
MERGE_TM = 256
FFN_TM = 256


def _prep_w_in(w_in):
    o = [int(v) for v in np.cumsum((0,) + IN_SPLITS)]
    small = o[5] + 4 * GDN_W
    pieces = [w_in[:, o[1]:o[2]], w_in[:, o[0]:o[1]], w_in[:, o[3]:o[4]], w_in[:, o[4]:o[5]], w_in[:, o[5]:small],
              w_in[:, o[2]:o[3]], w_in[:, small:o[6]]]
    used = sum(pc.shape[1] for pc in pieces)
    pieces.append(jnp.zeros((D_MODEL, PROJ_COLS - used), w_in.dtype))
    return jnp.concatenate(pieces, axis=1), w_in[:, o[6]:]


def _modulated_norm(x, g, sc, sh):
    y = x * lax.rsqrt(jnp.mean(x * x, axis=-1, keepdims=True) + EPS)
    return (y * g) * (1.0 + sc) + sh


def _resident(shape):
    return pl.BlockSpec(shape, lambda *_: (0,) * len(shape), pipeline_mode=pl.Buffered(1))


def _mod_spec(mod, rows_per_group, tm):
    if mod.shape[1] == 1:
        return pl.BlockSpec((1, 1, D_MODEL), lambda i, *_: (i // (rows_per_group // tm), 0, 0))
    return pl.BlockSpec((1, tm, D_MODEL), lambda i, *_: (0, i, 0))


def _proj_kernel(x_ref, g_ref, sc_ref, sh_ref, w_ref, o_ref, wb_ref):
    @pl.when(pl.program_id(0) == 0)
    def _():
        wb_ref[...] = w_ref[...].astype(jnp.bfloat16)

    h = _modulated_norm(x_ref[...], g_ref[...], sc_ref[0], sh_ref[0]).astype(jnp.bfloat16)
    o_ref[...] = jnp.dot(h, wb_ref[...], preferred_element_type=jnp.float32)


def _proj(x2, g, sc, sh, w, rows_per_group, tm):
    M = x2.shape[0]
    N = w.shape[1]
    return pl.pallas_call(
        _proj_kernel,
        grid=(M // tm,),
        in_specs=[pl.BlockSpec((tm, D_MODEL), lambda i: (i, 0)), _resident((1, D_MODEL)),
                  _mod_spec(sc, rows_per_group, tm), _mod_spec(sh, rows_per_group, tm), _resident((D_MODEL, N))],
        out_specs=pl.BlockSpec((tm, N), lambda i: (i, 0)),
        out_shape=jax.ShapeDtypeStruct((M, N), jnp.float32),
        scratch_shapes=[pltpu.VMEM((D_MODEL, N), jnp.bfloat16)],
        compiler_params=pltpu.CompilerParams(dimension_semantics=("arbitrary",), vmem_limit_bytes=VMEM_LIMIT),
        name="in_proj",
    )(x2, g, sc, sh, w)


def _merge_kernel(x_ref, g0_ref, sc_ref, sh_ref, gt_ref, g1_ref, b0_ref, b1_ref, b2_ref, b3_ref,
                  wmg_ref, wbr_ref, wout_ref, o_ref, wmgb_ref):
    f32, bf16 = jnp.float32, jnp.bfloat16

    @pl.when(pl.program_id(0) == 0)
    def _():
        wmgb_ref[...] = wmg_ref[...].astype(bf16)

    x = x_ref[...]
    h = _modulated_norm(x, g0_ref[...], sc_ref[0], sh_ref[0]).astype(bf16)
    acc = jnp.zeros(x.shape, f32)
    for n, b_ref in enumerate((b0_ref, b1_ref, b2_ref, b3_ref)):
        gate = jax.nn.sigmoid(jnp.dot(h, wmgb_ref[:, n * D_MODEL:(n + 1) * D_MODEL], preferred_element_type=f32))
        acc = acc + gate * jnp.dot(b_ref[...].astype(bf16), wbr_ref[n], preferred_element_type=f32)
    mixed = jnp.dot(acc.astype(bf16), wout_ref[...], preferred_element_type=f32)
    y = mixed * lax.rsqrt(jnp.mean(mixed * mixed, axis=-1, keepdims=True) + EPS) * g1_ref[...]
    o_ref[...] = x + gt_ref[0] * y


def _merge(x2, g0, sc, sh, gt, g1, branches, wmg, wbr, wout, rows_per_group, tm):
    M = x2.shape[0]
    row = lambda w: pl.BlockSpec((tm, w), lambda i: (i, 0))
    mspec = _mod_spec(sc, rows_per_group, tm)
    return pl.pallas_call(
        _merge_kernel,
        grid=(M // tm,),
        in_specs=[row(D_MODEL), _resident((1, D_MODEL)), mspec, mspec, mspec, _resident((1, D_MODEL))]
                 + [row(BRANCH_W)] * N_BRANCH
                 + [_resident(wmg.shape), _resident(wbr.shape), _resident(wout.shape)],
        out_specs=row(D_MODEL),
        out_shape=jax.ShapeDtypeStruct((M, D_MODEL), jnp.float32),
        scratch_shapes=[pltpu.VMEM(wmg.shape, jnp.bfloat16)],
        compiler_params=pltpu.CompilerParams(dimension_semantics=("arbitrary",), vmem_limit_bytes=VMEM_LIMIT),
        name="merge",
    )(x2, g0, sc, sh, gt, g1, *branches, wmg, wbr, wout)


def _ffn_kernel(x_ref, g2_ref, sc_ref, sh_ref, gt_ref, g3_ref, buf_ref, wup_ref, dw_ref, wdn_ref,
                o_ref, st_ref, gp_ref):
    f32, bf16 = jnp.float32, jnp.bfloat16
    tm = x_ref.shape[1]
    HALO = SUBLANES

    @pl.when(pl.program_id(1) == 0)
    def _():
        gp_ref[0:HALO, :] = buf_ref[0]

    x = x_ref[0]
    h = _modulated_norm(x, g2_ref[...], sc_ref[0], sh_ref[0]).astype(bf16)
    gp_ref[HALO:HALO + tm, :] = jnp.dot(h, wup_ref[:, 0:D_FF], preferred_element_type=f32)
    val = jnp.dot(h, wup_ref[:, D_FF:2 * D_FF], preferred_element_type=f32)
    gconv = (dw_ref[2:3, :] * gp_ref[HALO:HALO + tm, :] + dw_ref[1:2, :] * gp_ref[HALO - 1:HALO - 1 + tm, :]
             + dw_ref[0:1, :] * gp_ref[HALO - 2:HALO - 2 + tm, :])
    a = (jax.nn.gelu(gconv) * val).astype(bf16)
    f = jnp.dot(a, wdn_ref[...], preferred_element_type=f32)
    y = f * lax.rsqrt(jnp.mean(f * f, axis=-1, keepdims=True) + EPS) * g3_ref[...]
    o_ref[0] = x + gt_ref[0] * y
    tail = gp_ref[tm:tm + HALO, :]
    gp_ref[0:HALO, :] = tail
    st_ref[0] = tail


def _ffn(x3, g2, sc, sh, gt, g3, buf, wup, dw, wdn, tm):
    B, T, _ = x3.shape
    buf8 = jnp.pad(buf, ((0, 0), (SUBLANES - (FFN_CONV - 1), 0), (0, 0)))
    mspec = pl.BlockSpec((1, 1, D_MODEL), lambda b, i: (b, 0, 0))
    y, st = pl.pallas_call(
        _ffn_kernel,
        grid=(B, T // tm),
        in_specs=[pl.BlockSpec((1, tm, D_MODEL), lambda b, i: (b, i, 0)), _resident((1, D_MODEL)), mspec, mspec, mspec,
                  _resident((1, D_MODEL)), pl.BlockSpec((1, SUBLANES, D_FF), lambda b, i: (b, 0, 0)),
                  _resident(wup.shape), _resident(dw.shape), _resident(wdn.shape)],
        out_specs=[pl.BlockSpec((1, tm, D_MODEL), lambda b, i: (b, i, 0)),
                   pl.BlockSpec((1, SUBLANES, D_FF), lambda b, i: (b, 0, 0))],
        out_shape=[jax.ShapeDtypeStruct((B, T, D_MODEL), jnp.float32),
                   jax.ShapeDtypeStruct((B, SUBLANES, D_FF), jnp.float32)],
        scratch_shapes=[pltpu.VMEM((tm + SUBLANES, D_FF), jnp.float32)],
        compiler_params=pltpu.CompilerParams(dimension_semantics=("arbitrary", "arbitrary"),
                                             vmem_limit_bytes=VMEM_LIMIT),
        name="conv_ffn",
    )(x3, g2, sc, sh, gt, g3, buf8, wup, dw, wdn)
    return y, st[:, SUBLANES - (FFN_CONV - 1):]


def _ffn_rows_kernel(x_ref, g2_ref, sc_ref, sh_ref, gt_ref, g3_ref, p1_ref, p2_ref, wup_ref, dw_ref, wdn_ref,
                     o_ref, gpre_ref, gp_ref, *, seg):
    f32, bf16 = jnp.float32, jnp.bfloat16
    M = x_ref.shape[0]
    HALO = SUBLANES
    x = x_ref[...]
    h = _modulated_norm(x, g2_ref[...], sc_ref[0], sh_ref[0]).astype(bf16)
    gpre = jnp.dot(h, wup_ref[:, 0:D_FF], preferred_element_type=f32)
    val = jnp.dot(h, wup_ref[:, D_FF:2 * D_FF], preferred_element_type=f32)
    gp_ref[0:HALO, :] = jnp.zeros((HALO, D_FF), f32)
    gp_ref[HALO:HALO + M, :] = gpre
    t = lax.broadcasted_iota(jnp.int32, (M, 1), 0) % seg
    prev1 = jnp.where(t >= 1, gp_ref[HALO - 1:HALO - 1 + M, :], p1_ref[...])
    prev2 = jnp.where(t >= 2, gp_ref[HALO - 2:HALO - 2 + M, :], p2_ref[...])
    gconv = dw_ref[2:3, :] * gpre + dw_ref[1:2, :] * prev1 + dw_ref[0:1, :] * prev2
    a = (jax.nn.gelu(gconv) * val).astype(bf16)
    f = jnp.dot(a, wdn_ref[...], preferred_element_type=f32)
    y = f * lax.rsqrt(jnp.mean(f * f, axis=-1, keepdims=True) + EPS) * g3_ref[...]
    o_ref[...] = x + gt_ref[0] * y
    gpre_ref[...] = gpre


def _ffn_rows(x3, g2, sc, sh, gt, g3, buf, wup, dw, wdn):
    B, T, _ = x3.shape
    M = B * T
    assert FFN_CONV == 3 and T >= FFN_CONV - 1
    zeros = jnp.zeros((B, T, D_FF), jnp.float32)
    p1 = zeros.at[:, 0].set(buf[:, 1]).reshape(M, D_FF)
    p2 = zeros.at[:, 0].set(buf[:, 0]).at[:, 1].set(buf[:, 1]).reshape(M, D_FF)
    full = lambda shape: pl.BlockSpec(shape, lambda i: (0,) * len(shape))
    y, gpre = pl.pallas_call(
        functools.partial(_ffn_rows_kernel, seg=T),
        grid=(1,),
        in_specs=[full((M, D_MODEL)), full((1, D_MODEL)), full((1, M, D_MODEL)), full((1, M, D_MODEL)),
                  full((1, M, D_MODEL)), full((1, D_MODEL)), full((M, D_FF)), full((M, D_FF)),
                  _resident(wup.shape), _resident(dw.shape), _resident(wdn.shape)],
        out_specs=[full((M, D_MODEL)), full((M, D_FF))],
        out_shape=[jax.ShapeDtypeStruct((M, D_MODEL), jnp.float32), jax.ShapeDtypeStruct((M, D_FF), jnp.float32)],
        scratch_shapes=[pltpu.VMEM((M + SUBLANES, D_FF), jnp.float32)],
        compiler_params=pltpu.CompilerParams(dimension_semantics=("arbitrary",), vmem_limit_bytes=VMEM_LIMIT),
        name="conv_ffn_rows",
    )(x3.reshape(M, D_MODEL), g2, sc, sh, gt, g3, p1, p2, wup, dw, wdn)
    return y.reshape(B, T, D_MODEL), gpre.reshape(B, T, D_FF)[:, T - (FFN_CONV - 1):]


CONF_HALO = 32
CONF_TM = 1024


def _conformer_kernel(u_ref, buf_ref, dw_ref, dwb_ref, lng_ref, lnb_ref, o_ref, st_ref, xp_ref):
    tm = u_ref.shape[1]
    first = CONF_HALO - (CONV_WIDTH - 1)

    @pl.when(pl.program_id(1) == 0)
    def _():
        xp_ref[0:CONF_HALO, :] = buf_ref[0]

    u = u_ref[0]
    xp_ref[CONF_HALO:CONF_HALO + tm, :] = u[:, :CONV_CH] * jax.nn.sigmoid(u[:, CONV_CH:])
    acc = jnp.zeros((tm, CONV_CH), jnp.float32)
    for k in range(CONV_WIDTH):
        acc = acc + dw_ref[k:k + 1, :] * xp_ref[first + k:first + k + tm, :]
    y = acc + dwb_ref[...]
    mu = jnp.mean(y, axis=-1, keepdims=True)
    var = jnp.mean(jnp.square(y - mu), axis=-1, keepdims=True)
    yn = (y - mu) * lax.rsqrt(var + EPS) * lng_ref[...] + lnb_ref[...]
    o_ref[0] = yn * jax.nn.sigmoid(yn)
    tail = xp_ref[tm:tm + CONF_HALO, :]
    xp_ref[0:CONF_HALO, :] = tail
    st_ref[0] = tail


def _conformer(P, col_blk, buf, dw, dw_b, ln_g, ln_b, tm):
    B, T, _ = P.shape
    bufp = jnp.pad(buf, ((0, 0), (CONF_HALO - (CONV_WIDTH - 1), 0), (0, 0)))
    vec = lambda: _resident((1, CONV_CH))
    o, st = pl.pallas_call(
        _conformer_kernel,
        grid=(B, T // tm),
        in_specs=[pl.BlockSpec((1, tm, 2 * CONV_CH), lambda b, i: (b, i, col_blk)),
                  pl.BlockSpec((1, CONF_HALO, CONV_CH), lambda b, i: (b, 0, 0)),
                  _resident((CONV_WIDTH, CONV_CH)), vec(), vec(), vec()],
        out_specs=[pl.BlockSpec((1, tm, CONV_CH), lambda b, i: (b, i, 0)),
                   pl.BlockSpec((1, CONF_HALO, CONV_CH), lambda b, i: (b, 0, 0))],
        out_shape=[jax.ShapeDtypeStruct((B, T, CONV_CH), jnp.float32),
                   jax.ShapeDtypeStruct((B, CONF_HALO, CONV_CH), jnp.float32)],
        scratch_shapes=[pltpu.VMEM((tm + CONF_HALO, CONV_CH), jnp.float32)],
        compiler_params=pltpu.CompilerParams(dimension_semantics=("arbitrary", "arbitrary"),
                                             vmem_limit_bytes=VMEM_LIMIT),
        name="conformer",
    )(P, bufp, dw, dw_b[None], ln_g[None], ln_b[None])
    return o, st[:, CONF_HALO - (CONV_WIDTH - 1):]


RET_LOG_GAMMA = tuple(math.log1p(-2.0 ** (-5 - h)) for h in range(RET_HEADS))
RET_TM = 256


def _retention_kernel(q_ref, k_ref, v_ref, g_ref, cos_ref, sin_ref, s0_ref, gn_ref, o_ref, sfin_ref,
                      s_ref, dec_ref, *, n_valid):
    f32, bf16 = jnp.float32, jnp.bfloat16
    C = q_ref.shape[1]
    W = RET_W
    half = HEAD_DIM // 2

    @pl.when(pl.program_id(1) == 0)
    def _():
        s_ref[...] = s0_ref[0]

    @pl.when((pl.program_id(0) == 0) & (pl.program_id(1) == 0))
    def _():
        ii = lax.broadcasted_iota(jnp.int32, (C, C), 0)
        jj = lax.broadcasted_iota(jnp.int32, (C, C), 1)
        d = (ii - jj).astype(f32)
        for h in range(RET_HEADS):
            dec_ref[h] = jnp.where((ii >= jj) & (jj < n_valid), jnp.exp(jnp.maximum(d, 0.0) * RET_LOG_GAMMA[h]), 0.0)

    lane = lax.broadcasted_iota(jnp.int32, (C, W), 1)
    low = lane % HEAD_DIM < half
    cos = cos_ref[...]
    sin = sin_ref[...]

    def rot(x):
        other = jnp.where(low, pltpu.roll(x, W - half, axis=1), pltpu.roll(x, half, axis=1))
        return x * cos + other * sin

    q = rot(q_ref[0])
    k = rot(k_ref[0]) * HEAD_DIM ** -0.5
    v = v_ref[0]
    row = lax.broadcasted_iota(jnp.int32, (C, HEAD_DIM), 0)
    rowf = row.astype(f32)
    valid = row < n_valid
    outs = []
    for h in range(RET_HEADS):
        cs = slice(h * HEAD_DIM, (h + 1) * HEAD_DIM)
        lg = RET_LOG_GAMMA[h]
        qh, kh, vh = q[:, cs], k[:, cs], v[:, cs].astype(bf16)
        s_old = s_ref[h]
        att = _dot_nt(qh.astype(bf16), kh.astype(bf16)) * dec_ref[h]
        o = (jnp.dot(att.astype(bf16), vh, preferred_element_type=f32)
             + jnp.dot((qh * jnp.exp((rowf + 1.0) * lg)).astype(bf16), s_old.astype(bf16), preferred_element_type=f32))
        kz = jnp.where(valid, kh * jnp.exp((n_valid - 1.0 - rowf) * lg), 0.0)
        s_ref[h] = s_old * math.exp(n_valid * lg) + jnp.dot(kz.T.astype(bf16), vh, preferred_element_type=f32)
        mu = jnp.mean(o, axis=-1, keepdims=True)
        var = jnp.mean(jnp.square(o - mu), axis=-1, keepdims=True)
        outs.append((o - mu) * lax.rsqrt(var + EPS))
    g = g_ref[0]
    o_ref[0] = jnp.concatenate(outs, axis=1) * gn_ref[...] * (g * jax.nn.sigmoid(g))

    @pl.when(pl.program_id(1) == pl.num_programs(1) - 1)
    def _():
        sfin_ref[0] = s_ref[...]


def _retention(P, col_blk0, S0, pos, gn, chunk, n_valid):
    B, T, _ = P.shape
    half = HEAD_DIM // 2
    inv = ROPE_BASE ** (-jnp.arange(half, dtype=jnp.float32) / half)
    ang = pos.astype(jnp.float32)[:, None] * inv[None, :]
    cos, sin = jnp.cos(ang), jnp.sin(ang)
    cosf = jnp.tile(jnp.concatenate([cos, cos], axis=1), (1, RET_HEADS))
    sinf = jnp.tile(jnp.concatenate([-sin, sin], axis=1), (1, RET_HEADS))
    col = lambda j: pl.BlockSpec((1, chunk, RET_W), lambda b, i: (b, i, col_blk0 + j))
    tab = pl.BlockSpec((chunk, RET_W), lambda b, i: (i, 0))
    state = pl.BlockSpec((1, RET_HEADS, HEAD_DIM, HEAD_DIM), lambda b, i: (b, 0, 0, 0))
    return pl.pallas_call(
        functools.partial(_retention_kernel, n_valid=n_valid),
        grid=(B, T // chunk),
        in_specs=[col(0), col(1), col(2), col(3), tab, tab, state, _resident((1, RET_W))],
        out_specs=[pl.BlockSpec((1, chunk, RET_W), lambda b, i: (b, i, 0)), state],
        out_shape=[jax.ShapeDtypeStruct((B, T, RET_W), jnp.float32),
                   jax.ShapeDtypeStruct((B, RET_HEADS, HEAD_DIM, HEAD_DIM), jnp.float32)],
        scratch_shapes=[pltpu.VMEM((RET_HEADS, HEAD_DIM, HEAD_DIM), jnp.float32),
                        pltpu.VMEM((RET_HEADS, chunk, chunk), jnp.float32)],
        compiler_params=pltpu.CompilerParams(dimension_semantics=("arbitrary", "arbitrary"),
                                             vmem_limit_bytes=VMEM_LIMIT),
        name="retention",
    )(P, P, P, P, cosf, sinf, S0, gn[None])


GDN_RT = 4 * GDN_CHUNK
GDN_HALO = SUBLANES
GDN_A_LANE = 3 * NSA_HEADS
GDN_B_LANE = GDN_A_LANE + GDN_HEADS


def _mm1(a, b):
    return jnp.dot(a.astype(jnp.bfloat16), b.astype(jnp.bfloat16), preferred_element_type=jnp.float32)


def _mm3(a, b):
    f32, bf16 = jnp.float32, jnp.bfloat16
    ah, bh = a.astype(bf16), b.astype(bf16)
    al, bl = (a - ah.astype(f32)).astype(bf16), (b - bh.astype(f32)).astype(bf16)
    d = lambda x, y: jnp.dot(x, y, preferred_element_type=f32)
    return d(ah, bh) + (d(ah, bl) + d(al, bh))


def _segment_cumsum(x, axis, seg):
    idx = lax.broadcasted_iota(jnp.int32, x.shape, axis) % seg
    s = 1
    while s < seg:
        x = x + jnp.where(idx >= s, pltpu.roll(x, s, axis=axis), 0.0)
        s *= 2
    return x


def _gdn_kernel(q_ref, k_ref, v_ref, z_ref, ab_ref, abt_ref, buf_ref, s0_ref, cw_ref, ng_ref, alane_ref, dlane_ref,
                acol_ref, dcol_ref, o_ref, sfin_ref, s_ref, xp_ref, *, n_valid):
    f32 = jnp.float32
    RT = q_ref.shape[1]
    C = GDN_CHUNK
    W = GDN_W
    first = GDN_HALO - (GDN_CONV - 1)

    @pl.when(pl.program_id(1) == 0)
    def _():
        s_ref[...] = s0_ref[0]
        xp_ref[0:GDN_HALO, :] = buf_ref[0]

    xp_ref[GDN_HALO:GDN_HALO + RT, 0:W] = q_ref[0]
    xp_ref[GDN_HALO:GDN_HALO + RT, W:2 * W] = k_ref[0]
    xp_ref[GDN_HALO:GDN_HALO + RT, 2 * W:3 * W] = v_ref[0]
    y = jnp.zeros((RT, 3 * W), f32)
    for t in range(GDN_CONV):
        y = y + cw_ref[t:t + 1, :] * xp_ref[first + t:first + t + RT, :]
    y = y * jax.nn.sigmoid(y)
    xp_ref[0:GDN_HALO, :] = xp_ref[RT:RT + GDN_HALO, :]

    ab = ab_ref[0]
    g_lanes = -jnp.exp(alane_ref[...]) * jax.nn.softplus(ab + dlane_ref[...])
    beta_lanes = jax.nn.sigmoid(ab)
    g_rows = -jnp.exp(acol_ref[...]) * jax.nn.softplus(abt_ref[0] + dcol_ref[...])
    if n_valid < C:
        g_lanes = jnp.where(lax.broadcasted_iota(jnp.int32, g_lanes.shape, 0) % C < n_valid, g_lanes, 0.0)
        beta_lanes = jnp.where(lax.broadcasted_iota(jnp.int32, g_lanes.shape, 0) % C < n_valid, beta_lanes, 0.0)
        g_rows = jnp.where(lax.broadcasted_iota(jnp.int32, g_rows.shape, 1) % C < n_valid, g_rows, 0.0)
    gc_lanes = _segment_cumsum(g_lanes, 0, C)
    gc_rows = _segment_cumsum(g_rows, 1, C)

    ii = lax.broadcasted_iota(jnp.int32, (C, C), 0)
    jj = lax.broadcasted_iota(jnp.int32, (C, C), 1)
    tri = ii >= jj
    strict = ii > jj
    eye = (ii == jj).astype(f32)
    z = z_ref[0]
    bf16 = jnp.bfloat16
    nch = RT // C
    pairs = [(c, h) for c in range(nch) for h in range(GDN_HEADS)]
    qs, ks, ms, atts, rhss, gcs = {}, {}, {}, {}, {}, {}
    for c, h in pairs:
        rs = slice(c * C, (c + 1) * C)
        qh = y[rs, h * HEAD_DIM:(h + 1) * HEAD_DIM]
        kh = y[rs, W + h * HEAD_DIM:W + (h + 1) * HEAD_DIM]
        vh = y[rs, 2 * W + h * HEAD_DIM:2 * W + (h + 1) * HEAD_DIM]
        qh = qh * lax.rsqrt(jnp.sum(qh * qh, axis=-1, keepdims=True) + EPS) * HEAD_DIM ** -0.5
        kh = kh * lax.rsqrt(jnp.sum(kh * kh, axis=-1, keepdims=True) + EPS)
        gc_col = gc_lanes[rs, GDN_A_LANE + h:GDN_A_LANE + h + 1]
        beta = beta_lanes[rs, GDN_B_LANE + h:GDN_B_LANE + h + 1]
        gc_row = gc_rows[h:h + 1, c * C:(c + 1) * C]
        e_col = jnp.exp(gc_col)
        lm = jnp.exp(jnp.where(tri, gc_col - gc_row, NEG_INF))
        kb = kh * beta
        ms[c, h] = jnp.where(strict, _dot_nt(kb.astype(bf16), kh.astype(bf16)) * lm, 0.0)
        atts[c, h] = _dot_nt(qh.astype(bf16), kh.astype(bf16)) * lm
        rhss[c, h] = jnp.concatenate([vh * beta, kb * e_col], axis=1)
        qs[c, h], ks[c, h], gcs[c, h] = qh * e_col, kh, gc_col
    pw = {p: -ms[p] for p in pairs}
    xs = {p: eye + pw[p] for p in pairs}
    for _ in range(int(math.log2(C)) - 1):
        pw = {p: _mm1(pw[p], pw[p]) for p in pairs}
        xs = {p: xs[p] + _mm1(xs[p], pw[p]) for p in pairs}
    res = {p: eye - (xs[p] + _mm3(ms[p], xs[p])) for p in pairs}
    xs = {p: xs[p] + _mm1(xs[p], res[p]) for p in pairs}
    sols = {p: _mm3(xs[p], rhss[p]) for p in pairs}
    state = [s_ref[h] for h in range(GDN_HEADS)]
    outs = {}
    for c in range(nch):
        heads = range(GDN_HEADS)
        g_last = [gcs[c, h][C - 1:C, :] for h in heads]
        vn = [sols[c, h][:, :HEAD_DIM] - _mm1(sols[c, h][:, HEAD_DIM:], state[h]) for h in heads]
        o_in = [_mm1(qs[c, h], state[h]) for h in heads]
        kd_t = [(ks[c, h] * jnp.exp(g_last[h] - gcs[c, h])).T for h in heads]
        for h in heads:
            outs[c, h] = o_in[h] + _mm1(atts[c, h], vn[h])
        state = [state[h] * jnp.exp(g_last[h]) + _mm1(kd_t[h], vn[h]) for h in heads]
    for h in range(GDN_HEADS):
        s_ref[h] = state[h]
    row_outs = []
    for c in range(nch):
        head_outs = []
        for h in range(GDN_HEADS):
            o = outs[c, h]
            o = o * lax.rsqrt(jnp.mean(o * o, axis=-1, keepdims=True) + EPS) * ng_ref[...]
            zh = z[c * C:(c + 1) * C, h * HEAD_DIM:(h + 1) * HEAD_DIM]
            head_outs.append(o * (zh * jax.nn.sigmoid(zh)))
        row_outs.append(jnp.concatenate(head_outs, axis=1))
    o_ref[0] = jnp.concatenate(row_outs, axis=0)

    @pl.when(pl.program_id(1) == pl.num_programs(1) - 1)
    def _():
        sfin_ref[0] = s_ref[...]


def _gdn(P, blk_q, blk_misc, ga, conv_buf, S0, conv_w, A_log, dt_bias, norm_g, rt, n_valid):
    B, T, _ = P.shape
    f32 = jnp.float32
    abt = jnp.pad(jnp.transpose(ga, (0, 2, 1)), ((0, 0), (0, SUBLANES - GDN_HEADS), (0, 0)))
    buf8 = jnp.pad(conv_buf, ((0, 0), (GDN_HALO - (GDN_CONV - 1), 0), (0, 0)))
    lane_vec = lambda v: jnp.zeros((1, LANES), f32).at[0, GDN_A_LANE:GDN_A_LANE + GDN_HEADS].set(v)
    col_vec = lambda v: jnp.zeros((SUBLANES, 1), f32).at[0:GDN_HEADS, 0].set(v)
    col = lambda j: pl.BlockSpec((1, rt, GDN_W), lambda b, i: (b, i, blk_q + j))
    state = pl.BlockSpec((1, GDN_HEADS, HEAD_DIM, HEAD_DIM), lambda b, i: (b, 0, 0, 0))
    return pl.pallas_call(
        functools.partial(_gdn_kernel, n_valid=n_valid),
        grid=(B, T // rt),
        in_specs=[col(0), col(1), col(2), col(3),
                  pl.BlockSpec((1, rt, LANES), lambda b, i: (b, i, blk_misc)),
                  pl.BlockSpec((1, SUBLANES, rt), lambda b, i: (b, 0, i)),
                  pl.BlockSpec((1, GDN_HALO, 3 * GDN_W), lambda b, i: (b, 0, 0)), state,
                  _resident((GDN_CONV, 3 * GDN_W)), _resident((1, HEAD_DIM)),
                  _resident((1, LANES)), _resident((1, LANES)), _resident((SUBLANES, 1)), _resident((SUBLANES, 1))],
        out_specs=[pl.BlockSpec((1, rt, GDN_W), lambda b, i: (b, i, 0)), state],
        out_shape=[jax.ShapeDtypeStruct((B, T, GDN_W), f32),
                   jax.ShapeDtypeStruct((B, GDN_HEADS, HEAD_DIM, HEAD_DIM), f32)],
        scratch_shapes=[pltpu.VMEM((GDN_HEADS, HEAD_DIM, HEAD_DIM), f32),
                        pltpu.VMEM((rt + GDN_HALO, 3 * GDN_W), f32)],
        compiler_params=pltpu.CompilerParams(dimension_semantics=("arbitrary", "arbitrary"),
                                             vmem_limit_bytes=VMEM_LIMIT),
        name="gated_deltanet",
    )(P, P, P, P, P, abt, buf8, S0, conv_w, norm_g[None], lane_vec(A_log), lane_vec(dt_bias),
      col_vec(A_log), col_vec(dt_bias))


def _prep_layer(p):
    w_main, w_mg = _prep_w_in(p["w_in"])
    q = dict(p)
    q.update(w_main=w_main, w_mg=w_mg, w_branch_b=p["w_branch"].astype(jnp.bfloat16),
             w_out_b=p["w_out"].astype(jnp.bfloat16), ffn_up_b=p["ffn_up"].astype(jnp.bfloat16),
             ffn_down_b=p["ffn_down"].astype(jnp.bfloat16))
    return q


def trunk_layer(x, mod, pos0, nsa_past, win_buf, conv_buf, ret_s, gdn_buf, gdn_s, ffn_buf, p, rel_bias):
    B, T, _ = x.shape
    M = B * T
    per_row = T < MERGE_TM
    if per_row:
        sh1, sc1, gt1, sh2, sc2, gt2 = [jnp.repeat(m, T, axis=0)[None] for m in jnp.split(mod, 6, axis=-1)]
    else:
        sh1, sc1, gt1, sh2, sc2, gt2 = [m[:, None, :] for m in jnp.split(mod, 6, axis=-1)]
    norms = p["norms"][:, None, :]
    x2 = x.reshape(M, D_MODEL)
    P = _proj(x2, norms[0], sc1, sh1, p["w_main"], T, M if per_row else PROJ_TM).reshape(B, T, PROJ_COLS)
    nkv = P[:, :, PROJ_KV:PROJ_KV + 6 * KVW]
    ngt = P[:, :, PROJ_MISC:PROJ_MISC + 3 * NSA_HEADS]
    ga = P[:, :, PROJ_MISC + 3 * NSA_HEADS:PROJ_MISC + 3 * NSA_HEADS + GDN_HEADS]
    gb = P[:, :, PROJ_MISC + 3 * NSA_HEADS + GDN_HEADS:PROJ_MISC + 3 * NSA_HEADS + 2 * GDN_HEADS]
    keep = min(NSA_WINDOW, T)
    kw = P[:, T - keep:, PROJ_KV + 4 * KVW:PROJ_KV + 6 * KVW].reshape(B, keep, 2, NSA_KV_HEADS, HEAD_DIM)
    if nsa_past is None:
        o_nsa = _nsa_prompt(P, PROJ_Q // (NSA_HEADS * HEAD_DIM), P, PROJ_KV // (6 * KVW), ngt,
                            p["cmp_pool"], p["cmp_pe"], rel_bias)
        new_win = kw
    else:
        cache, layer, page_table = nsa_past
        o_nsa = _nsa_sample(P[:, :, PROJ_Q:PROJ_Q + NSA_HEADS * HEAD_DIM], nkv, ngt, cache, layer, page_table,
                            win_buf, p["cmp_pool"], p["cmp_pe"], rel_bias)
        real = jnp.concatenate([win_buf, kw], axis=1)
        new_win = real[:, real.shape[1] - min(NSA_WINDOW, real.shape[1]):]
    o_conv, new_conv = _conformer(P, PROJ_UCV // (2 * CONV_CH), conv_buf, p["conv_dw"], p["conv_dw_b"],
                                  p["conv_ln_g"], p["conv_ln_b"], T if per_row else CONF_TM)
    if per_row:
        Pr = jnp.pad(P[:, :, PROJ_RET:PROJ_RET + 4 * RET_W], ((0, 0), (0, RET_CHUNK - T), (0, 0)))
        o_ret, new_ret = _retention(Pr, 0, ret_s, pos0 + jnp.arange(RET_CHUNK, dtype=jnp.int32), p["ret_gn"],
                                    RET_CHUNK, T)
        o_ret = o_ret[:, :T]
    else:
        o_ret, new_ret = _retention(P, PROJ_RET // RET_W, ret_s, pos0 + jnp.arange(T, dtype=jnp.int32), p["ret_gn"],
                                    RET_TM, RET_TM)
    gdn_w = (p["gdn_conv_w"], p["gdn_A_log"], p["gdn_dt_bias"], p["gdn_norm"])
    if per_row:
        pad_rows = ((0, 0), (0, GDN_CHUNK - T), (0, 0))
        Pg = jnp.pad(P[:, :, PROJ_GQKV:PROJ_COLS], pad_rows)
        o_gdn, new_gdn = _gdn(Pg, 0, (PROJ_MISC - PROJ_GQKV) // LANES, jnp.pad(ga, pad_rows), gdn_buf, gdn_s, *gdn_w,
                              GDN_CHUNK, T)
        o_gdn = o_gdn[:, :T]
    else:
        o_gdn, new_gdn = _gdn(P, PROJ_GQKV // GDN_W, PROJ_MISC // LANES, ga, gdn_buf, gdn_s, *gdn_w,
                              GDN_RT, GDN_CHUNK)
    new_gdn_buf = jnp.concatenate([gdn_buf, P[:, :, PROJ_GQKV:PROJ_GQKV + 3 * GDN_W]], axis=1)[:, T:]
    branches = [o.reshape(M, BRANCH_W) for o in (o_nsa, o_conv, o_ret, o_gdn)]
    x1 = _merge(x2, norms[0], sc1, sh1, gt1, norms[1], branches, p["w_mg"], p["w_branch_b"], p["w_out_b"],
                T, M if per_row else MERGE_TM).reshape(B, T, D_MODEL)
    if per_row:
        x_out, new_ffn = _ffn_rows(x1, norms[2], sc2, sh2, gt2, norms[3], ffn_buf, p["ffn_up_b"], p["ffn_dw"],
                                   p["ffn_down_b"])
    else:
        x_out, new_ffn = _ffn(x1, norms[2], sc2, sh2, gt2, norms[3], ffn_buf, p["ffn_up_b"], p["ffn_dw"],
                              p["ffn_down_b"], FFN_TM)
    kv_rows = nkv[:, :, :4 * KVW].reshape(B, T, 4, NSA_KV_HEADS, HEAD_DIM)
    return x_out, (kv_rows, new_win, new_conv, new_ret, new_gdn_buf, new_gdn, new_ffn)


def kernel(x_prompt, x_sample, cache_nsa_kv, cache_nsa_win, state_conv, state_ret, state_gdn_conv, state_gdn,
           state_ffn_conv, page_table, c_prompt, c_sample, w_ada, b_ada, norms, w_in, cmp_pool, cmp_pe, rel_bias,
           conv_dw, conv_dw_b, conv_ln_g, conv_ln_b, ret_gn, gdn_conv_w, gdn_A_log, gdn_dt_bias, gdn_norm,
           w_branch, w_out, ffn_up, ffn_dw, ffn_down):
    B = x_prompt.shape[0]
    Bd = x_sample.shape[0]
    past = page_table.shape[1] * PAGE_SIZE
    mods = _adaln(jnp.concatenate([c_prompt, c_sample], axis=0), w_ada, b_ada)
    layer_w = {"norms": norms, "w_in": w_in, "cmp_pool": cmp_pool,
               "cmp_pe": cmp_pe, "conv_dw": conv_dw, "conv_dw_b": conv_dw_b, "conv_ln_g": conv_ln_g,
               "conv_ln_b": conv_ln_b, "ret_gn": ret_gn, "gdn_conv_w": gdn_conv_w, "gdn_A_log": gdn_A_log,
               "gdn_dt_bias": gdn_dt_bias, "gdn_norm": gdn_norm, "w_branch": w_branch, "w_out": w_out,
               "ffn_up": ffn_up, "ffn_dw": ffn_dw, "ffn_down": ffn_down}
    yp, ys = x_prompt, x_sample
    st_p, st_s = [], []
    for l in range(DEPTH):
        p = _prep_layer({name: w[l] for name, w in layer_w.items()})
        yp, sp = trunk_layer(
            yp, mods[l, :B], 0, None, None,
            jnp.zeros((B, CONV_WIDTH - 1, CONV_CH), x_prompt.dtype),
            jnp.zeros((B, RET_HEADS, HEAD_DIM, HEAD_DIM), jnp.float32),
            jnp.zeros((B, GDN_CONV - 1, 3 * GDN_W), x_prompt.dtype),
            jnp.zeros((B, GDN_HEADS, HEAD_DIM, HEAD_DIM), jnp.float32),
            jnp.zeros((B, FFN_CONV - 1, D_FF), x_prompt.dtype),
            p, rel_bias)
        ys, ss = trunk_layer(
            ys, mods[l, B:], past, (cache_nsa_kv, l, page_table), cache_nsa_win[l], state_conv[l], state_ret[l],
            state_gdn_conv[l], state_gdn[l], state_ffn_conv[l], p, rel_bias)
        st_p.append(sp)
        st_s.append(ss)

    def stack(outs, i, axis):
        return jnp.stack([o[i] for o in outs], axis=axis)

    kv_p, kv_s = stack(st_p, 0, 1), stack(st_s, 0, 1)
    win_p, win_s = stack(st_p, 1, 0), stack(st_s, 1, 0)
    conv_p, conv_s = stack(st_p, 2, 0), stack(st_s, 2, 0)
    ret_p, ret_s = stack(st_p, 3, 0), stack(st_s, 3, 0)
    gdnc_p, gdnc_s = stack(st_p, 4, 0), stack(st_s, 4, 0)
    gdn_p, gdn_s = stack(st_p, 5, 0), stack(st_s, 5, 0)
    ffn_p, ffn_s = stack(st_p, 6, 0), stack(st_s, 6, 0)
    return (yp, ys, kv_p, kv_s, win_p, win_s, conv_p, conv_s, ret_p, ret_s, gdnc_p, gdnc_s, gdn_p, gdn_s, ffn_p, ffn_s)
```
